```python
import jax, jax.numpy as jnp
from jax import lax
import numpy as np

D_MODEL = 1024
BATCH = 4
SEQ = 4096
DEPTH = 4
DEC_BATCH = 32
DEC_SEQ = 1
PAST_LEN = 8192
PAGE_SIZE = 128

N_HEADS = 16
HEAD_DIM = D_MODEL // N_HEADS
N_KV_HEADS = 4
GROUP = N_HEADS // N_KV_HEADS
ROPE_FRACTION = 4
ROPE_THETA = 500000.0
N_IDX_HEADS = 8
IDX_DIM = 64
TOPK_MAX = 256
D_FF = 2816
CONV_W = 3
Q_BLOCK = 128
N_MIXERS = 2
N_A_LAYERS = (DEPTH + 1) // 2
N_B_LAYERS = DEPTH // 2
EPS = 1e-6
FORGET_BIAS_MEAN = 4.0
QKV_COLS = (N_HEADS + 2 * N_KV_HEADS) * HEAD_DIM
IN_A_COLS = QKV_COLS + N_IDX_HEADS * IDX_DIM + IDX_DIM + N_IDX_HEADS
IN_B_COLS = QKV_COLS + N_HEADS
A_SPLITS = (N_HEADS * HEAD_DIM, (N_HEADS + N_KV_HEADS) * HEAD_DIM, QKV_COLS,
            QKV_COLS + N_IDX_HEADS * IDX_DIM, QKV_COLS + N_IDX_HEADS * IDX_DIM + IDX_DIM)
B_SPLITS = (N_HEADS * HEAD_DIM, (N_HEADS + N_KV_HEADS) * HEAD_DIM, QKV_COLS)

kernel_name = 'hybrid_dsa_fox_convffn_adaln_step'


def rmsnorm(x, g):
    xf = x.astype(jnp.float32)
    y = xf * lax.rsqrt(jnp.mean(xf * xf, axis=-1, keepdims=True) + EPS)
    return (y * g.astype(jnp.float32)).astype(x.dtype)


def adaln(c, w, b):
    m = jax.nn.silu(c) @ w + b
    return jnp.split(m[:, None, :], 6, axis=-1)


def modulate(h, shift, scale):
    return h * (1 + scale) + shift


def rope_partial(x, pos):
    rd = x.shape[-1] // ROPE_FRACTION
    half = rd // 2
    inv_freq = ROPE_THETA ** (-jnp.arange(half, dtype=jnp.float32) / half)
    ang = pos.astype(jnp.float32)[:, None] * inv_freq[None, :]
    cos = jnp.cos(ang)[:, None, :]
    sin = jnp.sin(ang)[:, None, :]
    xf = x.astype(jnp.float32)
    x1, x2 = xf[..., :half], xf[..., half:rd]
    rot = jnp.concatenate([x1 * cos - x2 * sin, x1 * sin + x2 * cos], axis=-1).astype(x.dtype)
    return jnp.concatenate([rot, x[..., rd:]], axis=-1)


def paged_past(cache, layer, page_table):
    g = cache[layer, page_table]
    return g.reshape((g.shape[0], g.shape[1] * g.shape[2]) + g.shape[3:])


def dsa_project(h, w_in, pos):
    B, T, _ = h.shape
    q, k, v, iq, ik, iw = jnp.split(h @ w_in, A_SPLITS, axis=-1)
    q = rope_partial(q.reshape(B, T, N_HEADS, HEAD_DIM), pos)
    k = rope_partial(k.reshape(B, T, N_KV_HEADS, HEAD_DIM), pos)
    v = v.reshape(B, T, N_KV_HEADS, HEAD_DIM)
    iq = rope_partial(iq.reshape(B, T, N_IDX_HEADS, IDX_DIM), pos)
    ik = rope_partial(ik.reshape(B, T, 1, IDX_DIM), pos)[:, :, 0]
    return q, k, v, iq, ik, iw


def dsa_attend(q, iq, iw, qpos, k, v, ik, kpos, n_sel):
    B, Tq = q.shape[:2]
    f32 = jnp.float32
    dots = jnp.einsum('bqhd,bsd->bqhs', iq.astype(f32), ik.astype(f32))
    score = jnp.einsum('bqhs,bqh->bqs', jax.nn.relu(dots), iw.astype(f32))
    causal = kpos[None, :] <= qpos[:, None]
    score = jnp.where(causal[None], score, -jnp.inf)
    top_score, idx = lax.top_k(score, n_sel)
    valid = jnp.isfinite(top_score)
    ks = jax.vmap(lambda a, i: a[i])(k, idx)
    vs = jax.vmap(lambda a, i: a[i])(v, idx)
    qg = q.reshape(B, Tq, N_KV_HEADS, GROUP, HEAD_DIM).astype(f32)
    logits = jnp.einsum('bqkgd,bqnkd->bqkgn', qg, ks.astype(f32)) * (HEAD_DIM ** -0.5)
    logits = jnp.where(valid[:, :, None, None, :], logits, -jnp.inf)
    p = jax.nn.softmax(logits, axis=-1)
    o = jnp.einsum('bqkgn,bqnkd->bqkgd', p.astype(vs.dtype), vs)
    return o.reshape(B, Tq, N_HEADS * HEAD_DIM)


def dsa_prompt(h, w_in, w_o):
    B, T, _ = h.shape
    pos = jnp.arange(T)
    q, k, v, iq, ik, iw = dsa_project(h, w_in, pos)
    n_sel = min(TOPK_MAX, T // 4)

    def block(start):
        sl = lambda a: lax.dynamic_slice_in_dim(a, start, Q_BLOCK, axis=1)
        qpos = start + jnp.arange(Q_BLOCK)
        return dsa_attend(sl(q), sl(iq), sl(iw), qpos, k, v, ik, pos, n_sel)

    o = lax.map(block, jnp.arange(0, T, Q_BLOCK))
    o = jnp.swapaxes(o, 0, 1).reshape(B, T, N_HEADS * HEAD_DIM)
    return o @ w_o, k, v, ik


def dsa_sample(h, w_in, w_o, k_past, v_past, ik_past):
    B, T, _ = h.shape
    qpos = PAST_LEN + jnp.arange(T)
    q, k, v, iq, ik, iw = dsa_project(h, w_in, qpos)
    k_all = jnp.concatenate([k_past.astype(k.dtype), k], axis=1)
    v_all = jnp.concatenate([v_past.astype(v.dtype), v], axis=1)
    ik_all = jnp.concatenate([ik_past.astype(ik.dtype), ik], axis=1)
    L = k_all.shape[1]
    n_sel = min(TOPK_MAX, L // 4)
    o = dsa_attend(q, iq, iw, qpos, k_all, v_all, ik_all, jnp.arange(L), n_sel)
    return o @ w_o, k, v, ik


def fox_project(h, w_in, b_f):
    B, T, _ = h.shape
    q, k, v, fg = jnp.split(h @ w_in, B_SPLITS, axis=-1)
    logf = jax.nn.log_sigmoid(fg.astype(jnp.float32) + b_f.astype(jnp.float32))
    return (q.reshape(B, T, N_HEADS, HEAD_DIM), k.reshape(B, T, N_KV_HEADS, HEAD_DIM),
            v.reshape(B, T, N_KV_HEADS, HEAD_DIM), logf)


def fox_attend(q, cq, qpos, k, v, ck, kpos):
    B, Tq = q.shape[:2]
    L = k.shape[1]
    f32 = jnp.float32
    qg = q.reshape(B, Tq, N_KV_HEADS, GROUP, HEAD_DIM).astype(f32)
    logits = jnp.einsum('bqkgd,bskd->bkgqs', qg, k.astype(f32)) * (HEAD_DIM ** -0.5)
    cq_ = cq.reshape(B, Tq, N_KV_HEADS, GROUP).transpose(0, 2, 3, 1)[..., :, None]
    ck_ = ck.reshape(B, L, N_KV_HEADS, GROUP).transpose(0, 2, 3, 1)[..., None, :]
    logits = logits + (cq_ - ck_)
    causal = kpos[None, :] <= qpos[:, None]
    p = jax.nn.softmax(jnp.where(causal, logits, -jnp.inf), axis=-1)
    o = jnp.einsum('bkgqs,bskd->bqkgd', p.astype(v.dtype), v)
    return o.reshape(B, Tq, N_HEADS * HEAD_DIM)


def fox_prompt(h, w_in, b_f, w_o):
    B, T, _ = h.shape
    q, k, v, logf = fox_project(h, w_in, b_f)
    c = jnp.cumsum(logf, axis=1)
    pos = jnp.arange(T)

    def block(start):
        sl = lambda a: lax.dynamic_slice_in_dim(a, start, Q_BLOCK, axis=1)
        qpos = start + jnp.arange(Q_BLOCK)
        return fox_attend(sl(q), sl(c), qpos, k, v, c, pos)

    o = lax.map(block, jnp.arange(0, T, Q_BLOCK))
    o = jnp.swapaxes(o, 0, 1).reshape(B, T, N_HEADS * HEAD_DIM)
    return o @ w_o, k, v, logf


def fox_sample(h, w_in, b_f, w_o, k_past, v_past, logf_past):
    B, T, _ = h.shape
    q, k, v, logf = fox_project(h, w_in, b_f)
    k_all = jnp.concatenate([k_past.astype(k.dtype), k], axis=1)
    v_all = jnp.concatenate([v_past.astype(v.dtype), v], axis=1)
    c = jnp.cumsum(jnp.concatenate([logf_past.astype(jnp.float32), logf], axis=1), axis=1)
    L = k_all.shape[1]
    qpos = PAST_LEN + jnp.arange(T)
    o = fox_attend(q, c[:, L - T:], qpos, k_all, v_all, c, jnp.arange(L))
    return o @ w_o, k, v, logf


def conv_ffn(h, w_up, conv_w, conv_b, w_down, buf):
    T = h.shape[1]
    a, b = jnp.split(h @ w_up, 2, axis=-1)
    ext = jnp.concatenate([buf.astype(a.dtype), a], axis=1)
    conv = conv_b + sum(ext[:, i:i + T] * conv_w[i] for i in range(CONV_W))
    y = (jax.nn.silu(conv) * b) @ w_down
    return y, ext[:, T:]


def setup_inputs(seed: int = 0) -> dict:
    key = jax.random.key(seed)
    k = jax.random.split(key, 32)
    f32 = jnp.float32
    nrm = lambda kk, shape, s: jax.random.normal(kk, shape, f32) * s
    n_pages = PAST_LEN // PAGE_SIZE
    n_used = DEC_BATCH * n_pages
    n_pool = n_used + (n_used + 3) // 4
    page_table = jax.random.permutation(k[0], n_pool)[:n_used].reshape(DEC_BATCH, n_pages).astype(jnp.int32)
    kv_a = (N_A_LAYERS, n_pool, PAGE_SIZE, N_KV_HEADS, HEAD_DIM)
    kv_b = (N_B_LAYERS, n_pool, PAGE_SIZE, N_KV_HEADS, HEAD_DIM)
    d_in = D_MODEL ** -0.5
    return {
        'x_prompt': nrm(k[1], (BATCH, SEQ, D_MODEL), 1.0),
        'x_sample': nrm(k[2], (DEC_BATCH, DEC_SEQ, D_MODEL), 1.0),
        'cache_k_a': nrm(k[3], kv_a, 1.0),
        'cache_v_a': nrm(k[4], kv_a, 1.0),
        'cache_kidx_a': nrm(k[5], (N_A_LAYERS, n_pool, PAGE_SIZE, IDX_DIM), 1.0),
        'cache_k_b': nrm(k[6], kv_b, 1.0),
        'cache_v_b': nrm(k[7], kv_b, 1.0),
        'cache_logf_b': jax.nn.log_sigmoid(FORGET_BIAS_MEAN + nrm(k[8], (N_B_LAYERS, n_pool, PAGE_SIZE, N_HEADS), 0.5)),
        'state_conv': nrm(k[9], (DEPTH, DEC_BATCH, CONV_W - 1, D_FF), 1.0),
        'page_table': page_table,
        'c_prompt': nrm(k[10], (BATCH, D_MODEL), 1.0),
        'c_sample': nrm(k[11], (DEC_BATCH, D_MODEL), 1.0),
        'w_ada': nrm(k[12], (DEPTH, D_MODEL, 6 * D_MODEL), 0.5 * d_in),
        'b_ada': nrm(k[13], (DEPTH, 6 * D_MODEL), 0.02),
        'g_mix': 1.0 + nrm(k[14], (DEPTH, D_MODEL), 0.02),
        'g_ffn': 1.0 + nrm(k[15], (DEPTH, D_MODEL), 0.02),
        'w_in_a': nrm(k[16], (N_A_LAYERS, D_MODEL, IN_A_COLS), d_in),
        'w_o_a': nrm(k[17], (N_A_LAYERS, N_HEADS * HEAD_DIM, D_MODEL), (N_HEADS * HEAD_DIM) ** -0.5),
        'w_in_b': nrm(k[18], (N_B_LAYERS, D_MODEL, IN_B_COLS), d_in),
        'b_f': FORGET_BIAS_MEAN + nrm(k[19], (N_B_LAYERS, N_HEADS), 0.5),
        'w_o_b': nrm(k[20], (N_B_LAYERS, N_HEADS * HEAD_DIM, D_MODEL), (N_HEADS * HEAD_DIM) ** -0.5),
        'w_up': nrm(k[21], (DEPTH, D_MODEL, 2 * D_FF), d_in),
        'conv_w': nrm(k[22], (DEPTH, CONV_W, D_FF), CONV_W ** -0.5),
        'conv_b': nrm(k[23], (DEPTH, D_FF), 0.02),
        'w_down': nrm(k[24], (DEPTH, D_FF, D_MODEL), D_FF ** -0.5),
        'g_final': 1.0 + nrm(k[25], (D_MODEL,), 0.02),
    }


def reference(x_prompt, x_sample, cache_k_a, cache_v_a, cache_kidx_a, cache_k_b, cache_v_b,
              cache_logf_b, state_conv, page_table, c_prompt, c_sample, w_ada, b_ada, g_mix,
              g_ffn, w_in_a, w_o_a, w_in_b, b_f, w_o_b, w_up, conv_w, conv_b, w_down, g_final):
    xp, xs = x_prompt, x_sample
    B = xp.shape[0]
    ka_p, va_p, ia_p, kb_p, vb_p, lb_p, cv_p = [], [], [], [], [], [], []
    ka_s, va_s, ia_s, kb_s, vb_s, lb_s, cv_s = [], [], [], [], [], [], []
    for i in range(DEPTH):
        mp = adaln(c_prompt, w_ada[i], b_ada[i])
        ms = adaln(c_sample, w_ada[i], b_ada[i])
        hp = modulate(rmsnorm(xp, g_mix[i]), mp[0], mp[1])
        hs = modulate(rmsnorm(xs, g_mix[i]), ms[0], ms[1])
        j = i // N_MIXERS
        if i % N_MIXERS == 0:
            op, k1, v1, i1 = dsa_prompt(hp, w_in_a[j], w_o_a[j])
            os_, k2, v2, i2 = dsa_sample(hs, w_in_a[j], w_o_a[j],
                                         paged_past(cache_k_a, j, page_table),
                                         paged_past(cache_v_a, j, page_table),
                                         paged_past(cache_kidx_a, j, page_table))
            ka_p.append(k1); va_p.append(v1); ia_p.append(i1)
            ka_s.append(k2); va_s.append(v2); ia_s.append(i2)
        else:
            op, k1, v1, l1 = fox_prompt(hp, w_in_b[j], b_f[j], w_o_b[j])
            os_, k2, v2, l2 = fox_sample(hs, w_in_b[j], b_f[j], w_o_b[j],
                                         paged_past(cache_k_b, j, page_table),
                                         paged_past(cache_v_b, j, page_table),
                                         paged_past(cache_logf_b, j, page_table))
            kb_p.append(k1); vb_p.append(v1); lb_p.append(l1)
            kb_s.append(k2); vb_s.append(v2); lb_s.append(l2)
        xp = xp + mp[2] * op
        xs = xs + ms[2] * os_
        hp = modulate(rmsnorm(xp, g_ffn[i]), mp[3], mp[4])
        hs = modulate(rmsnorm(xs, g_ffn[i]), ms[3], ms[4])
        fp, bp = conv_ffn(hp, w_up[i], conv_w[i], conv_b[i], w_down[i],
                          jnp.zeros((B, CONV_W - 1, D_FF), hp.dtype))
        fs, bs = conv_ffn(hs, w_up[i], conv_w[i], conv_b[i], w_down[i], state_conv[i])
        xp = xp + mp[5] * fp
        xs = xs + ms[5] * fs
        cv_p.append(bp); cv_s.append(bs)
    y_prompt = rmsnorm(xp, g_final)
    y_sample = rmsnorm(xs, g_final)
    return (y_prompt, y_sample,
            jnp.stack(ka_p), jnp.stack(va_p), jnp.stack(ia_p),
            jnp.stack(kb_p), jnp.stack(vb_p), jnp.stack(lb_p), jnp.stack(cv_p),
            jnp.stack(ka_s), jnp.stack(va_s), jnp.stack(ia_s),
            jnp.stack(kb_s), jnp.stack(vb_s), jnp.stack(lb_s), jnp.stack(cv_s))
```

```python
import functools

import jax
import jax.numpy as jnp
from jax import lax
from jax.experimental import pallas as pl
from jax.experimental.pallas import tpu as pltpu

f32 = jnp.float32
bf16 = jnp.bfloat16
i32 = jnp.int32

D_MODEL = 1024
N_HEADS = 16
HEAD_DIM = 64
N_KV_HEADS = 4
GROUP = N_HEADS // N_KV_HEADS
KV_COLS = N_KV_HEADS * HEAD_DIM
QKV_COLS = (N_HEADS + 2 * N_KV_HEADS) * HEAD_DIM
ROPE_DIMS = HEAD_DIM // 4
ROPE_HALF = ROPE_DIMS // 2
ROPE_THETA = 500000.0
N_IDX_HEADS = 8
IDX_DIM = 64
TOPK_MAX = 256
D_FF = 2816
CONV_W = 3
EPS = 1e-6
PAGE_SIZE = 128
Q_SCALE = HEAD_DIM ** -0.5

LANES = 128
SUBLANES = 8
VMEM_LIMIT = 56 * 1024 * 1024
NEG = -1e30
INT_MIN = -2 ** 31
INT_MAX = 2 ** 31 - 1
KEY_NEG_INF = (0xFF800000 ^ 0x7FFFFFFF) - 2 ** 32


def _cparams(*sem):
    return pltpu.CompilerParams(dimension_semantics=sem, vmem_limit_bytes=VMEM_LIMIT)


def _dot(a, b):
    return jnp.dot(a, b, preferred_element_type=f32)


def _dot_nt(a, b):
    return lax.dot_general(a, b, (((1,), (1,)), ((), ())), preferred_element_type=f32)


def _split3(x):
    hi = x.astype(bf16)
    r1 = x - hi.astype(f32)
    mid = r1.astype(bf16)
    lo = (r1 - mid.astype(f32)).astype(bf16)
    return hi, mid, lo


def _norm_mod(x, g, shift, scale):
    xn = x * lax.rsqrt(jnp.mean(x * x, axis=-1, keepdims=True) + EPS)
    return (xn * g) * (1.0 + scale) + shift


def _silu(x):
    return x * jax.nn.sigmoid(x)


def _sort_key(s):
    bits = pltpu.bitcast(s, i32)
    return bits ^ ((bits >> 31) & 0x7FFFFFFF)


def _adaln_kernel(c_ref, w_ref, b_ref, o_ref):
    a = _silu(c_ref[...]).astype(bf16)
    o_ref[0] = _dot(a, w_ref[0].astype(bf16)) + b_ref[0]


def _adaln(c_all, w_ada, b_ada):
    depth, d, n = w_ada.shape
    rows = c_all.shape[0]
    tn = 1536
    return pl.pallas_call(
        _adaln_kernel,
        grid=(depth, n // tn),
        in_specs=[pl.BlockSpec((rows, d), lambda l, j: (0, 0)),
                  pl.BlockSpec((1, d, tn), lambda l, j: (l, 0, j)),
                  pl.BlockSpec((1, 1, tn), lambda l, j: (l, 0, j))],
        out_specs=pl.BlockSpec((1, rows, tn), lambda l, j: (l, 0, j)),
        out_shape=jax.ShapeDtypeStruct((depth, rows, n), f32),
        compiler_params=_cparams("parallel", "parallel"),
        name="adaln",
    )(c_all, w_ada, b_ada.reshape(depth, 1, n))


SEGS_A = ((0, 1024, "rope", ((0, Q_SCALE),)),
          (1024, 256, "rope", ((1, 1.0), (2, 1.0))),
          (1280, 256, None, ((3, 1.0), (4, 1.0))),
          (1536, 512, "rope", ((5, 1.0),)),
          (2048, 128, "rope", ((6, 1.0), (7, 1.0))),
          (2176, 128, None, ((8, 1.0),)))
OUTS_A = ((1024, bf16), (256, f32), (256, bf16), (256, f32), (256, bf16),
          (512, bf16), (128, f32), (128, bf16), (128, f32))
SEGS_B = ((0, 1024, None, ((0, Q_SCALE),)),
          (1024, 256, None, ((1, 1.0), (2, 1.0))),
          (1280, 256, None, ((3, 1.0), (4, 1.0))),
          (1536, 128, "logf", ((5, 1.0),)))
OUTS_B = ((1024, bf16), (256, f32), (256, bf16), (256, f32), (256, bf16), (128, f32))


def _in_proj_kernel(x_ref, g_ref, shift_ref, scale_ref, w_ref, cos_ref, sa_ref, sb_ref,
                    bf_ref, *out_refs, segs):
    h = _norm_mod(x_ref[...], g_ref[...], shift_ref[0], scale_ref[0]).astype(bf16)
    for c0, width, epi, outs in segs:
        for c in range(width // LANES):
            lo = c0 + c * LANES
            a = _dot(h, w_ref[:, lo:lo + LANES])
            if epi == "rope":
                a = (a * cos_ref[...] + pltpu.roll(a, ROPE_HALF, 1) * sa_ref[...]
                     + pltpu.roll(a, LANES - ROPE_HALF, 1) * sb_ref[...])
            elif epi == "logf":
                z = a + bf_ref[...]
                a = jnp.minimum(z, 0.0) - jnp.log(1.0 + jnp.exp(-jnp.abs(z)))
            for oi, sc in outs:
                val = a if sc == 1.0 else a * sc
                out_refs[oi][:, c * LANES:(c + 1) * LANES] = val.astype(out_refs[oi].dtype)


def _in_proj(x, g, shift, scale, w, tabs, b_f, segs, outs, tm, rows_per_mod):
    m, d = x.shape
    n = w.shape[1]
    r = shift.shape[1]
    cos, sa, sb = tabs
    tab_blocks = cos.shape[0] // tm
    mod_spec = pl.BlockSpec((1, r, d), lambda i: (i * tm // rows_per_mod, 0, 0))
    tab_spec = pl.BlockSpec((tm, LANES), lambda i: (i % tab_blocks, 0))
    return pl.pallas_call(
        functools.partial(_in_proj_kernel, segs=segs),
        grid=(m // tm,),
        in_specs=[pl.BlockSpec((tm, d), lambda i: (i, 0)),
                  pl.BlockSpec((1, d), lambda i: (0, 0)),
                  mod_spec, mod_spec,
                  pl.BlockSpec((d, n), lambda i: (0, 0)),
                  tab_spec, tab_spec, tab_spec,
                  pl.BlockSpec((1, LANES), lambda i: (0, 0))],
        out_specs=[pl.BlockSpec((tm, wd), lambda i: (i, 0)) for wd, _ in outs],
        out_shape=[jax.ShapeDtypeStruct((m, wd), dt) for wd, dt in outs],
        compiler_params=_cparams("parallel"),
        name="in_proj",
    )(x, g, shift, scale, w, cos, sa, sb, b_f)


def _rope_tables(pos):
    inv_freq = ROPE_THETA ** (-jnp.arange(ROPE_HALF, dtype=f32) / ROPE_HALF)
    ang = pos.astype(f32)[:, None] * inv_freq[None, :]
    cos, sin = jnp.cos(ang), jnp.sin(ang)
    t = pos.shape[0]
    ones = jnp.ones((t, HEAD_DIM - ROPE_DIMS), f32)
    zeros = jnp.zeros((t, HEAD_DIM - ROPE_DIMS), f32)
    zh = jnp.zeros((t, ROPE_HALF), f32)
    c = jnp.concatenate([cos, cos, ones], axis=1)
    sa = jnp.concatenate([zh, sin, zeros], axis=1)
    sb = jnp.concatenate([-sin, zh, zeros], axis=1)
    rep = LANES // HEAD_DIM
    return tuple(jnp.tile(a, (1, rep)) for a in (c, sa, sb))


def _flash_init(m_sc, l_sc, acc_sc):
    m_sc[...] = jnp.full(m_sc.shape, NEG, f32)
    l_sc[...] = jnp.zeros(l_sc.shape, f32)
    acc_sc[...] = jnp.zeros(acc_sc.shape, f32)


def _flash_step(s, v, m_sc, l_sc, acc_sc):
    m_old = m_sc[...]
    m_new = jnp.maximum(m_old, jnp.max(s, axis=1, keepdims=True))
    p = jnp.exp(s - m_new)
    alpha = jnp.exp(m_old - m_new)
    l_sc[...] = alpha * l_sc[...] + jnp.sum(p, axis=1, keepdims=True)
    acc_sc[...] = alpha * acc_sc[...] + _dot(p.astype(bf16), v)
    m_sc[...] = m_new


def _fox_attn_kernel(q_ref, k_ref, v_ref, cq_ref, ckt_ref, o_ref, m_sc, l_sc, acc_sc, *, tq):
    qi = pl.program_id(1)
    d_iota = lax.broadcasted_iota(i32, (tq, tq), 1) - lax.broadcasted_iota(i32, (tq, tq), 0)
    lane = lax.broadcasted_iota(i32, (1, LANES), 1)

    def head_body(h, carry):
        g = h // GROUP
        q = q_ref[0, h]
        cq = jnp.sum(jnp.where(lane == h, cq_ref[0], 0.0), axis=1, keepdims=True)
        _flash_init(m_sc, l_sc, acc_sc)

        def kv_body(j, c):
            off = pl.multiple_of(j * tq, tq)
            s = _dot_nt(q, k_ref[0, g, pl.ds(off, tq), :])
            ck = ckt_ref[0, pl.ds(h, 1), pl.ds(off, tq)]
            s = s + (cq - ck)
            s = jnp.where(d_iota <= (qi - j) * tq, s, NEG)
            _flash_step(s, v_ref[0, g, pl.ds(off, tq), :], m_sc, l_sc, acc_sc)
            return c

        lax.fori_loop(0, qi + 1, kv_body, 0)
        o_ref[0, h] = (acc_sc[...] / l_sc[...]).astype(o_ref.dtype)
        return carry

    lax.fori_loop(0, N_HEADS, head_body, 0)


def _fox_attn(q, k, v, c_pad, c_t, tq):
    b, nh, t, hd = q.shape
    kv_spec = pl.BlockSpec((1, N_KV_HEADS, t, hd), lambda bi, qi: (bi, 0, 0, 0))
    return pl.pallas_call(
        functools.partial(_fox_attn_kernel, tq=tq),
        grid=(b, t // tq),
        in_specs=[pl.BlockSpec((1, nh, tq, hd), lambda bi, qi: (bi, 0, qi, 0)),
                  kv_spec, kv_spec,
                  pl.BlockSpec((1, tq, LANES), lambda bi, qi: (bi, qi, 0)),
                  pl.BlockSpec((1, nh, t), lambda bi, qi: (bi, 0, 0))],
        out_specs=pl.BlockSpec((1, nh, tq, hd), lambda bi, qi: (bi, 0, qi, 0)),
        out_shape=jax.ShapeDtypeStruct((b, nh, t, hd), bf16),
        scratch_shapes=[pltpu.VMEM((tq, 1), f32), pltpu.VMEM((tq, 1), f32),
                        pltpu.VMEM((tq, hd), f32)],
        compiler_params=_cparams("parallel", "arbitrary"),
        name="fox_attn",
    )(q, k, v, c_pad, c_t)


def _head_rows_t(x):
    eye = (lax.broadcasted_iota(i32, (N_HEADS, x.shape[1]), 0)
           == lax.broadcasted_iota(i32, (N_HEADS, x.shape[1]), 1)).astype(bf16)
    return sum(_dot_nt(eye, p) for p in _split3(x))


def _cumsum_t_kernel(lf_ref, ct_ref, carry_sc, *, tb):
    @pl.when(pl.program_id(1) == 0)
    def _():
        carry_sc[...] = jnp.zeros(carry_sc.shape, f32)

    tri = (lax.broadcasted_iota(i32, (tb, tb), 0)
           <= lax.broadcasted_iota(i32, (tb, tb), 1)).astype(bf16)
    lft = _head_rows_t(lf_ref[0])
    csum = sum(_dot(p, tri) for p in _split3(lft))
    ct_ref[0] = csum + carry_sc[...]
    carry_sc[...] = carry_sc[...] + jnp.sum(lft, axis=1, keepdims=True)


def _cumsum_t(lf_pad, tb):
    b, t, _ = lf_pad.shape
    return pl.pallas_call(
        functools.partial(_cumsum_t_kernel, tb=tb),
        grid=(b, t // tb),
        in_specs=[pl.BlockSpec((1, tb, LANES), lambda bi, j: (bi, j, 0))],
        out_specs=pl.BlockSpec((1, N_HEADS, tb), lambda bi, j: (bi, 0, j)),
        out_shape=jax.ShapeDtypeStruct((b, N_HEADS, t), f32),
        scratch_shapes=[pltpu.VMEM((N_HEADS, 1), f32)],
        compiler_params=_cparams("parallel", "arbitrary"),
        name="cumsum_t",
    )(lf_pad)


def _kth_threshold(count_ge, rows, n_sel):
    zero = jnp.zeros((rows, 1), i32)
    ans = jnp.where(count_ge(zero) >= n_sel, zero, jnp.full((rows, 1), INT_MIN, i32))

    def bit_body(i, ans):
        cand = ans | jnp.left_shift(jnp.int32(1), 30 - i)
        return jnp.where(count_ge(cand) >= n_sel, cand, ans)

    return lax.fori_loop(0, 31, bit_body, ans)


def _tie_cut(count_tied_below, need, rows, n_bits):
    def body(i, p):
        cand = p | jnp.left_shift(jnp.int32(1), n_bits - 1 - i)
        return jnp.where(count_tied_below(cand) < need, cand, p)
    return lax.fori_loop(0, n_bits, body, jnp.zeros((rows, 1), i32))


def _fold_lanes(ind, c):
    for cc in range(ind.shape[1] // LANES):
        c = c + ind[:, cc * LANES:(cc + 1) * LANES]
    return c


def _dsa_attn_kernel(q_ref, k_ref, v_ref, iq_ref, ik_ref, iw_ref, o_ref,
                     key_sc, bias_sc, cut_sc, m_sc, l_sc, acc_sc, *, tq, n_sel, n_bits):
    qi = pl.program_id(1)
    n_k = qi + 1
    col = lax.broadcasted_iota(i32, (tq, tq), 1)
    d_iota = col - lax.broadcasted_iota(i32, (tq, tq), 0)
    iw = iw_ref[0]

    def score_body(j, c):
        off = pl.multiple_of(j * tq, tq)
        ikt = ik_ref[0, pl.ds(off, tq), :]
        s = jnp.zeros((tq, tq), f32)
        for h in range(N_IDX_HEADS):
            s = s + jnp.maximum(_dot_nt(iq_ref[0, h], ikt), 0.0) * iw[:, h:h + 1]
        key_sc[:, pl.ds(off, tq)] = jnp.where(d_iota <= (qi - j) * tq, _sort_key(s),
                                              KEY_NEG_INF)
        return c

    lax.fori_loop(0, n_k, score_body, 0)

    def count(pred):
        def body(j, c):
            off = pl.multiple_of(j * tq, tq)
            return _fold_lanes(jnp.where(pred(key_sc[:, pl.ds(off, tq)], j), 1, 0), c)
        c = lax.fori_loop(0, n_k, body, jnp.zeros((tq, LANES), i32))
        return jnp.sum(c, axis=1, keepdims=True)

    thr = _kth_threshold(lambda cand: count(lambda kt, j: kt >= cand), tq, n_sel)
    real = thr > KEY_NEG_INF
    n_ge = count(lambda kt, j: kt >= thr)
    cut_sc[...] = jnp.where(real, INT_MAX, -1)

    @pl.when(jnp.max(jnp.where(real, n_ge, 0)) > n_sel)
    def _():
        need = n_sel - count(lambda kt, j: kt > thr)
        cut = _tie_cut(lambda cand: count(lambda kt, j: (kt == thr) & (col + j * tq < cand)),
                       need, tq, n_bits)
        cut_sc[...] = jnp.where(real, cut, -1)

    cut = cut_sc[...]

    def bias_body(j, c):
        off = pl.multiple_of(j * tq, tq)
        kt = key_sc[:, pl.ds(off, tq)]
        sel = (kt > thr) | ((kt == thr) & (col + j * tq <= cut))
        bias_sc[:, pl.ds(off, tq)] = jnp.where(sel, 0.0, NEG)
        return c

    lax.fori_loop(0, n_k, bias_body, 0)

    def head_body(h, carry):
        g = h // GROUP
        q = q_ref[0, h]
        _flash_init(m_sc, l_sc, acc_sc)

        def kv_body(j, c):
            off = pl.multiple_of(j * tq, tq)
            s = _dot_nt(q, k_ref[0, g, pl.ds(off, tq), :]) + bias_sc[:, pl.ds(off, tq)]
            _flash_step(s, v_ref[0, g, pl.ds(off, tq), :], m_sc, l_sc, acc_sc)
            return c

        lax.fori_loop(0, n_k, kv_body, 0)
        o_ref[0, h] = (acc_sc[...] / l_sc[...]).astype(o_ref.dtype)
        return carry

    lax.fori_loop(0, N_HEADS, head_body, 0)


def _dsa_attn(q, k, v, iq, ik, iw, tq, n_sel):
    b, nh, t, hd = q.shape
    kv_spec = pl.BlockSpec((1, N_KV_HEADS, t, hd), lambda bi, qi: (bi, 0, 0, 0))
    return pl.pallas_call(
        functools.partial(_dsa_attn_kernel, tq=tq, n_sel=n_sel, n_bits=(t - 1).bit_length()),
        grid=(b, t // tq),
        in_specs=[pl.BlockSpec((1, nh, tq, hd), lambda bi, qi: (bi, 0, qi, 0)),
                  kv_spec, kv_spec,
                  pl.BlockSpec((1, N_IDX_HEADS, tq, IDX_DIM), lambda bi, qi: (bi, 0, qi, 0)),
                  pl.BlockSpec((1, t, IDX_DIM), lambda bi, qi: (bi, 0, 0)),
                  pl.BlockSpec((1, tq, LANES), lambda bi, qi: (bi, qi, 0))],
        out_specs=pl.BlockSpec((1, nh, tq, hd), lambda bi, qi: (bi, 0, qi, 0)),
        out_shape=jax.ShapeDtypeStruct((b, nh, t, hd), bf16),
        scratch_shapes=[pltpu.VMEM((tq, t), i32), pltpu.VMEM((tq, t), f32),
                        pltpu.VMEM((tq, 1), i32),
                        pltpu.VMEM((tq, 1), f32), pltpu.VMEM((tq, 1), f32),
                        pltpu.VMEM((tq, hd), f32)],
        compiler_params=_cparams("parallel", "arbitrary"),
        name="dsa_attn",
    )(q, k, v, iq, ik, iw)


def _out_proj_kernel(x_ref, o_ref, w_ref, gate_ref, y_ref):
    y_ref[...] = x_ref[...] + gate_ref[0] * _dot(o_ref[...], w_ref[...])


def _out_proj(x, o, w, gate, tm, rows_per_mod):
    m, d = x.shape
    kdim = o.shape[1]
    r = gate.shape[1]
    return pl.pallas_call(
        _out_proj_kernel,
        grid=(m // tm,),
        in_specs=[pl.BlockSpec((tm, d), lambda i: (i, 0)),
                  pl.BlockSpec((tm, kdim), lambda i: (i, 0)),
                  pl.BlockSpec((kdim, d), lambda i: (0, 0)),
                  pl.BlockSpec((1, r, d), lambda i: (i * tm // rows_per_mod, 0, 0))],
        out_specs=pl.BlockSpec((tm, d), lambda i: (i, 0)),
        out_shape=jax.ShapeDtypeStruct((m, d), f32),
        compiler_params=_cparams("parallel"),
        name="out_proj",
    )(x, o, w, gate)


FFN_CHUNK = 256


def _ffn_kernel(x_ref, xh_ref, g_ref, shift_ref, scale_ref, gate_ref, wa_ref, wb_ref,
                cw_ref, cb_ref, wd_ref, o_ref, tail_ref, acc_sc, *, tm, tiles_per_seq):
    i = pl.program_id(0)
    x = x_ref[...]
    h = _norm_mod(x, g_ref[...], shift_ref[0], scale_ref[0]).astype(bf16)
    hh = _norm_mod(xh_ref[...], g_ref[...], shift_ref[0], scale_ref[0]).astype(bf16)
    keep = jnp.where(i % tiles_per_seq == 0, 0.0, 1.0)
    row = lax.broadcasted_iota(i32, (tm, FFN_CHUNK), 0)
    for c in range(D_FF // FFN_CHUNK):
        sl = slice(c * FFN_CHUNK, (c + 1) * FFN_CHUNK)
        a = _dot(h, wa_ref[:, sl])
        b = _dot(h, wb_ref[:, sl])
        ah = _dot(hh, wa_ref[:, sl]) * keep
        last = ah[SUBLANES - 1:SUBLANES]
        p1 = jnp.where(row == 0, last, pltpu.roll(a, 1, 0))
        p2 = jnp.where(row == 0, ah[SUBLANES - 2:SUBLANES - 1],
                       jnp.where(row == 1, last, pltpu.roll(a, 2, 0)))
        conv = cb_ref[:, sl] + p2 * cw_ref[0:1, sl] + p1 * cw_ref[1:2, sl] + a * cw_ref[2:3, sl]
        y = _dot((_silu(conv) * b).astype(bf16), wd_ref[sl, :])
        if c == 0:
            acc_sc[...] = y
        else:
            acc_sc[...] += y
        tail_ref[0, :, sl] = a[tm - SUBLANES:tm]
    o_ref[...] = x + gate_ref[0] * acc_sc[...]


def _ffn(x, g, shift, scale, gate, wa, wb, cw, cb, wd, tm, rows_per_seq):
    m, d = x.shape
    n_tiles = m // tm
    hb = tm // SUBLANES
    mod_spec = pl.BlockSpec((1, 1, d), lambda i: (i * tm // rows_per_seq, 0, 0))
    const = lambda shape: pl.BlockSpec(shape, lambda i: (0,) * len(shape))
    return pl.pallas_call(
        functools.partial(_ffn_kernel, tm=tm, tiles_per_seq=rows_per_seq // tm),
        grid=(n_tiles,),
        in_specs=[pl.BlockSpec((tm, d), lambda i: (i, 0)),
                  pl.BlockSpec((SUBLANES, d), lambda i: (jnp.maximum(i * hb - 1, 0), 0)),
                  const((1, d)), mod_spec, mod_spec, mod_spec,
                  const((d, D_FF)), const((d, D_FF)), const((SUBLANES, D_FF)),
                  const((1, D_FF)), const((D_FF, d))],
        out_specs=[pl.BlockSpec((tm, d), lambda i: (i, 0)),
                   pl.BlockSpec((1, SUBLANES, D_FF), lambda i: (i, 0, 0))],
        out_shape=[jax.ShapeDtypeStruct((m, d), f32),
                   jax.ShapeDtypeStruct((n_tiles, SUBLANES, D_FF), f32)],
        scratch_shapes=[pltpu.VMEM((tm, d), f32)],
        compiler_params=_cparams("parallel"),
        name="ffn",
    )(x, x, g, shift, scale, gate, wa, wb, cw, cb, wd)


def _ffn_sample_kernel(x_ref, g_ref, shift_ref, scale_ref, gate_ref, p2_ref, p1_ref, wa_ref,
                       wb_ref, cw_ref, cb_ref, wd_ref, o_ref, a_ref, acc_sc):
    j = pl.program_id(0)

    @pl.when(j == 0)
    def _():
        acc_sc[...] = jnp.zeros(acc_sc.shape, f32)

    x = x_ref[...]
    h = _norm_mod(x, g_ref[...], shift_ref[0], scale_ref[0]).astype(bf16)
    a = _dot(h, wa_ref[...])
    b = _dot(h, wb_ref[...])
    conv = (cb_ref[...] + p2_ref[...] * cw_ref[0:1, :] + p1_ref[...] * cw_ref[1:2, :]
            + a * cw_ref[2:3, :])
    acc_sc[...] += _dot((_silu(conv) * b).astype(bf16), wd_ref[...])
    a_ref[...] = a

    @pl.when(j == pl.num_programs(0) - 1)
    def _():
        o_ref[...] = x + gate_ref[0] * acc_sc[...]


def _ffn_sample(x, g, shift, scale, gate, p2, p1, wa, wb, cw, cb, wd, tn):
    m, d = x.shape
    const = lambda shape: pl.BlockSpec(shape, lambda j: (0,) * len(shape))
    return pl.pallas_call(
        _ffn_sample_kernel,
        grid=(D_FF // tn,),
        in_specs=[const((m, d)), const((1, d)), const((1, m, d)), const((1, m, d)),
                  const((1, m, d)),
                  pl.BlockSpec((m, tn), lambda j: (0, j)), pl.BlockSpec((m, tn), lambda j: (0, j)),
                  pl.BlockSpec((d, tn), lambda j: (0, j)), pl.BlockSpec((d, tn), lambda j: (0, j)),
                  pl.BlockSpec((SUBLANES, tn), lambda j: (0, j)),
                  pl.BlockSpec((1, tn), lambda j: (0, j)),
                  pl.BlockSpec((tn, d), lambda j: (j, 0))],
        out_specs=[const((m, d)), pl.BlockSpec((m, tn), lambda j: (0, j))],
        out_shape=[jax.ShapeDtypeStruct((m, d), f32), jax.ShapeDtypeStruct((m, D_FF), f32)],
        scratch_shapes=[pltpu.VMEM((m, d), f32)],
        compiler_params=_cparams("arbitrary"),
        name="ffn_sample",
    )(x, g, shift, scale, gate, p2, p1, wa, wb, cw, cb, wd)


def _final_norm_kernel(x_ref, g_ref, o_ref):
    x = x_ref[...]
    o_ref[...] = x * lax.rsqrt(jnp.mean(x * x, axis=-1, keepdims=True) + EPS) * g_ref[...]


def _final_norm(x, g, tm):
    m, d = x.shape
    return pl.pallas_call(
        _final_norm_kernel,
        grid=(m // tm,),
        in_specs=[pl.BlockSpec((tm, d), lambda i: (i, 0)), pl.BlockSpec((1, d), lambda i: (0, 0))],
        out_specs=pl.BlockSpec((tm, d), lambda i: (i, 0)),
        out_shape=jax.ShapeDtypeStruct((m, d), f32),
        compiler_params=_cparams("parallel"),
        name="final_norm",
    )(x, g)


def _sample_score_kernel(pt_ref, iq_ref, iw_ref, kidx_ref, s_ref):
    d = _dot_nt(iq_ref[0], kidx_ref[0, 0].astype(bf16))
    s_ref[0, 0] = jnp.sum(jnp.maximum(d, 0.0) * iw_ref[0], axis=0, keepdims=True)


def _sample_scores(page_table, iq, iw, cache_kidx, layer):
    s, n_pages = page_table.shape
    grid_spec = pltpu.PrefetchScalarGridSpec(
        num_scalar_prefetch=1, grid=(s, n_pages),
        in_specs=[pl.BlockSpec((1, N_IDX_HEADS, IDX_DIM), lambda b, p, pt: (b, 0, 0)),
                  pl.BlockSpec((1, N_IDX_HEADS, 1), lambda b, p, pt: (b, 0, 0)),
                  pl.BlockSpec((1, 1, PAGE_SIZE, IDX_DIM), lambda b, p, pt: (layer, pt[b, p], 0, 0))],
        out_specs=pl.BlockSpec((1, 1, 1, PAGE_SIZE), lambda b, p, pt: (b, p, 0, 0)))
    out = pl.pallas_call(
        _sample_score_kernel, grid_spec=grid_spec,
        out_shape=jax.ShapeDtypeStruct((s, n_pages, 1, PAGE_SIZE), f32),
        compiler_params=_cparams("parallel", "arbitrary"),
        name="sample_scores",
    )(page_table, iq, iw, cache_kidx)
    return out.reshape(s, n_pages * PAGE_SIZE)


SEL_TILE = 1024


def _sample_select_kernel(s_ref, iq_ref, ikt_ref, iw_ref, bias_ref, bnew_ref, key_sc,
                          *, n_sel, n_bits):
    rows, length = s_ref.shape
    n_tiles = length // SEL_TILE
    prod = iq_ref[...].astype(f32) * ikt_ref[...].astype(f32)
    grp = (lax.broadcasted_iota(i32, (prod.shape[1], LANES), 0) // IDX_DIM
           == lax.broadcasted_iota(i32, (prod.shape[1], LANES), 1)).astype(bf16)
    d_new = sum(_dot(p, grp) for p in _split3(prod))
    s_new = jnp.sum(jnp.maximum(d_new, 0.0) * iw_ref[...], axis=1, keepdims=True)
    key_new = _sort_key(s_new)
    key_sc[...] = _sort_key(s_ref[...])
    col = lax.broadcasted_iota(i32, (rows, SEL_TILE), 1)

    def count(pred, pred_new):
        def body(j, c):
            off = pl.multiple_of(j * SEL_TILE, SEL_TILE)
            return _fold_lanes(jnp.where(pred(key_sc[:, pl.ds(off, SEL_TILE)], j), 1, 0), c)
        c = lax.fori_loop(0, n_tiles, body, jnp.zeros((rows, LANES), i32))
        return jnp.sum(c, axis=1, keepdims=True) + jnp.where(pred_new, 1, 0)

    thr = _kth_threshold(lambda cand: count(lambda kt, j: kt >= cand, key_new >= cand),
                         rows, n_sel)
    need = n_sel - count(lambda kt, j: kt > thr, key_new > thr)
    cut = _tie_cut(lambda cand: count(lambda kt, j: (kt == thr) & (col + j * SEL_TILE < cand),
                                      (key_new == thr) & (length < cand)),
                   need, rows, n_bits)

    def bias_body(j, c):
        off = pl.multiple_of(j * SEL_TILE, SEL_TILE)
        kt = key_sc[:, pl.ds(off, SEL_TILE)]
        sel = (kt > thr) | ((kt == thr) & (col + j * SEL_TILE <= cut))
        bias_ref[:, pl.ds(off, SEL_TILE)] = jnp.where(sel, 0.0, NEG)
        return c

    lax.fori_loop(0, n_tiles, bias_body, 0)
    sel_new = (key_new > thr) | ((key_new == thr) & (length <= cut))
    bnew_ref[...] = jnp.broadcast_to(jnp.where(sel_new, 0.0, NEG), bnew_ref.shape)


def _sample_select(scores, iq, ik_tiled, iw, n_sel):
    rows, length = scores.shape
    return pl.pallas_call(
        functools.partial(_sample_select_kernel, n_sel=n_sel, n_bits=length.bit_length()),
        out_shape=[jax.ShapeDtypeStruct((rows, length), f32),
                   jax.ShapeDtypeStruct((rows, LANES), f32)],
        scratch_shapes=[pltpu.VMEM((rows, length), i32)],
        compiler_params=pltpu.CompilerParams(vmem_limit_bytes=VMEM_LIMIT),
        name="sample_select",
    )(scores, iq, ik_tiled, iw)


def _suffix_bias_kernel(pt_ref, lfnew_ref, lf_ref, bias_ref, carry_sc):
    @pl.when(pl.program_id(1) == 0)
    def _():
        carry_sc[...] = lfnew_ref[0]

    tri = (lax.broadcasted_iota(i32, (PAGE_SIZE, PAGE_SIZE), 0)
           > lax.broadcasted_iota(i32, (PAGE_SIZE, PAGE_SIZE), 1)).astype(bf16)
    lft = _head_rows_t(lf_ref[0, 0])
    suffix = sum(_dot(p, tri) for p in _split3(lft))
    bias_ref[0] = suffix + carry_sc[...]
    carry_sc[...] = carry_sc[...] + jnp.sum(lft, axis=1, keepdims=True)


def _suffix_bias(page_table, lf_new, cache_logf, layer):
    s, n_pages = page_table.shape
    last = n_pages - 1
    grid_spec = pltpu.PrefetchScalarGridSpec(
        num_scalar_prefetch=1, grid=(s, n_pages),
        in_specs=[pl.BlockSpec((1, N_HEADS, 1), lambda b, p, pt: (b, 0, 0)),
                  pl.BlockSpec((1, 1, PAGE_SIZE, N_HEADS),
                               lambda b, p, pt: (layer, pt[b, last - p], 0, 0))],
        out_specs=pl.BlockSpec((1, N_HEADS, PAGE_SIZE), lambda b, p, pt: (b, 0, last - p)),
        scratch_shapes=[pltpu.VMEM((N_HEADS, 1), f32)])
    return pl.pallas_call(
        _suffix_bias_kernel, grid_spec=grid_spec,
        out_shape=jax.ShapeDtypeStruct((s, N_HEADS, n_pages * PAGE_SIZE), f32),
        compiler_params=_cparams("parallel", "arbitrary"),
        name="suffix_bias",
    )(page_table, lf_new, cache_logf)


def _decode_attn_kernel(pt_ref, q_ref, knew_ref, vnew_ref, bnew_ref, bias_ref, k_ref, v_ref,
                        o_ref, m_sc, l_sc, acc_sc):
    p = pl.program_id(1)
    q = q_ref[0]

    @pl.when(p == 0)
    def _():
        kn = knew_ref[0].astype(bf16).astype(f32)
        m_sc[...] = jnp.sum(q.astype(f32) * kn, axis=1, keepdims=True) + bnew_ref[0][:, 0:1]
        l_sc[...] = jnp.ones(l_sc.shape, f32)
        acc_sc[...] = jnp.broadcast_to(vnew_ref[0].astype(bf16).astype(f32), acc_sc.shape)

    s = _dot_nt(q, k_ref[0, 0].astype(bf16)) + bias_ref[0]
    _flash_step(s, v_ref[0, 0].astype(bf16), m_sc, l_sc, acc_sc)

    @pl.when(p == pl.num_programs(1) - 1)
    def _():
        o_ref[0] = acc_sc[...] / l_sc[...]


def _decode_attn(page_table, qmat, k_new, v_new, bias_new, bias, cache_k, cache_v, layer):
    s, n_pages = page_table.shape
    hb = bias.shape[1]
    row = lambda shape: pl.BlockSpec(shape, lambda b, p, pt: (b, 0, 0))
    page = pl.BlockSpec((1, 1, PAGE_SIZE, KV_COLS), lambda b, p, pt: (layer, pt[b, p], 0, 0))
    grid_spec = pltpu.PrefetchScalarGridSpec(
        num_scalar_prefetch=1, grid=(s, n_pages),
        in_specs=[row((1, N_HEADS, KV_COLS)), row((1, 1, KV_COLS)), row((1, 1, KV_COLS)),
                  row((1, hb, LANES)),
                  pl.BlockSpec((1, hb, PAGE_SIZE), lambda b, p, pt: (b, 0, p)),
                  page, page],
        out_specs=row((1, N_HEADS, KV_COLS)),
        scratch_shapes=[pltpu.VMEM((N_HEADS, 1), f32), pltpu.VMEM((N_HEADS, 1), f32),
                        pltpu.VMEM((N_HEADS, KV_COLS), f32)])
    return pl.pallas_call(
        _decode_attn_kernel, grid_spec=grid_spec,
        out_shape=jax.ShapeDtypeStruct((s, N_HEADS, KV_COLS), f32),
        compiler_params=_cparams("parallel", "arbitrary"),
        name="decode_attn",
    )(page_table, qmat, k_new, v_new, bias_new, bias, cache_k, cache_v)


def _head_major(x2d, b, t, n):
    return x2d.reshape(b, t, n, x2d.shape[1] // n).transpose(0, 2, 1, 3)


def _pad_cols(w, n):
    return jnp.pad(w, ((0, 0), (0, n)))


def _decode_q(q_bf):
    s = q_bf.shape[0]
    onehot = (jnp.arange(N_HEADS)[:, None] // GROUP == jnp.arange(N_KV_HEADS)[None, :])
    q = q_bf.reshape(s, N_HEADS, 1, HEAD_DIM) * onehot[None, :, :, None].astype(q_bf.dtype)
    return q.reshape(s, N_HEADS, KV_COLS)


def _decode_o(o_full):
    s = o_full.shape[0]
    o = o_full.reshape(s, N_KV_HEADS, GROUP, N_KV_HEADS, HEAD_DIM)
    return jnp.stack([o[:, g, :, g, :] for g in range(N_KV_HEADS)], axis=1).reshape(s, -1)


def kernel(x_prompt, x_sample, cache_k_a, cache_v_a, cache_kidx_a, cache_k_b, cache_v_b,
           cache_logf_b, state_conv, page_table, c_prompt, c_sample, w_ada, b_ada, g_mix,
           g_ffn, w_in_a, w_o_a, w_in_b, b_f, w_o_b, w_up, conv_w, conv_b, w_down, g_final):
    bsz, t, d = x_prompt.shape
    s = x_sample.shape[0]
    depth = w_ada.shape[0]
    m = bsz * t
    past_len = page_table.shape[1] * PAGE_SIZE
    n_pool = cache_k_a.shape[1]
    tm = 512
    tq = 256

    pad_rows = (-(bsz + s)) % SUBLANES
    c_all = jnp.concatenate([c_prompt, c_sample, jnp.zeros((pad_rows, d), f32)], axis=0)
    mods = _adaln(c_all, w_ada, b_ada)
    mods_p = mods[:, :bsz].reshape(depth, bsz, 6, d)
    mods_s = mods[:, bsz:bsz + s].reshape(depth, s, 6, d)

    tabs_p = _rope_tables(jnp.arange(t))
    tabs_s = _rope_tables(jnp.full((s,), past_len))
    zero_bf = jnp.zeros((1, LANES), f32)

    xp = x_prompt.reshape(m, d)
    xs = x_sample.reshape(s, d)
    kv_shape_p = (bsz, t, N_KV_HEADS, HEAD_DIM)
    kv_shape_s = (s, 1, N_KV_HEADS, HEAD_DIM)
    ka_p, va_p, ia_p, kb_p, vb_p, lb_p, cv_p = [], [], [], [], [], [], []
    ka_s, va_s, ia_s, kb_s, vb_s, lb_s, cv_s = [], [], [], [], [], [], []

    cache_k_a = cache_k_a.reshape(cache_k_a.shape[:3] + (KV_COLS,))
    cache_v_a = cache_v_a.reshape(cache_v_a.shape[:3] + (KV_COLS,))
    cache_k_b = cache_k_b.reshape(cache_k_b.shape[:3] + (KV_COLS,))
    cache_v_b = cache_v_b.reshape(cache_v_b.shape[:3] + (KV_COLS,))

    for i in range(depth):
        j = i // 2
        mp = [mods_p[i, :, c][:, None, :] for c in range(6)]
        ms = [mods_s[i, :, c][None] for c in range(6)]
        g_m = g_mix[i][None]
        if i % 2 == 0:
            w = w_in_a[j]
            n_qkvi = QKV_COLS + N_IDX_HEADS * IDX_DIM + IDX_DIM
            w_in = jnp.concatenate([_pad_cols(w[:, :n_qkvi], LANES - IDX_DIM),
                                    _pad_cols(w[:, n_qkvi:], LANES - N_IDX_HEADS)],
                                   axis=1).astype(bf16)
            w_o = w_o_a[j].astype(bf16)
            q, k32, kb, v32, vb, iq, ik32, ikb, iw = _in_proj(
                xp, g_m, mp[0], mp[1], w_in, tabs_p, zero_bf, SEGS_A, OUTS_A, tm, t)
            o = _dsa_attn(_head_major(q, bsz, t, N_HEADS), _head_major(kb, bsz, t, N_KV_HEADS),
                          _head_major(vb, bsz, t, N_KV_HEADS),
                          _head_major(iq, bsz, t, N_IDX_HEADS),
                          ikb[:, :IDX_DIM].reshape(bsz, t, IDX_DIM), iw.reshape(bsz, t, LANES),
                          tq, min(TOPK_MAX, t // 4))
            xp = _out_proj(xp, o.transpose(0, 2, 1, 3).reshape(m, N_HEADS * HEAD_DIM), w_o,
                           mp[2], tm, t)
            ka_p.append(k32.reshape(kv_shape_p))
            va_p.append(v32.reshape(kv_shape_p))
            ia_p.append(ik32[:, :IDX_DIM].reshape(bsz, t, IDX_DIM))
            q, k32, kb, v32, vb, iq, ik32, ikb, iw = _in_proj(
                xs, g_m, ms[0], ms[1], w_in, tabs_s, zero_bf, SEGS_A, OUTS_A, s, s)
            scores = _sample_scores(page_table, iq.reshape(s, N_IDX_HEADS, IDX_DIM),
                                    iw[:, :N_IDX_HEADS].reshape(s, N_IDX_HEADS, 1),
                                    cache_kidx_a, j)
            bias, bias_new = _sample_select(scores, iq, jnp.tile(ikb[:, :IDX_DIM], (1, N_IDX_HEADS)),
                                            iw, min(TOPK_MAX, (past_len + 1) // 4))
            o_full = _decode_attn(page_table, _decode_q(q), k32[:, None], v32[:, None],
                                  bias_new[:, None], bias[:, None], cache_k_a, cache_v_a, j)
            xs = _out_proj(xs, _decode_o(o_full).astype(bf16), w_o, ms[2], s, s)
            ka_s.append(k32.reshape(kv_shape_s))
            va_s.append(v32.reshape(kv_shape_s))
            ia_s.append(ik32[:, :IDX_DIM].reshape(s, 1, IDX_DIM))
        else:
            w_in = _pad_cols(w_in_b[j], LANES - N_HEADS).astype(bf16)
            w_o = w_o_b[j].astype(bf16)
            bf_pad = _pad_cols(b_f[j][None], LANES - N_HEADS)
            q, k32, kb, v32, vb, lf = _in_proj(
                xp, g_m, mp[0], mp[1], w_in, tabs_p, bf_pad, SEGS_B, OUTS_B, tm, t)
            c_t = _cumsum_t(lf.reshape(bsz, t, LANES), tq)
            c_pad = jnp.pad(c_t.transpose(0, 2, 1), ((0, 0), (0, 0), (0, LANES - N_HEADS)))
            o = _fox_attn(_head_major(q, bsz, t, N_HEADS), _head_major(kb, bsz, t, N_KV_HEADS),
                          _head_major(vb, bsz, t, N_KV_HEADS), c_pad, c_t, tq)
            xp = _out_proj(xp, o.transpose(0, 2, 1, 3).reshape(m, N_HEADS * HEAD_DIM), w_o,
                           mp[2], tm, t)
            kb_p.append(k32.reshape(kv_shape_p))
            vb_p.append(v32.reshape(kv_shape_p))
            lb_p.append(lf[:, :N_HEADS].reshape(bsz, t, N_HEADS))
            q, k32, kb, v32, vb, lf = _in_proj(
                xs, g_m, ms[0], ms[1], w_in, tabs_s, bf_pad, SEGS_B, OUTS_B, s, s)
            bias = _suffix_bias(page_table, lf[:, :N_HEADS, None], cache_logf_b, j)
            o_full = _decode_attn(page_table, _decode_q(q), k32[:, None], v32[:, None],
                                  jnp.zeros((s, N_HEADS, LANES), f32), bias,
                                  cache_k_b, cache_v_b, j)
            xs = _out_proj(xs, _decode_o(o_full).astype(bf16), w_o, ms[2], s, s)
            kb_s.append(k32.reshape(kv_shape_s))
            vb_s.append(v32.reshape(kv_shape_s))
            lb_s.append(lf[:, :N_HEADS].reshape(s, 1, N_HEADS))

        wa = w_up[i][:, :D_FF].astype(bf16)
        wb = w_up[i][:, D_FF:].astype(bf16)
        wd = w_down[i].astype(bf16)
        cw = jnp.pad(conv_w[i], ((0, SUBLANES - CONV_W), (0, 0)))
        cb = conv_b[i][None]
        g_f = g_ffn[i][None]
        xp, tail = _ffn(xp, g_f, mp[3], mp[4], mp[5], wa, wb, cw, cb, wd, tm, t)
        cv_p.append(tail.reshape(bsz, t // tm, SUBLANES, D_FF)[:, -1, SUBLANES - (CONV_W - 1):])
        xs, a_s = _ffn_sample(xs, g_f, ms[3], ms[4], ms[5], state_conv[i][:, 0],
                              state_conv[i][:, 1], wa, wb, cw, cb, wd, D_FF // 2)
        cv_s.append(jnp.stack([state_conv[i][:, 1], a_s], axis=1))

    y_prompt = _final_norm(xp, g_final[None], tm).reshape(bsz, t, d)
    y_sample = _final_norm(xs, g_final[None], s).reshape(s, 1, d)
    return (y_prompt, y_sample,
            jnp.stack(ka_p), jnp.stack(va_p), jnp.stack(ia_p),
            jnp.stack(kb_p), jnp.stack(vb_p), jnp.stack(lb_p), jnp.stack(cv_p),
            jnp.stack(ka_s), jnp.stack(va_s), jnp.stack(ia_s),
            jnp.stack(kb_s), jnp.stack(vb_s), jnp.stack(lb_s), jnp.stack(cv_s))
```

```python
import functools

import jax
import jax.numpy as jnp
from jax import lax
from jax.experimental import pallas as pl
from jax.experimental.pallas import tpu as pltpu

f32 = jnp.float32
bf16 = jnp.bfloat16
i32 = jnp.int32

D_MODEL = 1024
N_HEADS = 16
HEAD_DIM = 64
N_KV_HEADS = 4
GROUP = N_HEADS // N_KV_HEADS
KV_COLS = N_KV_HEADS * HEAD_DIM
QKV_COLS = (N_HEADS + 2 * N_KV_HEADS) * HEAD_DIM
ROPE_DIMS = HEAD_DIM // 4
ROPE_HALF = ROPE_DIMS // 2
ROPE_THETA = 500000.0
N_IDX_HEADS = 8
IDX_DIM = 64
TOPK_MAX = 256
D_FF = 2816
CONV_W = 3
EPS = 1e-6
PAGE_SIZE = 128
Q_SCALE = HEAD_DIM ** -0.5

LANES = 128
SUBLANES = 8
VMEM_LIMIT = 56 * 1024 * 1024
NEG = -1e30
INT_MIN = -2 ** 31
INT_MAX = 2 ** 31 - 1
KEY_NEG_INF = (0xFF800000 ^ 0x7FFFFFFF) - 2 ** 32


def _cparams(*sem):
    return pltpu.CompilerParams(dimension_semantics=sem, vmem_limit_bytes=VMEM_LIMIT)


def _dot(a, b):
    return jnp.dot(a, b, preferred_element_type=f32)


def _dot_nt(a, b):
    return lax.dot_general(a, b, (((1,), (1,)), ((), ())), preferred_element_type=f32)


def _split3(x):
    hi = x.astype(bf16)
    r1 = x - hi.astype(f32)
    mid = r1.astype(bf16)
    lo = (r1 - mid.astype(f32)).astype(bf16)
    return hi, mid, lo


def _norm_mod(x, g, shift, scale):
    xn = x * lax.rsqrt(jnp.mean(x * x, axis=-1, keepdims=True) + EPS)
    return (xn * g) * (1.0 + scale) + shift


def _silu(x):
    return x * jax.nn.sigmoid(x)


def _sort_key(s):
    bits = pltpu.bitcast(s, i32)
    return bits ^ ((bits >> 31) & 0x7FFFFFFF)


def _adaln_kernel(c_ref, w_ref, b_ref, o_ref):
    a = _silu(c_ref[...]).astype(bf16)
    o_ref[0] = _dot(a, w_ref[0].astype(bf16)) + b_ref[0]


def _adaln(c_all, w_ada, b_ada):
    depth, d, n = w_ada.shape
    rows = c_all.shape[0]
    tn = 1536
    return pl.pallas_call(
        _adaln_kernel,
        grid=(depth, n // tn),
        in_specs=[pl.BlockSpec((rows, d), lambda l, j: (0, 0)),
                  pl.BlockSpec((1, d, tn), lambda l, j: (l, 0, j)),
                  pl.BlockSpec((1, 1, tn), lambda l, j: (l, 0, j))],
        out_specs=pl.BlockSpec((1, rows, tn), lambda l, j: (l, 0, j)),
        out_shape=jax.ShapeDtypeStruct((depth, rows, n), f32),
        compiler_params=_cparams("parallel", "parallel"),
        name="adaln",
    )(c_all, w_ada, b_ada.reshape(depth, 1, n))


SEGS_A = ((0, 1024, "rope", ((0, Q_SCALE),)),
          (1024, 256, "rope", ((1, 1.0), (2, 1.0))),
          (1280, 256, None, ((3, 1.0), (4, 1.0))),
          (1536, 512, "rope", ((5, 1.0),)),
          (2048, 128, "rope", ((6, 1.0), (7, 1.0))),
          (2176, 128, None, ((8, 1.0),)))
OUTS_A = ((1024, bf16), (256, f32), (256, bf16), (256, f32), (256, bf16),
          (512, bf16), (128, f32), (128, bf16), (128, f32))
SEGS_B = ((0, 1024, None, ((0, Q_SCALE),)),
          (1024, 256, None, ((1, 1.0), (2, 1.0))),
          (1280, 256, None, ((3, 1.0), (4, 1.0))),
          (1536, 128, "logf", ((5, 1.0),)))
OUTS_B = ((1024, bf16), (256, f32), (256, bf16), (256, f32), (256, bf16), (128, f32))


def _in_proj_kernel(x_ref, g_ref, shift_ref, scale_ref, w_ref, cos_ref, sa_ref, sb_ref,
                    bf_ref, *out_refs, segs):
    h = _norm_mod(x_ref[...], g_ref[...], shift_ref[0], scale_ref[0]).astype(bf16)
    for c0, width, epi, outs in segs:
        for c in range(width // LANES):
            lo = c0 + c * LANES
            a = _dot(h, w_ref[:, lo:lo + LANES])
            if epi == "rope":
                a = (a * cos_ref[...] + pltpu.roll(a, ROPE_HALF, 1) * sa_ref[...]
                     + pltpu.roll(a, LANES - ROPE_HALF, 1) * sb_ref[...])
            elif epi == "logf":
                z = a + bf_ref[...]
                a = jnp.minimum(z, 0.0) - jnp.log(1.0 + jnp.exp(-jnp.abs(z)))
            for oi, sc in outs:
                val = a if sc == 1.0 else a * sc
                out_refs[oi][:, c * LANES:(c + 1) * LANES] = val.astype(out_refs[oi].dtype)


def _in_proj(x, g, shift, scale, w, tabs, b_f, segs, outs, tm, rows_per_mod):
    m, d = x.shape
    n = w.shape[1]
    r = shift.shape[1]
    cos, sa, sb = tabs
    tab_blocks = cos.shape[0] // tm
    mod_spec = pl.BlockSpec((1, r, d), lambda i: (i * tm // rows_per_mod, 0, 0))
    tab_spec = pl.BlockSpec((tm, LANES), lambda i: (i % tab_blocks, 0))
    return pl.pallas_call(
        functools.partial(_in_proj_kernel, segs=segs),
        grid=(m // tm,),
        in_specs=[pl.BlockSpec((tm, d), lambda i: (i, 0)),
                  pl.BlockSpec((1, d), lambda i: (0, 0)),
                  mod_spec, mod_spec,
                  pl.BlockSpec((d, n), lambda i: (0, 0)),
                  tab_spec, tab_spec, tab_spec,
                  pl.BlockSpec((1, LANES), lambda i: (0, 0))],
        out_specs=[pl.BlockSpec((tm, wd), lambda i: (i, 0)) for wd, _ in outs],
        out_shape=[jax.ShapeDtypeStruct((m, wd), dt) for wd, dt in outs],
        compiler_params=_cparams("parallel"),
        name="in_proj",
    )(x, g, shift, scale, w, cos, sa, sb, b_f)


def _rope_tables(pos):
    inv_freq = ROPE_THETA ** (-jnp.arange(ROPE_HALF, dtype=f32) / ROPE_HALF)
    ang = pos.astype(f32)[:, None] * inv_freq[None, :]
    cos, sin = jnp.cos(ang), jnp.sin(ang)
    t = pos.shape[0]
    ones = jnp.ones((t, HEAD_DIM - ROPE_DIMS), f32)
    zeros = jnp.zeros((t, HEAD_DIM - ROPE_DIMS), f32)
    zh = jnp.zeros((t, ROPE_HALF), f32)
    c = jnp.concatenate([cos, cos, ones], axis=1)
    sa = jnp.concatenate([zh, sin, zeros], axis=1)
    sb = jnp.concatenate([-sin, zh, zeros], axis=1)
    rep = LANES // HEAD_DIM
    return tuple(jnp.tile(a, (1, rep)) for a in (c, sa, sb))


WT_Q = 0
WT_V = N_HEADS * HEAD_DIM
WT_IQ = WT_V + KV_COLS
WT_IW = WT_IQ + N_IDX_HEADS * IDX_DIM
IW_ROWS = 2 * SUBLANES


def _rope_rows(a, cos_t, sin_t):
    parts = []
    for b0 in range(0, a.shape[0], HEAD_DIM):
        x1 = a[b0:b0 + ROPE_HALF]
        x2 = a[b0 + ROPE_HALF:b0 + ROPE_DIMS]
        parts += [x1 * cos_t - x2 * sin_t, x1 * sin_t + x2 * cos_t, a[b0 + ROPE_DIMS:b0 + HEAD_DIM]]
    return jnp.concatenate(parts, axis=0)


def _in_proj_t_kernel(x_ref, g_ref, shift_ref, scale_ref, w_ref, wt_ref, cos_ref, sa_ref, sb_ref,
                      cost_ref, sint_ref, bf_ref, *outs, is_a):
    if is_a:
        qt_ref, vt_ref, kpad_ref, k32_ref, v32_ref, iqt_ref, iwt_ref, ikb_ref, ik32_ref = outs
    else:
        qt_ref, vt_ref, kpad_ref, k32_ref, v32_ref, lf_ref = outs
    h = _norm_mod(x_ref[...], g_ref[...], shift_ref[0], scale_ref[0]).astype(bf16)
    cos_t, sin_t = cost_ref[...], sint_ref[...]

    def rows(r0):
        return _dot_nt(wt_ref[r0:r0 + LANES, :], h)

    def rope_cols(a):
        return (a * cos_ref[...] + pltpu.roll(a, ROPE_HALF, 1) * sa_ref[...]
                + pltpu.roll(a, LANES - ROPE_HALF, 1) * sb_ref[...])

    for c in range(N_HEADS * HEAD_DIM // LANES):
        a = rows(WT_Q + c * LANES)
        if is_a:
            a = _rope_rows(a, cos_t, sin_t)
        qt_ref[c * LANES:(c + 1) * LANES, :] = (a * Q_SCALE).astype(bf16)
    for c in range(KV_COLS // LANES):
        vt_ref[c * LANES:(c + 1) * LANES, :] = rows(WT_V + c * LANES).astype(bf16)
    if is_a:
        for c in range(N_IDX_HEADS * IDX_DIM // LANES):
            a = _rope_rows(rows(WT_IQ + c * LANES), cos_t, sin_t)
            iqt_ref[c * LANES:(c + 1) * LANES, :] = a.astype(bf16)
        iwt_ref[...] = _dot_nt(wt_ref[WT_IW:WT_IW + IW_ROWS, :], h)

    lane = lax.broadcasted_iota(i32, (1, LANES), 1)
    for c in range(KV_COLS // LANES):
        a = _dot(h, w_ref[:, c * LANES:(c + 1) * LANES])
        if is_a:
            a = rope_cols(a)
        k32_ref[:, c * LANES:(c + 1) * LANES] = a
        kpad_ref[0, 2 * c] = jnp.where(lane < HEAD_DIM, a, 0.0).astype(bf16)
        kpad_ref[0, 2 * c + 1] = jnp.where(lane < HEAD_DIM, pltpu.roll(a, HEAD_DIM, 1),
                                           0.0).astype(bf16)
    for c in range(KV_COLS // LANES):
        lo = KV_COLS + c * LANES
        v32_ref[:, c * LANES:(c + 1) * LANES] = _dot(h, w_ref[:, lo:lo + LANES])
    a = _dot(h, w_ref[:, 2 * KV_COLS:2 * KV_COLS + LANES])
    if is_a:
        a = rope_cols(a)
        ik32_ref[...] = a
        ikb_ref[...] = a[:, :IDX_DIM].astype(bf16)
    else:
        z = a + bf_ref[...]
        lf_ref[...] = jnp.minimum(z, 0.0) - jnp.log(1.0 + jnp.exp(-jnp.abs(z)))


def _in_proj_t(x, g, shift, scale, w, wt, tabs, tabs_t, b_f, is_a, tm, t):
    m, d = x.shape
    bsz = m // t
    tpb = t // tm
    col = lambda rows_: pl.BlockSpec((rows_, tm), lambda i: (0, i))
    row = lambda cols_: pl.BlockSpec((tm, cols_), lambda i: (i, 0))
    const = lambda a: pl.BlockSpec(a.shape, lambda i: (0,) * a.ndim)
    mod_spec = pl.BlockSpec((1, 1, d), lambda i: (i // tpb, 0, 0))
    tab_spec = pl.BlockSpec((tm, LANES), lambda i: (i % tpb, 0))
    tabt_spec = pl.BlockSpec((ROPE_HALF, tm), lambda i: (0, i % tpb))
    nq = N_HEADS * HEAD_DIM
    out_specs = [col(nq), col(KV_COLS),
                 pl.BlockSpec((1, N_KV_HEADS, tm, LANES), lambda i: (i // tpb, 0, i % tpb, 0)),
                 row(KV_COLS), row(KV_COLS)]
    out_shape = [jax.ShapeDtypeStruct((nq, m), bf16), jax.ShapeDtypeStruct((KV_COLS, m), bf16),
                 jax.ShapeDtypeStruct((bsz, N_KV_HEADS, t, LANES), bf16),
                 jax.ShapeDtypeStruct((m, KV_COLS), f32), jax.ShapeDtypeStruct((m, KV_COLS), f32)]
    if is_a:
        out_specs += [col(N_IDX_HEADS * IDX_DIM), col(IW_ROWS), row(IDX_DIM), row(LANES)]
        out_shape += [jax.ShapeDtypeStruct((N_IDX_HEADS * IDX_DIM, m), bf16),
                      jax.ShapeDtypeStruct((IW_ROWS, m), f32),
                      jax.ShapeDtypeStruct((m, IDX_DIM), bf16),
                      jax.ShapeDtypeStruct((m, LANES), f32)]
    else:
        out_specs += [row(LANES)]
        out_shape += [jax.ShapeDtypeStruct((m, LANES), f32)]
    return pl.pallas_call(
        functools.partial(_in_proj_t_kernel, is_a=is_a),
        grid=(m // tm,),
        in_specs=[row(d), const(g), mod_spec, mod_spec, const(w), const(wt),
                  tab_spec, tab_spec, tab_spec, tabt_spec, tabt_spec, const(b_f)],
        out_specs=out_specs, out_shape=out_shape,
        compiler_params=_cparams("parallel"),
        name="in_proj_t",
    )(x, g, shift, scale, w, wt, *tabs, *tabs_t, b_f)


def _rope_tables_t(pos):
    inv_freq = ROPE_THETA ** (-jnp.arange(ROPE_HALF, dtype=f32) / ROPE_HALF)
    ang = inv_freq[:, None] * pos.astype(f32)[None, :]
    return jnp.cos(ang), jnp.sin(ang)


def _flash_init(m_sc, l_sc, acc_sc):
    m_sc[...] = jnp.full(m_sc.shape, NEG, f32)
    l_sc[...] = jnp.zeros(l_sc.shape, f32)
    acc_sc[...] = jnp.zeros(acc_sc.shape, f32)


def _flash_step(s, v, m_sc, l_sc, acc_sc):
    m_old = m_sc[...]
    m_new = jnp.maximum(m_old, jnp.max(s, axis=1, keepdims=True))
    p = jnp.exp(s - m_new)
    alpha = jnp.exp(m_old - m_new)
    l_sc[...] = alpha * l_sc[...] + jnp.sum(p, axis=1, keepdims=True)
    acc_sc[...] = alpha * acc_sc[...] + _dot(p.astype(bf16), v)
    m_sc[...] = m_new


AUG_STRIDE = 4
AUG_ONES = GROUP * AUG_STRIDE


def _t_init(m_sc, l_sc, acc_sc):
    m_sc[...] = jnp.full(m_sc.shape, NEG, f32)
    l_sc[...] = jnp.zeros(l_sc.shape, f32)
    acc_sc[...] = jnp.zeros(acc_sc.shape, f32)


def _t_step(s, vt, h, m_sc, l_sc, acc_sc):
    m_old = m_sc[h]
    m_new = jnp.maximum(m_old, jnp.max(s, axis=0, keepdims=True))
    p = jnp.exp(s - m_new)
    alpha = jnp.exp(m_old - m_new)
    l_sc[h] = alpha * l_sc[h] + jnp.sum(p, axis=0, keepdims=True)
    acc_sc[h] = alpha * acc_sc[h] + _dot(vt, p.astype(bf16))
    m_sc[h] = m_new


def _t_tile(off, tq, kt_of, vt_ref, qa_sc, post, m_sc, l_sc, acc_sc):
    kts = [kt_of(g) for g in range(N_KV_HEADS)]
    vts = [vt_ref[g * HEAD_DIM:(g + 1) * HEAD_DIM, pl.ds(off, tq)] for g in range(N_KV_HEADS)]
    s_next = _dot(kts[0], qa_sc[0])
    for g in range(N_KV_HEADS):
        s = s_next
        if g + 1 < N_KV_HEADS:
            s_next = _dot(kts[g + 1], qa_sc[g + 1])
        _t_step(post(s), vts[g], g, m_sc, l_sc, acc_sc)


def _t_fill_q(qa_sc, h, tq, q, aug):
    g, jj = divmod(h, GROUP)
    qa_sc[g, :HEAD_DIM, jj * tq:(jj + 1) * tq] = q
    qa_sc[g, HEAD_DIM:, jj * tq:(jj + 1) * tq] = aug


def _t_finish(tq, l_sc, acc_sc, ot_sc):
    for h in range(N_HEADS):
        g, jj = divmod(h, GROUP)
        cols = slice(jj * tq, (jj + 1) * tq)
        ot_sc[h * HEAD_DIM:(h + 1) * HEAD_DIM, :] = (acc_sc[g][:, cols]
                                                    / l_sc[g][:, cols]).astype(bf16)


def _t_out(x_ref, gate_ref, wot_ref, ot_sc, o_ref):
    y_t = _dot(wot_ref[...], ot_sc[...])
    o_ref[...] = x_ref[...] + gate_ref[0] * y_t.T


def _fox_attn_kernel(x_ref, gate_ref, qt_ref, kpad_ref, vt_ref, cpad_ref, ct_ref, wot_ref,
                     o_ref, kaug_sc, m_sc, l_sc, acc_sc, qa_sc, ot_sc, *, tq, chunk):
    qi = pl.program_id(1)
    t = kaug_sc.shape[1]

    @pl.when(qi == 0)
    def _():
        r = lax.broadcasted_iota(i32, (LANES, LANES), 0)
        l = lax.broadcasted_iota(i32, (LANES, LANES), 1) - HEAD_DIM
        lane = lax.broadcasted_iota(i32, (1, LANES), 1) - HEAD_DIM
        ones = jnp.where((lane >= AUG_ONES) & (lane < AUG_ONES + 3), 1.0, 0.0)
        for g in range(N_KV_HEADS):
            perm = [((l >= 0) & (l < AUG_ONES) & ((l >> 2) == r - g * GROUP)
                     & ((l & 3) == p)).astype(bf16) for p in range(3)]

            def body(i, c, g=g, perm=perm):
                off = pl.multiple_of(i * chunk, chunk)
                pieces = _split3(-cpad_ref[0, pl.ds(off, chunk), :])
                aug = sum(_dot(pc, pm) for pc, pm in zip(pieces, perm)) + ones
                kaug_sc[g, pl.ds(off, chunk), :] = (
                    kpad_ref[0, g, pl.ds(off, chunk), :].astype(f32) + aug).astype(bf16)
                return c

            lax.fori_loop(0, t // chunk, body, 0)

    ri = lax.broadcasted_iota(i32, (HEAD_DIM, tq), 0)
    wide = (tq, GROUP * tq)
    d_iota = (lax.broadcasted_iota(i32, wide, 0)
              - (lax.broadcasted_iota(i32, wide, 1) & (tq - 1)))
    for h in range(N_HEADS):
        jj = h % GROUP
        cq = _split3(ct_ref[0, h:h + 1, :])
        aug = jnp.where((ri >= jj * AUG_STRIDE) & (ri < jj * AUG_STRIDE + 3), 1.0, 0.0)
        for p in range(3):
            aug = jnp.where(ri == AUG_ONES + p, cq[p].astype(f32), aug)
        _t_fill_q(qa_sc, h, tq, qt_ref[h * HEAD_DIM:(h + 1) * HEAD_DIM, :], aug.astype(bf16))
    _t_init(m_sc, l_sc, acc_sc)

    def tile(j, post):
        off = pl.multiple_of(j * tq, tq)
        _t_tile(off, tq, lambda g: kaug_sc[g, pl.ds(off, tq), :], vt_ref, qa_sc, post,
                m_sc, l_sc, acc_sc)

    def body(j, c):
        tile(j, lambda s: s)
        return c

    lax.fori_loop(0, qi, body, 0)
    tile(qi, lambda s: jnp.where(d_iota <= 0, s, NEG))
    _t_finish(tq, l_sc, acc_sc, ot_sc)
    _t_out(x_ref, gate_ref, wot_ref, ot_sc, o_ref)


def _attn_specs(bsz, t, d, tq):
    nq = t // tq
    return dict(
        x=pl.BlockSpec((tq, d), lambda b, q: (b * nq + q, 0)),
        gate=pl.BlockSpec((1, 1, d), lambda b, q: (b, 0, 0)),
        qt=pl.BlockSpec((N_HEADS * HEAD_DIM, tq), lambda b, q: (0, b * nq + q)),
        kpad=pl.BlockSpec((1, N_KV_HEADS, t, LANES), lambda b, q: (b, 0, 0, 0)),
        vt=pl.BlockSpec((KV_COLS, t), lambda b, q: (0, b)),
        wot=pl.BlockSpec((d, N_HEADS * HEAD_DIM), lambda b, q: (0, 0)),
        scratch=[pltpu.VMEM((N_KV_HEADS, 1, GROUP * tq), f32),
                 pltpu.VMEM((N_KV_HEADS, 1, GROUP * tq), f32),
                 pltpu.VMEM((N_KV_HEADS, HEAD_DIM, GROUP * tq), f32),
                 pltpu.VMEM((N_KV_HEADS, LANES, GROUP * tq), bf16),
                 pltpu.VMEM((N_HEADS * HEAD_DIM, tq), bf16)])


def _fox_attn(x, gate, qt, kpad, vt, c_pad, c_t, wot, tq):
    bsz, _, t, _ = kpad.shape
    m, d = x.shape
    sp = _attn_specs(bsz, t, d, tq)
    return pl.pallas_call(
        functools.partial(_fox_attn_kernel, tq=tq, chunk=512),
        grid=(bsz, t // tq),
        in_specs=[sp["x"], sp["gate"], sp["qt"], sp["kpad"], sp["vt"],
                  pl.BlockSpec((1, t, LANES), lambda b, q: (b, 0, 0)),
                  pl.BlockSpec((1, N_HEADS, tq), lambda b, q: (b, 0, q)),
                  sp["wot"]],
        out_specs=sp["x"],
        out_shape=jax.ShapeDtypeStruct((m, d), f32),
        scratch_shapes=[pltpu.VMEM((N_KV_HEADS, t, LANES), bf16)] + sp["scratch"],
        compiler_params=_cparams("parallel", "arbitrary"),
        name="fox_attn",
    )(x, gate, qt, kpad, vt, c_pad, c_t, wot)


def _head_rows_t(x):
    eye = (lax.broadcasted_iota(i32, (N_HEADS, x.shape[1]), 0)
           == lax.broadcasted_iota(i32, (N_HEADS, x.shape[1]), 1)).astype(bf16)
    return sum(_dot_nt(eye, p) for p in _split3(x))


def _cumsum_t_kernel(lf_ref, ct_ref, carry_sc, *, tb):
    @pl.when(pl.program_id(1) == 0)
    def _():
        carry_sc[...] = jnp.zeros(carry_sc.shape, f32)

    tri = (lax.broadcasted_iota(i32, (tb, tb), 0)
           <= lax.broadcasted_iota(i32, (tb, tb), 1)).astype(bf16)
    lft = _head_rows_t(lf_ref[0])
    csum = sum(_dot(p, tri) for p in _split3(lft))
    ct_ref[0] = csum + carry_sc[...]
    carry_sc[...] = carry_sc[...] + jnp.sum(lft, axis=1, keepdims=True)


def _cumsum_t(lf_pad, tb):
    b, t, _ = lf_pad.shape
    return pl.pallas_call(
        functools.partial(_cumsum_t_kernel, tb=tb),
        grid=(b, t // tb),
        in_specs=[pl.BlockSpec((1, tb, LANES), lambda bi, j: (bi, j, 0))],
        out_specs=pl.BlockSpec((1, N_HEADS, tb), lambda bi, j: (bi, 0, j)),
        out_shape=jax.ShapeDtypeStruct((b, N_HEADS, t), f32),
        scratch_shapes=[pltpu.VMEM((N_HEADS, 1), f32)],
        compiler_params=_cparams("parallel", "arbitrary"),
        name="cumsum_t",
    )(lf_pad)


def _kth_threshold(count_ge, shape, n_sel):
    zero = jnp.zeros(shape, i32)
    ans = jnp.where(count_ge(zero) >= n_sel, zero, jnp.full(shape, INT_MIN, i32))

    def bit_body(i, ans):
        cand = ans | jnp.left_shift(jnp.int32(1), 30 - i)
        return jnp.where(count_ge(cand) >= n_sel, cand, ans)

    return lax.fori_loop(0, 31, bit_body, ans)


def _tie_cut(count_tied_below, need, shape, n_bits):
    def body(i, p):
        cand = p | jnp.left_shift(jnp.int32(1), n_bits - 1 - i)
        return jnp.where(count_tied_below(cand) < need, cand, p)
    return lax.fori_loop(0, n_bits, body, jnp.zeros(shape, i32))


def _fold_lanes(ind, c):
    for cc in range(ind.shape[1] // LANES):
        c = c + ind[:, cc * LANES:(cc + 1) * LANES]
    return c


def _dsa_attn_kernel(x_ref, gate_ref, qt_ref, kpad_ref, vt_ref, iqt_ref, iwt_ref, ik_ref,
                     wot_ref, o_ref, key_sc, bias_sc, cut_sc, m_sc, l_sc, acc_sc, qa_sc, ot_sc,
                     *, tq, n_sel, n_bits):
    qi = pl.program_id(1)
    n_k = qi + 1
    row = lax.broadcasted_iota(i32, (tq, tq), 0)
    d_iota = row - lax.broadcasted_iota(i32, (tq, tq), 1)

    def score_body(j, c):
        off = pl.multiple_of(j * tq, tq)
        ikt = ik_ref[0, pl.ds(off, tq), :]
        s = jnp.zeros((tq, tq), f32)
        for h in range(N_IDX_HEADS):
            dots = _dot(ikt, iqt_ref[h * IDX_DIM:(h + 1) * IDX_DIM, :])
            s = s + jnp.maximum(dots, 0.0) * iwt_ref[h:h + 1, :]
        key_sc[pl.ds(off, tq), :] = jnp.where(d_iota <= (qi - j) * tq, _sort_key(s),
                                              KEY_NEG_INF)
        return c

    lax.fori_loop(0, n_k, score_body, 0)

    def count(pred):
        def body(j, c):
            off = pl.multiple_of(j * tq, tq)
            ind = jnp.where(pred(key_sc[pl.ds(off, tq), :], j), 1, 0)
            return c + jnp.sum(ind, axis=0, keepdims=True)
        return lax.fori_loop(0, n_k, body, jnp.zeros((1, tq), i32))

    thr = _kth_threshold(lambda cand: count(lambda kt, j: kt >= cand), (1, tq), n_sel)
    real = thr > KEY_NEG_INF
    n_ge = count(lambda kt, j: kt >= thr)
    cut_sc[...] = jnp.where(real, INT_MAX, -1)

    @pl.when(jnp.max(jnp.where(real, n_ge, 0)) > n_sel)
    def _():
        need = n_sel - count(lambda kt, j: kt > thr)
        cut = _tie_cut(lambda cand: count(lambda kt, j: (kt == thr) & (row + j * tq < cand)),
                       need, (1, tq), n_bits)
        cut_sc[...] = jnp.where(real, cut, -1)

    cut = cut_sc[...]

    def bias_body(j, c):
        off = pl.multiple_of(j * tq, tq)
        kt = key_sc[pl.ds(off, tq), :]
        sel = (kt > thr) | ((kt == thr) & (row + j * tq <= cut))
        bias_sc[pl.ds(off, tq), :] = jnp.where(sel, 0.0, NEG)
        return c

    lax.fori_loop(0, n_k, bias_body, 0)

    for h in range(N_HEADS):
        _t_fill_q(qa_sc, h, tq, qt_ref[h * HEAD_DIM:(h + 1) * HEAD_DIM, :],
                  jnp.zeros((LANES - HEAD_DIM, tq), bf16))
    _t_init(m_sc, l_sc, acc_sc)

    def body(j, c):
        off = pl.multiple_of(j * tq, tq)

        def masked(s):
            return jnp.concatenate([s[:, jj * tq:(jj + 1) * tq] + bias_sc[pl.ds(off, tq), :]
                                    for jj in range(GROUP)], axis=1)

        _t_tile(off, tq, lambda g: kpad_ref[0, g, pl.ds(off, tq), :], vt_ref, qa_sc,
                masked, m_sc, l_sc, acc_sc)
        return c

    lax.fori_loop(0, n_k, body, 0)
    _t_finish(tq, l_sc, acc_sc, ot_sc)
    _t_out(x_ref, gate_ref, wot_ref, ot_sc, o_ref)


def _dsa_attn(x, gate, qt, kpad, vt, iqt, iwt, ik, wot, tq, n_sel):
    bsz, _, t, _ = kpad.shape
    m, d = x.shape
    nq = t // tq
    sp = _attn_specs(bsz, t, d, tq)
    return pl.pallas_call(
        functools.partial(_dsa_attn_kernel, tq=tq, n_sel=n_sel, n_bits=(t - 1).bit_length()),
        grid=(bsz, nq),
        in_specs=[sp["x"], sp["gate"], sp["qt"], sp["kpad"], sp["vt"],
                  pl.BlockSpec((N_IDX_HEADS * IDX_DIM, tq), lambda b, q: (0, b * nq + q)),
                  pl.BlockSpec((2 * SUBLANES, tq), lambda b, q: (0, b * nq + q)),
                  pl.BlockSpec((1, t, IDX_DIM), lambda b, q: (b, 0, 0)),
                  sp["wot"]],
        out_specs=sp["x"],
        out_shape=jax.ShapeDtypeStruct((m, d), f32),
        scratch_shapes=[pltpu.VMEM((t, tq), i32), pltpu.VMEM((t, tq), f32),
                        pltpu.VMEM((1, tq), i32)] + sp["scratch"],
        compiler_params=_cparams("parallel", "arbitrary"),
        name="dsa_attn",
    )(x, gate, qt, kpad, vt, iqt, iwt, ik, wot)


def _out_proj_kernel(x_ref, o_ref, w_ref, gate_ref, y_ref):
    y_ref[...] = x_ref[...] + gate_ref[0] * _dot(o_ref[...], w_ref[...])


def _out_proj(x, o, w, gate, tm, rows_per_mod):
    m, d = x.shape
    kdim = o.shape[1]
    r = gate.shape[1]
    return pl.pallas_call(
        _out_proj_kernel,
        grid=(m // tm,),
        in_specs=[pl.BlockSpec((tm, d), lambda i: (i, 0)),
                  pl.BlockSpec((tm, kdim), lambda i: (i, 0)),
                  pl.BlockSpec((kdim, d), lambda i: (0, 0)),
                  pl.BlockSpec((1, r, d), lambda i: (i * tm // rows_per_mod, 0, 0))],
        out_specs=pl.BlockSpec((tm, d), lambda i: (i, 0)),
        out_shape=jax.ShapeDtypeStruct((m, d), f32),
        compiler_params=_cparams("parallel"),
        name="out_proj",
    )(x, o, w, gate)


FFN_CHUNK = 256


def _ffn_kernel(x_ref, xh_ref, g_ref, shift_ref, scale_ref, gate_ref, wa_ref, wb_ref,
                cw_ref, cb_ref, wd_ref, o_ref, tail_ref, acc_sc, *, tm, tiles_per_seq):
    i = pl.program_id(0)
    x = x_ref[...]
    h = _norm_mod(x, g_ref[...], shift_ref[0], scale_ref[0]).astype(bf16)
    hh = _norm_mod(xh_ref[...], g_ref[...], shift_ref[0], scale_ref[0]).astype(bf16)
    keep = jnp.where(i % tiles_per_seq == 0, 0.0, 1.0)
    row = lax.broadcasted_iota(i32, (tm, FFN_CHUNK), 0)
    for c in range(D_FF // FFN_CHUNK):
        sl = slice(c * FFN_CHUNK, (c + 1) * FFN_CHUNK)
        a = _dot(h, wa_ref[:, sl])
        b = _dot(h, wb_ref[:, sl])
        ah = _dot(hh, wa_ref[:, sl]) * keep
        last = ah[SUBLANES - 1:SUBLANES]
        p1 = jnp.where(row == 0, last, pltpu.roll(a, 1, 0))
        p2 = jnp.where(row == 0, ah[SUBLANES - 2:SUBLANES - 1],
                       jnp.where(row == 1, last, pltpu.roll(a, 2, 0)))
        conv = cb_ref[:, sl] + p2 * cw_ref[0:1, sl] + p1 * cw_ref[1:2, sl] + a * cw_ref[2:3, sl]
        y = _dot((_silu(conv) * b).astype(bf16), wd_ref[sl, :])
        if c == 0:
            acc_sc[...] = y
        else:
            acc_sc[...] += y
        tail_ref[0, :, sl] = a[tm - SUBLANES:tm]
    o_ref[...] = x + gate_ref[0] * acc_sc[...]


def _ffn(x, g, shift, scale, gate, wa, wb, cw, cb, wd, tm, rows_per_seq):
    m, d = x.shape
    n_tiles = m // tm
    hb = tm // SUBLANES
    mod_spec = pl.BlockSpec((1, 1, d), lambda i: (i * tm // rows_per_seq, 0, 0))
    const = lambda shape: pl.BlockSpec(shape, lambda i: (0,) * len(shape))
    return pl.pallas_call(
        functools.partial(_ffn_kernel, tm=tm, tiles_per_seq=rows_per_seq // tm),
        grid=(n_tiles,),
        in_specs=[pl.BlockSpec((tm, d), lambda i: (i, 0)),
                  pl.BlockSpec((SUBLANES, d), lambda i: (jnp.maximum(i * hb - 1, 0), 0)),
                  const((1, d)), mod_spec, mod_spec, mod_spec,
                  const((d, D_FF)), const((d, D_FF)), const((SUBLANES, D_FF)),
                  const((1, D_FF)), const((D_FF, d))],
        out_specs=[pl.BlockSpec((tm, d), lambda i: (i, 0)),
                   pl.BlockSpec((1, SUBLANES, D_FF), lambda i: (i, 0, 0))],
        out_shape=[jax.ShapeDtypeStruct((m, d), f32),
                   jax.ShapeDtypeStruct((n_tiles, SUBLANES, D_FF), f32)],
        scratch_shapes=[pltpu.VMEM((tm, d), f32)],
        compiler_params=_cparams("parallel"),
        name="ffn",
    )(x, x, g, shift, scale, gate, wa, wb, cw, cb, wd)


def _ffn_sample_kernel(x_ref, g_ref, shift_ref, scale_ref, gate_ref, p2_ref, p1_ref, wa_ref,
                       wb_ref, cw_ref, cb_ref, wd_ref, o_ref, a_ref, acc_sc):
    j = pl.program_id(0)

    @pl.when(j == 0)
    def _():
        acc_sc[...] = jnp.zeros(acc_sc.shape, f32)

    x = x_ref[...]
    h = _norm_mod(x, g_ref[...], shift_ref[0], scale_ref[0]).astype(bf16)
    a = _dot(h, wa_ref[...])
    b = _dot(h, wb_ref[...])
    conv = (cb_ref[...] + p2_ref[...] * cw_ref[0:1, :] + p1_ref[...] * cw_ref[1:2, :]
            + a * cw_ref[2:3, :])
    acc_sc[...] += _dot((_silu(conv) * b).astype(bf16), wd_ref[...])
    a_ref[...] = a

    @pl.when(j == pl.num_programs(0) - 1)
    def _():
        o_ref[...] = x + gate_ref[0] * acc_sc[...]


def _ffn_sample(x, g, shift, scale, gate, p2, p1, wa, wb, cw, cb, wd, tn):
    m, d = x.shape
    const = lambda shape: pl.BlockSpec(shape, lambda j: (0,) * len(shape))
    return pl.pallas_call(
        _ffn_sample_kernel,
        grid=(D_FF // tn,),
        in_specs=[const((m, d)), const((1, d)), const((1, m, d)), const((1, m, d)),
                  const((1, m, d)),
                  pl.BlockSpec((m, tn), lambda j: (0, j)), pl.BlockSpec((m, tn), lambda j: (0, j)),
                  pl.BlockSpec((d, tn), lambda j: (0, j)), pl.BlockSpec((d, tn), lambda j: (0, j)),
                  pl.BlockSpec((SUBLANES, tn), lambda j: (0, j)),
                  pl.BlockSpec((1, tn), lambda j: (0, j)),
                  pl.BlockSpec((tn, d), lambda j: (j, 0))],
        out_specs=[const((m, d)), pl.BlockSpec((m, tn), lambda j: (0, j))],
        out_shape=[jax.ShapeDtypeStruct((m, d), f32), jax.ShapeDtypeStruct((m, D_FF), f32)],
        scratch_shapes=[pltpu.VMEM((m, d), f32)],
        compiler_params=_cparams("arbitrary"),
        name="ffn_sample",
    )(x, g, shift, scale, gate, p2, p1, wa, wb, cw, cb, wd)


def _final_norm_kernel(x_ref, g_ref, o_ref):
    x = x_ref[...]
    o_ref[...] = x * lax.rsqrt(jnp.mean(x * x, axis=-1, keepdims=True) + EPS) * g_ref[...]


def _final_norm(x, g, tm):
    m, d = x.shape
    return pl.pallas_call(
        _final_norm_kernel,
        grid=(m // tm,),
        in_specs=[pl.BlockSpec((tm, d), lambda i: (i, 0)), pl.BlockSpec((1, d), lambda i: (0, 0))],
        out_specs=pl.BlockSpec((tm, d), lambda i: (i, 0)),
        out_shape=jax.ShapeDtypeStruct((m, d), f32),
        compiler_params=_cparams("parallel"),
        name="final_norm",
    )(x, g)


def _sample_score_kernel(pt_ref, iq_ref, iw_ref, kidx_ref, s_ref):
    d = _dot_nt(iq_ref[0], kidx_ref[0, 0].astype(bf16))
    s_ref[0, 0] = jnp.sum(jnp.maximum(d, 0.0) * iw_ref[0], axis=0, keepdims=True)


def _sample_scores(page_table, iq, iw, cache_kidx, layer):
    s, n_pages = page_table.shape
    grid_spec = pltpu.PrefetchScalarGridSpec(
        num_scalar_prefetch=1, grid=(s, n_pages),
        in_specs=[pl.BlockSpec((1, N_IDX_HEADS, IDX_DIM), lambda b, p, pt: (b, 0, 0)),
                  pl.BlockSpec((1, N_IDX_HEADS, 1), lambda b, p, pt: (b, 0, 0)),
                  pl.BlockSpec((1, 1, PAGE_SIZE, IDX_DIM), lambda b, p, pt: (layer, pt[b, p], 0, 0))],
        out_specs=pl.BlockSpec((1, 1, 1, PAGE_SIZE), lambda b, p, pt: (b, p, 0, 0)))
    out = pl.pallas_call(
        _sample_score_kernel, grid_spec=grid_spec,
        out_shape=jax.ShapeDtypeStruct((s, n_pages, 1, PAGE_SIZE), f32),
        compiler_params=_cparams("parallel", "arbitrary"),
        name="sample_scores",
    )(page_table, iq, iw, cache_kidx)
    return out.reshape(s, n_pages * PAGE_SIZE)


SEL_TILE = 1024


def _sample_select_kernel(s_ref, iq_ref, ikt_ref, iw_ref, bias_ref, bnew_ref, key_sc,
                          *, n_sel, n_bits):
    rows, length = s_ref.shape
    n_tiles = length // SEL_TILE
    prod = iq_ref[...].astype(f32) * ikt_ref[...].astype(f32)
    grp = (lax.broadcasted_iota(i32, (prod.shape[1], LANES), 0) // IDX_DIM
           == lax.broadcasted_iota(i32, (prod.shape[1], LANES), 1)).astype(bf16)
    d_new = sum(_dot(p, grp) for p in _split3(prod))
    s_new = jnp.sum(jnp.maximum(d_new, 0.0) * iw_ref[...], axis=1, keepdims=True)
    key_new = _sort_key(s_new)
    key_sc[...] = _sort_key(s_ref[...])
    col = lax.broadcasted_iota(i32, (rows, SEL_TILE), 1)

    def count(pred, pred_new):
        def body(j, c):
            off = pl.multiple_of(j * SEL_TILE, SEL_TILE)
            return _fold_lanes(jnp.where(pred(key_sc[:, pl.ds(off, SEL_TILE)], j), 1, 0), c)
        c = lax.fori_loop(0, n_tiles, body, jnp.zeros((rows, LANES), i32))
        return jnp.sum(c, axis=1, keepdims=True) + jnp.where(pred_new, 1, 0)

    thr = _kth_threshold(lambda cand: count(lambda kt, j: kt >= cand, key_new >= cand),
                         (rows, 1), n_sel)
    need = n_sel - count(lambda kt, j: kt > thr, key_new > thr)
    cut = _tie_cut(lambda cand: count(lambda kt, j: (kt == thr) & (col + j * SEL_TILE < cand),
                                      (key_new == thr) & (length < cand)),
                   need, (rows, 1), n_bits)

    def bias_body(j, c):
        off = pl.multiple_of(j * SEL_TILE, SEL_TILE)
        kt = key_sc[:, pl.ds(off, SEL_TILE)]
        sel = (kt > thr) | ((kt == thr) & (col + j * SEL_TILE <= cut))
        bias_ref[:, pl.ds(off, SEL_TILE)] = jnp.where(sel, 0.0, NEG)
        return c

    lax.fori_loop(0, n_tiles, bias_body, 0)
    sel_new = (key_new > thr) | ((key_new == thr) & (length <= cut))
    bnew_ref[...] = jnp.broadcast_to(jnp.where(sel_new, 0.0, NEG), bnew_ref.shape)


def _sample_select(scores, iq, ik_tiled, iw, n_sel):
    rows, length = scores.shape
    return pl.pallas_call(
        functools.partial(_sample_select_kernel, n_sel=n_sel, n_bits=length.bit_length()),
        out_shape=[jax.ShapeDtypeStruct((rows, length), f32),
                   jax.ShapeDtypeStruct((rows, LANES), f32)],
        scratch_shapes=[pltpu.VMEM((rows, length), i32)],
        compiler_params=pltpu.CompilerParams(vmem_limit_bytes=VMEM_LIMIT),
        name="sample_select",
    )(scores, iq, ik_tiled, iw)


def _suffix_bias_kernel(pt_ref, lfnew_ref, lf_ref, bias_ref, carry_sc):
    @pl.when(pl.program_id(1) == 0)
    def _():
        carry_sc[...] = lfnew_ref[0]

    tri = (lax.broadcasted_iota(i32, (PAGE_SIZE, PAGE_SIZE), 0)
           > lax.broadcasted_iota(i32, (PAGE_SIZE, PAGE_SIZE), 1)).astype(bf16)
    lft = _head_rows_t(lf_ref[0, 0])
    suffix = sum(_dot(p, tri) for p in _split3(lft))
    bias_ref[0] = suffix + carry_sc[...]
    carry_sc[...] = carry_sc[...] + jnp.sum(lft, axis=1, keepdims=True)


def _suffix_bias(page_table, lf_new, cache_logf, layer):
    s, n_pages = page_table.shape
    last = n_pages - 1
    grid_spec = pltpu.PrefetchScalarGridSpec(
        num_scalar_prefetch=1, grid=(s, n_pages),
        in_specs=[pl.BlockSpec((1, N_HEADS, 1), lambda b, p, pt: (b, 0, 0)),
                  pl.BlockSpec((1, 1, PAGE_SIZE, N_HEADS),
                               lambda b, p, pt: (layer, pt[b, last - p], 0, 0))],
        out_specs=pl.BlockSpec((1, N_HEADS, PAGE_SIZE), lambda b, p, pt: (b, 0, last - p)),
        scratch_shapes=[pltpu.VMEM((N_HEADS, 1), f32)])
    return pl.pallas_call(
        _suffix_bias_kernel, grid_spec=grid_spec,
        out_shape=jax.ShapeDtypeStruct((s, N_HEADS, n_pages * PAGE_SIZE), f32),
        compiler_params=_cparams("parallel", "arbitrary"),
        name="suffix_bias",
    )(page_table, lf_new, cache_logf)


def _decode_attn_kernel(pt_ref, q_ref, knew_ref, vnew_ref, bnew_ref, bias_ref, k_ref, v_ref,
                        o_ref, m_sc, l_sc, acc_sc):
    p = pl.program_id(1)
    q = q_ref[0]

    @pl.when(p == 0)
    def _():
        kn = knew_ref[0].astype(bf16).astype(f32)
        m_sc[...] = jnp.sum(q.astype(f32) * kn, axis=1, keepdims=True) + bnew_ref[0][:, 0:1]
        l_sc[...] = jnp.ones(l_sc.shape, f32)
        acc_sc[...] = jnp.broadcast_to(vnew_ref[0].astype(bf16).astype(f32), acc_sc.shape)

    s = _dot_nt(q, k_ref[0, 0].astype(bf16)) + bias_ref[0]
    _flash_step(s, v_ref[0, 0].astype(bf16), m_sc, l_sc, acc_sc)

    @pl.when(p == pl.num_programs(1) - 1)
    def _():
        o_ref[0] = acc_sc[...] / l_sc[...]


def _decode_attn(page_table, qmat, k_new, v_new, bias_new, bias, cache_k, cache_v, layer):
    s, n_pages = page_table.shape
    hb = bias.shape[1]
    row = lambda shape: pl.BlockSpec(shape, lambda b, p, pt: (b, 0, 0))
    page = pl.BlockSpec((1, 1, PAGE_SIZE, KV_COLS), lambda b, p, pt: (layer, pt[b, p], 0, 0))
    grid_spec = pltpu.PrefetchScalarGridSpec(
        num_scalar_prefetch=1, grid=(s, n_pages),
        in_specs=[row((1, N_HEADS, KV_COLS)), row((1, 1, KV_COLS)), row((1, 1, KV_COLS)),
                  row((1, hb, LANES)),
                  pl.BlockSpec((1, hb, PAGE_SIZE), lambda b, p, pt: (b, 0, p)),
                  page, page],
        out_specs=row((1, N_HEADS, KV_COLS)),
        scratch_shapes=[pltpu.VMEM((N_HEADS, 1), f32), pltpu.VMEM((N_HEADS, 1), f32),
                        pltpu.VMEM((N_HEADS, KV_COLS), f32)])
    return pl.pallas_call(
        _decode_attn_kernel, grid_spec=grid_spec,
        out_shape=jax.ShapeDtypeStruct((s, N_HEADS, KV_COLS), f32),
        compiler_params=_cparams("parallel", "arbitrary"),
        name="decode_attn",
    )(page_table, qmat, k_new, v_new, bias_new, bias, cache_k, cache_v)


def _pad_cols(w, n):
    return jnp.pad(w, ((0, 0), (0, n)))


def _decode_q(q_bf):
    s = q_bf.shape[0]
    onehot = (jnp.arange(N_HEADS)[:, None] // GROUP == jnp.arange(N_KV_HEADS)[None, :])
    q = q_bf.reshape(s, N_HEADS, 1, HEAD_DIM) * onehot[None, :, :, None].astype(q_bf.dtype)
    return q.reshape(s, N_HEADS, KV_COLS)


def _decode_o(o_full):
    s = o_full.shape[0]
    o = o_full.reshape(s, N_KV_HEADS, GROUP, N_KV_HEADS, HEAD_DIM)
    return jnp.stack([o[:, g, :, g, :] for g in range(N_KV_HEADS)], axis=1).reshape(s, -1)


def kernel(x_prompt, x_sample, cache_k_a, cache_v_a, cache_kidx_a, cache_k_b, cache_v_b,
           cache_logf_b, state_conv, page_table, c_prompt, c_sample, w_ada, b_ada, g_mix,
           g_ffn, w_in_a, w_o_a, w_in_b, b_f, w_o_b, w_up, conv_w, conv_b, w_down, g_final):
    bsz, t, d = x_prompt.shape
    s = x_sample.shape[0]
    depth = w_ada.shape[0]
    m = bsz * t
    past_len = page_table.shape[1] * PAGE_SIZE
    n_pool = cache_k_a.shape[1]
    tm = 512
    tq = 256

    pad_rows = (-(bsz + s)) % SUBLANES
    c_all = jnp.concatenate([c_prompt, c_sample, jnp.zeros((pad_rows, d), f32)], axis=0)
    mods = _adaln(c_all, w_ada, b_ada)
    mods_p = mods[:, :bsz].reshape(depth, bsz, 6, d)
    mods_s = mods[:, bsz:bsz + s].reshape(depth, s, 6, d)

    tabs_p = _rope_tables(jnp.arange(t))
    tabs_pt = _rope_tables_t(jnp.arange(t))
    tabs_s = _rope_tables(jnp.full((s,), past_len))
    zero_bf = jnp.zeros((1, LANES), f32)

    xp = x_prompt.reshape(m, d)
    xs = x_sample.reshape(s, d)
    kv_shape_p = (bsz, t, N_KV_HEADS, HEAD_DIM)
    kv_shape_s = (s, 1, N_KV_HEADS, HEAD_DIM)
    ka_p, va_p, ia_p, kb_p, vb_p, lb_p, cv_p = [], [], [], [], [], [], []
    ka_s, va_s, ia_s, kb_s, vb_s, lb_s, cv_s = [], [], [], [], [], [], []

    cache_k_a = cache_k_a.reshape(cache_k_a.shape[:3] + (KV_COLS,))
    cache_v_a = cache_v_a.reshape(cache_v_a.shape[:3] + (KV_COLS,))
    cache_k_b = cache_k_b.reshape(cache_k_b.shape[:3] + (KV_COLS,))
    cache_v_b = cache_v_b.reshape(cache_v_b.shape[:3] + (KV_COLS,))

    for i in range(depth):
        j = i // 2
        mp = [mods_p[i, :, c][:, None, :] for c in range(6)]
        ms = [mods_s[i, :, c][None] for c in range(6)]
        g_m = g_mix[i][None]
        if i % 2 == 0:
            w = w_in_a[j]
            n_qkvi = QKV_COLS + N_IDX_HEADS * IDX_DIM + IDX_DIM
            w_in = jnp.concatenate([_pad_cols(w[:, :n_qkvi], LANES - IDX_DIM),
                                    _pad_cols(w[:, n_qkvi:], LANES - N_IDX_HEADS)],
                                   axis=1).astype(bf16)
            w_o = w_o_a[j].astype(bf16)
            nq = N_HEADS * HEAD_DIM
            n_iq = QKV_COLS + N_IDX_HEADS * IDX_DIM
            w_row = jnp.concatenate([w[:, nq:QKV_COLS], _pad_cols(w[:, n_iq:n_qkvi], LANES - IDX_DIM)],
                                    axis=1).astype(bf16)
            w_t = jnp.concatenate([w[:, :nq], w[:, nq + KV_COLS:QKV_COLS], w[:, QKV_COLS:n_iq],
                                   _pad_cols(w[:, n_qkvi:], IW_ROWS - N_IDX_HEADS)],
                                  axis=1).T.astype(bf16)
            qt, vt, kpad, k32, v32, iqt, iwt, ikb, ik32 = _in_proj_t(
                xp, g_m, mp[0], mp[1], w_row, w_t, tabs_p, tabs_pt, zero_bf, True, tm, t)
            xp = _dsa_attn(xp, mp[2], qt, kpad, vt, iqt, iwt, ikb.reshape(bsz, t, IDX_DIM),
                           w_o.T, tq, min(TOPK_MAX, t // 4))
            ka_p.append(k32.reshape(kv_shape_p))
            va_p.append(v32.reshape(kv_shape_p))
            ia_p.append(ik32[:, :IDX_DIM].reshape(bsz, t, IDX_DIM))
            q, k32, kb, v32, vb, iq, ik32, ikb, iw = _in_proj(
                xs, g_m, ms[0], ms[1], w_in, tabs_s, zero_bf, SEGS_A, OUTS_A, s, s)
            scores = _sample_scores(page_table, iq.reshape(s, N_IDX_HEADS, IDX_DIM),
                                    iw[:, :N_IDX_HEADS].reshape(s, N_IDX_HEADS, 1),
                                    cache_kidx_a, j)
            bias, bias_new = _sample_select(scores, iq, jnp.tile(ikb[:, :IDX_DIM], (1, N_IDX_HEADS)),
                                            iw, min(TOPK_MAX, (past_len + 1) // 4))
            o_full = _decode_attn(page_table, _decode_q(q), k32[:, None], v32[:, None],
                                  bias_new[:, None], bias[:, None], cache_k_a, cache_v_a, j)
            xs = _out_proj(xs, _decode_o(o_full).astype(bf16), w_o, ms[2], s, s)
            ka_s.append(k32.reshape(kv_shape_s))
            va_s.append(v32.reshape(kv_shape_s))
            ia_s.append(ik32[:, :IDX_DIM].reshape(s, 1, IDX_DIM))
        else:
            w_in = _pad_cols(w_in_b[j], LANES - N_HEADS).astype(bf16)
            w_o = w_o_b[j].astype(bf16)
            bf_pad = _pad_cols(b_f[j][None], LANES - N_HEADS)
            nq = N_HEADS * HEAD_DIM
            w_row = w_in[:, nq:]
            w_t = jnp.concatenate([w_in[:, :nq], w_in[:, nq + KV_COLS:QKV_COLS]], axis=1).T
            qt, vt, kpad, k32, v32, lf = _in_proj_t(
                xp, g_m, mp[0], mp[1], w_row, w_t, tabs_p, tabs_pt, bf_pad, False, tm, t)
            c_t = _cumsum_t(lf.reshape(bsz, t, LANES), tq)
            c_pad = jnp.pad(c_t.transpose(0, 2, 1), ((0, 0), (0, 0), (0, LANES - N_HEADS)))
            xp = _fox_attn(xp, mp[2], qt, kpad, vt, c_pad, c_t, w_o.T, tq)
            kb_p.append(k32.reshape(kv_shape_p))
            vb_p.append(v32.reshape(kv_shape_p))
            lb_p.append(lf[:, :N_HEADS].reshape(bsz, t, N_HEADS))
            q, k32, kb, v32, vb, lf = _in_proj(
                xs, g_m, ms[0], ms[1], w_in, tabs_s, bf_pad, SEGS_B, OUTS_B, s, s)
            bias = _suffix_bias(page_table, lf[:, :N_HEADS, None], cache_logf_b, j)
            o_full = _decode_attn(page_table, _decode_q(q), k32[:, None], v32[:, None],
                                  jnp.zeros((s, N_HEADS, LANES), f32), bias,
                                  cache_k_b, cache_v_b, j)
            xs = _out_proj(xs, _decode_o(o_full).astype(bf16), w_o, ms[2], s, s)
            kb_s.append(k32.reshape(kv_shape_s))
            vb_s.append(v32.reshape(kv_shape_s))
            lb_s.append(lf[:, :N_HEADS].reshape(s, 1, N_HEADS))

        wa = w_up[i][:, :D_FF].astype(bf16)
        wb = w_up[i][:, D_FF:].astype(bf16)
        wd = w_down[i].astype(bf16)
        cw = jnp.pad(conv_w[i], ((0, SUBLANES - CONV_W), (0, 0)))
        cb = conv_b[i][None]
        g_f = g_ffn[i][None]
        xp, tail = _ffn(xp, g_f, mp[3], mp[4], mp[5], wa, wb, cw, cb, wd, tm, t)
        cv_p.append(tail.reshape(bsz, t // tm, SUBLANES, D_FF)[:, -1, SUBLANES - (CONV_W - 1):])
        xs, a_s = _ffn_sample(xs, g_f, ms[3], ms[4], ms[5], state_conv[i][:, 0],
                              state_conv[i][:, 1], wa, wb, cw, cb, wd, D_FF // 2)
        cv_s.append(jnp.stack([state_conv[i][:, 1], a_s], axis=1))

    y_prompt = _final_norm(xp, g_final[None], tm).reshape(bsz, t, d)
    y_sample = _final_norm(xs, g_final[None], s).reshape(s, 1, d)
    return (y_prompt, y_sample,
            jnp.stack(ka_p), jnp.stack(va_p), jnp.stack(ia_p),
            jnp.stack(kb_p), jnp.stack(vb_p), jnp.stack(lb_p), jnp.stack(cv_p),
            jnp.stack(ka_s), jnp.stack(va_s), jnp.stack(ia_s),
            jnp.stack(kb_s), jnp.stack(vb_s), jnp.stack(lb_s), jnp.stack(cv_s))
```

```python
import functools

import jax
import jax.numpy as jnp
from jax import lax
from jax.experimental import pallas as pl
from jax.experimental.pallas import tpu as pltpu

f32 = jnp.float32
bf16 = jnp.bfloat16
i32 = jnp.int32

D_MODEL = 1024
N_HEADS = 16
HEAD_DIM = 64
N_KV_HEADS = 4
GROUP = N_HEADS // N_KV_HEADS
KV_COLS = N_KV_HEADS * HEAD_DIM
QKV_COLS = (N_HEADS + 2 * N_KV_HEADS) * HEAD_DIM
ROPE_DIMS = HEAD_DIM // 4
ROPE_HALF = ROPE_DIMS // 2
ROPE_THETA = 500000.0
N_IDX_HEADS = 8
IDX_DIM = 64
TOPK_MAX = 256
D_FF = 2816
CONV_W = 3
EPS = 1e-6
PAGE_SIZE = 128
Q_SCALE = HEAD_DIM ** -0.5

LANES = 128
SUBLANES = 8
VMEM_LIMIT = 56 * 1024 * 1024
NEG = -1e30
INT_MIN = -2 ** 31
INT_MAX = 2 ** 31 - 1
KEY_NEG_INF = (0xFF800000 ^ 0x7FFFFFFF) - 2 ** 32


def _cparams(*sem):
    return pltpu.CompilerParams(dimension_semantics=sem, vmem_limit_bytes=VMEM_LIMIT)


def _dot(a, b):
    return jnp.dot(a, b, preferred_element_type=f32)


def _dot_nt(a, b):
    return lax.dot_general(a, b, (((1,), (1,)), ((), ())), preferred_element_type=f32)


def _split3(x):
    hi = x.astype(bf16)
    r1 = x - hi.astype(f32)
    mid = r1.astype(bf16)
    lo = (r1 - mid.astype(f32)).astype(bf16)
    return hi, mid, lo


def _norm_mod(x, g, shift, scale):
    xn = x * lax.rsqrt(jnp.mean(x * x, axis=-1, keepdims=True) + EPS)
    return (xn * g) * (1.0 + scale) + shift


def _silu(x):
    return x * jax.nn.sigmoid(x)


def _sort_key(s):
    bits = pltpu.bitcast(s, i32)
    return bits ^ ((bits >> 31) & 0x7FFFFFFF)


def _adaln_kernel(c_ref, w_ref, b_ref, o_ref):
    a = _silu(c_ref[...]).astype(bf16)
    o_ref[0] = _dot(a, w_ref[0].astype(bf16)) + b_ref[0]


def _adaln(c_all, w_ada, b_ada):
    depth, d, n = w_ada.shape
    rows = c_all.shape[0]
    tn = 1536
    return pl.pallas_call(
        _adaln_kernel,
        grid=(depth, n // tn),
        in_specs=[pl.BlockSpec((rows, d), lambda l, j: (0, 0)),
                  pl.BlockSpec((1, d, tn), lambda l, j: (l, 0, j)),
                  pl.BlockSpec((1, 1, tn), lambda l, j: (l, 0, j))],
        out_specs=pl.BlockSpec((1, rows, tn), lambda l, j: (l, 0, j)),
        out_shape=jax.ShapeDtypeStruct((depth, rows, n), f32),
        compiler_params=_cparams("parallel", "parallel"),
        name="adaln",
    )(c_all, w_ada, b_ada.reshape(depth, 1, n))


SEGS_A = ((0, 1024, "rope", ((0, Q_SCALE),)),
          (1024, 256, "rope", ((1, 1.0), (2, 1.0))),
          (1280, 256, None, ((3, 1.0), (4, 1.0))),
          (1536, 512, "rope", ((5, 1.0),)),
          (2048, 128, "rope", ((6, 1.0), (7, 1.0))),
          (2176, 128, None, ((8, 1.0),)))
OUTS_A = ((1024, bf16), (256, f32), (256, bf16), (256, f32), (256, bf16),
          (512, bf16), (128, f32), (128, bf16), (128, f32))
SEGS_B = ((0, 1024, None, ((0, Q_SCALE),)),
          (1024, 256, None, ((1, 1.0), (2, 1.0))),
          (1280, 256, None, ((3, 1.0), (4, 1.0))),
          (1536, 128, "logf", ((5, 1.0),)))
OUTS_B = ((1024, bf16), (256, f32), (256, bf16), (256, f32), (256, bf16), (128, f32))


def _in_proj_kernel(x_ref, g_ref, shift_ref, scale_ref, w_ref, cos_ref, sa_ref, sb_ref,
                    bf_ref, *out_refs, segs):
    h = _norm_mod(x_ref[...], g_ref[...], shift_ref[0], scale_ref[0]).astype(bf16)
    for c0, width, epi, outs in segs:
        for c in range(width // LANES):
            lo = c0 + c * LANES
            a = _dot(h, w_ref[:, lo:lo + LANES])
            if epi == "rope":
                a = (a * cos_ref[...] + pltpu.roll(a, ROPE_HALF, 1) * sa_ref[...]
                     + pltpu.roll(a, LANES - ROPE_HALF, 1) * sb_ref[...])
            elif epi == "logf":
                z = a + bf_ref[...]
                a = jnp.minimum(z, 0.0) - jnp.log(1.0 + jnp.exp(-jnp.abs(z)))
            for oi, sc in outs:
                val = a if sc == 1.0 else a * sc
                out_refs[oi][:, c * LANES:(c + 1) * LANES] = val.astype(out_refs[oi].dtype)


def _in_proj(x, g, shift, scale, w, tabs, b_f, segs, outs, tm, rows_per_mod):
    m, d = x.shape
    n = w.shape[1]
    r = shift.shape[1]
    cos, sa, sb = tabs
    tab_blocks = cos.shape[0] // tm
    mod_spec = pl.BlockSpec((1, r, d), lambda i: (i * tm // rows_per_mod, 0, 0))
    tab_spec = pl.BlockSpec((tm, LANES), lambda i: (i % tab_blocks, 0))
    return pl.pallas_call(
        functools.partial(_in_proj_kernel, segs=segs),
        grid=(m // tm,),
        in_specs=[pl.BlockSpec((tm, d), lambda i: (i, 0)),
                  pl.BlockSpec((1, d), lambda i: (0, 0)),
                  mod_spec, mod_spec,
                  pl.BlockSpec((d, n), lambda i: (0, 0)),
                  tab_spec, tab_spec, tab_spec,
                  pl.BlockSpec((1, LANES), lambda i: (0, 0))],
        out_specs=[pl.BlockSpec((tm, wd), lambda i: (i, 0)) for wd, _ in outs],
        out_shape=[jax.ShapeDtypeStruct((m, wd), dt) for wd, dt in outs],
        compiler_params=_cparams("parallel"),
        name="in_proj",
    )(x, g, shift, scale, w, cos, sa, sb, b_f)


def _rope_tables(pos):
    inv_freq = ROPE_THETA ** (-jnp.arange(ROPE_HALF, dtype=f32) / ROPE_HALF)
    ang = pos.astype(f32)[:, None] * inv_freq[None, :]
    cos, sin = jnp.cos(ang), jnp.sin(ang)
    t = pos.shape[0]
    ones = jnp.ones((t, HEAD_DIM - ROPE_DIMS), f32)
    zeros = jnp.zeros((t, HEAD_DIM - ROPE_DIMS), f32)
    zh = jnp.zeros((t, ROPE_HALF), f32)
    c = jnp.concatenate([cos, cos, ones], axis=1)
    sa = jnp.concatenate([zh, sin, zeros], axis=1)
    sb = jnp.concatenate([-sin, zh, zeros], axis=1)
    rep = LANES // HEAD_DIM
    return tuple(jnp.tile(a, (1, rep)) for a in (c, sa, sb))


WT_Q = 0
WT_V = N_HEADS * HEAD_DIM
WT_IQ = WT_V + KV_COLS
WT_IW = WT_IQ + N_IDX_HEADS * IDX_DIM
IW_ROWS = 2 * SUBLANES


def _rope_rows(a, cos_t, sin_t):
    parts = []
    for b0 in range(0, a.shape[0], HEAD_DIM):
        x1 = a[b0:b0 + ROPE_HALF]
        x2 = a[b0 + ROPE_HALF:b0 + ROPE_DIMS]
        parts += [x1 * cos_t - x2 * sin_t, x1 * sin_t + x2 * cos_t, a[b0 + ROPE_DIMS:b0 + HEAD_DIM]]
    return jnp.concatenate(parts, axis=0)


def _in_proj_t_kernel(x_ref, g_ref, shift_ref, scale_ref, w_ref, wt_ref, cos_ref, sa_ref, sb_ref,
                      cost_ref, sint_ref, bf_ref, *outs, is_a):
    if is_a:
        qt_ref, vt_ref, kpad_ref, k32_ref, v32_ref, iqt_ref, iwt_ref, ikb_ref, ik32_ref = outs
    else:
        qt_ref, vt_ref, kpad_ref, k32_ref, v32_ref, lf_ref = outs
    h = _norm_mod(x_ref[...], g_ref[...], shift_ref[0], scale_ref[0]).astype(bf16)
    cos_t, sin_t = cost_ref[...], sint_ref[...]

    def rows(r0):
        return _dot_nt(wt_ref[r0:r0 + LANES, :], h)

    def rope_cols(a):
        return (a * cos_ref[...] + pltpu.roll(a, ROPE_HALF, 1) * sa_ref[...]
                + pltpu.roll(a, LANES - ROPE_HALF, 1) * sb_ref[...])

    for c in range(N_HEADS * HEAD_DIM // LANES):
        a = rows(WT_Q + c * LANES)
        if is_a:
            a = _rope_rows(a, cos_t, sin_t)
        qt_ref[c * LANES:(c + 1) * LANES, :] = (a * Q_SCALE).astype(bf16)
    for c in range(KV_COLS // LANES):
        vt_ref[c * LANES:(c + 1) * LANES, :] = rows(WT_V + c * LANES).astype(bf16)
    if is_a:
        for c in range(N_IDX_HEADS * IDX_DIM // LANES):
            a = _rope_rows(rows(WT_IQ + c * LANES), cos_t, sin_t)
            iqt_ref[c * LANES:(c + 1) * LANES, :] = a.astype(bf16)
        iwt_ref[...] = _dot_nt(wt_ref[WT_IW:WT_IW + IW_ROWS, :], h)

    lane = lax.broadcasted_iota(i32, (1, LANES), 1)
    for c in range(KV_COLS // LANES):
        a = _dot(h, w_ref[:, c * LANES:(c + 1) * LANES])
        if is_a:
            a = rope_cols(a)
        k32_ref[:, c * LANES:(c + 1) * LANES] = a
        kpad_ref[0, 2 * c] = jnp.where(lane < HEAD_DIM, a, 0.0).astype(bf16)
        kpad_ref[0, 2 * c + 1] = jnp.where(lane < HEAD_DIM, pltpu.roll(a, HEAD_DIM, 1),
                                           0.0).astype(bf16)
    for c in range(KV_COLS // LANES):
        lo = KV_COLS + c * LANES
        v32_ref[:, c * LANES:(c + 1) * LANES] = _dot(h, w_ref[:, lo:lo + LANES])
    a = _dot(h, w_ref[:, 2 * KV_COLS:2 * KV_COLS + LANES])
    if is_a:
        a = rope_cols(a)
        ik32_ref[...] = a
        ikb_ref[...] = a[:, :IDX_DIM].astype(bf16)
    else:
        z = a + bf_ref[...]
        lf_ref[...] = jnp.minimum(z, 0.0) - jnp.log(1.0 + jnp.exp(-jnp.abs(z)))


def _in_proj_t(x, g, shift, scale, w, wt, tabs, tabs_t, b_f, is_a, tm, t):
    m, d = x.shape
    bsz = m // t
    tpb = t // tm
    col = lambda rows_: pl.BlockSpec((rows_, tm), lambda i: (0, i))
    row = lambda cols_: pl.BlockSpec((tm, cols_), lambda i: (i, 0))
    const = lambda a: pl.BlockSpec(a.shape, lambda i: (0,) * a.ndim)
    mod_spec = pl.BlockSpec((1, 1, d), lambda i: (i // tpb, 0, 0))
    tab_spec = pl.BlockSpec((tm, LANES), lambda i: (i % tpb, 0))
    tabt_spec = pl.BlockSpec((ROPE_HALF, tm), lambda i: (0, i % tpb))
    nq = N_HEADS * HEAD_DIM
    out_specs = [col(nq), col(KV_COLS),
                 pl.BlockSpec((1, N_KV_HEADS, tm, LANES), lambda i: (i // tpb, 0, i % tpb, 0)),
                 row(KV_COLS), row(KV_COLS)]
    out_shape = [jax.ShapeDtypeStruct((nq, m), bf16), jax.ShapeDtypeStruct((KV_COLS, m), bf16),
                 jax.ShapeDtypeStruct((bsz, N_KV_HEADS, t, LANES), bf16),
                 jax.ShapeDtypeStruct((m, KV_COLS), f32), jax.ShapeDtypeStruct((m, KV_COLS), f32)]
    if is_a:
        out_specs += [col(N_IDX_HEADS * IDX_DIM), col(IW_ROWS), row(IDX_DIM), row(LANES)]
        out_shape += [jax.ShapeDtypeStruct((N_IDX_HEADS * IDX_DIM, m), bf16),
                      jax.ShapeDtypeStruct((IW_ROWS, m), f32),
                      jax.ShapeDtypeStruct((m, IDX_DIM), bf16),
                      jax.ShapeDtypeStruct((m, LANES), f32)]
    else:
        out_specs += [row(LANES)]
        out_shape += [jax.ShapeDtypeStruct((m, LANES), f32)]
    return pl.pallas_call(
        functools.partial(_in_proj_t_kernel, is_a=is_a),
        grid=(m // tm,),
        in_specs=[row(d), const(g), mod_spec, mod_spec, const(w), const(wt),
                  tab_spec, tab_spec, tab_spec, tabt_spec, tabt_spec, const(b_f)],
        out_specs=out_specs, out_shape=out_shape,
        compiler_params=_cparams("parallel"),
        name="in_proj_t",
    )(x, g, shift, scale, w, wt, *tabs, *tabs_t, b_f)


def _rope_tables_t(pos):
    inv_freq = ROPE_THETA ** (-jnp.arange(ROPE_HALF, dtype=f32) / ROPE_HALF)
    ang = inv_freq[:, None] * pos.astype(f32)[None, :]
    return jnp.cos(ang), jnp.sin(ang)


AUG_STRIDE = 4
AUG_ONES = GROUP * AUG_STRIDE


def _t_init(m_sc, l_sc, acc_sc):
    m_sc[...] = jnp.full(m_sc.shape, NEG, f32)
    l_sc[...] = jnp.zeros(l_sc.shape, f32)
    acc_sc[...] = jnp.zeros(acc_sc.shape, f32)


def _t_step(s, vt, h, m_sc, l_sc, acc_sc):
    m_old = m_sc[h]
    m_new = jnp.maximum(m_old, jnp.max(s, axis=0, keepdims=True))
    p = jnp.exp(s - m_new)
    alpha = jnp.exp(m_old - m_new)
    l_sc[h] = alpha * l_sc[h] + jnp.sum(p, axis=0, keepdims=True)
    acc_sc[h] = alpha * acc_sc[h] + _dot(vt, p.astype(bf16))
    m_sc[h] = m_new


def _t_tile(off, tq, kt_of, vt_ref, qa_sc, post, m_sc, l_sc, acc_sc):
    kts = [kt_of(g) for g in range(N_KV_HEADS)]
    vts = [vt_ref[g * HEAD_DIM:(g + 1) * HEAD_DIM, pl.ds(off, tq)] for g in range(N_KV_HEADS)]
    s_next = _dot(kts[0], qa_sc[0])
    for g in range(N_KV_HEADS):
        s = s_next
        if g + 1 < N_KV_HEADS:
            s_next = _dot(kts[g + 1], qa_sc[g + 1])
        _t_step(post(s), vts[g], g, m_sc, l_sc, acc_sc)


def _t_fill_q(qa_sc, h, tq, q, aug):
    g, jj = divmod(h, GROUP)
    qa_sc[g, :HEAD_DIM, jj * tq:(jj + 1) * tq] = q
    qa_sc[g, HEAD_DIM:, jj * tq:(jj + 1) * tq] = aug


def _t_finish(tq, l_sc, acc_sc, ot_sc):
    for h in range(N_HEADS):
        g, jj = divmod(h, GROUP)
        cols = slice(jj * tq, (jj + 1) * tq)
        ot_sc[h * HEAD_DIM:(h + 1) * HEAD_DIM, :] = (acc_sc[g][:, cols]
                                                    / l_sc[g][:, cols]).astype(bf16)


def _t_out(x_ref, gate_ref, wot_ref, ot_sc, o_ref):
    y_t = _dot(wot_ref[...], ot_sc[...])
    o_ref[...] = x_ref[...] + gate_ref[0] * y_t.T


def _fox_attn_kernel(x_ref, gate_ref, qt_ref, kpad_ref, vt_ref, cpad_ref, ct_ref, wot_ref,
                     o_ref, kaug_sc, m_sc, l_sc, acc_sc, qa_sc, ot_sc, *, tq, chunk):
    qi = pl.program_id(1)
    t = kaug_sc.shape[1]

    @pl.when(qi == 0)
    def _():
        r = lax.broadcasted_iota(i32, (LANES, LANES), 0)
        l = lax.broadcasted_iota(i32, (LANES, LANES), 1) - HEAD_DIM
        lane = lax.broadcasted_iota(i32, (1, LANES), 1) - HEAD_DIM
        ones = jnp.where((lane >= AUG_ONES) & (lane < AUG_ONES + 3), 1.0, 0.0)
        for g in range(N_KV_HEADS):
            perm = [((l >= 0) & (l < AUG_ONES) & ((l >> 2) == r - g * GROUP)
                     & ((l & 3) == p)).astype(bf16) for p in range(3)]

            def body(i, c, g=g, perm=perm):
                off = pl.multiple_of(i * chunk, chunk)
                pieces = _split3(-cpad_ref[0, pl.ds(off, chunk), :])
                aug = sum(_dot(pc, pm) for pc, pm in zip(pieces, perm)) + ones
                kaug_sc[g, pl.ds(off, chunk), :] = (
                    kpad_ref[0, g, pl.ds(off, chunk), :].astype(f32) + aug).astype(bf16)
                return c

            lax.fori_loop(0, t // chunk, body, 0)

    ri = lax.broadcasted_iota(i32, (HEAD_DIM, tq), 0)
    wide = (tq, GROUP * tq)
    d_iota = (lax.broadcasted_iota(i32, wide, 0)
              - (lax.broadcasted_iota(i32, wide, 1) & (tq - 1)))
    for h in range(N_HEADS):
        jj = h % GROUP
        cq = _split3(ct_ref[0, h:h + 1, :])
        aug = jnp.where((ri >= jj * AUG_STRIDE) & (ri < jj * AUG_STRIDE + 3), 1.0, 0.0)
        for p in range(3):
            aug = jnp.where(ri == AUG_ONES + p, cq[p].astype(f32), aug)
        _t_fill_q(qa_sc, h, tq, qt_ref[h * HEAD_DIM:(h + 1) * HEAD_DIM, :], aug.astype(bf16))
    _t_init(m_sc, l_sc, acc_sc)

    def tile(j, post):
        off = pl.multiple_of(j * tq, tq)
        _t_tile(off, tq, lambda g: kaug_sc[g, pl.ds(off, tq), :], vt_ref, qa_sc, post,
                m_sc, l_sc, acc_sc)

    def body(j, c):
        tile(j, lambda s: s)
        return c

    lax.fori_loop(0, qi, body, 0)
    tile(qi, lambda s: jnp.where(d_iota <= 0, s, NEG))
    _t_finish(tq, l_sc, acc_sc, ot_sc)
    _t_out(x_ref, gate_ref, wot_ref, ot_sc, o_ref)


def _attn_specs(bsz, t, d, tq):
    nq = t // tq
    return dict(
        x=pl.BlockSpec((tq, d), lambda b, q: (b * nq + q, 0)),
        gate=pl.BlockSpec((1, 1, d), lambda b, q: (b, 0, 0)),
        qt=pl.BlockSpec((N_HEADS * HEAD_DIM, tq), lambda b, q: (0, b * nq + q)),
        kpad=pl.BlockSpec((1, N_KV_HEADS, t, LANES), lambda b, q: (b, 0, 0, 0)),
        vt=pl.BlockSpec((KV_COLS, t), lambda b, q: (0, b)),
        wot=pl.BlockSpec((d, N_HEADS * HEAD_DIM), lambda b, q: (0, 0)),
        scratch=[pltpu.VMEM((N_KV_HEADS, 1, GROUP * tq), f32),
                 pltpu.VMEM((N_KV_HEADS, 1, GROUP * tq), f32),
                 pltpu.VMEM((N_KV_HEADS, HEAD_DIM, GROUP * tq), f32),
                 pltpu.VMEM((N_KV_HEADS, LANES, GROUP * tq), bf16),
                 pltpu.VMEM((N_HEADS * HEAD_DIM, tq), bf16)])


def _fox_attn(x, gate, qt, kpad, vt, c_pad, c_t, wot, tq):
    bsz, _, t, _ = kpad.shape
    m, d = x.shape
    sp = _attn_specs(bsz, t, d, tq)
    return pl.pallas_call(
        functools.partial(_fox_attn_kernel, tq=tq, chunk=512),
        grid=(bsz, t // tq),
        in_specs=[sp["x"], sp["gate"], sp["qt"], sp["kpad"], sp["vt"],
                  pl.BlockSpec((1, t, LANES), lambda b, q: (b, 0, 0)),
                  pl.BlockSpec((1, N_HEADS, tq), lambda b, q: (b, 0, q)),
                  sp["wot"]],
        out_specs=sp["x"],
        out_shape=jax.ShapeDtypeStruct((m, d), f32),
        scratch_shapes=[pltpu.VMEM((N_KV_HEADS, t, LANES), bf16)] + sp["scratch"],
        compiler_params=_cparams("parallel", "arbitrary"),
        name="fox_attn",
    )(x, gate, qt, kpad, vt, c_pad, c_t, wot)


def _head_rows_t(x):
    eye = (lax.broadcasted_iota(i32, (N_HEADS, x.shape[1]), 0)
           == lax.broadcasted_iota(i32, (N_HEADS, x.shape[1]), 1)).astype(bf16)
    return sum(_dot_nt(eye, p) for p in _split3(x))


def _cumsum_t_kernel(lf_ref, ct_ref, carry_sc, *, tb):
    @pl.when(pl.program_id(1) == 0)
    def _():
        carry_sc[...] = jnp.zeros(carry_sc.shape, f32)

    tri = (lax.broadcasted_iota(i32, (tb, tb), 0)
           <= lax.broadcasted_iota(i32, (tb, tb), 1)).astype(bf16)
    lft = _head_rows_t(lf_ref[0])
    csum = sum(_dot(p, tri) for p in _split3(lft))
    ct_ref[0] = csum + carry_sc[...]
    carry_sc[...] = carry_sc[...] + jnp.sum(lft, axis=1, keepdims=True)


def _cumsum_t(lf_pad, tb):
    b, t, _ = lf_pad.shape
    return pl.pallas_call(
        functools.partial(_cumsum_t_kernel, tb=tb),
        grid=(b, t // tb),
        in_specs=[pl.BlockSpec((1, tb, LANES), lambda bi, j: (bi, j, 0))],
        out_specs=pl.BlockSpec((1, N_HEADS, tb), lambda bi, j: (bi, 0, j)),
        out_shape=jax.ShapeDtypeStruct((b, N_HEADS, t), f32),
        scratch_shapes=[pltpu.VMEM((N_HEADS, 1), f32)],
        compiler_params=_cparams("parallel", "arbitrary"),
        name="cumsum_t",
    )(lf_pad)


def _kth_threshold(count_ge, shape, n_sel):
    zero = jnp.zeros(shape, i32)
    ans = jnp.where(count_ge(zero) >= n_sel, zero, jnp.full(shape, INT_MIN, i32))

    def bit_body(i, ans):
        cand = ans | jnp.left_shift(jnp.int32(1), 30 - i)
        return jnp.where(count_ge(cand) >= n_sel, cand, ans)

    return lax.fori_loop(0, 31, bit_body, ans)


def _tie_cut(count_tied_below, need, shape, n_bits):
    def body(i, p):
        cand = p | jnp.left_shift(jnp.int32(1), n_bits - 1 - i)
        return jnp.where(count_tied_below(cand) < need, cand, p)
    return lax.fori_loop(0, n_bits, body, jnp.zeros(shape, i32))


def _fold_lanes(ind, c):
    for cc in range(ind.shape[1] // LANES):
        c = c + ind[:, cc * LANES:(cc + 1) * LANES]
    return c


def _dsa_attn_kernel(x_ref, gate_ref, qt_ref, kpad_ref, vt_ref, iqt_ref, iwt_ref, ik_ref,
                     wot_ref, o_ref, key_sc, bias_sc, cut_sc, m_sc, l_sc, acc_sc, qa_sc, ot_sc,
                     *, tq, n_sel, n_bits):
    qi = pl.program_id(1)
    n_k = qi + 1
    row = lax.broadcasted_iota(i32, (tq, tq), 0)
    d_iota = row - lax.broadcasted_iota(i32, (tq, tq), 1)

    def score_body(j, c):
        off = pl.multiple_of(j * tq, tq)
        ikt = ik_ref[0, pl.ds(off, tq), :]
        s = jnp.zeros((tq, tq), f32)
        for h in range(N_IDX_HEADS):
            dots = _dot(ikt, iqt_ref[h * IDX_DIM:(h + 1) * IDX_DIM, :])
            s = s + jnp.maximum(dots, 0.0) * iwt_ref[h:h + 1, :]
        key_sc[pl.ds(off, tq), :] = jnp.where(d_iota <= (qi - j) * tq, _sort_key(s),
                                              KEY_NEG_INF)
        return c

    lax.fori_loop(0, n_k, score_body, 0)

    def count(pred):
        def body(j, c):
            off = pl.multiple_of(j * tq, tq)
            ind = jnp.where(pred(key_sc[pl.ds(off, tq), :], j), 1, 0)
            return c + jnp.sum(ind, axis=0, keepdims=True)
        return lax.fori_loop(0, n_k, body, jnp.zeros((1, tq), i32))

    thr = _kth_threshold(lambda cand: count(lambda kt, j: kt >= cand), (1, tq), n_sel)
    real = thr > KEY_NEG_INF
    n_ge = count(lambda kt, j: kt >= thr)
    cut_sc[...] = jnp.where(real, INT_MAX, -1)

    @pl.when(jnp.max(jnp.where(real, n_ge, 0)) > n_sel)
    def _():
        need = n_sel - count(lambda kt, j: kt > thr)
        cut = _tie_cut(lambda cand: count(lambda kt, j: (kt == thr) & (row + j * tq < cand)),
                       need, (1, tq), n_bits)
        cut_sc[...] = jnp.where(real, cut, -1)

    cut = cut_sc[...]

    def bias_body(j, c):
        off = pl.multiple_of(j * tq, tq)
        kt = key_sc[pl.ds(off, tq), :]
        sel = (kt > thr) | ((kt == thr) & (row + j * tq <= cut))
        bias_sc[pl.ds(off, tq), :] = jnp.where(sel, 0.0, NEG)
        return c

    lax.fori_loop(0, n_k, bias_body, 0)

    for h in range(N_HEADS):
        _t_fill_q(qa_sc, h, tq, qt_ref[h * HEAD_DIM:(h + 1) * HEAD_DIM, :],
                  jnp.zeros((LANES - HEAD_DIM, tq), bf16))
    _t_init(m_sc, l_sc, acc_sc)

    def body(j, c):
        off = pl.multiple_of(j * tq, tq)

        def masked(s):
            return jnp.concatenate([s[:, jj * tq:(jj + 1) * tq] + bias_sc[pl.ds(off, tq), :]
                                    for jj in range(GROUP)], axis=1)

        _t_tile(off, tq, lambda g: kpad_ref[0, g, pl.ds(off, tq), :], vt_ref, qa_sc,
                masked, m_sc, l_sc, acc_sc)
        return c

    lax.fori_loop(0, n_k, body, 0)
    _t_finish(tq, l_sc, acc_sc, ot_sc)
    _t_out(x_ref, gate_ref, wot_ref, ot_sc, o_ref)


def _dsa_attn(x, gate, qt, kpad, vt, iqt, iwt, ik, wot, tq, n_sel):
    bsz, _, t, _ = kpad.shape
    m, d = x.shape
    nq = t // tq
    sp = _attn_specs(bsz, t, d, tq)
    return pl.pallas_call(
        functools.partial(_dsa_attn_kernel, tq=tq, n_sel=n_sel, n_bits=(t - 1).bit_length()),
        grid=(bsz, nq),
        in_specs=[sp["x"], sp["gate"], sp["qt"], sp["kpad"], sp["vt"],
                  pl.BlockSpec((N_IDX_HEADS * IDX_DIM, tq), lambda b, q: (0, b * nq + q)),
                  pl.BlockSpec((2 * SUBLANES, tq), lambda b, q: (0, b * nq + q)),
                  pl.BlockSpec((1, t, IDX_DIM), lambda b, q: (b, 0, 0)),
                  sp["wot"]],
        out_specs=sp["x"],
        out_shape=jax.ShapeDtypeStruct((m, d), f32),
        scratch_shapes=[pltpu.VMEM((t, tq), i32), pltpu.VMEM((t, tq), f32),
                        pltpu.VMEM((1, tq), i32)] + sp["scratch"],
        compiler_params=_cparams("parallel", "arbitrary"),
        name="dsa_attn",
    )(x, gate, qt, kpad, vt, iqt, iwt, ik, wot)


def _out_proj_kernel(x_ref, o_ref, w_ref, gate_ref, y_ref):
    y_ref[...] = x_ref[...] + gate_ref[0] * _dot(o_ref[...], w_ref[...])


def _out_proj(x, o, w, gate, tm, rows_per_mod):
    m, d = x.shape
    kdim = o.shape[1]
    r = gate.shape[1]
    return pl.pallas_call(
        _out_proj_kernel,
        grid=(m // tm,),
        in_specs=[pl.BlockSpec((tm, d), lambda i: (i, 0)),
                  pl.BlockSpec((tm, kdim), lambda i: (i, 0)),
                  pl.BlockSpec((kdim, d), lambda i: (0, 0)),
                  pl.BlockSpec((1, r, d), lambda i: (i * tm // rows_per_mod, 0, 0))],
        out_specs=pl.BlockSpec((tm, d), lambda i: (i, 0)),
        out_shape=jax.ShapeDtypeStruct((m, d), f32),
        compiler_params=_cparams("parallel"),
        name="out_proj",
    )(x, o, w, gate)


FFN_CHUNK = 256


def _ffn_kernel(x_ref, xh_ref, g_ref, shift_ref, scale_ref, gate_ref, wa_ref, wb_ref,
                cw_ref, cb_ref, wd_ref, o_ref, tail_ref, acc_sc, *, tm, tiles_per_seq):
    i = pl.program_id(0)
    x = x_ref[...]
    h = _norm_mod(x, g_ref[...], shift_ref[0], scale_ref[0]).astype(bf16)
    hh = _norm_mod(xh_ref[...], g_ref[...], shift_ref[0], scale_ref[0]).astype(bf16)
    keep = jnp.where(i % tiles_per_seq == 0, 0.0, 1.0)
    row = lax.broadcasted_iota(i32, (tm, FFN_CHUNK), 0)
    for c in range(D_FF // FFN_CHUNK):
        sl = slice(c * FFN_CHUNK, (c + 1) * FFN_CHUNK)
        a = _dot(h, wa_ref[:, sl])
        b = _dot(h, wb_ref[:, sl])
        ah = _dot(hh, wa_ref[:, sl]) * keep
        last = ah[SUBLANES - 1:SUBLANES]
        p1 = jnp.where(row == 0, last, pltpu.roll(a, 1, 0))
        p2 = jnp.where(row == 0, ah[SUBLANES - 2:SUBLANES - 1],
                       jnp.where(row == 1, last, pltpu.roll(a, 2, 0)))
        conv = cb_ref[:, sl] + p2 * cw_ref[0:1, sl] + p1 * cw_ref[1:2, sl] + a * cw_ref[2:3, sl]
        y = _dot((_silu(conv) * b).astype(bf16), wd_ref[sl, :])
        if c == 0:
            acc_sc[...] = y
        else:
            acc_sc[...] += y
        tail_ref[0, :, sl] = a[tm - SUBLANES:tm]
    o_ref[...] = x + gate_ref[0] * acc_sc[...]


def _ffn(x, g, shift, scale, gate, wa, wb, cw, cb, wd, tm, rows_per_seq):
    m, d = x.shape
    n_tiles = m // tm
    hb = tm // SUBLANES
    mod_spec = pl.BlockSpec((1, 1, d), lambda i: (i * tm // rows_per_seq, 0, 0))
    const = lambda shape: pl.BlockSpec(shape, lambda i: (0,) * len(shape))
    return pl.pallas_call(
        functools.partial(_ffn_kernel, tm=tm, tiles_per_seq=rows_per_seq // tm),
        grid=(n_tiles,),
        in_specs=[pl.BlockSpec((tm, d), lambda i: (i, 0)),
                  pl.BlockSpec((SUBLANES, d), lambda i: (jnp.maximum(i * hb - 1, 0), 0)),
                  const((1, d)), mod_spec, mod_spec, mod_spec,
                  const((d, D_FF)), const((d, D_FF)), const((SUBLANES, D_FF)),
                  const((1, D_FF)), const((D_FF, d))],
        out_specs=[pl.BlockSpec((tm, d), lambda i: (i, 0)),
                   pl.BlockSpec((1, SUBLANES, D_FF), lambda i: (i, 0, 0))],
        out_shape=[jax.ShapeDtypeStruct((m, d), f32),
                   jax.ShapeDtypeStruct((n_tiles, SUBLANES, D_FF), f32)],
        scratch_shapes=[pltpu.VMEM((tm, d), f32)],
        compiler_params=_cparams("parallel"),
        name="ffn",
    )(x, x, g, shift, scale, gate, wa, wb, cw, cb, wd)


def _ffn_sample_kernel(x_ref, g_ref, shift_ref, scale_ref, gate_ref, p2_ref, p1_ref, wa_ref,
                       wb_ref, cw_ref, cb_ref, wd_ref, o_ref, a_ref, acc_sc):
    j = pl.program_id(0)

    @pl.when(j == 0)
    def _():
        acc_sc[...] = jnp.zeros(acc_sc.shape, f32)

    x = x_ref[...]
    h = _norm_mod(x, g_ref[...], shift_ref[0], scale_ref[0]).astype(bf16)
    a = _dot(h, wa_ref[...])
    b = _dot(h, wb_ref[...])
    conv = (cb_ref[...] + p2_ref[...] * cw_ref[0:1, :] + p1_ref[...] * cw_ref[1:2, :]
            + a * cw_ref[2:3, :])
    acc_sc[...] += _dot((_silu(conv) * b).astype(bf16), wd_ref[...])
    a_ref[...] = a

    @pl.when(j == pl.num_programs(0) - 1)
    def _():
        o_ref[...] = x + gate_ref[0] * acc_sc[...]


def _ffn_sample(x, g, shift, scale, gate, p2, p1, wa, wb, cw, cb, wd, tn):
    m, d = x.shape
    const = lambda shape: pl.BlockSpec(shape, lambda j: (0,) * len(shape))
    return pl.pallas_call(
        _ffn_sample_kernel,
        grid=(D_FF // tn,),
        in_specs=[const((m, d)), const((1, d)), const((1, m, d)), const((1, m, d)),
                  const((1, m, d)),
                  pl.BlockSpec((m, tn), lambda j: (0, j)), pl.BlockSpec((m, tn), lambda j: (0, j)),
                  pl.BlockSpec((d, tn), lambda j: (0, j)), pl.BlockSpec((d, tn), lambda j: (0, j)),
                  pl.BlockSpec((SUBLANES, tn), lambda j: (0, j)),
                  pl.BlockSpec((1, tn), lambda j: (0, j)),
                  pl.BlockSpec((tn, d), lambda j: (j, 0))],
        out_specs=[const((m, d)), pl.BlockSpec((m, tn), lambda j: (0, j))],
        out_shape=[jax.ShapeDtypeStruct((m, d), f32), jax.ShapeDtypeStruct((m, D_FF), f32)],
        scratch_shapes=[pltpu.VMEM((m, d), f32)],
        compiler_params=_cparams("arbitrary"),
        name="ffn_sample",
    )(x, g, shift, scale, gate, p2, p1, wa, wb, cw, cb, wd)


def _final_norm_kernel(x_ref, g_ref, o_ref):
    x = x_ref[...]
    o_ref[...] = x * lax.rsqrt(jnp.mean(x * x, axis=-1, keepdims=True) + EPS) * g_ref[...]


def _final_norm(x, g, tm):
    m, d = x.shape
    return pl.pallas_call(
        _final_norm_kernel,
        grid=(m // tm,),
        in_specs=[pl.BlockSpec((tm, d), lambda i: (i, 0)), pl.BlockSpec((1, d), lambda i: (0, 0))],
        out_specs=pl.BlockSpec((tm, d), lambda i: (i, 0)),
        out_shape=jax.ShapeDtypeStruct((m, d), f32),
        compiler_params=_cparams("parallel"),
        name="final_norm",
    )(x, g)


def _seq_page_copies(pt_ref, hbm, layer, seq, buf, slot, sem, n_pages):
    return [pltpu.make_async_copy(hbm.at[layer, pt_ref[seq, p]],
                                  buf.at[slot, :, pl.ds(p * PAGE_SIZE, PAGE_SIZE)], sem.at[slot])
            for p in range(n_pages)]


def _gather_seq(pt_ref, hbm, layer, buf, sem, n_pages):
    b = pl.program_id(0)
    slot = b % 2

    @pl.when(b == 0)
    def _():
        for cp in _seq_page_copies(pt_ref, hbm, layer, 0, buf, 0, sem, n_pages):
            cp.start()

    @pl.when(b + 1 < pl.num_programs(0))
    def _():
        for cp in _seq_page_copies(pt_ref, hbm, layer, b + 1, buf, 1 - slot, sem, n_pages):
            cp.start()

    for cp in _seq_page_copies(pt_ref, hbm, layer, b, buf, slot, sem, n_pages):
        cp.wait()
    return slot


def _sample_score_kernel(pt_ref, iq_ref, iw_ref, kidx_hbm, s_ref, buf, sem, *, layer, n_pages):
    slot = _gather_seq(pt_ref, kidx_hbm, layer, buf, sem, n_pages)
    d = _dot(iq_ref[0], buf[slot].astype(bf16))
    s_ref[0] = jnp.sum(jnp.maximum(d, 0.0) * iw_ref[0], axis=0, keepdims=True)


def _sample_scores(page_table, iq, iw, cache_kidx_t, layer):
    s, n_pages = page_table.shape
    length = n_pages * PAGE_SIZE
    grid_spec = pltpu.PrefetchScalarGridSpec(
        num_scalar_prefetch=1, grid=(s,),
        in_specs=[pl.BlockSpec((1, N_IDX_HEADS, IDX_DIM), lambda b, pt: (b, 0, 0)),
                  pl.BlockSpec((1, N_IDX_HEADS, 1), lambda b, pt: (b, 0, 0)),
                  pl.BlockSpec(memory_space=pl.ANY)],
        out_specs=pl.BlockSpec((1, 1, length), lambda b, pt: (b, 0, 0)),
        scratch_shapes=[pltpu.VMEM((2, IDX_DIM, length), f32), pltpu.SemaphoreType.DMA((2,))])
    out = pl.pallas_call(
        functools.partial(_sample_score_kernel, layer=layer, n_pages=n_pages),
        grid_spec=grid_spec,
        out_shape=jax.ShapeDtypeStruct((s, 1, length), f32),
        compiler_params=_cparams("arbitrary"),
        name="sample_scores",
    )(page_table, iq, iw, cache_kidx_t)
    return out.reshape(s, length)


SEL_TILE = 1024


def _sample_select_kernel(s_ref, iq_ref, ikt_ref, iw_ref, bias_ref, bnew_ref, key_sc,
                          *, n_sel, n_bits):
    rows, length = s_ref.shape
    n_tiles = length // SEL_TILE
    prod = iq_ref[...].astype(f32) * ikt_ref[...].astype(f32)
    grp = (lax.broadcasted_iota(i32, (prod.shape[1], LANES), 0) // IDX_DIM
           == lax.broadcasted_iota(i32, (prod.shape[1], LANES), 1)).astype(bf16)
    d_new = sum(_dot(p, grp) for p in _split3(prod))
    s_new = jnp.sum(jnp.maximum(d_new, 0.0) * iw_ref[...], axis=1, keepdims=True)
    key_new = _sort_key(s_new)
    key_sc[...] = _sort_key(s_ref[...])
    col = lax.broadcasted_iota(i32, (rows, SEL_TILE), 1)

    def count(pred, pred_new):
        def body(j, c):
            off = pl.multiple_of(j * SEL_TILE, SEL_TILE)
            return _fold_lanes(jnp.where(pred(key_sc[:, pl.ds(off, SEL_TILE)], j), 1, 0), c)
        c = lax.fori_loop(0, n_tiles, body, jnp.zeros((rows, LANES), i32))
        return jnp.sum(c, axis=1, keepdims=True) + jnp.where(pred_new, 1, 0)

    thr = _kth_threshold(lambda cand: count(lambda kt, j: kt >= cand, key_new >= cand),
                         (rows, 1), n_sel)
    need = n_sel - count(lambda kt, j: kt > thr, key_new > thr)
    cut = _tie_cut(lambda cand: count(lambda kt, j: (kt == thr) & (col + j * SEL_TILE < cand),
                                      (key_new == thr) & (length < cand)),
                   need, (rows, 1), n_bits)

    def bias_body(j, c):
        off = pl.multiple_of(j * SEL_TILE, SEL_TILE)
        kt = key_sc[:, pl.ds(off, SEL_TILE)]
        sel = (kt > thr) | ((kt == thr) & (col + j * SEL_TILE <= cut))
        bias_ref[:, pl.ds(off, SEL_TILE)] = jnp.where(sel, 0.0, NEG)
        return c

    lax.fori_loop(0, n_tiles, bias_body, 0)
    sel_new = (key_new > thr) | ((key_new == thr) & (length <= cut))
    bnew_ref[...] = jnp.broadcast_to(jnp.where(sel_new, 0.0, NEG), bnew_ref.shape)


def _sample_select(scores, iq, ik_tiled, iw, n_sel):
    rows, length = scores.shape
    return pl.pallas_call(
        functools.partial(_sample_select_kernel, n_sel=n_sel, n_bits=length.bit_length()),
        out_shape=[jax.ShapeDtypeStruct((rows, length), f32),
                   jax.ShapeDtypeStruct((rows, LANES), f32)],
        scratch_shapes=[pltpu.VMEM((rows, length), i32)],
        compiler_params=pltpu.CompilerParams(vmem_limit_bytes=VMEM_LIMIT),
        name="sample_select",
    )(scores, iq, ik_tiled, iw)


SUFFIX_CHUNK = 256


def _suffix_bias_kernel(pt_ref, lfnew_ref, lf_hbm, bias_ref, buf, sem, *, layer, n_pages):
    slot = _gather_seq(pt_ref, lf_hbm, layer, buf, sem, n_pages)
    w = SUFFIX_CHUNK
    tri = (lax.broadcasted_iota(i32, (w, w), 0)
           > lax.broadcasted_iota(i32, (w, w), 1)).astype(bf16)
    carry = lfnew_ref[0]
    for c in reversed(range(n_pages * PAGE_SIZE // w)):
        x = buf[slot, :, c * w:(c + 1) * w]
        bias_ref[0, :, c * w:(c + 1) * w] = sum(_dot(p, tri) for p in _split3(x)) + carry
        carry = carry + jnp.sum(x, axis=1, keepdims=True)


def _suffix_bias(page_table, lf_new, cache_logf_t, layer):
    s, n_pages = page_table.shape
    length = n_pages * PAGE_SIZE
    grid_spec = pltpu.PrefetchScalarGridSpec(
        num_scalar_prefetch=1, grid=(s,),
        in_specs=[pl.BlockSpec((1, N_HEADS, 1), lambda b, pt: (b, 0, 0)),
                  pl.BlockSpec(memory_space=pl.ANY)],
        out_specs=pl.BlockSpec((1, N_HEADS, length), lambda b, pt: (b, 0, 0)),
        scratch_shapes=[pltpu.VMEM((2, N_HEADS, length), f32), pltpu.SemaphoreType.DMA((2,))])
    return pl.pallas_call(
        functools.partial(_suffix_bias_kernel, layer=layer, n_pages=n_pages),
        grid_spec=grid_spec,
        out_shape=jax.ShapeDtypeStruct((s, N_HEADS, length), f32),
        compiler_params=_cparams("arbitrary"),
        name="suffix_bias",
    )(page_table, lf_new, cache_logf_t)


DEC_PAGES = 8


def _kv_chunk_copies(pt_ref, k_hbm, v_hbm, layer, seq, chunk, kbuf, vbuf, slot, sem):
    cps = []
    for i in range(DEC_PAGES):
        page = pt_ref[seq, chunk * DEC_PAGES + i]
        cols = pl.ds(i * PAGE_SIZE, PAGE_SIZE)
        cps.append(pltpu.make_async_copy(k_hbm.at[layer, page], kbuf.at[slot, :, cols],
                                         sem.at[0, slot]))
        cps.append(pltpu.make_async_copy(v_hbm.at[layer, page], vbuf.at[slot, :, cols],
                                         sem.at[1, slot]))
    return cps


def _decode_attn_kernel(pt_ref, q_ref, knew_ref, vnew_ref, bnew_ref, bias_ref, k_hbm, v_hbm,
                        o_ref, kbuf, vbuf, sem, *, layer, n_chunks):
    b = pl.program_id(0)
    q = q_ref[0]
    width = DEC_PAGES * PAGE_SIZE
    copies = functools.partial(_kv_chunk_copies, pt_ref, k_hbm, v_hbm, layer)

    @pl.when(b == 0)
    def _():
        for cp in copies(0, 0, kbuf, vbuf, 0, sem):
            cp.start()

    kn = knew_ref[0].astype(bf16).astype(f32)
    m = jnp.sum(q.astype(f32) * kn, axis=1, keepdims=True) + bnew_ref[0][:, 0:1]
    l = jnp.ones_like(m)
    acc = jnp.broadcast_to(vnew_ref[0].astype(bf16).astype(f32), (N_HEADS, KV_COLS))

    for c in range(n_chunks):
        slot = c % 2
        if c + 1 < n_chunks:
            for cp in copies(b, c + 1, kbuf, vbuf, 1 - slot, sem):
                cp.start()
        else:
            @pl.when(b + 1 < pl.num_programs(0))
            def _():
                for cp in copies(b + 1, 0, kbuf, vbuf, 1 - slot, sem):
                    cp.start()
        for cp in copies(b, c, kbuf, vbuf, slot, sem):
            cp.wait()
        s = _dot(q, kbuf[slot].astype(bf16)) + bias_ref[0, :, c * width:(c + 1) * width]
        m_new = jnp.maximum(m, jnp.max(s, axis=1, keepdims=True))
        p = jnp.exp(s - m_new)
        alpha = jnp.exp(m - m_new)
        l = alpha * l + jnp.sum(p, axis=1, keepdims=True)
        acc = alpha * acc + _dot_nt(p.astype(bf16), vbuf[slot].astype(bf16))
        m = m_new
    o_ref[0] = acc / l


def _decode_attn(page_table, qmat, k_new, v_new, bias_new, bias, cache_kt, cache_vt, layer):
    s, n_pages = page_table.shape
    hb = bias.shape[1]
    n_chunks = n_pages // DEC_PAGES
    assert n_chunks * DEC_PAGES == n_pages and n_chunks % 2 == 0
    row = lambda shape: pl.BlockSpec(shape, lambda b, pt: (b, 0, 0))
    hbm = pl.BlockSpec(memory_space=pl.ANY)
    width = DEC_PAGES * PAGE_SIZE
    grid_spec = pltpu.PrefetchScalarGridSpec(
        num_scalar_prefetch=1, grid=(s,),
        in_specs=[row((1, N_HEADS, KV_COLS)), row((1, 1, KV_COLS)), row((1, 1, KV_COLS)),
                  row((1, hb, LANES)), row((1, hb, n_pages * PAGE_SIZE)), hbm, hbm],
        out_specs=row((1, N_HEADS, KV_COLS)),
        scratch_shapes=[pltpu.VMEM((2, KV_COLS, width), f32), pltpu.VMEM((2, KV_COLS, width), f32),
                        pltpu.SemaphoreType.DMA((2, 2))])
    return pl.pallas_call(
        functools.partial(_decode_attn_kernel, layer=layer, n_chunks=n_chunks),
        grid_spec=grid_spec,
        out_shape=jax.ShapeDtypeStruct((s, N_HEADS, KV_COLS), f32),
        compiler_params=_cparams("arbitrary"),
        name="decode_attn",
    )(page_table, qmat, k_new, v_new, bias_new, bias, cache_kt, cache_vt)


def _pad_cols(w, n):
    return jnp.pad(w, ((0, 0), (0, n)))


def _decode_q(q_bf):
    s = q_bf.shape[0]
    onehot = (jnp.arange(N_HEADS)[:, None] // GROUP == jnp.arange(N_KV_HEADS)[None, :])
    q = q_bf.reshape(s, N_HEADS, 1, HEAD_DIM) * onehot[None, :, :, None].astype(q_bf.dtype)
    return q.reshape(s, N_HEADS, KV_COLS)


def _decode_o(o_full):
    s = o_full.shape[0]
    o = o_full.reshape(s, N_KV_HEADS, GROUP, N_KV_HEADS, HEAD_DIM)
    return jnp.stack([o[:, g, :, g, :] for g in range(N_KV_HEADS)], axis=1).reshape(s, -1)


def kernel(x_prompt, x_sample, cache_k_a, cache_v_a, cache_kidx_a, cache_k_b, cache_v_b,
           cache_logf_b, state_conv, page_table, c_prompt, c_sample, w_ada, b_ada, g_mix,
           g_ffn, w_in_a, w_o_a, w_in_b, b_f, w_o_b, w_up, conv_w, conv_b, w_down, g_final):
    bsz, t, d = x_prompt.shape
    s = x_sample.shape[0]
    depth = w_ada.shape[0]
    m = bsz * t
    past_len = page_table.shape[1] * PAGE_SIZE
    n_pool = cache_k_a.shape[1]
    tm = 512
    tq = 256

    pad_rows = (-(bsz + s)) % SUBLANES
    c_all = jnp.concatenate([c_prompt, c_sample, jnp.zeros((pad_rows, d), f32)], axis=0)
    mods = _adaln(c_all, w_ada, b_ada)
    mods_p = mods[:, :bsz].reshape(depth, bsz, 6, d)
    mods_s = mods[:, bsz:bsz + s].reshape(depth, s, 6, d)

    tabs_p = _rope_tables(jnp.arange(t))
    tabs_pt = _rope_tables_t(jnp.arange(t))
    tabs_s = _rope_tables(jnp.full((s,), past_len))
    zero_bf = jnp.zeros((1, LANES), f32)

    xp = x_prompt.reshape(m, d)
    xs = x_sample.reshape(s, d)
    kv_shape_p = (bsz, t, N_KV_HEADS, HEAD_DIM)
    kv_shape_s = (s, 1, N_KV_HEADS, HEAD_DIM)
    ka_p, va_p, ia_p, kb_p, vb_p, lb_p, cv_p = [], [], [], [], [], [], []
    ka_s, va_s, ia_s, kb_s, vb_s, lb_s, cv_s = [], [], [], [], [], [], []

    def pages_t(cache):
        ct = jnp.moveaxis(cache, 2, -1)
        return ct.reshape(ct.shape[:2] + (-1, PAGE_SIZE))

    cache_k_a, cache_v_a, cache_kidx_a = pages_t(cache_k_a), pages_t(cache_v_a), pages_t(cache_kidx_a)
    cache_k_b, cache_v_b, cache_logf_b = pages_t(cache_k_b), pages_t(cache_v_b), pages_t(cache_logf_b)

    for i in range(depth):
        j = i // 2
        mp = [mods_p[i, :, c][:, None, :] for c in range(6)]
        ms = [mods_s[i, :, c][None] for c in range(6)]
        g_m = g_mix[i][None]
        if i % 2 == 0:
            w = w_in_a[j]
            n_qkvi = QKV_COLS + N_IDX_HEADS * IDX_DIM + IDX_DIM
            w_in = jnp.concatenate([_pad_cols(w[:, :n_qkvi], LANES - IDX_DIM),
                                    _pad_cols(w[:, n_qkvi:], LANES - N_IDX_HEADS)],
                                   axis=1).astype(bf16)
            w_o = w_o_a[j].astype(bf16)
            nq = N_HEADS * HEAD_DIM
            n_iq = QKV_COLS + N_IDX_HEADS * IDX_DIM
            w_row = jnp.concatenate([w[:, nq:QKV_COLS], _pad_cols(w[:, n_iq:n_qkvi], LANES - IDX_DIM)],
                                    axis=1).astype(bf16)
            w_t = jnp.concatenate([w[:, :nq], w[:, nq + KV_COLS:QKV_COLS], w[:, QKV_COLS:n_iq],
                                   _pad_cols(w[:, n_qkvi:], IW_ROWS - N_IDX_HEADS)],
                                  axis=1).T.astype(bf16)
            qt, vt, kpad, k32, v32, iqt, iwt, ikb, ik32 = _in_proj_t(
                xp, g_m, mp[0], mp[1], w_row, w_t, tabs_p, tabs_pt, zero_bf, True, tm, t)
            xp = _dsa_attn(xp, mp[2], qt, kpad, vt, iqt, iwt, ikb.reshape(bsz, t, IDX_DIM),
                           w_o.T, tq, min(TOPK_MAX, t // 4))
            ka_p.append(k32.reshape(kv_shape_p))
            va_p.append(v32.reshape(kv_shape_p))
            ia_p.append(ik32[:, :IDX_DIM].reshape(bsz, t, IDX_DIM))
            q, k32, kb, v32, vb, iq, ik32, ikb, iw = _in_proj(
                xs, g_m, ms[0], ms[1], w_in, tabs_s, zero_bf, SEGS_A, OUTS_A, s, s)
            scores = _sample_scores(page_table, iq.reshape(s, N_IDX_HEADS, IDX_DIM),
                                    iw[:, :N_IDX_HEADS].reshape(s, N_IDX_HEADS, 1),
                                    cache_kidx_a, j)
            bias, bias_new = _sample_select(scores, iq, jnp.tile(ikb[:, :IDX_DIM], (1, N_IDX_HEADS)),
                                            iw, min(TOPK_MAX, (past_len + 1) // 4))
            o_full = _decode_attn(page_table, _decode_q(q), k32[:, None], v32[:, None],
                                  bias_new[:, None], bias[:, None], cache_k_a, cache_v_a, j)
            xs = _out_proj(xs, _decode_o(o_full).astype(bf16), w_o, ms[2], s, s)
            ka_s.append(k32.reshape(kv_shape_s))
            va_s.append(v32.reshape(kv_shape_s))
            ia_s.append(ik32[:, :IDX_DIM].reshape(s, 1, IDX_DIM))
        else:
            w_in = _pad_cols(w_in_b[j], LANES - N_HEADS).astype(bf16)
            w_o = w_o_b[j].astype(bf16)
            bf_pad = _pad_cols(b_f[j][None], LANES - N_HEADS)
            nq = N_HEADS * HEAD_DIM
            w_row = w_in[:, nq:]
            w_t = jnp.concatenate([w_in[:, :nq], w_in[:, nq + KV_COLS:QKV_COLS]], axis=1).T
            qt, vt, kpad, k32, v32, lf = _in_proj_t(
                xp, g_m, mp[0], mp[1], w_row, w_t, tabs_p, tabs_pt, bf_pad, False, tm, t)
            c_t = _cumsum_t(lf.reshape(bsz, t, LANES), tq)
            c_pad = jnp.pad(c_t.transpose(0, 2, 1), ((0, 0), (0, 0), (0, LANES - N_HEADS)))
            xp = _fox_attn(xp, mp[2], qt, kpad, vt, c_pad, c_t, w_o.T, tq)
            kb_p.append(k32.reshape(kv_shape_p))
            vb_p.append(v32.reshape(kv_shape_p))
            lb_p.append(lf[:, :N_HEADS].reshape(bsz, t, N_HEADS))
            q, k32, kb, v32, vb, lf = _in_proj(
                xs, g_m, ms[0], ms[1], w_in, tabs_s, bf_pad, SEGS_B, OUTS_B, s, s)
            bias = _suffix_bias(page_table, lf[:, :N_HEADS, None], cache_logf_b, j)
            o_full = _decode_attn(page_table, _decode_q(q), k32[:, None], v32[:, None],
                                  jnp.zeros((s, N_HEADS, LANES), f32), bias,
                                  cache_k_b, cache_v_b, j)
            xs = _out_proj(xs, _decode_o(o_full).astype(bf16), w_o, ms[2], s, s)
            kb_s.append(k32.reshape(kv_shape_s))
            vb_s.append(v32.reshape(kv_shape_s))
            lb_s.append(lf[:, :N_HEADS].reshape(s, 1, N_HEADS))

        wa = w_up[i][:, :D_FF].astype(bf16)
        wb = w_up[i][:, D_FF:].astype(bf16)
        wd = w_down[i].astype(bf16)
        cw = jnp.pad(conv_w[i], ((0, SUBLANES - CONV_W), (0, 0)))
        cb = conv_b[i][None]
        g_f = g_ffn[i][None]
        xp, tail = _ffn(xp, g_f, mp[3], mp[4], mp[5], wa, wb, cw, cb, wd, tm, t)
        cv_p.append(tail.reshape(bsz, t // tm, SUBLANES, D_FF)[:, -1, SUBLANES - (CONV_W - 1):])
        xs, a_s = _ffn_sample(xs, g_f, ms[3], ms[4], ms[5], state_conv[i][:, 0],
                              state_conv[i][:, 1], wa, wb, cw, cb, wd, D_FF // 2)
        cv_s.append(jnp.stack([state_conv[i][:, 1], a_s], axis=1))

    y_prompt = _final_norm(xp, g_final[None], tm).reshape(bsz, t, d)
    y_sample = _final_norm(xs, g_final[None], s).reshape(s, 1, d)
    return (y_prompt, y_sample,
            jnp.stack(ka_p), jnp.stack(va_p), jnp.stack(ia_p),
            jnp.stack(kb_p), jnp.stack(vb_p), jnp.stack(lb_p), jnp.stack(cv_p),
            jnp.stack(ka_s), jnp.stack(va_s), jnp.stack(ia_s),
            jnp.stack(kb_s), jnp.stack(vb_s), jnp.stack(lb_s), jnp.stack(cv_s))
```

```python
import functools

import jax
import jax.numpy as jnp
from jax import lax
from jax.experimental import pallas as pl
from jax.experimental.pallas import tpu as pltpu

f32 = jnp.float32
bf16 = jnp.bfloat16
i32 = jnp.int32

D_MODEL = 1024
N_HEADS = 16
HEAD_DIM = 64
N_KV_HEADS = 4
GROUP = N_HEADS // N_KV_HEADS
KV_COLS = N_KV_HEADS * HEAD_DIM
QKV_COLS = (N_HEADS + 2 * N_KV_HEADS) * HEAD_DIM
ROPE_DIMS = HEAD_DIM // 4
ROPE_HALF = ROPE_DIMS // 2
ROPE_THETA = 500000.0
N_IDX_HEADS = 8
IDX_DIM = 64
TOPK_MAX = 256
D_FF = 2816
CONV_W = 3
EPS = 1e-6
PAGE_SIZE = 128
Q_SCALE = HEAD_DIM ** -0.5
LOG2E = 1.4426950408889634

LANES = 128
SUBLANES = 8
VMEM_LIMIT = 56 * 1024 * 1024
NEG = -1e30
INT_MIN = -2 ** 31
INT_MAX = 2 ** 31 - 1
KEY_NEG_INF = (0xFF800000 ^ 0x7FFFFFFF) - 2 ** 32


def _cparams(*sem):
    return pltpu.CompilerParams(dimension_semantics=sem, vmem_limit_bytes=VMEM_LIMIT)


def _dot(a, b):
    return jnp.dot(a, b, preferred_element_type=f32)


def _dot_nt(a, b):
    return lax.dot_general(a, b, (((1,), (1,)), ((), ())), preferred_element_type=f32)


def _split3(x):
    hi = x.astype(bf16)
    r1 = x - hi.astype(f32)
    mid = r1.astype(bf16)
    lo = (r1 - mid.astype(f32)).astype(bf16)
    return hi, mid, lo


def _norm_mod(x, g, shift, scale):
    xn = x * lax.rsqrt(jnp.mean(x * x, axis=-1, keepdims=True) + EPS)
    return (xn * g) * (1.0 + scale) + shift


def _silu(x):
    return x * jax.nn.sigmoid(x)


def _sort_key(s):
    bits = pltpu.bitcast(s, i32)
    return bits ^ ((bits >> 31) & 0x7FFFFFFF)


def _adaln_kernel(c_ref, w_ref, b_ref, o_ref):
    a = _silu(c_ref[...]).astype(bf16)
    o_ref[0] = _dot(a, w_ref[0].astype(bf16)) + b_ref[0]


def _adaln(c_all, w_ada, b_ada):
    depth, d, n = w_ada.shape
    rows = c_all.shape[0]
    tn = 1536
    return pl.pallas_call(
        _adaln_kernel,
        grid=(depth, n // tn),
        in_specs=[pl.BlockSpec((rows, d), lambda l, j: (0, 0)),
                  pl.BlockSpec((1, d, tn), lambda l, j: (l, 0, j)),
                  pl.BlockSpec((1, 1, tn), lambda l, j: (l, 0, j))],
        out_specs=pl.BlockSpec((1, rows, tn), lambda l, j: (l, 0, j)),
        out_shape=jax.ShapeDtypeStruct((depth, rows, n), f32),
        compiler_params=_cparams("parallel", "parallel"),
        name="adaln",
    )(c_all, w_ada, b_ada.reshape(depth, 1, n))


SEGS_A = ((0, 1024, "rope", ((0, Q_SCALE),)),
          (1024, 256, "rope", ((1, 1.0), (2, 1.0))),
          (1280, 256, None, ((3, 1.0), (4, 1.0))),
          (1536, 512, "rope", ((5, 1.0),)),
          (2048, 128, "rope", ((6, 1.0), (7, 1.0))),
          (2176, 128, None, ((8, 1.0),)))
OUTS_A = ((1024, bf16), (256, f32), (256, bf16), (256, f32), (256, bf16),
          (512, bf16), (128, f32), (128, bf16), (128, f32))
SEGS_B = ((0, 1024, None, ((0, Q_SCALE),)),
          (1024, 256, None, ((1, 1.0), (2, 1.0))),
          (1280, 256, None, ((3, 1.0), (4, 1.0))),
          (1536, 128, "logf", ((5, 1.0),)))
OUTS_B = ((1024, bf16), (256, f32), (256, bf16), (256, f32), (256, bf16), (128, f32))


def _in_proj_kernel(x_ref, g_ref, shift_ref, scale_ref, w_ref, cos_ref, sa_ref, sb_ref,
                    bf_ref, *out_refs, segs):
    h = _norm_mod(x_ref[...], g_ref[...], shift_ref[0], scale_ref[0]).astype(bf16)
    for c0, width, epi, outs in segs:
        for c in range(width // LANES):
            lo = c0 + c * LANES
            a = _dot(h, w_ref[:, lo:lo + LANES])
            if epi == "rope":
                a = (a * cos_ref[...] + pltpu.roll(a, ROPE_HALF, 1) * sa_ref[...]
                     + pltpu.roll(a, LANES - ROPE_HALF, 1) * sb_ref[...])
            elif epi == "logf":
                z = a + bf_ref[...]
                a = jnp.minimum(z, 0.0) - jnp.log(1.0 + jnp.exp(-jnp.abs(z)))
            for oi, sc in outs:
                val = a if sc == 1.0 else a * sc
                out_refs[oi][:, c * LANES:(c + 1) * LANES] = val.astype(out_refs[oi].dtype)


def _in_proj(x, g, shift, scale, w, tabs, b_f, segs, outs, tm, rows_per_mod):
    m, d = x.shape
    n = w.shape[1]
    r = shift.shape[1]
    cos, sa, sb = tabs
    tab_blocks = cos.shape[0] // tm
    mod_spec = pl.BlockSpec((1, r, d), lambda i: (i * tm // rows_per_mod, 0, 0))
    tab_spec = pl.BlockSpec((tm, LANES), lambda i: (i % tab_blocks, 0))
    return pl.pallas_call(
        functools.partial(_in_proj_kernel, segs=segs),
        grid=(m // tm,),
        in_specs=[pl.BlockSpec((tm, d), lambda i: (i, 0)),
                  pl.BlockSpec((1, d), lambda i: (0, 0)),
                  mod_spec, mod_spec,
                  pl.BlockSpec((d, n), lambda i: (0, 0)),
                  tab_spec, tab_spec, tab_spec,
                  pl.BlockSpec((1, LANES), lambda i: (0, 0))],
        out_specs=[pl.BlockSpec((tm, wd), lambda i: (i, 0)) for wd, _ in outs],
        out_shape=[jax.ShapeDtypeStruct((m, wd), dt) for wd, dt in outs],
        compiler_params=_cparams("parallel"),
        name="in_proj",
    )(x, g, shift, scale, w, cos, sa, sb, b_f)


def _rope_tables(pos):
    inv_freq = ROPE_THETA ** (-jnp.arange(ROPE_HALF, dtype=f32) / ROPE_HALF)
    ang = pos.astype(f32)[:, None] * inv_freq[None, :]
    cos, sin = jnp.cos(ang), jnp.sin(ang)
    t = pos.shape[0]
    ones = jnp.ones((t, HEAD_DIM - ROPE_DIMS), f32)
    zeros = jnp.zeros((t, HEAD_DIM - ROPE_DIMS), f32)
    zh = jnp.zeros((t, ROPE_HALF), f32)
    c = jnp.concatenate([cos, cos, ones], axis=1)
    sa = jnp.concatenate([zh, sin, zeros], axis=1)
    sb = jnp.concatenate([-sin, zh, zeros], axis=1)
    rep = LANES // HEAD_DIM
    return tuple(jnp.tile(a, (1, rep)) for a in (c, sa, sb))


WT_Q = 0
WT_V = N_HEADS * HEAD_DIM
WT_IQ = WT_V + KV_COLS
WT_IW = WT_IQ + N_IDX_HEADS * IDX_DIM
IW_ROWS = 2 * SUBLANES


def _rope_rows(a, cos_t, sin_t):
    parts = []
    for b0 in range(0, a.shape[0], HEAD_DIM):
        x1 = a[b0:b0 + ROPE_HALF]
        x2 = a[b0 + ROPE_HALF:b0 + ROPE_DIMS]
        parts += [x1 * cos_t - x2 * sin_t, x1 * sin_t + x2 * cos_t, a[b0 + ROPE_DIMS:b0 + HEAD_DIM]]
    return jnp.concatenate(parts, axis=0)


def _in_proj_t_kernel(x_ref, g_ref, shift_ref, scale_ref, w_ref, wt_ref, cos_ref, sa_ref, sb_ref,
                      cost_ref, sint_ref, bf_ref, *outs, is_a):
    if is_a:
        qt_ref, vt_ref, kpad_ref, k32_ref, v32_ref, iqt_ref, iwt_ref, ikb_ref, ik32_ref = outs
    else:
        qt_ref, vt_ref, kpad_ref, k32_ref, v32_ref, lf_ref = outs
    h = _norm_mod(x_ref[...], g_ref[...], shift_ref[0], scale_ref[0]).astype(bf16)
    cos_t, sin_t = cost_ref[...], sint_ref[...]

    def rows(r0):
        return _dot_nt(wt_ref[r0:r0 + LANES, :], h)

    def rope_cols(a):
        return (a * cos_ref[...] + pltpu.roll(a, ROPE_HALF, 1) * sa_ref[...]
                + pltpu.roll(a, LANES - ROPE_HALF, 1) * sb_ref[...])

    for c in range(N_HEADS * HEAD_DIM // LANES):
        a = rows(WT_Q + c * LANES)
        if is_a:
            a = _rope_rows(a, cos_t, sin_t)
        qt_ref[c * LANES:(c + 1) * LANES, :] = (a * (Q_SCALE * LOG2E)).astype(bf16)
    for c in range(KV_COLS // LANES):
        vt_ref[c * LANES:(c + 1) * LANES, :] = rows(WT_V + c * LANES).astype(bf16)
    if is_a:
        for c in range(N_IDX_HEADS * IDX_DIM // LANES):
            a = _rope_rows(rows(WT_IQ + c * LANES), cos_t, sin_t)
            iqt_ref[c * LANES:(c + 1) * LANES, :] = a.astype(bf16)
        iwt_ref[...] = _dot_nt(wt_ref[WT_IW:WT_IW + IW_ROWS, :], h)

    lane = lax.broadcasted_iota(i32, (1, LANES), 1)
    for c in range(KV_COLS // LANES):
        a = _dot(h, w_ref[:, c * LANES:(c + 1) * LANES])
        if is_a:
            a = rope_cols(a)
        k32_ref[:, c * LANES:(c + 1) * LANES] = a
        kpad_ref[0, 2 * c] = jnp.where(lane < HEAD_DIM, a, 0.0).astype(bf16)
        kpad_ref[0, 2 * c + 1] = jnp.where(lane < HEAD_DIM, pltpu.roll(a, HEAD_DIM, 1),
                                           0.0).astype(bf16)
    for c in range(KV_COLS // LANES):
        lo = KV_COLS + c * LANES
        v32_ref[:, c * LANES:(c + 1) * LANES] = _dot(h, w_ref[:, lo:lo + LANES])
    a = _dot(h, w_ref[:, 2 * KV_COLS:2 * KV_COLS + LANES])
    if is_a:
        a = rope_cols(a)
        ik32_ref[...] = a
        ikb_ref[...] = a[:, :IDX_DIM].astype(bf16)
    else:
        z = a + bf_ref[...]
        lf_ref[...] = jnp.minimum(z, 0.0) - jnp.log(1.0 + jnp.exp(-jnp.abs(z)))


def _in_proj_t(x, g, shift, scale, w, wt, tabs, tabs_t, b_f, is_a, tm, t):
    m, d = x.shape
    bsz = m // t
    tpb = t // tm
    col = lambda rows_: pl.BlockSpec((rows_, tm), lambda i: (0, i))
    row = lambda cols_: pl.BlockSpec((tm, cols_), lambda i: (i, 0))
    const = lambda a: pl.BlockSpec(a.shape, lambda i: (0,) * a.ndim)
    mod_spec = pl.BlockSpec((1, 1, d), lambda i: (i // tpb, 0, 0))
    tab_spec = pl.BlockSpec((tm, LANES), lambda i: (i % tpb, 0))
    tabt_spec = pl.BlockSpec((ROPE_HALF, tm), lambda i: (0, i % tpb))
    nq = N_HEADS * HEAD_DIM
    out_specs = [col(nq), col(KV_COLS),
                 pl.BlockSpec((1, N_KV_HEADS, tm, LANES), lambda i: (i // tpb, 0, i % tpb, 0)),
                 row(KV_COLS), row(KV_COLS)]
    out_shape = [jax.ShapeDtypeStruct((nq, m), bf16), jax.ShapeDtypeStruct((KV_COLS, m), bf16),
                 jax.ShapeDtypeStruct((bsz, N_KV_HEADS, t, LANES), bf16),
                 jax.ShapeDtypeStruct((m, KV_COLS), f32), jax.ShapeDtypeStruct((m, KV_COLS), f32)]
    if is_a:
        out_specs += [col(N_IDX_HEADS * IDX_DIM), col(IW_ROWS), row(IDX_DIM), row(LANES)]
        out_shape += [jax.ShapeDtypeStruct((N_IDX_HEADS * IDX_DIM, m), bf16),
                      jax.ShapeDtypeStruct((IW_ROWS, m), f32),
                      jax.ShapeDtypeStruct((m, IDX_DIM), bf16),
                      jax.ShapeDtypeStruct((m, LANES), f32)]
    else:
        out_specs += [row(LANES)]
        out_shape += [jax.ShapeDtypeStruct((m, LANES), f32)]
    return pl.pallas_call(
        functools.partial(_in_proj_t_kernel, is_a=is_a),
        grid=(m // tm,),
        in_specs=[row(d), const(g), mod_spec, mod_spec, const(w), const(wt),
                  tab_spec, tab_spec, tab_spec, tabt_spec, tabt_spec, const(b_f)],
        out_specs=out_specs, out_shape=out_shape,
        compiler_params=_cparams("parallel"),
        name="in_proj_t",
    )(x, g, shift, scale, w, wt, *tabs, *tabs_t, b_f)


def _rope_tables_t(pos):
    inv_freq = ROPE_THETA ** (-jnp.arange(ROPE_HALF, dtype=f32) / ROPE_HALF)
    ang = inv_freq[:, None] * pos.astype(f32)[None, :]
    return jnp.cos(ang), jnp.sin(ang)


AUG_STRIDE = 4
AUG_ONES = GROUP * AUG_STRIDE


SUM_ROWS = 2 * SUBLANES
ACC_ROWS = HEAD_DIM + SUM_ROWS


def _t_init(m_sc, acc_sc):
    m_sc[...] = jnp.full(m_sc.shape, NEG, f32)
    acc_sc[...] = jnp.zeros(acc_sc.shape, f32)


def _t_step(s, vt1, h, m_sc, acc_sc):
    m_old = m_sc[h]
    m_new = jnp.maximum(m_old, jnp.max(s, axis=0, keepdims=True))
    p = jnp.exp2((s - m_new).astype(bf16))
    acc_sc[h] = jnp.exp2(m_old - m_new) * acc_sc[h] + _dot(vt1, p)
    m_sc[h] = m_new


def _t_tile(off, tq, kt_of, vt_ref, qa_sc, post, m_sc, acc_sc):
    kts = [kt_of(g) for g in range(N_KV_HEADS)]
    ones = jnp.ones((SUM_ROWS, tq), bf16)
    vts = [jnp.concatenate([vt_ref[g * HEAD_DIM:(g + 1) * HEAD_DIM, pl.ds(off, tq)], ones], axis=0)
           for g in range(N_KV_HEADS)]
    s_next = _dot(kts[0], qa_sc[0])
    for g in range(N_KV_HEADS):
        s = s_next
        if g + 1 < N_KV_HEADS:
            s_next = _dot(kts[g + 1], qa_sc[g + 1])
        _t_step(post(s), vts[g], g, m_sc, acc_sc)


def _t_fill_q(qa_sc, h, tq, q, aug):
    g, jj = divmod(h, GROUP)
    qa_sc[g, :HEAD_DIM, jj * tq:(jj + 1) * tq] = q
    qa_sc[g, HEAD_DIM:, jj * tq:(jj + 1) * tq] = aug


def _t_finish(tq, acc_sc, ot_sc):
    for h in range(N_HEADS):
        g, jj = divmod(h, GROUP)
        cols = slice(jj * tq, (jj + 1) * tq)
        ot_sc[h * HEAD_DIM:(h + 1) * HEAD_DIM, :] = (
            acc_sc[g, :HEAD_DIM, cols] / acc_sc[g, HEAD_DIM:HEAD_DIM + 1, cols]).astype(bf16)


def _t_out(x_ref, gate_ref, wot_ref, ot_sc, o_ref):
    y_t = _dot(wot_ref[...], ot_sc[...])
    o_ref[...] = x_ref[...] + gate_ref[0] * y_t.T


def _fox_attn_kernel(x_ref, gate_ref, qt_ref, kpad_ref, vt_ref, cpad_ref, ct_ref, wot_ref,
                     o_ref, kaug_sc, m_sc, acc_sc, qa_sc, ot_sc, *, tq, chunk):
    qi = pl.program_id(1)
    t = kaug_sc.shape[1]

    @pl.when(qi == 0)
    def _():
        r = lax.broadcasted_iota(i32, (LANES, LANES), 0)
        l = lax.broadcasted_iota(i32, (LANES, LANES), 1) - HEAD_DIM
        lane = lax.broadcasted_iota(i32, (1, LANES), 1) - HEAD_DIM
        ones = jnp.where((lane >= AUG_ONES) & (lane < AUG_ONES + 3), 1.0, 0.0)
        for g in range(N_KV_HEADS):
            perm = [((l >= 0) & (l < AUG_ONES) & ((l >> 2) == r - g * GROUP)
                     & ((l & 3) == p)).astype(bf16) for p in range(3)]

            def body(i, c, g=g, perm=perm):
                off = pl.multiple_of(i * chunk, chunk)
                pieces = _split3(cpad_ref[0, pl.ds(off, chunk), :] * -LOG2E)
                aug = sum(_dot(pc, pm) for pc, pm in zip(pieces, perm)) + ones
                kaug_sc[g, pl.ds(off, chunk), :] = (
                    kpad_ref[0, g, pl.ds(off, chunk), :].astype(f32) + aug).astype(bf16)
                return c

            lax.fori_loop(0, t // chunk, body, 0)

    ri = lax.broadcasted_iota(i32, (HEAD_DIM, tq), 0)
    wide = (tq, GROUP * tq)
    d_iota = (lax.broadcasted_iota(i32, wide, 0)
              - (lax.broadcasted_iota(i32, wide, 1) & (tq - 1)))
    for h in range(N_HEADS):
        jj = h % GROUP
        cq = _split3(ct_ref[0, h:h + 1, :] * LOG2E)
        aug = jnp.where((ri >= jj * AUG_STRIDE) & (ri < jj * AUG_STRIDE + 3), 1.0, 0.0)
        for p in range(3):
            aug = jnp.where(ri == AUG_ONES + p, cq[p].astype(f32), aug)
        _t_fill_q(qa_sc, h, tq, qt_ref[h * HEAD_DIM:(h + 1) * HEAD_DIM, :], aug.astype(bf16))
    _t_init(m_sc, acc_sc)

    def tile(j, post):
        off = pl.multiple_of(j * tq, tq)
        _t_tile(off, tq, lambda g: kaug_sc[g, pl.ds(off, tq), :], vt_ref, qa_sc, post,
                m_sc, acc_sc)

    def body(j, c):
        tile(j, lambda s: s)
        return c

    lax.fori_loop(0, qi, body, 0)
    tile(qi, lambda s: jnp.where(d_iota <= 0, s, NEG))
    _t_finish(tq, acc_sc, ot_sc)
    _t_out(x_ref, gate_ref, wot_ref, ot_sc, o_ref)


def _attn_specs(bsz, t, d, tq):
    nq = t // tq
    return dict(
        x=pl.BlockSpec((tq, d), lambda b, q: (b * nq + q, 0)),
        gate=pl.BlockSpec((1, 1, d), lambda b, q: (b, 0, 0)),
        qt=pl.BlockSpec((N_HEADS * HEAD_DIM, tq), lambda b, q: (0, b * nq + q)),
        kpad=pl.BlockSpec((1, N_KV_HEADS, t, LANES), lambda b, q: (b, 0, 0, 0)),
        vt=pl.BlockSpec((KV_COLS, t), lambda b, q: (0, b)),
        wot=pl.BlockSpec((d, N_HEADS * HEAD_DIM), lambda b, q: (0, 0)),
        scratch=[pltpu.VMEM((N_KV_HEADS, 1, GROUP * tq), f32),
                 pltpu.VMEM((N_KV_HEADS, ACC_ROWS, GROUP * tq), f32),
                 pltpu.VMEM((N_KV_HEADS, LANES, GROUP * tq), bf16),
                 pltpu.VMEM((N_HEADS * HEAD_DIM, tq), bf16)])


def _fox_attn(x, gate, qt, kpad, vt, c_pad, c_t, wot, tq):
    bsz, _, t, _ = kpad.shape
    m, d = x.shape
    sp = _attn_specs(bsz, t, d, tq)
    return pl.pallas_call(
        functools.partial(_fox_attn_kernel, tq=tq, chunk=512),
        grid=(bsz, t // tq),
        in_specs=[sp["x"], sp["gate"], sp["qt"], sp["kpad"], sp["vt"],
                  pl.BlockSpec((1, t, LANES), lambda b, q: (b, 0, 0)),
                  pl.BlockSpec((1, N_HEADS, tq), lambda b, q: (b, 0, q)),
                  sp["wot"]],
        out_specs=sp["x"],
        out_shape=jax.ShapeDtypeStruct((m, d), f32),
        scratch_shapes=[pltpu.VMEM((N_KV_HEADS, t, LANES), bf16)] + sp["scratch"],
        compiler_params=_cparams("parallel", "arbitrary"),
        name="fox_attn",
    )(x, gate, qt, kpad, vt, c_pad, c_t, wot)


def _head_rows_t(x):
    eye = (lax.broadcasted_iota(i32, (N_HEADS, x.shape[1]), 0)
           == lax.broadcasted_iota(i32, (N_HEADS, x.shape[1]), 1)).astype(bf16)
    return sum(_dot_nt(eye, p) for p in _split3(x))


def _cumsum_t_kernel(lf_ref, ct_ref, carry_sc, *, tb):
    @pl.when(pl.program_id(1) == 0)
    def _():
        carry_sc[...] = jnp.zeros(carry_sc.shape, f32)

    tri = (lax.broadcasted_iota(i32, (tb, tb), 0)
           <= lax.broadcasted_iota(i32, (tb, tb), 1)).astype(bf16)
    lft = _head_rows_t(lf_ref[0])
    csum = sum(_dot(p, tri) for p in _split3(lft))
    ct_ref[0] = csum + carry_sc[...]
    carry_sc[...] = carry_sc[...] + jnp.sum(lft, axis=1, keepdims=True)


def _cumsum_t(lf_pad, tb):
    b, t, _ = lf_pad.shape
    return pl.pallas_call(
        functools.partial(_cumsum_t_kernel, tb=tb),
        grid=(b, t // tb),
        in_specs=[pl.BlockSpec((1, tb, LANES), lambda bi, j: (bi, j, 0))],
        out_specs=pl.BlockSpec((1, N_HEADS, tb), lambda bi, j: (bi, 0, j)),
        out_shape=jax.ShapeDtypeStruct((b, N_HEADS, t), f32),
        scratch_shapes=[pltpu.VMEM((N_HEADS, 1), f32)],
        compiler_params=_cparams("parallel", "arbitrary"),
        name="cumsum_t",
    )(lf_pad)


def _kth_threshold(count_ge, shape, n_sel):
    zero = jnp.zeros(shape, i32)
    ans = jnp.where(count_ge(zero) >= n_sel, zero, jnp.full(shape, INT_MIN, i32))

    def bit_body(i, ans):
        cand = ans | jnp.left_shift(jnp.int32(1), 30 - i)
        return jnp.where(count_ge(cand) >= n_sel, cand, ans)

    return lax.fori_loop(0, 31, bit_body, ans)


def _kth_threshold_early(count_ge, shape, n_sel, n_total):
    zero = jnp.zeros(shape, i32)
    c0 = count_ge(zero)
    ans = jnp.where(c0 >= n_sel, zero, jnp.full(shape, INT_MIN, i32))
    exact = jnp.where((c0 == n_sel) | ((c0 < n_sel) & (n_total == n_sel)), 1, 0)

    def cond(state):
        i, _, _, n_open = state
        return (i < 31) & (n_open > 0)

    def body(state):
        i, ans, exact, _ = state
        cand = ans | jnp.left_shift(jnp.int32(1), 30 - i)
        c = count_ge(cand)
        take = (exact == 0) & (c >= n_sel)
        ans = jnp.where(take, cand, ans)
        exact = jnp.where(take & (c == n_sel), 1, exact)
        return i + 1, ans, exact, jnp.sum(1 - exact)

    _, ans, exact, _ = lax.while_loop(cond, body, (jnp.int32(0), ans, exact, jnp.sum(1 - exact)))
    return ans, exact


def _tie_cut(count_tied_below, need, shape, n_bits):
    def body(i, p):
        cand = p | jnp.left_shift(jnp.int32(1), n_bits - 1 - i)
        return jnp.where(count_tied_below(cand) < need, cand, p)
    return lax.fori_loop(0, n_bits, body, jnp.zeros(shape, i32))


def _fold_lanes(ind, c):
    for cc in range(ind.shape[1] // LANES):
        c = c + ind[:, cc * LANES:(cc + 1) * LANES]
    return c


def _dsa_attn_kernel(x_ref, gate_ref, qt_ref, kpad_ref, vt_ref, iqt_ref, iwt_ref, ik_ref,
                     wot_ref, o_ref, key_sc, bias_sc, cut_sc, m_sc, acc_sc, qa_sc, ot_sc,
                     *, tq, n_sel, n_bits):
    qi = pl.program_id(1)
    n_k = qi + 1
    row = lax.broadcasted_iota(i32, (tq, tq), 0)
    d_iota = row - lax.broadcasted_iota(i32, (tq, tq), 1)

    def score_body(j, c):
        off = pl.multiple_of(j * tq, tq)
        ikt = ik_ref[0, pl.ds(off, tq), :]
        s = jnp.zeros((tq, tq), f32)
        for h in range(N_IDX_HEADS):
            dots = _dot(ikt, iqt_ref[h * IDX_DIM:(h + 1) * IDX_DIM, :])
            s = s + jnp.maximum(dots, 0.0) * iwt_ref[h:h + 1, :]
        key_sc[pl.ds(off, tq), :] = jnp.where(d_iota <= (qi - j) * tq, _sort_key(s),
                                              KEY_NEG_INF)
        return c

    lax.fori_loop(0, n_k, score_body, 0)

    def count(pred):
        def body(j, c):
            off = pl.multiple_of(j * tq, tq)
            ind = jnp.where(pred(key_sc[pl.ds(off, tq), :], j), 1, 0)
            return c + jnp.sum(ind, axis=0, keepdims=True)
        return lax.fori_loop(0, n_k, body, jnp.zeros((1, tq), i32))

    thr, exact = _kth_threshold_early(lambda cand: count(lambda kt, j: kt >= cand), (1, tq),
                                      n_sel, n_k * tq)
    real = thr > KEY_NEG_INF
    cut_sc[...] = jnp.where(real, INT_MAX, -1)

    @pl.when(jnp.max(jnp.where(real, 1 - exact, 0)) > 0)
    def _():
        need = n_sel - count(lambda kt, j: kt > thr)
        cut = _tie_cut(lambda cand: count(lambda kt, j: (kt == thr) & (row + j * tq < cand)),
                       need, (1, tq), n_bits)
        cut_sc[...] = jnp.where(real, jnp.where(exact == 1, INT_MAX, cut), -1)

    cut = cut_sc[...]
    floor = jnp.maximum(thr, KEY_NEG_INF)

    def bias_body(j, c):
        off = pl.multiple_of(j * tq, tq)
        kt = key_sc[pl.ds(off, tq), :]
        sel = (kt > floor) | ((kt == thr) & (row + j * tq <= cut))
        bias_sc[pl.ds(off, tq), :] = jnp.where(sel, 0.0, NEG)
        return c

    lax.fori_loop(0, n_k, bias_body, 0)

    for h in range(N_HEADS):
        _t_fill_q(qa_sc, h, tq, qt_ref[h * HEAD_DIM:(h + 1) * HEAD_DIM, :],
                  jnp.zeros((LANES - HEAD_DIM, tq), bf16))
    _t_init(m_sc, acc_sc)

    def body(j, c):
        off = pl.multiple_of(j * tq, tq)

        def masked(s):
            return jnp.concatenate([s[:, jj * tq:(jj + 1) * tq] + bias_sc[pl.ds(off, tq), :]
                                    for jj in range(GROUP)], axis=1)

        _t_tile(off, tq, lambda g: kpad_ref[0, g, pl.ds(off, tq), :], vt_ref, qa_sc,
                masked, m_sc, acc_sc)
        return c

    lax.fori_loop(0, n_k, body, 0)
    _t_finish(tq, acc_sc, ot_sc)
    _t_out(x_ref, gate_ref, wot_ref, ot_sc, o_ref)


def _dsa_attn(x, gate, qt, kpad, vt, iqt, iwt, ik, wot, tq, n_sel):
    bsz, _, t, _ = kpad.shape
    m, d = x.shape
    nq = t // tq
    sp = _attn_specs(bsz, t, d, tq)
    return pl.pallas_call(
        functools.partial(_dsa_attn_kernel, tq=tq, n_sel=n_sel, n_bits=(t - 1).bit_length()),
        grid=(bsz, nq),
        in_specs=[sp["x"], sp["gate"], sp["qt"], sp["kpad"], sp["vt"],
                  pl.BlockSpec((N_IDX_HEADS * IDX_DIM, tq), lambda b, q: (0, b * nq + q)),
                  pl.BlockSpec((2 * SUBLANES, tq), lambda b, q: (0, b * nq + q)),
                  pl.BlockSpec((1, t, IDX_DIM), lambda b, q: (b, 0, 0)),
                  sp["wot"]],
        out_specs=sp["x"],
        out_shape=jax.ShapeDtypeStruct((m, d), f32),
        scratch_shapes=[pltpu.VMEM((t, tq), i32), pltpu.VMEM((t, tq), f32),
                        pltpu.VMEM((1, tq), i32)] + sp["scratch"],
        compiler_params=_cparams("parallel", "arbitrary"),
        name="dsa_attn",
    )(x, gate, qt, kpad, vt, iqt, iwt, ik, wot)


def _out_proj_kernel(x_ref, o_ref, w_ref, gate_ref, y_ref):
    y_ref[...] = x_ref[...] + gate_ref[0] * _dot(o_ref[...], w_ref[...])


def _out_proj(x, o, w, gate, tm, rows_per_mod):
    m, d = x.shape
    kdim = o.shape[1]
    r = gate.shape[1]
    return pl.pallas_call(
        _out_proj_kernel,
        grid=(m // tm,),
        in_specs=[pl.BlockSpec((tm, d), lambda i: (i, 0)),
                  pl.BlockSpec((tm, kdim), lambda i: (i, 0)),
                  pl.BlockSpec((kdim, d), lambda i: (0, 0)),
                  pl.BlockSpec((1, r, d), lambda i: (i * tm // rows_per_mod, 0, 0))],
        out_specs=pl.BlockSpec((tm, d), lambda i: (i, 0)),
        out_shape=jax.ShapeDtypeStruct((m, d), f32),
        compiler_params=_cparams("parallel"),
        name="out_proj",
    )(x, o, w, gate)


FFN_CHUNK = 256


def _ffn_kernel(x_ref, xh_ref, g_ref, shift_ref, scale_ref, gate_ref, wa_ref, wb_ref,
                cw_ref, cb_ref, wd_ref, o_ref, tail_ref, acc_sc, *, tm, tiles_per_seq):
    i = pl.program_id(0)
    x = x_ref[...]
    h = _norm_mod(x, g_ref[...], shift_ref[0], scale_ref[0]).astype(bf16)
    hh = _norm_mod(xh_ref[...], g_ref[...], shift_ref[0], scale_ref[0]).astype(bf16)
    keep = jnp.where(i % tiles_per_seq == 0, 0.0, 1.0)
    row = lax.broadcasted_iota(i32, (tm, FFN_CHUNK), 0)
    for c in range(D_FF // FFN_CHUNK):
        sl = slice(c * FFN_CHUNK, (c + 1) * FFN_CHUNK)
        a = _dot(h, wa_ref[:, sl])
        b = _dot(h, wb_ref[:, sl])
        ah = _dot(hh, wa_ref[:, sl]) * keep
        last = ah[SUBLANES - 1:SUBLANES]
        p1 = jnp.where(row == 0, last, pltpu.roll(a, 1, 0))
        p2 = jnp.where(row == 0, ah[SUBLANES - 2:SUBLANES - 1],
                       jnp.where(row == 1, last, pltpu.roll(a, 2, 0)))
        conv = cb_ref[:, sl] + p2 * cw_ref[0:1, sl] + p1 * cw_ref[1:2, sl] + a * cw_ref[2:3, sl]
        y = _dot((_silu(conv) * b).astype(bf16), wd_ref[sl, :])
        if c == 0:
            acc_sc[...] = y
        else:
            acc_sc[...] += y
        tail_ref[0, :, sl] = a[tm - SUBLANES:tm]
    o_ref[...] = x + gate_ref[0] * acc_sc[...]


def _ffn(x, g, shift, scale, gate, wa, wb, cw, cb, wd, tm, rows_per_seq):
    m, d = x.shape
    n_tiles = m // tm
    hb = tm // SUBLANES
    mod_spec = pl.BlockSpec((1, 1, d), lambda i: (i * tm // rows_per_seq, 0, 0))
    const = lambda shape: pl.BlockSpec(shape, lambda i: (0,) * len(shape))
    return pl.pallas_call(
        functools.partial(_ffn_kernel, tm=tm, tiles_per_seq=rows_per_seq // tm),
        grid=(n_tiles,),
        in_specs=[pl.BlockSpec((tm, d), lambda i: (i, 0)),
                  pl.BlockSpec((SUBLANES, d), lambda i: (jnp.maximum(i * hb - 1, 0), 0)),
                  const((1, d)), mod_spec, mod_spec, mod_spec,
                  const((d, D_FF)), const((d, D_FF)), const((SUBLANES, D_FF)),
                  const((1, D_FF)), const((D_FF, d))],
        out_specs=[pl.BlockSpec((tm, d), lambda i: (i, 0)),
                   pl.BlockSpec((1, SUBLANES, D_FF), lambda i: (i, 0, 0))],
        out_shape=[jax.ShapeDtypeStruct((m, d), f32),
                   jax.ShapeDtypeStruct((n_tiles, SUBLANES, D_FF), f32)],
        scratch_shapes=[pltpu.VMEM((tm, d), f32)],
        compiler_params=_cparams("parallel"),
        name="ffn",
    )(x, x, g, shift, scale, gate, wa, wb, cw, cb, wd)


def _ffn_sample_kernel(x_ref, g_ref, shift_ref, scale_ref, gate_ref, p2_ref, p1_ref, wa_ref,
                       wb_ref, cw_ref, cb_ref, wd_ref, o_ref, a_ref, acc_sc):
    j = pl.program_id(0)

    @pl.when(j == 0)
    def _():
        acc_sc[...] = jnp.zeros(acc_sc.shape, f32)

    x = x_ref[...]
    h = _norm_mod(x, g_ref[...], shift_ref[0], scale_ref[0]).astype(bf16)
    a = _dot(h, wa_ref[...])
    b = _dot(h, wb_ref[...])
    conv = (cb_ref[...] + p2_ref[...] * cw_ref[0:1, :] + p1_ref[...] * cw_ref[1:2, :]
            + a * cw_ref[2:3, :])
    acc_sc[...] += _dot((_silu(conv) * b).astype(bf16), wd_ref[...])
    a_ref[...] = a

    @pl.when(j == pl.num_programs(0) - 1)
    def _():
        o_ref[...] = x + gate_ref[0] * acc_sc[...]


def _ffn_sample(x, g, shift, scale, gate, p2, p1, wa, wb, cw, cb, wd, tn):
    m, d = x.shape
    const = lambda shape: pl.BlockSpec(shape, lambda j: (0,) * len(shape))
    return pl.pallas_call(
        _ffn_sample_kernel,
        grid=(D_FF // tn,),
        in_specs=[const((m, d)), const((1, d)), const((1, m, d)), const((1, m, d)),
                  const((1, m, d)),
                  pl.BlockSpec((m, tn), lambda j: (0, j)), pl.BlockSpec((m, tn), lambda j: (0, j)),
                  pl.BlockSpec((d, tn), lambda j: (0, j)), pl.BlockSpec((d, tn), lambda j: (0, j)),
                  pl.BlockSpec((SUBLANES, tn), lambda j: (0, j)),
                  pl.BlockSpec((1, tn), lambda j: (0, j)),
                  pl.BlockSpec((tn, d), lambda j: (j, 0))],
        out_specs=[const((m, d)), pl.BlockSpec((m, tn), lambda j: (0, j))],
        out_shape=[jax.ShapeDtypeStruct((m, d), f32), jax.ShapeDtypeStruct((m, D_FF), f32)],
        scratch_shapes=[pltpu.VMEM((m, d), f32)],
        compiler_params=_cparams("arbitrary"),
        name="ffn_sample",
    )(x, g, shift, scale, gate, p2, p1, wa, wb, cw, cb, wd)


def _final_norm_kernel(x_ref, g_ref, o_ref):
    x = x_ref[...]
    o_ref[...] = x * lax.rsqrt(jnp.mean(x * x, axis=-1, keepdims=True) + EPS) * g_ref[...]


def _final_norm(x, g, tm):
    m, d = x.shape
    return pl.pallas_call(
        _final_norm_kernel,
        grid=(m // tm,),
        in_specs=[pl.BlockSpec((tm, d), lambda i: (i, 0)), pl.BlockSpec((1, d), lambda i: (0, 0))],
        out_specs=pl.BlockSpec((tm, d), lambda i: (i, 0)),
        out_shape=jax.ShapeDtypeStruct((m, d), f32),
        compiler_params=_cparams("parallel"),
        name="final_norm",
    )(x, g)


def _seq_page_copies(pt_ref, hbm, layer, seq, buf, slot, sem, n_pages):
    return [pltpu.make_async_copy(hbm.at[layer, pt_ref[seq, p]],
                                  buf.at[slot, :, pl.ds(p * PAGE_SIZE, PAGE_SIZE)], sem.at[slot])
            for p in range(n_pages)]


def _gather_seq(pt_ref, hbm, layer, buf, sem, n_pages):
    b = pl.program_id(0)
    slot = b % 2

    @pl.when(b == 0)
    def _():
        for cp in _seq_page_copies(pt_ref, hbm, layer, 0, buf, 0, sem, n_pages):
            cp.start()

    @pl.when(b + 1 < pl.num_programs(0))
    def _():
        for cp in _seq_page_copies(pt_ref, hbm, layer, b + 1, buf, 1 - slot, sem, n_pages):
            cp.start()

    for cp in _seq_page_copies(pt_ref, hbm, layer, b, buf, slot, sem, n_pages):
        cp.wait()
    return slot


def _sample_score_kernel(pt_ref, iq_ref, iw_ref, kidx_hbm, s_ref, buf, sem, *, layer, n_pages):
    slot = _gather_seq(pt_ref, kidx_hbm, layer, buf, sem, n_pages)
    d = _dot(iq_ref[0], buf[slot].astype(bf16))
    s_ref[0] = jnp.sum(jnp.maximum(d, 0.0) * iw_ref[0], axis=0, keepdims=True)


def _sample_scores(page_table, iq, iw, cache_kidx_t, layer):
    s, n_pages = page_table.shape
    length = n_pages * PAGE_SIZE
    grid_spec = pltpu.PrefetchScalarGridSpec(
        num_scalar_prefetch=1, grid=(s,),
        in_specs=[pl.BlockSpec((1, N_IDX_HEADS, IDX_DIM), lambda b, pt: (b, 0, 0)),
                  pl.BlockSpec((1, N_IDX_HEADS, 1), lambda b, pt: (b, 0, 0)),
                  pl.BlockSpec(memory_space=pl.ANY)],
        out_specs=pl.BlockSpec((1, 1, length), lambda b, pt: (b, 0, 0)),
        scratch_shapes=[pltpu.VMEM((2, IDX_DIM, length), f32), pltpu.SemaphoreType.DMA((2,))])
    out = pl.pallas_call(
        functools.partial(_sample_score_kernel, layer=layer, n_pages=n_pages),
        grid_spec=grid_spec,
        out_shape=jax.ShapeDtypeStruct((s, 1, length), f32),
        compiler_params=_cparams("arbitrary"),
        name="sample_scores",
    )(page_table, iq, iw, cache_kidx_t)
    return out.reshape(s, length)


SEL_TILE = 1024


def _sample_select_kernel(s_ref, iq_ref, ikt_ref, iw_ref, bias_ref, bnew_ref, key_sc,
                          *, n_sel, n_bits):
    rows, length = s_ref.shape
    n_tiles = length // SEL_TILE
    prod = iq_ref[...].astype(f32) * ikt_ref[...].astype(f32)
    grp = (lax.broadcasted_iota(i32, (prod.shape[1], LANES), 0) // IDX_DIM
           == lax.broadcasted_iota(i32, (prod.shape[1], LANES), 1)).astype(bf16)
    d_new = sum(_dot(p, grp) for p in _split3(prod))
    s_new = jnp.sum(jnp.maximum(d_new, 0.0) * iw_ref[...], axis=1, keepdims=True)
    key_new = _sort_key(s_new)
    key_sc[...] = _sort_key(s_ref[...])
    col = lax.broadcasted_iota(i32, (rows, SEL_TILE), 1)

    def count(pred, pred_new):
        def body(j, c):
            off = pl.multiple_of(j * SEL_TILE, SEL_TILE)
            return _fold_lanes(jnp.where(pred(key_sc[:, pl.ds(off, SEL_TILE)], j), 1, 0), c)
        c = lax.fori_loop(0, n_tiles, body, jnp.zeros((rows, LANES), i32))
        return jnp.sum(c, axis=1, keepdims=True) + jnp.where(pred_new, 1, 0)

    thr = _kth_threshold(lambda cand: count(lambda kt, j: kt >= cand, key_new >= cand),
                         (rows, 1), n_sel)
    need = n_sel - count(lambda kt, j: kt > thr, key_new > thr)
    cut = _tie_cut(lambda cand: count(lambda kt, j: (kt == thr) & (col + j * SEL_TILE < cand),
                                      (key_new == thr) & (length < cand)),
                   need, (rows, 1), n_bits)

    def bias_body(j, c):
        off = pl.multiple_of(j * SEL_TILE, SEL_TILE)
        kt = key_sc[:, pl.ds(off, SEL_TILE)]
        sel = (kt > thr) | ((kt == thr) & (col + j * SEL_TILE <= cut))
        bias_ref[:, pl.ds(off, SEL_TILE)] = jnp.where(sel, 0.0, NEG)
        return c

    lax.fori_loop(0, n_tiles, bias_body, 0)
    sel_new = (key_new > thr) | ((key_new == thr) & (length <= cut))
    bnew_ref[...] = jnp.broadcast_to(jnp.where(sel_new, 0.0, NEG), bnew_ref.shape)


def _sample_select(scores, iq, ik_tiled, iw, n_sel):
    rows, length = scores.shape
    return pl.pallas_call(
        functools.partial(_sample_select_kernel, n_sel=n_sel, n_bits=length.bit_length()),
        out_shape=[jax.ShapeDtypeStruct((rows, length), f32),
                   jax.ShapeDtypeStruct((rows, LANES), f32)],
        scratch_shapes=[pltpu.VMEM((rows, length), i32)],
        compiler_params=pltpu.CompilerParams(vmem_limit_bytes=VMEM_LIMIT),
        name="sample_select",
    )(scores, iq, ik_tiled, iw)


SUFFIX_CHUNK = 256


def _suffix_bias_kernel(pt_ref, lfnew_ref, lf_hbm, bias_ref, buf, sem, *, layer, n_pages):
    slot = _gather_seq(pt_ref, lf_hbm, layer, buf, sem, n_pages)
    w = SUFFIX_CHUNK
    tri = (lax.broadcasted_iota(i32, (w, w), 0)
           > lax.broadcasted_iota(i32, (w, w), 1)).astype(bf16)
    carry = lfnew_ref[0]
    for c in reversed(range(n_pages * PAGE_SIZE // w)):
        x = buf[slot, :, c * w:(c + 1) * w]
        bias_ref[0, :, c * w:(c + 1) * w] = sum(_dot(p, tri) for p in _split3(x)) + carry
        carry = carry + jnp.sum(x, axis=1, keepdims=True)


def _suffix_bias(page_table, lf_new, cache_logf_t, layer):
    s, n_pages = page_table.shape
    length = n_pages * PAGE_SIZE
    grid_spec = pltpu.PrefetchScalarGridSpec(
        num_scalar_prefetch=1, grid=(s,),
        in_specs=[pl.BlockSpec((1, N_HEADS, 1), lambda b, pt: (b, 0, 0)),
                  pl.BlockSpec(memory_space=pl.ANY)],
        out_specs=pl.BlockSpec((1, N_HEADS, length), lambda b, pt: (b, 0, 0)),
        scratch_shapes=[pltpu.VMEM((2, N_HEADS, length), f32), pltpu.SemaphoreType.DMA((2,))])
    return pl.pallas_call(
        functools.partial(_suffix_bias_kernel, layer=layer, n_pages=n_pages),
        grid_spec=grid_spec,
        out_shape=jax.ShapeDtypeStruct((s, N_HEADS, length), f32),
        compiler_params=_cparams("arbitrary"),
        name="suffix_bias",
    )(page_table, lf_new, cache_logf_t)


DEC_PAGES = 8


def _kv_chunk_copies(pt_ref, k_hbm, v_hbm, layer, seq, chunk, kbuf, vbuf, slot, sem):
    cps = []
    for i in range(DEC_PAGES):
        page = pt_ref[seq, chunk * DEC_PAGES + i]
        cols = pl.ds(i * PAGE_SIZE, PAGE_SIZE)
        cps.append(pltpu.make_async_copy(k_hbm.at[layer, page], kbuf.at[slot, :, cols],
                                         sem.at[0, slot]))
        cps.append(pltpu.make_async_copy(v_hbm.at[layer, page], vbuf.at[slot, :, cols],
                                         sem.at[1, slot]))
    return cps


def _decode_attn_kernel(pt_ref, q_ref, knew_ref, vnew_ref, bnew_ref, bias_ref, k_hbm, v_hbm,
                        o_ref, kbuf, vbuf, sem, *, layer, n_chunks):
    b = pl.program_id(0)
    q = q_ref[0]
    width = DEC_PAGES * PAGE_SIZE
    copies = functools.partial(_kv_chunk_copies, pt_ref, k_hbm, v_hbm, layer)

    @pl.when(b == 0)
    def _():
        for cp in copies(0, 0, kbuf, vbuf, 0, sem):
            cp.start()

    kn = knew_ref[0].astype(bf16).astype(f32)
    m = jnp.sum(q.astype(f32) * kn, axis=1, keepdims=True) + bnew_ref[0][:, 0:1]
    l = jnp.ones_like(m)
    acc = jnp.broadcast_to(vnew_ref[0].astype(bf16).astype(f32), (N_HEADS, KV_COLS))

    for c in range(n_chunks):
        slot = c % 2
        if c + 1 < n_chunks:
            for cp in copies(b, c + 1, kbuf, vbuf, 1 - slot, sem):
                cp.start()
        else:
            @pl.when(b + 1 < pl.num_programs(0))
            def _():
                for cp in copies(b + 1, 0, kbuf, vbuf, 1 - slot, sem):
                    cp.start()
        for cp in copies(b, c, kbuf, vbuf, slot, sem):
            cp.wait()
        s = _dot(q, kbuf[slot].astype(bf16)) + bias_ref[0, :, c * width:(c + 1) * width]
        m_new = jnp.maximum(m, jnp.max(s, axis=1, keepdims=True))
        p = jnp.exp(s - m_new)
        alpha = jnp.exp(m - m_new)
        l = alpha * l + jnp.sum(p, axis=1, keepdims=True)
        acc = alpha * acc + _dot_nt(p.astype(bf16), vbuf[slot].astype(bf16))
        m = m_new
    o_ref[0] = acc / l


def _decode_attn(page_table, qmat, k_new, v_new, bias_new, bias, cache_kt, cache_vt, layer):
    s, n_pages = page_table.shape
    hb = bias.shape[1]
    n_chunks = n_pages // DEC_PAGES
    assert n_chunks * DEC_PAGES == n_pages and n_chunks % 2 == 0
    row = lambda shape: pl.BlockSpec(shape, lambda b, pt: (b, 0, 0))
    hbm = pl.BlockSpec(memory_space=pl.ANY)
    width = DEC_PAGES * PAGE_SIZE
    grid_spec = pltpu.PrefetchScalarGridSpec(
        num_scalar_prefetch=1, grid=(s,),
        in_specs=[row((1, N_HEADS, KV_COLS)), row((1, 1, KV_COLS)), row((1, 1, KV_COLS)),
                  row((1, hb, LANES)), row((1, hb, n_pages * PAGE_SIZE)), hbm, hbm],
        out_specs=row((1, N_HEADS, KV_COLS)),
        scratch_shapes=[pltpu.VMEM((2, KV_COLS, width), f32), pltpu.VMEM((2, KV_COLS, width), f32),
                        pltpu.SemaphoreType.DMA((2, 2))])
    return pl.pallas_call(
        functools.partial(_decode_attn_kernel, layer=layer, n_chunks=n_chunks),
        grid_spec=grid_spec,
        out_shape=jax.ShapeDtypeStruct((s, N_HEADS, KV_COLS), f32),
        compiler_params=_cparams("arbitrary"),
        name="decode_attn",
    )(page_table, qmat, k_new, v_new, bias_new, bias, cache_kt, cache_vt)


def _pad_cols(w, n):
    return jnp.pad(w, ((0, 0), (0, n)))


def _decode_q(q_bf):
    s = q_bf.shape[0]
    onehot = (jnp.arange(N_HEADS)[:, None] // GROUP == jnp.arange(N_KV_HEADS)[None, :])
    q = q_bf.reshape(s, N_HEADS, 1, HEAD_DIM) * onehot[None, :, :, None].astype(q_bf.dtype)
    return q.reshape(s, N_HEADS, KV_COLS)


def _decode_o(o_full):
    s = o_full.shape[0]
    o = o_full.reshape(s, N_KV_HEADS, GROUP, N_KV_HEADS, HEAD_DIM)
    return jnp.stack([o[:, g, :, g, :] for g in range(N_KV_HEADS)], axis=1).reshape(s, -1)


def kernel(x_prompt, x_sample, cache_k_a, cache_v_a, cache_kidx_a, cache_k_b, cache_v_b,
           cache_logf_b, state_conv, page_table, c_prompt, c_sample, w_ada, b_ada, g_mix,
           g_ffn, w_in_a, w_o_a, w_in_b, b_f, w_o_b, w_up, conv_w, conv_b, w_down, g_final):
    bsz, t, d = x_prompt.shape
    s = x_sample.shape[0]
    depth = w_ada.shape[0]
    m = bsz * t
    past_len = page_table.shape[1] * PAGE_SIZE
    n_pool = cache_k_a.shape[1]
    tm = 512
    tq = 256

    pad_rows = (-(bsz + s)) % SUBLANES
    c_all = jnp.concatenate([c_prompt, c_sample, jnp.zeros((pad_rows, d), f32)], axis=0)
    mods = _adaln(c_all, w_ada, b_ada)
    mods_p = mods[:, :bsz].reshape(depth, bsz, 6, d)
    mods_s = mods[:, bsz:bsz + s].reshape(depth, s, 6, d)

    tabs_p = _rope_tables(jnp.arange(t))
    tabs_pt = _rope_tables_t(jnp.arange(t))
    tabs_s = _rope_tables(jnp.full((s,), past_len))
    zero_bf = jnp.zeros((1, LANES), f32)

    xp = x_prompt.reshape(m, d)
    xs = x_sample.reshape(s, d)
    kv_shape_p = (bsz, t, N_KV_HEADS, HEAD_DIM)
    kv_shape_s = (s, 1, N_KV_HEADS, HEAD_DIM)
    ka_p, va_p, ia_p, kb_p, vb_p, lb_p, cv_p = [], [], [], [], [], [], []
    ka_s, va_s, ia_s, kb_s, vb_s, lb_s, cv_s = [], [], [], [], [], [], []

    def pages_t(cache):
        ct = jnp.moveaxis(cache, 2, -1)
        return ct.reshape(ct.shape[:2] + (-1, PAGE_SIZE))

    cache_k_a, cache_v_a, cache_kidx_a = pages_t(cache_k_a), pages_t(cache_v_a), pages_t(cache_kidx_a)
    cache_k_b, cache_v_b, cache_logf_b = pages_t(cache_k_b), pages_t(cache_v_b), pages_t(cache_logf_b)

    for i in range(depth):
        j = i // 2
        mp = [mods_p[i, :, c][:, None, :] for c in range(6)]
        ms = [mods_s[i, :, c][None] for c in range(6)]
        g_m = g_mix[i][None]
        if i % 2 == 0:
            w = w_in_a[j]
            n_qkvi = QKV_COLS + N_IDX_HEADS * IDX_DIM + IDX_DIM
            w_in = jnp.concatenate([_pad_cols(w[:, :n_qkvi], LANES - IDX_DIM),
                                    _pad_cols(w[:, n_qkvi:], LANES - N_IDX_HEADS)],
                                   axis=1).astype(bf16)
            w_o = w_o_a[j].astype(bf16)
            nq = N_HEADS * HEAD_DIM
            n_iq = QKV_COLS + N_IDX_HEADS * IDX_DIM
            w_row = jnp.concatenate([w[:, nq:QKV_COLS], _pad_cols(w[:, n_iq:n_qkvi], LANES - IDX_DIM)],
                                    axis=1).astype(bf16)
            w_t = jnp.concatenate([w[:, :nq], w[:, nq + KV_COLS:QKV_COLS], w[:, QKV_COLS:n_iq],
                                   _pad_cols(w[:, n_qkvi:], IW_ROWS - N_IDX_HEADS)],
                                  axis=1).T.astype(bf16)
            qt, vt, kpad, k32, v32, iqt, iwt, ikb, ik32 = _in_proj_t(
                xp, g_m, mp[0], mp[1], w_row, w_t, tabs_p, tabs_pt, zero_bf, True, tm, t)
            xp = _dsa_attn(xp, mp[2], qt, kpad, vt, iqt, iwt, ikb.reshape(bsz, t, IDX_DIM),
                           w_o.T, tq, min(TOPK_MAX, t // 4))
            ka_p.append(k32.reshape(kv_shape_p))
            va_p.append(v32.reshape(kv_shape_p))
            ia_p.append(ik32[:, :IDX_DIM].reshape(bsz, t, IDX_DIM))
            q, k32, kb, v32, vb, iq, ik32, ikb, iw = _in_proj(
                xs, g_m, ms[0], ms[1], w_in, tabs_s, zero_bf, SEGS_A, OUTS_A, s, s)
            scores = _sample_scores(page_table, iq.reshape(s, N_IDX_HEADS, IDX_DIM),
                                    iw[:, :N_IDX_HEADS].reshape(s, N_IDX_HEADS, 1),
                                    cache_kidx_a, j)
            bias, bias_new = _sample_select(scores, iq, jnp.tile(ikb[:, :IDX_DIM], (1, N_IDX_HEADS)),
                                            iw, min(TOPK_MAX, (past_len + 1) // 4))
            o_full = _decode_attn(page_table, _decode_q(q), k32[:, None], v32[:, None],
                                  bias_new[:, None], bias[:, None], cache_k_a, cache_v_a, j)
            xs = _out_proj(xs, _decode_o(o_full).astype(bf16), w_o, ms[2], s, s)
            ka_s.append(k32.reshape(kv_shape_s))
            va_s.append(v32.reshape(kv_shape_s))
            ia_s.append(ik32[:, :IDX_DIM].reshape(s, 1, IDX_DIM))
        else:
            w_in = _pad_cols(w_in_b[j], LANES - N_HEADS).astype(bf16)
            w_o = w_o_b[j].astype(bf16)
            bf_pad = _pad_cols(b_f[j][None], LANES - N_HEADS)
            nq = N_HEADS * HEAD_DIM
            w_row = w_in[:, nq:]
            w_t = jnp.concatenate([w_in[:, :nq], w_in[:, nq + KV_COLS:QKV_COLS]], axis=1).T
            qt, vt, kpad, k32, v32, lf = _in_proj_t(
                xp, g_m, mp[0], mp[1], w_row, w_t, tabs_p, tabs_pt, bf_pad, False, tm, t)
            c_t = _cumsum_t(lf.reshape(bsz, t, LANES), tq)
            c_pad = jnp.pad(c_t.transpose(0, 2, 1), ((0, 0), (0, 0), (0, LANES - N_HEADS)))
            xp = _fox_attn(xp, mp[2], qt, kpad, vt, c_pad, c_t, w_o.T, tq)
            kb_p.append(k32.reshape(kv_shape_p))
            vb_p.append(v32.reshape(kv_shape_p))
            lb_p.append(lf[:, :N_HEADS].reshape(bsz, t, N_HEADS))
            q, k32, kb, v32, vb, lf = _in_proj(
                xs, g_m, ms[0], ms[1], w_in, tabs_s, bf_pad, SEGS_B, OUTS_B, s, s)
            bias = _suffix_bias(page_table, lf[:, :N_HEADS, None], cache_logf_b, j)
            o_full = _decode_attn(page_table, _decode_q(q), k32[:, None], v32[:, None],
                                  jnp.zeros((s, N_HEADS, LANES), f32), bias,
                                  cache_k_b, cache_v_b, j)
            xs = _out_proj(xs, _decode_o(o_full).astype(bf16), w_o, ms[2], s, s)
            kb_s.append(k32.reshape(kv_shape_s))
            vb_s.append(v32.reshape(kv_shape_s))
            lb_s.append(lf[:, :N_HEADS].reshape(s, 1, N_HEADS))

        wa = w_up[i][:, :D_FF].astype(bf16)
        wb = w_up[i][:, D_FF:].astype(bf16)
        wd = w_down[i].astype(bf16)
        cw = jnp.pad(conv_w[i], ((0, SUBLANES - CONV_W), (0, 0)))
        cb = conv_b[i][None]
        g_f = g_ffn[i][None]
        xp, tail = _ffn(xp, g_f, mp[3], mp[4], mp[5], wa, wb, cw, cb, wd, tm, t)
        cv_p.append(tail.reshape(bsz, t // tm, SUBLANES, D_FF)[:, -1, SUBLANES - (CONV_W - 1):])
        xs, a_s = _ffn_sample(xs, g_f, ms[3], ms[4], ms[5], state_conv[i][:, 0],
                              state_conv[i][:, 1], wa, wb, cw, cb, wd, D_FF // 2)
        cv_s.append(jnp.stack([state_conv[i][:, 1], a_s], axis=1))

    y_prompt = _final_norm(xp, g_final[None], tm).reshape(bsz, t, d)
    y_sample = _final_norm(xs, g_final[None], s).reshape(s, 1, d)
    return (y_prompt, y_sample,
            jnp.stack(ka_p), jnp.stack(va_p), jnp.stack(ia_p),
            jnp.stack(kb_p), jnp.stack(vb_p), jnp.stack(lb_p), jnp.stack(cv_p),
            jnp.stack(ka_s), jnp.stack(va_s), jnp.stack(ia_s),
            jnp.stack(kb_s), jnp.stack(vb_s), jnp.stack(lb_s), jnp.stack(cv_s))
```

```python
import functools

import jax
import jax.numpy as jnp
from jax import lax
from jax.experimental import pallas as pl
from jax.experimental.pallas import tpu as pltpu

f32 = jnp.float32
bf16 = jnp.bfloat16
i32 = jnp.int32

D_MODEL = 1024
N_HEADS = 16
HEAD_DIM = 64
N_KV_HEADS = 4
GROUP = N_HEADS // N_KV_HEADS
KV_COLS = N_KV_HEADS * HEAD_DIM
QKV_COLS = (N_HEADS + 2 * N_KV_HEADS) * HEAD_DIM
ROPE_DIMS = HEAD_DIM // 4
ROPE_HALF = ROPE_DIMS // 2
ROPE_THETA = 500000.0
N_IDX_HEADS = 8
IDX_DIM = 64
TOPK_MAX = 256
D_FF = 2816
CONV_W = 3
EPS = 1e-6
PAGE_SIZE = 128
Q_SCALE = HEAD_DIM ** -0.5
LOG2E = 1.4426950408889634

LANES = 128
SUBLANES = 8
VMEM_LIMIT = 56 * 1024 * 1024
NEG = -1e30
INT_MIN = -2 ** 31
INT_MAX = 2 ** 31 - 1
KEY_NEG_INF = (0xFF800000 ^ 0x7FFFFFFF) - 2 ** 32


def _cparams(*sem):
    return pltpu.CompilerParams(dimension_semantics=sem, vmem_limit_bytes=VMEM_LIMIT)


def _dot(a, b):
    return jnp.dot(a, b, preferred_element_type=f32)


def _dot_nt(a, b):
    return lax.dot_general(a, b, (((1,), (1,)), ((), ())), preferred_element_type=f32)


def _split3(x):
    hi = x.astype(bf16)
    r1 = x - hi.astype(f32)
    mid = r1.astype(bf16)
    lo = (r1 - mid.astype(f32)).astype(bf16)
    return hi, mid, lo


def _norm_mod(x, g, shift, scale):
    xn = x * lax.rsqrt(jnp.mean(x * x, axis=-1, keepdims=True) + EPS)
    return (xn * g) * (1.0 + scale) + shift


def _silu(x):
    return x * jax.nn.sigmoid(x)


def _sort_key(s):
    bits = pltpu.bitcast(s, i32)
    return bits ^ ((bits >> 31) & 0x7FFFFFFF)


def _adaln_kernel(c_ref, w_ref, b_ref, o_ref):
    a = _silu(c_ref[...]).astype(bf16)
    o_ref[0] = _dot(a, w_ref[0].astype(bf16)) + b_ref[0]


def _adaln(c_all, w_ada, b_ada):
    depth, d, n = w_ada.shape
    rows = c_all.shape[0]
    tn = 1536
    return pl.pallas_call(
        _adaln_kernel,
        grid=(depth, n // tn),
        in_specs=[pl.BlockSpec((rows, d), lambda l, j: (0, 0)),
                  pl.BlockSpec((1, d, tn), lambda l, j: (l, 0, j)),
                  pl.BlockSpec((1, 1, tn), lambda l, j: (l, 0, j))],
        out_specs=pl.BlockSpec((1, rows, tn), lambda l, j: (l, 0, j)),
        out_shape=jax.ShapeDtypeStruct((depth, rows, n), f32),
        compiler_params=_cparams("parallel", "parallel"),
        name="adaln",
    )(c_all, w_ada, b_ada.reshape(depth, 1, n))


SEGS_A = ((0, 1024, "rope", ((0, Q_SCALE),)),
          (1024, 256, "rope", ((1, 1.0), (2, 1.0))),
          (1280, 256, None, ((3, 1.0), (4, 1.0))),
          (1536, 512, "rope", ((5, 1.0),)),
          (2048, 128, "rope", ((6, 1.0), (7, 1.0))),
          (2176, 128, None, ((8, 1.0),)))
OUTS_A = ((1024, bf16), (256, f32), (256, bf16), (256, f32), (256, bf16),
          (512, bf16), (128, f32), (128, bf16), (128, f32))
SEGS_B = ((0, 1024, None, ((0, Q_SCALE),)),
          (1024, 256, None, ((1, 1.0), (2, 1.0))),
          (1280, 256, None, ((3, 1.0), (4, 1.0))),
          (1536, 128, "logf", ((5, 1.0),)))
OUTS_B = ((1024, bf16), (256, f32), (256, bf16), (256, f32), (256, bf16), (128, f32))


def _in_proj_kernel(x_ref, g_ref, shift_ref, scale_ref, w_ref, cos_ref, sa_ref, sb_ref,
                    bf_ref, *out_refs, segs):
    h = _norm_mod(x_ref[...], g_ref[...], shift_ref[0], scale_ref[0]).astype(bf16)
    for c0, width, epi, outs in segs:
        for c in range(width // LANES):
            lo = c0 + c * LANES
            a = _dot(h, w_ref[:, lo:lo + LANES])
            if epi == "rope":
                a = (a * cos_ref[...] + pltpu.roll(a, ROPE_HALF, 1) * sa_ref[...]
                     + pltpu.roll(a, LANES - ROPE_HALF, 1) * sb_ref[...])
            elif epi == "logf":
                z = a + bf_ref[...]
                a = jnp.minimum(z, 0.0) - jnp.log(1.0 + jnp.exp(-jnp.abs(z)))
            for oi, sc in outs:
                val = a if sc == 1.0 else a * sc
                out_refs[oi][:, c * LANES:(c + 1) * LANES] = val.astype(out_refs[oi].dtype)


def _in_proj(x, g, shift, scale, w, tabs, b_f, segs, outs, tm, rows_per_mod):
    m, d = x.shape
    n = w.shape[1]
    r = shift.shape[1]
    cos, sa, sb = tabs
    tab_blocks = cos.shape[0] // tm
    mod_spec = pl.BlockSpec((1, r, d), lambda i: (i * tm // rows_per_mod, 0, 0))
    tab_spec = pl.BlockSpec((tm, LANES), lambda i: (i % tab_blocks, 0))
    return pl.pallas_call(
        functools.partial(_in_proj_kernel, segs=segs),
        grid=(m // tm,),
        in_specs=[pl.BlockSpec((tm, d), lambda i: (i, 0)),
                  pl.BlockSpec((1, d), lambda i: (0, 0)),
                  mod_spec, mod_spec,
                  pl.BlockSpec((d, n), lambda i: (0, 0)),
                  tab_spec, tab_spec, tab_spec,
                  pl.BlockSpec((1, LANES), lambda i: (0, 0))],
        out_specs=[pl.BlockSpec((tm, wd), lambda i: (i, 0)) for wd, _ in outs],
        out_shape=[jax.ShapeDtypeStruct((m, wd), dt) for wd, dt in outs],
        compiler_params=_cparams("parallel"),
        name="in_proj",
    )(x, g, shift, scale, w, cos, sa, sb, b_f)


def _rope_tables(pos):
    inv_freq = ROPE_THETA ** (-jnp.arange(ROPE_HALF, dtype=f32) / ROPE_HALF)
    ang = pos.astype(f32)[:, None] * inv_freq[None, :]
    cos, sin = jnp.cos(ang), jnp.sin(ang)
    t = pos.shape[0]
    ones = jnp.ones((t, HEAD_DIM - ROPE_DIMS), f32)
    zeros = jnp.zeros((t, HEAD_DIM - ROPE_DIMS), f32)
    zh = jnp.zeros((t, ROPE_HALF), f32)
    c = jnp.concatenate([cos, cos, ones], axis=1)
    sa = jnp.concatenate([zh, sin, zeros], axis=1)
    sb = jnp.concatenate([-sin, zh, zeros], axis=1)
    rep = LANES // HEAD_DIM
    return tuple(jnp.tile(a, (1, rep)) for a in (c, sa, sb))


WT_Q = 0
WT_V = N_HEADS * HEAD_DIM
WT_K = WT_V + KV_COLS
WT_X = WT_K + KV_COLS
WT_IK = WT_X + N_IDX_HEADS * IDX_DIM
WT_IW = WT_IK + IDX_DIM
IW_ROWS = 2 * SUBLANES
ROW_BLOCK = 512


def _rope_rows(a, cos_t, sin_t):
    parts = []
    for b0 in range(0, a.shape[0], HEAD_DIM):
        x1 = a[b0:b0 + ROPE_HALF]
        x2 = a[b0 + ROPE_HALF:b0 + ROPE_DIMS]
        parts += [x1 * cos_t - x2 * sin_t, x1 * sin_t + x2 * cos_t, a[b0 + ROPE_DIMS:b0 + HEAD_DIM]]
    return jnp.concatenate(parts, axis=0)


def _in_proj_t_kernel(x_ref, g_ref, shift_ref, scale_ref, w_ref, wt_ref, cos_ref, sa_ref, sb_ref,
                      cost_ref, sint_ref, bf_ref, *outs, is_a):
    if is_a:
        qt_ref, vt_ref, kpad_ref, kt32_ref, vt32_ref, iqt_ref, iwt_ref, ikb_ref, ikt32_ref = outs
    else:
        qt_ref, vt_ref, kpad_ref, kt32_ref, vt32_ref, lft_ref = outs
    h = _norm_mod(x_ref[...], g_ref[...], shift_ref[0], scale_ref[0]).astype(bf16)
    cos_t, sin_t = cost_ref[...], sint_ref[...]

    def rows(r0, n):
        blk = _dot_nt(wt_ref[r0:r0 + n, :], h)
        return [blk[c * LANES:(c + 1) * LANES] for c in range(n // LANES)]

    def rope_cols(a):
        return (a * cos_ref[...] + pltpu.roll(a, ROPE_HALF, 1) * sa_ref[...]
                + pltpu.roll(a, LANES - ROPE_HALF, 1) * sb_ref[...])

    for r0 in range(0, N_HEADS * HEAD_DIM, ROW_BLOCK):
        for c, a in enumerate(rows(WT_Q + r0, ROW_BLOCK)):
            if is_a:
                a = _rope_rows(a, cos_t, sin_t)
            lo = r0 + c * LANES
            qt_ref[lo:lo + LANES, :] = (a * (Q_SCALE * LOG2E)).astype(bf16)
    for c, (v, k) in enumerate(zip(rows(WT_V, KV_COLS), rows(WT_K, KV_COLS))):
        sl = slice(c * LANES, (c + 1) * LANES)
        vt32_ref[0, sl, :] = v
        vt_ref[sl, :] = v.astype(bf16)
        kt32_ref[0, sl, :] = _rope_rows(k, cos_t, sin_t) if is_a else k
    if is_a:
        for c, a in enumerate(rows(WT_X, N_IDX_HEADS * IDX_DIM)):
            iqt_ref[c * LANES:(c + 1) * LANES, :] = _rope_rows(a, cos_t, sin_t).astype(bf16)
        ikt32_ref[0] = _rope_rows(_dot_nt(wt_ref[WT_IK:WT_IK + IDX_DIM, :], h), cos_t, sin_t)
        iwt_ref[...] = _dot_nt(wt_ref[WT_IW:WT_IW + IW_ROWS, :], h)
    else:
        z = _dot_nt(wt_ref[WT_X:WT_X + N_HEADS, :], h) + bf_ref[...]
        lft_ref[0] = jnp.minimum(z, 0.0) - jnp.log(1.0 + jnp.exp(-jnp.abs(z)))

    lane = lax.broadcasted_iota(i32, (1, LANES), 1)
    std = _dot(h, w_ref[...])
    for c in range(KV_COLS // LANES):
        a = std[:, c * LANES:(c + 1) * LANES]
        if is_a:
            a = rope_cols(a)
        kpad_ref[0, 2 * c] = jnp.where(lane < HEAD_DIM, a, 0.0).astype(bf16)
        kpad_ref[0, 2 * c + 1] = jnp.where(lane < HEAD_DIM, pltpu.roll(a, HEAD_DIM, 1),
                                           0.0).astype(bf16)
    if is_a:
        ikb_ref[...] = rope_cols(std[:, KV_COLS:KV_COLS + LANES])[:, :IDX_DIM].astype(bf16)


def _in_proj_t(x, g, shift, scale, w, wt, tabs, tabs_t, b_f, is_a, tm, t):
    m, d = x.shape
    bsz = m // t
    tpb = t // tm
    col = lambda rows_: pl.BlockSpec((rows_, tm), lambda i: (0, i))
    seq = lambda rows_: pl.BlockSpec((1, rows_, tm), lambda i: (i // tpb, 0, i % tpb))
    seq_shape = lambda rows_: jax.ShapeDtypeStruct((bsz, rows_, t), f32)
    const = lambda a: pl.BlockSpec(a.shape, lambda i: (0,) * a.ndim)
    mod_spec = pl.BlockSpec((1, 1, d), lambda i: (i // tpb, 0, 0))
    tab_spec = pl.BlockSpec((tm, LANES), lambda i: (i % tpb, 0))
    tabt_spec = pl.BlockSpec((ROPE_HALF, tm), lambda i: (0, i % tpb))
    nq = N_HEADS * HEAD_DIM
    out_specs = [col(nq), col(KV_COLS),
                 pl.BlockSpec((1, N_KV_HEADS, tm, LANES), lambda i: (i // tpb, 0, i % tpb, 0)),
                 seq(KV_COLS), seq(KV_COLS)]
    out_shape = [jax.ShapeDtypeStruct((nq, m), bf16), jax.ShapeDtypeStruct((KV_COLS, m), bf16),
                 jax.ShapeDtypeStruct((bsz, N_KV_HEADS, t, LANES), bf16),
                 seq_shape(KV_COLS), seq_shape(KV_COLS)]
    if is_a:
        out_specs += [col(N_IDX_HEADS * IDX_DIM), col(IW_ROWS),
                      pl.BlockSpec((tm, IDX_DIM), lambda i: (i, 0)), seq(IDX_DIM)]
        out_shape += [jax.ShapeDtypeStruct((N_IDX_HEADS * IDX_DIM, m), bf16),
                      jax.ShapeDtypeStruct((IW_ROWS, m), f32),
                      jax.ShapeDtypeStruct((m, IDX_DIM), bf16), seq_shape(IDX_DIM)]
    else:
        out_specs += [seq(N_HEADS)]
        out_shape += [seq_shape(N_HEADS)]
    return pl.pallas_call(
        functools.partial(_in_proj_t_kernel, is_a=is_a),
        grid=(m // tm,),
        in_specs=[pl.BlockSpec((tm, d), lambda i: (i, 0)), const(g), mod_spec, mod_spec,
                  const(w), const(wt), tab_spec, tab_spec, tab_spec, tabt_spec, tabt_spec,
                  const(b_f)],
        out_specs=out_specs, out_shape=out_shape,
        compiler_params=_cparams("parallel"),
        name="in_proj_t",
    )(x, g, shift, scale, w, wt, *tabs, *tabs_t, b_f)


def _rope_tables_t(pos):
    inv_freq = ROPE_THETA ** (-jnp.arange(ROPE_HALF, dtype=f32) / ROPE_HALF)
    ang = inv_freq[:, None] * pos.astype(f32)[None, :]
    return jnp.cos(ang), jnp.sin(ang)


AUG_STRIDE = 4
AUG_ONES = GROUP * AUG_STRIDE


ATTN_TK = 256
SUM_ROWS = 2 * SUBLANES
ACC_ROWS = HEAD_DIM + SUM_ROWS


def _t_init(m_sc, acc_sc):
    m_sc[...] = jnp.full(m_sc.shape, NEG, f32)
    acc_sc[...] = jnp.zeros(acc_sc.shape, f32)


def _t_step(s, vt1, h, m_sc, acc_sc):
    m_old = m_sc[h]
    m_new = jnp.maximum(m_old, jnp.max(s, axis=0, keepdims=True))
    p = jnp.exp2((s - m_new).astype(bf16))
    acc_sc[h] = jnp.exp2(m_old - m_new) * acc_sc[h] + _dot(vt1, p)
    m_sc[h] = m_new


def _t_tile(off, tk, kt_of, vt_ref, qa_sc, post, m_sc, acc_sc):
    kts = [kt_of(g) for g in range(N_KV_HEADS)]
    ones = jnp.ones((SUM_ROWS, tk), bf16)
    vts = [jnp.concatenate([vt_ref[g * HEAD_DIM:(g + 1) * HEAD_DIM, pl.ds(off, tk)], ones], axis=0)
           for g in range(N_KV_HEADS)]
    s_next = _dot(kts[0], qa_sc[0])
    for g in range(N_KV_HEADS):
        s = s_next
        if g + 1 < N_KV_HEADS:
            s_next = _dot(kts[g + 1], qa_sc[g + 1])
        _t_step(post(s), vts[g], g, m_sc, acc_sc)


def _t_fill_q(qa_sc, h, tq, q, aug):
    g, jj = divmod(h, GROUP)
    qa_sc[g, :HEAD_DIM, jj * tq:(jj + 1) * tq] = q
    qa_sc[g, HEAD_DIM:, jj * tq:(jj + 1) * tq] = aug


def _t_finish(tq, acc_sc, ot_sc):
    for h in range(N_HEADS):
        g, jj = divmod(h, GROUP)
        cols = slice(jj * tq, (jj + 1) * tq)
        ot_sc[h * HEAD_DIM:(h + 1) * HEAD_DIM, :] = (
            acc_sc[g, :HEAD_DIM, cols] / acc_sc[g, HEAD_DIM:HEAD_DIM + 1, cols]).astype(bf16)


def _t_out(x_ref, gate_ref, wot_ref, ot_sc, o_ref):
    y_t = _dot(wot_ref[...], ot_sc[...])
    o_ref[...] = x_ref[...] + gate_ref[0] * y_t.T


def _fox_attn_kernel(x_ref, gate_ref, qt_ref, kpad_ref, vt_ref, cpad_ref, ct_ref, wot_ref,
                     o_ref, kaug_sc, m_sc, acc_sc, qa_sc, ot_sc, *, tq, tk, chunk):
    qi = pl.program_id(1)
    t = kaug_sc.shape[1]

    @pl.when(qi == 0)
    def _():
        r = lax.broadcasted_iota(i32, (LANES, LANES), 0)
        l = lax.broadcasted_iota(i32, (LANES, LANES), 1) - HEAD_DIM
        lane = lax.broadcasted_iota(i32, (1, LANES), 1) - HEAD_DIM
        ones = jnp.where((lane >= AUG_ONES) & (lane < AUG_ONES + 3), 1.0, 0.0)
        for g in range(N_KV_HEADS):
            perm = [((l >= 0) & (l < AUG_ONES) & ((l >> 2) == r - g * GROUP)
                     & ((l & 3) == p)).astype(bf16) for p in range(3)]

            def body(i, c, g=g, perm=perm):
                off = pl.multiple_of(i * chunk, chunk)
                pieces = _split3(cpad_ref[0, pl.ds(off, chunk), :] * -LOG2E)
                aug = sum(_dot(pc, pm) for pc, pm in zip(pieces, perm)) + ones
                kaug_sc[g, pl.ds(off, chunk), :] = (
                    kpad_ref[0, g, pl.ds(off, chunk), :].astype(f32) + aug).astype(bf16)
                return c

            lax.fori_loop(0, t // chunk, body, 0)

    ri = lax.broadcasted_iota(i32, (HEAD_DIM, tq), 0)
    wide = (tk, GROUP * tq)
    d_iota = (lax.broadcasted_iota(i32, wide, 0)
              - (lax.broadcasted_iota(i32, wide, 1) & (tq - 1)))
    for h in range(N_HEADS):
        jj = h % GROUP
        cq = _split3(ct_ref[0, h:h + 1, :] * LOG2E)
        aug = jnp.where((ri >= jj * AUG_STRIDE) & (ri < jj * AUG_STRIDE + 3), 1.0, 0.0)
        for p in range(3):
            aug = jnp.where(ri == AUG_ONES + p, cq[p].astype(f32), aug)
        _t_fill_q(qa_sc, h, tq, qt_ref[h * HEAD_DIM:(h + 1) * HEAD_DIM, :], aug.astype(bf16))
    _t_init(m_sc, acc_sc)

    def tile(j, post):
        off = pl.multiple_of(j * tk, tk)
        _t_tile(off, tk, lambda g: kaug_sc[g, pl.ds(off, tk), :], vt_ref, qa_sc, post,
                m_sc, acc_sc)

    def body(j, c):
        tile(j, lambda s: s)
        return c

    sub = tq // tk
    lax.fori_loop(0, qi * sub, body, 0)
    for dd in range(sub):
        tile(qi * sub + dd, lambda s, dd=dd: jnp.where(d_iota <= -dd * tk, s, NEG))
    _t_finish(tq, acc_sc, ot_sc)
    _t_out(x_ref, gate_ref, wot_ref, ot_sc, o_ref)


def _attn_specs(bsz, t, d, tq):
    nq = t // tq
    return dict(
        x=pl.BlockSpec((tq, d), lambda b, q: (b * nq + q, 0)),
        gate=pl.BlockSpec((1, 1, d), lambda b, q: (b, 0, 0)),
        qt=pl.BlockSpec((N_HEADS * HEAD_DIM, tq), lambda b, q: (0, b * nq + q)),
        kpad=pl.BlockSpec((1, N_KV_HEADS, t, LANES), lambda b, q: (b, 0, 0, 0)),
        vt=pl.BlockSpec((KV_COLS, t), lambda b, q: (0, b)),
        wot=pl.BlockSpec((d, N_HEADS * HEAD_DIM), lambda b, q: (0, 0)),
        scratch=[pltpu.VMEM((N_KV_HEADS, 1, GROUP * tq), f32),
                 pltpu.VMEM((N_KV_HEADS, ACC_ROWS, GROUP * tq), f32),
                 pltpu.VMEM((N_KV_HEADS, LANES, GROUP * tq), bf16),
                 pltpu.VMEM((N_HEADS * HEAD_DIM, tq), bf16)])


def _fox_attn(x, gate, qt, kpad, vt, c_pad, c_t, wot, tq):
    bsz, _, t, _ = kpad.shape
    m, d = x.shape
    sp = _attn_specs(bsz, t, d, tq)
    return pl.pallas_call(
        functools.partial(_fox_attn_kernel, tq=tq, tk=ATTN_TK, chunk=512),
        grid=(bsz, t // tq),
        in_specs=[sp["x"], sp["gate"], sp["qt"], sp["kpad"], sp["vt"],
                  pl.BlockSpec((1, t, LANES), lambda b, q: (b, 0, 0)),
                  pl.BlockSpec((1, N_HEADS, tq), lambda b, q: (b, 0, q)),
                  sp["wot"]],
        out_specs=sp["x"],
        out_shape=jax.ShapeDtypeStruct((m, d), f32),
        scratch_shapes=[pltpu.VMEM((N_KV_HEADS, t, LANES), bf16)] + sp["scratch"],
        compiler_params=_cparams("parallel", "arbitrary"),
        name="fox_attn",
    )(x, gate, qt, kpad, vt, c_pad, c_t, wot)


def _cumsum_t_kernel(lft_ref, ct_ref, carry_sc, *, tb):
    @pl.when(pl.program_id(1) == 0)
    def _():
        carry_sc[...] = jnp.zeros(carry_sc.shape, f32)

    tri = (lax.broadcasted_iota(i32, (tb, tb), 0)
           <= lax.broadcasted_iota(i32, (tb, tb), 1)).astype(bf16)
    lft = lft_ref[0]
    csum = sum(_dot(p, tri) for p in _split3(lft))
    ct_ref[0] = csum + carry_sc[...]
    carry_sc[...] = carry_sc[...] + jnp.sum(lft, axis=1, keepdims=True)


def _cumsum_t(lf_t, tb):
    b, nh, t = lf_t.shape
    spec = pl.BlockSpec((1, nh, tb), lambda bi, j: (bi, 0, j))
    return pl.pallas_call(
        functools.partial(_cumsum_t_kernel, tb=tb),
        grid=(b, t // tb),
        in_specs=[spec],
        out_specs=spec,
        out_shape=jax.ShapeDtypeStruct((b, nh, t), f32),
        scratch_shapes=[pltpu.VMEM((nh, 1), f32)],
        compiler_params=_cparams("parallel", "arbitrary"),
        name="cumsum_t",
    )(lf_t)


def _kth_threshold(count_ge, shape, n_sel):
    zero = jnp.zeros(shape, i32)
    ans = jnp.where(count_ge(zero) >= n_sel, zero, jnp.full(shape, INT_MIN, i32))

    def bit_body(i, ans):
        cand = ans | jnp.left_shift(jnp.int32(1), 30 - i)
        return jnp.where(count_ge(cand) >= n_sel, cand, ans)

    return lax.fori_loop(0, 31, bit_body, ans)


def _kth_threshold_early(count_ge, shape, n_sel, n_total):
    zero = jnp.zeros(shape, i32)
    c0 = count_ge(zero)
    ans = jnp.where(c0 >= n_sel, zero, jnp.full(shape, INT_MIN, i32))
    exact = jnp.where((c0 == n_sel) | ((c0 < n_sel) & (n_total == n_sel)), 1, 0)

    def cond(state):
        i, _, _, n_open = state
        return (i < 31) & (n_open > 0)

    def body(state):
        i, ans, exact, _ = state
        cand = ans | jnp.left_shift(jnp.int32(1), 30 - i)
        c = count_ge(cand)
        take = (exact == 0) & (c >= n_sel)
        ans = jnp.where(take, cand, ans)
        exact = jnp.where(take & (c == n_sel), 1, exact)
        return i + 1, ans, exact, jnp.sum(1 - exact)

    _, ans, exact, _ = lax.while_loop(cond, body, (jnp.int32(0), ans, exact, jnp.sum(1 - exact)))
    return ans, exact


def _tie_cut(count_tied_below, need, shape, n_bits):
    def body(i, p):
        cand = p | jnp.left_shift(jnp.int32(1), n_bits - 1 - i)
        return jnp.where(count_tied_below(cand) < need, cand, p)
    return lax.fori_loop(0, n_bits, body, jnp.zeros(shape, i32))


def _fold_lanes(ind, c):
    for cc in range(ind.shape[1] // LANES):
        c = c + ind[:, cc * LANES:(cc + 1) * LANES]
    return c


def _dsa_attn_kernel(x_ref, gate_ref, qt_ref, kpad_ref, vt_ref, iqt_ref, iwt_ref, ik_ref,
                     wot_ref, o_ref, key_sc, bias_sc, cut_sc, m_sc, acc_sc, qa_sc, ot_sc,
                     *, tq, n_sel, n_bits):
    qi = pl.program_id(1)
    n_k = qi + 1
    row = lax.broadcasted_iota(i32, (tq, tq), 0)
    d_iota = row - lax.broadcasted_iota(i32, (tq, tq), 1)

    def score_body(j, c):
        off = pl.multiple_of(j * tq, tq)
        ikt = ik_ref[0, pl.ds(off, tq), :]
        s = jnp.zeros((tq, tq), f32)
        for h in range(N_IDX_HEADS):
            dots = _dot(ikt, iqt_ref[h * IDX_DIM:(h + 1) * IDX_DIM, :])
            s = s + jnp.maximum(dots, 0.0) * iwt_ref[h:h + 1, :]
        key_sc[pl.ds(off, tq), :] = jnp.where(d_iota <= (qi - j) * tq, _sort_key(s),
                                              KEY_NEG_INF)
        return c

    lax.fori_loop(0, n_k, score_body, 0)

    def count(pred):
        def body(j, c):
            off = pl.multiple_of(j * tq, tq)
            ind = jnp.where(pred(key_sc[pl.ds(off, tq), :], j), 1, 0)
            return c + jnp.sum(ind, axis=0, keepdims=True)
        return lax.fori_loop(0, n_k, body, jnp.zeros((1, tq), i32))

    thr, exact = _kth_threshold_early(lambda cand: count(lambda kt, j: kt >= cand), (1, tq),
                                      n_sel, n_k * tq)
    real = thr > KEY_NEG_INF
    cut_sc[...] = jnp.where(real, INT_MAX, -1)

    @pl.when(jnp.max(jnp.where(real, 1 - exact, 0)) > 0)
    def _():
        need = n_sel - count(lambda kt, j: kt > thr)
        cut = _tie_cut(lambda cand: count(lambda kt, j: (kt == thr) & (row + j * tq < cand)),
                       need, (1, tq), n_bits)
        cut_sc[...] = jnp.where(real, jnp.where(exact == 1, INT_MAX, cut), -1)

    cut = cut_sc[...]
    floor = jnp.maximum(thr, KEY_NEG_INF)

    def bias_body(j, c):
        off = pl.multiple_of(j * tq, tq)
        kt = key_sc[pl.ds(off, tq), :]
        sel = (kt > floor) | ((kt == thr) & (row + j * tq <= cut))
        bias_sc[pl.ds(off, tq), :] = jnp.where(sel, 0.0, NEG)
        return c

    lax.fori_loop(0, n_k, bias_body, 0)

    for h in range(N_HEADS):
        _t_fill_q(qa_sc, h, tq, qt_ref[h * HEAD_DIM:(h + 1) * HEAD_DIM, :],
                  jnp.zeros((LANES - HEAD_DIM, tq), bf16))
    _t_init(m_sc, acc_sc)

    def body(j, c):
        off = pl.multiple_of(j * tq, tq)

        def masked(s):
            return jnp.concatenate([s[:, jj * tq:(jj + 1) * tq] + bias_sc[pl.ds(off, tq), :]
                                    for jj in range(GROUP)], axis=1)

        _t_tile(off, tq, lambda g: kpad_ref[0, g, pl.ds(off, tq), :], vt_ref, qa_sc,
                masked, m_sc, acc_sc)
        return c

    lax.fori_loop(0, n_k, body, 0)
    _t_finish(tq, acc_sc, ot_sc)
    _t_out(x_ref, gate_ref, wot_ref, ot_sc, o_ref)


def _dsa_attn(x, gate, qt, kpad, vt, iqt, iwt, ik, wot, tq, n_sel):
    bsz, _, t, _ = kpad.shape
    m, d = x.shape
    nq = t // tq
    sp = _attn_specs(bsz, t, d, tq)
    return pl.pallas_call(
        functools.partial(_dsa_attn_kernel, tq=tq, n_sel=n_sel, n_bits=(t - 1).bit_length()),
        grid=(bsz, nq),
        in_specs=[sp["x"], sp["gate"], sp["qt"], sp["kpad"], sp["vt"],
                  pl.BlockSpec((N_IDX_HEADS * IDX_DIM, tq), lambda b, q: (0, b * nq + q)),
                  pl.BlockSpec((2 * SUBLANES, tq), lambda b, q: (0, b * nq + q)),
                  pl.BlockSpec((1, t, IDX_DIM), lambda b, q: (b, 0, 0)),
                  sp["wot"]],
        out_specs=sp["x"],
        out_shape=jax.ShapeDtypeStruct((m, d), f32),
        scratch_shapes=[pltpu.VMEM((t, tq), i32), pltpu.VMEM((t, tq), f32),
                        pltpu.VMEM((1, tq), i32)] + sp["scratch"],
        compiler_params=_cparams("parallel", "arbitrary"),
        name="dsa_attn",
    )(x, gate, qt, kpad, vt, iqt, iwt, ik, wot)


def _out_proj_kernel(x_ref, o_ref, w_ref, gate_ref, y_ref):
    y_ref[...] = x_ref[...] + gate_ref[0] * _dot(o_ref[...], w_ref[...])


def _out_proj(x, o, w, gate, tm, rows_per_mod):
    m, d = x.shape
    kdim = o.shape[1]
    r = gate.shape[1]
    return pl.pallas_call(
        _out_proj_kernel,
        grid=(m // tm,),
        in_specs=[pl.BlockSpec((tm, d), lambda i: (i, 0)),
                  pl.BlockSpec((tm, kdim), lambda i: (i, 0)),
                  pl.BlockSpec((kdim, d), lambda i: (0, 0)),
                  pl.BlockSpec((1, r, d), lambda i: (i * tm // rows_per_mod, 0, 0))],
        out_specs=pl.BlockSpec((tm, d), lambda i: (i, 0)),
        out_shape=jax.ShapeDtypeStruct((m, d), f32),
        compiler_params=_cparams("parallel"),
        name="out_proj",
    )(x, o, w, gate)


FFN_CHUNK = 256
FFN_HALO = 2 * SUBLANES


def _ffn_kernel(x_ref, xh_ref, g_ref, shift_ref, scale_ref, gate_ref, wa_ref, wb_ref,
                cw_ref, cb_ref, wd_ref, o_ref, tail_ref, acc_sc, hext_sc, *, tm, tiles_per_seq):
    i = pl.program_id(0)
    x = x_ref[...]
    keep = jnp.where(i % tiles_per_seq == 0, 0.0, 1.0)
    hext_sc[:FFN_HALO, :] = (_norm_mod(xh_ref[...], g_ref[...], shift_ref[0], scale_ref[0])
                             * keep).astype(bf16)
    hext_sc[FFN_HALO:, :] = _norm_mod(x, g_ref[...], shift_ref[0], scale_ref[0]).astype(bf16)
    for c in range(D_FF // FFN_CHUNK):
        sl = slice(c * FFN_CHUNK, (c + 1) * FFN_CHUNK)
        a_ext = _dot(hext_sc[...], wa_ref[:, sl])
        b = _dot(hext_sc[FFN_HALO:, :], wb_ref[:, sl])
        a = a_ext[FFN_HALO:]
        p1 = pltpu.roll(a_ext, 1, 0)[FFN_HALO:]
        p2 = pltpu.roll(a_ext, 2, 0)[FFN_HALO:]
        conv = cb_ref[:, sl] + p2 * cw_ref[0:1, sl] + p1 * cw_ref[1:2, sl] + a * cw_ref[2:3, sl]
        y = _dot((_silu(conv) * b).astype(bf16), wd_ref[sl, :])
        if c == 0:
            acc_sc[...] = y
        else:
            acc_sc[...] += y
        tail_ref[0, :, sl] = a_ext[FFN_HALO + tm - SUBLANES:]
    o_ref[...] = x + gate_ref[0] * acc_sc[...]


def _ffn(x, g, shift, scale, gate, wa, wb, cw, cb, wd, tm, rows_per_seq):
    m, d = x.shape
    n_tiles = m // tm
    hb = tm // FFN_HALO
    mod_spec = pl.BlockSpec((1, 1, d), lambda i: (i * tm // rows_per_seq, 0, 0))
    const = lambda shape: pl.BlockSpec(shape, lambda i: (0,) * len(shape))
    return pl.pallas_call(
        functools.partial(_ffn_kernel, tm=tm, tiles_per_seq=rows_per_seq // tm),
        grid=(n_tiles,),
        in_specs=[pl.BlockSpec((tm, d), lambda i: (i, 0)),
                  pl.BlockSpec((FFN_HALO, d), lambda i: (jnp.maximum(i * hb - 1, 0), 0)),
                  const((1, d)), mod_spec, mod_spec, mod_spec,
                  const((d, D_FF)), const((d, D_FF)), const((SUBLANES, D_FF)),
                  const((1, D_FF)), const((D_FF, d))],
        out_specs=[pl.BlockSpec((tm, d), lambda i: (i, 0)),
                   pl.BlockSpec((1, SUBLANES, D_FF), lambda i: (i, 0, 0))],
        out_shape=[jax.ShapeDtypeStruct((m, d), f32),
                   jax.ShapeDtypeStruct((n_tiles, SUBLANES, D_FF), f32)],
        scratch_shapes=[pltpu.VMEM((tm, d), f32), pltpu.VMEM((FFN_HALO + tm, d), bf16)],
        compiler_params=_cparams("parallel"),
        name="ffn",
    )(x, x, g, shift, scale, gate, wa, wb, cw, cb, wd)


def _ffn_sample_kernel(x_ref, g_ref, shift_ref, scale_ref, gate_ref, p2_ref, p1_ref, wa_ref,
                       wb_ref, cw_ref, cb_ref, wd_ref, o_ref, a_ref, acc_sc):
    j = pl.program_id(0)

    @pl.when(j == 0)
    def _():
        acc_sc[...] = jnp.zeros(acc_sc.shape, f32)

    x = x_ref[...]
    h = _norm_mod(x, g_ref[...], shift_ref[0], scale_ref[0]).astype(bf16)
    a = _dot(h, wa_ref[...])
    b = _dot(h, wb_ref[...])
    conv = (cb_ref[...] + p2_ref[...] * cw_ref[0:1, :] + p1_ref[...] * cw_ref[1:2, :]
            + a * cw_ref[2:3, :])
    acc_sc[...] += _dot((_silu(conv) * b).astype(bf16), wd_ref[...])
    a_ref[...] = a

    @pl.when(j == pl.num_programs(0) - 1)
    def _():
        o_ref[...] = x + gate_ref[0] * acc_sc[...]


def _ffn_sample(x, g, shift, scale, gate, p2, p1, wa, wb, cw, cb, wd, tn):
    m, d = x.shape
    const = lambda shape: pl.BlockSpec(shape, lambda j: (0,) * len(shape))
    return pl.pallas_call(
        _ffn_sample_kernel,
        grid=(D_FF // tn,),
        in_specs=[const((m, d)), const((1, d)), const((1, m, d)), const((1, m, d)),
                  const((1, m, d)),
                  pl.BlockSpec((m, tn), lambda j: (0, j)), pl.BlockSpec((m, tn), lambda j: (0, j)),
                  pl.BlockSpec((d, tn), lambda j: (0, j)), pl.BlockSpec((d, tn), lambda j: (0, j)),
                  pl.BlockSpec((SUBLANES, tn), lambda j: (0, j)),
                  pl.BlockSpec((1, tn), lambda j: (0, j)),
                  pl.BlockSpec((tn, d), lambda j: (j, 0))],
        out_specs=[const((m, d)), pl.BlockSpec((m, tn), lambda j: (0, j))],
        out_shape=[jax.ShapeDtypeStruct((m, d), f32), jax.ShapeDtypeStruct((m, D_FF), f32)],
        scratch_shapes=[pltpu.VMEM((m, d), f32)],
        compiler_params=_cparams("arbitrary"),
        name="ffn_sample",
    )(x, g, shift, scale, gate, p2, p1, wa, wb, cw, cb, wd)


def _final_norm_kernel(x_ref, g_ref, o_ref):
    x = x_ref[...]
    o_ref[...] = x * lax.rsqrt(jnp.mean(x * x, axis=-1, keepdims=True) + EPS) * g_ref[...]


def _final_norm(x, g, tm):
    m, d = x.shape
    return pl.pallas_call(
        _final_norm_kernel,
        grid=(m // tm,),
        in_specs=[pl.BlockSpec((tm, d), lambda i: (i, 0)), pl.BlockSpec((1, d), lambda i: (0, 0))],
        out_specs=pl.BlockSpec((tm, d), lambda i: (i, 0)),
        out_shape=jax.ShapeDtypeStruct((m, d), f32),
        compiler_params=_cparams("parallel"),
        name="final_norm",
    )(x, g)


def _seq_page_copies(pt_ref, hbm, layer, seq, buf, slot, sem, n_pages):
    return [pltpu.make_async_copy(hbm.at[layer, pt_ref[seq, p]],
                                  buf.at[slot, :, pl.ds(p * PAGE_SIZE, PAGE_SIZE)], sem.at[slot])
            for p in range(n_pages)]


def _gather_seq(pt_ref, hbm, layer, buf, sem, n_pages):
    b = pl.program_id(0)
    slot = b % 2

    @pl.when(b == 0)
    def _():
        for cp in _seq_page_copies(pt_ref, hbm, layer, 0, buf, 0, sem, n_pages):
            cp.start()

    @pl.when(b + 1 < pl.num_programs(0))
    def _():
        for cp in _seq_page_copies(pt_ref, hbm, layer, b + 1, buf, 1 - slot, sem, n_pages):
            cp.start()

    for cp in _seq_page_copies(pt_ref, hbm, layer, b, buf, slot, sem, n_pages):
        cp.wait()
    return slot


def _sample_score_kernel(pt_ref, iq_ref, iw_ref, kidx_hbm, s_ref, buf, sem, *, layer, n_pages):
    slot = _gather_seq(pt_ref, kidx_hbm, layer, buf, sem, n_pages)
    d = _dot(iq_ref[0], buf[slot].astype(bf16))
    s_ref[0] = jnp.sum(jnp.maximum(d, 0.0) * iw_ref[0], axis=0, keepdims=True)


def _sample_scores(page_table, iq, iw, cache_kidx_t, layer):
    s, n_pages = page_table.shape
    length = n_pages * PAGE_SIZE
    grid_spec = pltpu.PrefetchScalarGridSpec(
        num_scalar_prefetch=1, grid=(s,),
        in_specs=[pl.BlockSpec((1, N_IDX_HEADS, IDX_DIM), lambda b, pt: (b, 0, 0)),
                  pl.BlockSpec((1, N_IDX_HEADS, 1), lambda b, pt: (b, 0, 0)),
                  pl.BlockSpec(memory_space=pl.ANY)],
        out_specs=pl.BlockSpec((1, 1, length), lambda b, pt: (b, 0, 0)),
        scratch_shapes=[pltpu.VMEM((2, IDX_DIM, length), f32), pltpu.SemaphoreType.DMA((2,))])
    out = pl.pallas_call(
        functools.partial(_sample_score_kernel, layer=layer, n_pages=n_pages),
        grid_spec=grid_spec,
        out_shape=jax.ShapeDtypeStruct((s, 1, length), f32),
        compiler_params=_cparams("arbitrary"),
        name="sample_scores",
    )(page_table, iq, iw, cache_kidx_t)
    return out.reshape(s, length)


SEL_TILE = 1024


def _sample_select_kernel(s_ref, iq_ref, ikt_ref, iw_ref, bias_ref, bnew_ref, key_sc,
                          *, n_sel, n_bits):
    rows, length = s_ref.shape
    n_tiles = length // SEL_TILE
    prod = iq_ref[...].astype(f32) * ikt_ref[...].astype(f32)
    grp = (lax.broadcasted_iota(i32, (prod.shape[1], LANES), 0) // IDX_DIM
           == lax.broadcasted_iota(i32, (prod.shape[1], LANES), 1)).astype(bf16)
    d_new = sum(_dot(p, grp) for p in _split3(prod))
    s_new = jnp.sum(jnp.maximum(d_new, 0.0) * iw_ref[...], axis=1, keepdims=True)
    key_new = _sort_key(s_new)
    key_sc[...] = _sort_key(s_ref[...])
    col = lax.broadcasted_iota(i32, (rows, SEL_TILE), 1)

    def count(pred, pred_new):
        def body(j, c):
            off = pl.multiple_of(j * SEL_TILE, SEL_TILE)
            return _fold_lanes(jnp.where(pred(key_sc[:, pl.ds(off, SEL_TILE)], j), 1, 0), c)
        c = lax.fori_loop(0, n_tiles, body, jnp.zeros((rows, LANES), i32))
        return jnp.sum(c, axis=1, keepdims=True) + jnp.where(pred_new, 1, 0)

    thr = _kth_threshold(lambda cand: count(lambda kt, j: kt >= cand, key_new >= cand),
                         (rows, 1), n_sel)
    need = n_sel - count(lambda kt, j: kt > thr, key_new > thr)
    cut = _tie_cut(lambda cand: count(lambda kt, j: (kt == thr) & (col + j * SEL_TILE < cand),
                                      (key_new == thr) & (length < cand)),
                   need, (rows, 1), n_bits)

    def bias_body(j, c):
        off = pl.multiple_of(j * SEL_TILE, SEL_TILE)
        kt = key_sc[:, pl.ds(off, SEL_TILE)]
        sel = (kt > thr) | ((kt == thr) & (col + j * SEL_TILE <= cut))
        bias_ref[:, pl.ds(off, SEL_TILE)] = jnp.where(sel, 0.0, NEG)
        return c

    lax.fori_loop(0, n_tiles, bias_body, 0)
    sel_new = (key_new > thr) | ((key_new == thr) & (length <= cut))
    bnew_ref[...] = jnp.broadcast_to(jnp.where(sel_new, 0.0, NEG), bnew_ref.shape)


def _sample_select(scores, iq, ik_tiled, iw, n_sel):
    rows, length = scores.shape
    return pl.pallas_call(
        functools.partial(_sample_select_kernel, n_sel=n_sel, n_bits=length.bit_length()),
        out_shape=[jax.ShapeDtypeStruct((rows, length), f32),
                   jax.ShapeDtypeStruct((rows, LANES), f32)],
        scratch_shapes=[pltpu.VMEM((rows, length), i32)],
        compiler_params=pltpu.CompilerParams(vmem_limit_bytes=VMEM_LIMIT),
        name="sample_select",
    )(scores, iq, ik_tiled, iw)


SUFFIX_CHUNK = 256


def _suffix_bias_kernel(pt_ref, lfnew_ref, lf_hbm, bias_ref, buf, sem, *, layer, n_pages):
    slot = _gather_seq(pt_ref, lf_hbm, layer, buf, sem, n_pages)
    w = SUFFIX_CHUNK
    tri = (lax.broadcasted_iota(i32, (w, w), 0)
           > lax.broadcasted_iota(i32, (w, w), 1)).astype(bf16)
    carry = lfnew_ref[0]
    for c in reversed(range(n_pages * PAGE_SIZE // w)):
        x = buf[slot, :, c * w:(c + 1) * w]
        bias_ref[0, :, c * w:(c + 1) * w] = sum(_dot(p, tri) for p in _split3(x)) + carry
        carry = carry + jnp.sum(x, axis=1, keepdims=True)


def _suffix_bias(page_table, lf_new, cache_logf_t, layer):
    s, n_pages = page_table.shape
    length = n_pages * PAGE_SIZE
    grid_spec = pltpu.PrefetchScalarGridSpec(
        num_scalar_prefetch=1, grid=(s,),
        in_specs=[pl.BlockSpec((1, N_HEADS, 1), lambda b, pt: (b, 0, 0)),
                  pl.BlockSpec(memory_space=pl.ANY)],
        out_specs=pl.BlockSpec((1, N_HEADS, length), lambda b, pt: (b, 0, 0)),
        scratch_shapes=[pltpu.VMEM((2, N_HEADS, length), f32), pltpu.SemaphoreType.DMA((2,))])
    return pl.pallas_call(
        functools.partial(_suffix_bias_kernel, layer=layer, n_pages=n_pages),
        grid_spec=grid_spec,
        out_shape=jax.ShapeDtypeStruct((s, N_HEADS, length), f32),
        compiler_params=_cparams("arbitrary"),
        name="suffix_bias",
    )(page_table, lf_new, cache_logf_t)


DEC_PAGES = 8


def _kv_chunk_copies(pt_ref, k_hbm, v_hbm, layer, seq, chunk, kbuf, vbuf, slot, sem):
    cps = []
    for i in range(DEC_PAGES):
        page = pt_ref[seq, chunk * DEC_PAGES + i]
        cols = pl.ds(i * PAGE_SIZE, PAGE_SIZE)
        cps.append(pltpu.make_async_copy(k_hbm.at[layer, page], kbuf.at[slot, :, cols],
                                         sem.at[0, slot]))
        cps.append(pltpu.make_async_copy(v_hbm.at[layer, page], vbuf.at[slot, :, cols],
                                         sem.at[1, slot]))
    return cps


def _decode_attn_kernel(pt_ref, q_ref, knew_ref, vnew_ref, bnew_ref, bias_ref, k_hbm, v_hbm,
                        o_ref, kbuf, vbuf, sem, *, layer, n_chunks):
    b = pl.program_id(0)
    q = q_ref[0]
    width = DEC_PAGES * PAGE_SIZE
    copies = functools.partial(_kv_chunk_copies, pt_ref, k_hbm, v_hbm, layer)

    @pl.when(b == 0)
    def _():
        for cp in copies(0, 0, kbuf, vbuf, 0, sem):
            cp.start()

    kn = knew_ref[0].astype(bf16).astype(f32)
    m = jnp.sum(q.astype(f32) * kn, axis=1, keepdims=True) + bnew_ref[0][:, 0:1]
    l = jnp.ones_like(m)
    acc = jnp.broadcast_to(vnew_ref[0].astype(bf16).astype(f32), (N_HEADS, KV_COLS))

    for c in range(n_chunks):
        slot = c % 2
        if c + 1 < n_chunks:
            for cp in copies(b, c + 1, kbuf, vbuf, 1 - slot, sem):
                cp.start()
        else:
            @pl.when(b + 1 < pl.num_programs(0))
            def _():
                for cp in copies(b + 1, 0, kbuf, vbuf, 1 - slot, sem):
                    cp.start()
        for cp in copies(b, c, kbuf, vbuf, slot, sem):
            cp.wait()
        s = _dot(q, kbuf[slot].astype(bf16)) + bias_ref[0, :, c * width:(c + 1) * width]
        m_new = jnp.maximum(m, jnp.max(s, axis=1, keepdims=True))
        p = jnp.exp(s - m_new)
        alpha = jnp.exp(m - m_new)
        l = alpha * l + jnp.sum(p, axis=1, keepdims=True)
        acc = alpha * acc + _dot_nt(p.astype(bf16), vbuf[slot].astype(bf16))
        m = m_new
    o_ref[0] = acc / l


def _decode_attn(page_table, qmat, k_new, v_new, bias_new, bias, cache_kt, cache_vt, layer):
    s, n_pages = page_table.shape
    hb = bias.shape[1]
    n_chunks = n_pages // DEC_PAGES
    assert n_chunks * DEC_PAGES == n_pages and n_chunks % 2 == 0
    row = lambda shape: pl.BlockSpec(shape, lambda b, pt: (b, 0, 0))
    hbm = pl.BlockSpec(memory_space=pl.ANY)
    width = DEC_PAGES * PAGE_SIZE
    grid_spec = pltpu.PrefetchScalarGridSpec(
        num_scalar_prefetch=1, grid=(s,),
        in_specs=[row((1, N_HEADS, KV_COLS)), row((1, 1, KV_COLS)), row((1, 1, KV_COLS)),
                  row((1, hb, LANES)), row((1, hb, n_pages * PAGE_SIZE)), hbm, hbm],
        out_specs=row((1, N_HEADS, KV_COLS)),
        scratch_shapes=[pltpu.VMEM((2, KV_COLS, width), f32), pltpu.VMEM((2, KV_COLS, width), f32),
                        pltpu.SemaphoreType.DMA((2, 2))])
    return pl.pallas_call(
        functools.partial(_decode_attn_kernel, layer=layer, n_chunks=n_chunks),
        grid_spec=grid_spec,
        out_shape=jax.ShapeDtypeStruct((s, N_HEADS, KV_COLS), f32),
        compiler_params=_cparams("arbitrary"),
        name="decode_attn",
    )(page_table, qmat, k_new, v_new, bias_new, bias, cache_kt, cache_vt)


def _pad_cols(w, n):
    return jnp.pad(w, ((0, 0), (0, n)))


def _decode_q(q_bf):
    s = q_bf.shape[0]
    onehot = (jnp.arange(N_HEADS)[:, None] // GROUP == jnp.arange(N_KV_HEADS)[None, :])
    q = q_bf.reshape(s, N_HEADS, 1, HEAD_DIM) * onehot[None, :, :, None].astype(q_bf.dtype)
    return q.reshape(s, N_HEADS, KV_COLS)


def _decode_o(o_full):
    s = o_full.shape[0]
    o = o_full.reshape(s, N_KV_HEADS, GROUP, N_KV_HEADS, HEAD_DIM)
    return jnp.stack([o[:, g, :, g, :] for g in range(N_KV_HEADS)], axis=1).reshape(s, -1)


def kernel(x_prompt, x_sample, cache_k_a, cache_v_a, cache_kidx_a, cache_k_b, cache_v_b,
           cache_logf_b, state_conv, page_table, c_prompt, c_sample, w_ada, b_ada, g_mix,
           g_ffn, w_in_a, w_o_a, w_in_b, b_f, w_o_b, w_up, conv_w, conv_b, w_down, g_final):
    bsz, t, d = x_prompt.shape
    s = x_sample.shape[0]
    depth = w_ada.shape[0]
    m = bsz * t
    past_len = page_table.shape[1] * PAGE_SIZE
    n_pool = cache_k_a.shape[1]
    tm = 512
    tq = 256

    pad_rows = (-(bsz + s)) % SUBLANES
    c_all = jnp.concatenate([c_prompt, c_sample, jnp.zeros((pad_rows, d), f32)], axis=0)
    mods = _adaln(c_all, w_ada, b_ada)
    mods_p = mods[:, :bsz].reshape(depth, bsz, 6, d)
    mods_s = mods[:, bsz:bsz + s].reshape(depth, s, 6, d)

    tabs_p = _rope_tables(jnp.arange(t))
    tabs_pt = _rope_tables_t(jnp.arange(t))
    tabs_s = _rope_tables(jnp.full((s,), past_len))
    zero_bf = jnp.zeros((1, LANES), f32)
    zero_col = jnp.zeros((N_HEADS, 1), f32)

    def rows_t(a):
        return a.reshape(bsz, N_KV_HEADS, HEAD_DIM, t).transpose(0, 3, 1, 2)

    xp = x_prompt.reshape(m, d)
    xs = x_sample.reshape(s, d)
    kv_shape_p = (bsz, t, N_KV_HEADS, HEAD_DIM)
    kv_shape_s = (s, 1, N_KV_HEADS, HEAD_DIM)
    ka_p, va_p, ia_p, kb_p, vb_p, lb_p, cv_p = [], [], [], [], [], [], []
    ka_s, va_s, ia_s, kb_s, vb_s, lb_s, cv_s = [], [], [], [], [], [], []

    def pages_t(cache):
        ct = jnp.moveaxis(cache, 2, -1)
        return ct.reshape(ct.shape[:2] + (-1, PAGE_SIZE))

    cache_k_a, cache_v_a, cache_kidx_a = pages_t(cache_k_a), pages_t(cache_v_a), pages_t(cache_kidx_a)
    cache_k_b, cache_v_b, cache_logf_b = pages_t(cache_k_b), pages_t(cache_v_b), pages_t(cache_logf_b)

    for i in range(depth):
        j = i // 2
        mp = [mods_p[i, :, c][:, None, :] for c in range(6)]
        ms = [mods_s[i, :, c][None] for c in range(6)]
        g_m = g_mix[i][None]
        if i % 2 == 0:
            w = w_in_a[j]
            n_qkvi = QKV_COLS + N_IDX_HEADS * IDX_DIM + IDX_DIM
            w_in = jnp.concatenate([_pad_cols(w[:, :n_qkvi], LANES - IDX_DIM),
                                    _pad_cols(w[:, n_qkvi:], LANES - N_IDX_HEADS)],
                                   axis=1).astype(bf16)
            w_o = w_o_a[j].astype(bf16)
            nq = N_HEADS * HEAD_DIM
            n_iq = QKV_COLS + N_IDX_HEADS * IDX_DIM
            wk = w[:, nq:nq + KV_COLS]
            w_row = jnp.concatenate([wk, _pad_cols(w[:, n_iq:n_qkvi], LANES - IDX_DIM)],
                                    axis=1).astype(bf16)
            w_t = jnp.concatenate([w[:, :nq], w[:, nq + KV_COLS:QKV_COLS], wk, w[:, QKV_COLS:n_qkvi],
                                   _pad_cols(w[:, n_qkvi:], IW_ROWS - N_IDX_HEADS)],
                                  axis=1).T.astype(bf16)
            qt, vt, kpad, kt32, vt32, iqt, iwt, ikb, ikt32 = _in_proj_t(
                xp, g_m, mp[0], mp[1], w_row, w_t, tabs_p, tabs_pt, zero_col, True, tm, t)
            xp = _dsa_attn(xp, mp[2], qt, kpad, vt, iqt, iwt, ikb.reshape(bsz, t, IDX_DIM),
                           w_o.T, tq, min(TOPK_MAX, t // 4))
            ka_p.append(rows_t(kt32))
            va_p.append(rows_t(vt32))
            ia_p.append(ikt32.transpose(0, 2, 1))
            q, k32, kb, v32, vb, iq, ik32, ikb, iw = _in_proj(
                xs, g_m, ms[0], ms[1], w_in, tabs_s, zero_bf, SEGS_A, OUTS_A, s, s)
            scores = _sample_scores(page_table, iq.reshape(s, N_IDX_HEADS, IDX_DIM),
                                    iw[:, :N_IDX_HEADS].reshape(s, N_IDX_HEADS, 1),
                                    cache_kidx_a, j)
            bias, bias_new = _sample_select(scores, iq, jnp.tile(ikb[:, :IDX_DIM], (1, N_IDX_HEADS)),
                                            iw, min(TOPK_MAX, (past_len + 1) // 4))
            o_full = _decode_attn(page_table, _decode_q(q), k32[:, None], v32[:, None],
                                  bias_new[:, None], bias[:, None], cache_k_a, cache_v_a, j)
            xs = _out_proj(xs, _decode_o(o_full).astype(bf16), w_o, ms[2], s, s)
            ka_s.append(k32.reshape(kv_shape_s))
            va_s.append(v32.reshape(kv_shape_s))
            ia_s.append(ik32[:, :IDX_DIM].reshape(s, 1, IDX_DIM))
        else:
            w_in = _pad_cols(w_in_b[j], LANES - N_HEADS).astype(bf16)
            w_o = w_o_b[j].astype(bf16)
            bf_pad = _pad_cols(b_f[j][None], LANES - N_HEADS)
            nq = N_HEADS * HEAD_DIM
            wk = w_in[:, nq:nq + KV_COLS]
            w_t = jnp.concatenate([w_in[:, :nq], w_in[:, nq + KV_COLS:QKV_COLS], wk,
                                   w_in[:, QKV_COLS:QKV_COLS + N_HEADS]], axis=1).T
            qt, vt, kpad, kt32, vt32, lft = _in_proj_t(
                xp, g_m, mp[0], mp[1], wk, w_t, tabs_p, tabs_pt, b_f[j][:, None], False, tm, t)
            c_t = _cumsum_t(lft, tq)
            c_pad = jnp.pad(c_t.transpose(0, 2, 1), ((0, 0), (0, 0), (0, LANES - N_HEADS)))
            xp = _fox_attn(xp, mp[2], qt, kpad, vt, c_pad, c_t, w_o.T, tq)
            kb_p.append(rows_t(kt32))
            vb_p.append(rows_t(vt32))
            lb_p.append(lft.transpose(0, 2, 1))
            q, k32, kb, v32, vb, lf = _in_proj(
                xs, g_m, ms[0], ms[1], w_in, tabs_s, bf_pad, SEGS_B, OUTS_B, s, s)
            bias = _suffix_bias(page_table, lf[:, :N_HEADS, None], cache_logf_b, j)
            o_full = _decode_attn(page_table, _decode_q(q), k32[:, None], v32[:, None],
                                  jnp.zeros((s, N_HEADS, LANES), f32), bias,
                                  cache_k_b, cache_v_b, j)
            xs = _out_proj(xs, _decode_o(o_full).astype(bf16), w_o, ms[2], s, s)
            kb_s.append(k32.reshape(kv_shape_s))
            vb_s.append(v32.reshape(kv_shape_s))
            lb_s.append(lf[:, :N_HEADS].reshape(s, 1, N_HEADS))

        wa = w_up[i][:, :D_FF].astype(bf16)
        wb = w_up[i][:, D_FF:].astype(bf16)
        wd = w_down[i].astype(bf16)
        cw = jnp.pad(conv_w[i], ((0, SUBLANES - CONV_W), (0, 0)))
        cb = conv_b[i][None]
        g_f = g_ffn[i][None]
        xp, tail = _ffn(xp, g_f, mp[3], mp[4], mp[5], wa, wb, cw, cb, wd, tm, t)
        cv_p.append(tail.reshape(bsz, t // tm, SUBLANES, D_FF)[:, -1, SUBLANES - (CONV_W - 1):])
        xs, a_s = _ffn_sample(xs, g_f, ms[3], ms[4], ms[5], state_conv[i][:, 0],
                              state_conv[i][:, 1], wa, wb, cw, cb, wd, D_FF // 2)
        cv_s.append(jnp.stack([state_conv[i][:, 1], a_s], axis=1))

    y_prompt = _final_norm(xp, g_final[None], tm).reshape(bsz, t, d)
    y_sample = _final_norm(xs, g_final[None], s).reshape(s, 1, d)
    return (y_prompt, y_sample,
            jnp.stack(ka_p), jnp.stack(va_p), jnp.stack(ia_p),
            jnp.stack(kb_p), jnp.stack(vb_p), jnp.stack(lb_p), jnp.stack(cv_p),
            jnp.stack(ka_s), jnp.stack(va_s), jnp.stack(ia_s),
            jnp.stack(kb_s), jnp.stack(vb_s), jnp.stack(lb_s), jnp.stack(cv_s))
```

```python
import functools

import jax
import jax.numpy as jnp
from jax import lax
from jax.experimental import pallas as pl
from jax.experimental.pallas import tpu as pltpu

f32 = jnp.float32
bf16 = jnp.bfloat16
i32 = jnp.int32

D_MODEL = 1024
N_HEADS = 16
HEAD_DIM = 64
N_KV_HEADS = 4
GROUP = N_HEADS // N_KV_HEADS
KV_COLS = N_KV_HEADS * HEAD_DIM
QKV_COLS = (N_HEADS + 2 * N_KV_HEADS) * HEAD_DIM
ROPE_DIMS = HEAD_DIM // 4
ROPE_HALF = ROPE_DIMS // 2
ROPE_THETA = 500000.0
N_IDX_HEADS = 8
IDX_DIM = 64
TOPK_MAX = 256
D_FF = 2816
CONV_W = 3
EPS = 1e-6
PAGE_SIZE = 128
Q_SCALE = HEAD_DIM ** -0.5
LOG2E = 1.4426950408889634

LANES = 128
SUBLANES = 8
VMEM_LIMIT = 56 * 1024 * 1024
NEG = -1e30
INT_MIN = -2 ** 31
INT_MAX = 2 ** 31 - 1
KEY_NEG_INF = (0xFF800000 ^ 0x7FFFFFFF) - 2 ** 32


def _cparams(*sem):
    return pltpu.CompilerParams(dimension_semantics=sem, vmem_limit_bytes=VMEM_LIMIT)


def _dot(a, b):
    return jnp.dot(a, b, preferred_element_type=f32)


def _dot_nt(a, b):
    return lax.dot_general(a, b, (((1,), (1,)), ((), ())), preferred_element_type=f32)


def _split3(x):
    hi = x.astype(bf16)
    r1 = x - hi.astype(f32)
    mid = r1.astype(bf16)
    lo = (r1 - mid.astype(f32)).astype(bf16)
    return hi, mid, lo


def _norm_mod(x, g, shift, scale):
    xn = x * lax.rsqrt(jnp.mean(x * x, axis=-1, keepdims=True) + EPS)
    return (xn * g) * (1.0 + scale) + shift


def _silu(x):
    return x * jax.nn.sigmoid(x)


def _sort_key(s):
    bits = pltpu.bitcast(s, i32)
    return bits ^ ((bits >> 31) & 0x7FFFFFFF)


def _adaln_kernel(c_ref, w_ref, b_ref, o_ref):
    a = _silu(c_ref[...]).astype(bf16)
    o_ref[0] = _dot(a, w_ref[0].astype(bf16)) + b_ref[0]


def _adaln(c_all, w_ada, b_ada):
    depth, d, n = w_ada.shape
    rows = c_all.shape[0]
    tn = 1536
    return pl.pallas_call(
        _adaln_kernel,
        grid=(depth, n // tn),
        in_specs=[pl.BlockSpec((rows, d), lambda l, j: (0, 0)),
                  pl.BlockSpec((1, d, tn), lambda l, j: (l, 0, j)),
                  pl.BlockSpec((1, 1, tn), lambda l, j: (l, 0, j))],
        out_specs=pl.BlockSpec((1, rows, tn), lambda l, j: (l, 0, j)),
        out_shape=jax.ShapeDtypeStruct((depth, rows, n), f32),
        compiler_params=_cparams("parallel", "parallel"),
        name="adaln",
    )(c_all, w_ada, b_ada.reshape(depth, 1, n))


SEGS_A = ((0, 1024, "rope", ((0, Q_SCALE),)),
          (1024, 256, "rope", ((1, 1.0), (2, 1.0))),
          (1280, 256, None, ((3, 1.0), (4, 1.0))),
          (1536, 512, "rope", ((5, 1.0),)),
          (2048, 128, "rope", ((6, 1.0), (7, 1.0))),
          (2176, 128, None, ((8, 1.0),)))
OUTS_A = ((1024, bf16), (256, f32), (256, bf16), (256, f32), (256, bf16),
          (512, bf16), (128, f32), (128, bf16), (128, f32))
SEGS_B = ((0, 1024, None, ((0, Q_SCALE),)),
          (1024, 256, None, ((1, 1.0), (2, 1.0))),
          (1280, 256, None, ((3, 1.0), (4, 1.0))),
          (1536, 128, "logf", ((5, 1.0),)))
OUTS_B = ((1024, bf16), (256, f32), (256, bf16), (256, f32), (256, bf16), (128, f32))


def _in_proj_kernel(x_ref, g_ref, shift_ref, scale_ref, w_ref, cos_ref, sa_ref, sb_ref,
                    bf_ref, *out_refs, segs):
    h = _norm_mod(x_ref[...], g_ref[...], shift_ref[0], scale_ref[0]).astype(bf16)
    for c0, width, epi, outs in segs:
        for c in range(width // LANES):
            lo = c0 + c * LANES
            a = _dot(h, w_ref[:, lo:lo + LANES])
            if epi == "rope":
                a = (a * cos_ref[...] + pltpu.roll(a, ROPE_HALF, 1) * sa_ref[...]
                     + pltpu.roll(a, LANES - ROPE_HALF, 1) * sb_ref[...])
            elif epi == "logf":
                z = a + bf_ref[...]
                a = jnp.minimum(z, 0.0) - jnp.log(1.0 + jnp.exp(-jnp.abs(z)))
            for oi, sc in outs:
                val = a if sc == 1.0 else a * sc
                out_refs[oi][:, c * LANES:(c + 1) * LANES] = val.astype(out_refs[oi].dtype)


def _in_proj(x, g, shift, scale, w, tabs, b_f, segs, outs, tm, rows_per_mod):
    m, d = x.shape
    n = w.shape[1]
    r = shift.shape[1]
    cos, sa, sb = tabs
    tab_blocks = cos.shape[0] // tm
    mod_spec = pl.BlockSpec((1, r, d), lambda i: (i * tm // rows_per_mod, 0, 0))
    tab_spec = pl.BlockSpec((tm, LANES), lambda i: (i % tab_blocks, 0))
    return pl.pallas_call(
        functools.partial(_in_proj_kernel, segs=segs),
        grid=(m // tm,),
        in_specs=[pl.BlockSpec((tm, d), lambda i: (i, 0)),
                  pl.BlockSpec((1, d), lambda i: (0, 0)),
                  mod_spec, mod_spec,
                  pl.BlockSpec((d, n), lambda i: (0, 0)),
                  tab_spec, tab_spec, tab_spec,
                  pl.BlockSpec((1, LANES), lambda i: (0, 0))],
        out_specs=[pl.BlockSpec((tm, wd), lambda i: (i, 0)) for wd, _ in outs],
        out_shape=[jax.ShapeDtypeStruct((m, wd), dt) for wd, dt in outs],
        compiler_params=_cparams("parallel"),
        name="in_proj",
    )(x, g, shift, scale, w, cos, sa, sb, b_f)


def _rope_tables(pos):
    inv_freq = ROPE_THETA ** (-jnp.arange(ROPE_HALF, dtype=f32) / ROPE_HALF)
    ang = pos.astype(f32)[:, None] * inv_freq[None, :]
    cos, sin = jnp.cos(ang), jnp.sin(ang)
    t = pos.shape[0]
    ones = jnp.ones((t, HEAD_DIM - ROPE_DIMS), f32)
    zeros = jnp.zeros((t, HEAD_DIM - ROPE_DIMS), f32)
    zh = jnp.zeros((t, ROPE_HALF), f32)
    c = jnp.concatenate([cos, cos, ones], axis=1)
    sa = jnp.concatenate([zh, sin, zeros], axis=1)
    sb = jnp.concatenate([-sin, zh, zeros], axis=1)
    rep = LANES // HEAD_DIM
    return tuple(jnp.tile(a, (1, rep)) for a in (c, sa, sb))


WT_Q = 0
WT_V = N_HEADS * HEAD_DIM
WT_K = WT_V + KV_COLS
WT_X = WT_K + KV_COLS
WT_IK = WT_X + N_IDX_HEADS * IDX_DIM
WT_IW = WT_IK + IDX_DIM
IW_ROWS = 2 * SUBLANES
ROW_BLOCK = 512


def _rope_rows(a, cos_t, sin_t):
    parts = []
    for b0 in range(0, a.shape[0], HEAD_DIM):
        x1 = a[b0:b0 + ROPE_HALF]
        x2 = a[b0 + ROPE_HALF:b0 + ROPE_DIMS]
        parts += [x1 * cos_t - x2 * sin_t, x1 * sin_t + x2 * cos_t, a[b0 + ROPE_DIMS:b0 + HEAD_DIM]]
    return jnp.concatenate(parts, axis=0)


def _in_proj_t_kernel(x_ref, g_ref, shift_ref, scale_ref, w_ref, wt_ref, cos_ref, sa_ref, sb_ref,
                      cost_ref, sint_ref, bf_ref, *outs, is_a):
    if is_a:
        qt_ref, vt_ref, kpad_ref, kt32_ref, vt32_ref, iqt_ref, iwt_ref, ikb_ref, ikt32_ref = outs
    else:
        qt_ref, vt_ref, kpad_ref, kt32_ref, vt32_ref, lft_ref = outs
    h = _norm_mod(x_ref[...], g_ref[...], shift_ref[0], scale_ref[0]).astype(bf16)
    cos_t, sin_t = cost_ref[...], sint_ref[...]

    def rows(r0, n):
        blk = _dot_nt(wt_ref[r0:r0 + n, :], h)
        return [blk[c * LANES:(c + 1) * LANES] for c in range(n // LANES)]

    def rope_cols(a):
        return (a * cos_ref[...] + pltpu.roll(a, ROPE_HALF, 1) * sa_ref[...]
                + pltpu.roll(a, LANES - ROPE_HALF, 1) * sb_ref[...])

    for r0 in range(0, N_HEADS * HEAD_DIM, ROW_BLOCK):
        for c, a in enumerate(rows(WT_Q + r0, ROW_BLOCK)):
            if is_a:
                a = _rope_rows(a, cos_t, sin_t)
            lo = r0 + c * LANES
            qt_ref[lo:lo + LANES, :] = (a * (Q_SCALE * LOG2E)).astype(bf16)
    for c, (v, k) in enumerate(zip(rows(WT_V, KV_COLS), rows(WT_K, KV_COLS))):
        sl = slice(c * LANES, (c + 1) * LANES)
        vt32_ref[0, sl, :] = v
        vt_ref[sl, :] = v.astype(bf16)
        kt32_ref[0, sl, :] = _rope_rows(k, cos_t, sin_t) if is_a else k
    if is_a:
        for c, a in enumerate(rows(WT_X, N_IDX_HEADS * IDX_DIM)):
            iqt_ref[c * LANES:(c + 1) * LANES, :] = _rope_rows(a, cos_t, sin_t).astype(bf16)
        ikt32_ref[0] = _rope_rows(_dot_nt(wt_ref[WT_IK:WT_IK + IDX_DIM, :], h), cos_t, sin_t)
        iwt_ref[...] = _dot_nt(wt_ref[WT_IW:WT_IW + IW_ROWS, :], h)
    else:
        z = _dot_nt(wt_ref[WT_X:WT_X + N_HEADS, :], h) + bf_ref[...]
        lft_ref[0] = jnp.minimum(z, 0.0) - jnp.log(1.0 + jnp.exp(-jnp.abs(z)))

    lane = lax.broadcasted_iota(i32, (1, LANES), 1)
    std = _dot(h, w_ref[...])
    for c in range(KV_COLS // LANES):
        a = std[:, c * LANES:(c + 1) * LANES]
        if is_a:
            a = rope_cols(a)
        kpad_ref[0, 2 * c] = jnp.where(lane < HEAD_DIM, a, 0.0).astype(bf16)
        kpad_ref[0, 2 * c + 1] = jnp.where(lane < HEAD_DIM, pltpu.roll(a, HEAD_DIM, 1),
                                           0.0).astype(bf16)
    if is_a:
        ikb_ref[...] = rope_cols(std[:, KV_COLS:KV_COLS + LANES])[:, :IDX_DIM].astype(bf16)


def _in_proj_t(x, g, shift, scale, w, wt, tabs, tabs_t, b_f, is_a, tm, t):
    m, d = x.shape
    bsz = m // t
    tpb = t // tm
    col = lambda rows_: pl.BlockSpec((rows_, tm), lambda i: (0, i))
    seq = lambda rows_: pl.BlockSpec((1, rows_, tm), lambda i: (i // tpb, 0, i % tpb))
    seq_shape = lambda rows_: jax.ShapeDtypeStruct((bsz, rows_, t), f32)
    const = lambda a: pl.BlockSpec(a.shape, lambda i: (0,) * a.ndim)
    mod_spec = pl.BlockSpec((1, 1, d), lambda i: (i // tpb, 0, 0))
    tab_spec = pl.BlockSpec((tm, LANES), lambda i: (i % tpb, 0))
    tabt_spec = pl.BlockSpec((ROPE_HALF, tm), lambda i: (0, i % tpb))
    nq = N_HEADS * HEAD_DIM
    out_specs = [col(nq), col(KV_COLS),
                 pl.BlockSpec((1, N_KV_HEADS, tm, LANES), lambda i: (i // tpb, 0, i % tpb, 0)),
                 seq(KV_COLS), seq(KV_COLS)]
    out_shape = [jax.ShapeDtypeStruct((nq, m), bf16), jax.ShapeDtypeStruct((KV_COLS, m), bf16),
                 jax.ShapeDtypeStruct((bsz, N_KV_HEADS, t, LANES), bf16),
                 seq_shape(KV_COLS), seq_shape(KV_COLS)]
    if is_a:
        out_specs += [col(N_IDX_HEADS * IDX_DIM), col(IW_ROWS),
                      pl.BlockSpec((tm, IDX_DIM), lambda i: (i, 0)), seq(IDX_DIM)]
        out_shape += [jax.ShapeDtypeStruct((N_IDX_HEADS * IDX_DIM, m), bf16),
                      jax.ShapeDtypeStruct((IW_ROWS, m), f32),
                      jax.ShapeDtypeStruct((m, IDX_DIM), bf16), seq_shape(IDX_DIM)]
    else:
        out_specs += [seq(N_HEADS)]
        out_shape += [seq_shape(N_HEADS)]
    return pl.pallas_call(
        functools.partial(_in_proj_t_kernel, is_a=is_a),
        grid=(m // tm,),
        in_specs=[pl.BlockSpec((tm, d), lambda i: (i, 0)), const(g), mod_spec, mod_spec,
                  const(w), const(wt), tab_spec, tab_spec, tab_spec, tabt_spec, tabt_spec,
                  const(b_f)],
        out_specs=out_specs, out_shape=out_shape,
        compiler_params=_cparams("parallel"),
        name="in_proj_t",
    )(x, g, shift, scale, w, wt, *tabs, *tabs_t, b_f)


def _rope_tables_t(pos):
    inv_freq = ROPE_THETA ** (-jnp.arange(ROPE_HALF, dtype=f32) / ROPE_HALF)
    ang = inv_freq[:, None] * pos.astype(f32)[None, :]
    return jnp.cos(ang), jnp.sin(ang)


AUG_STRIDE = 4
AUG_ONES = GROUP * AUG_STRIDE


ATTN_TK = 256
SUM_ROWS = 2 * SUBLANES
ACC_ROWS = HEAD_DIM + SUM_ROWS


def _t_init(m_sc, acc_sc):
    m_sc[...] = jnp.full(m_sc.shape, NEG, f32)
    acc_sc[...] = jnp.zeros(acc_sc.shape, f32)


def _t_step(s, vt1, h, m_sc, acc_sc):
    m_old = m_sc[h]
    m_new = jnp.maximum(m_old, jnp.max(s, axis=0, keepdims=True))
    p = jnp.exp2((s - m_new).astype(bf16))
    acc_sc[h] = jnp.exp2(m_old - m_new) * acc_sc[h] + _dot(vt1, p)
    m_sc[h] = m_new


def _t_tile(off, tk, kt_of, vt_ref, qa_sc, post, m_sc, acc_sc):
    kts = [kt_of(g) for g in range(N_KV_HEADS)]
    ones = jnp.ones((SUM_ROWS, tk), bf16)
    vts = [jnp.concatenate([vt_ref[g * HEAD_DIM:(g + 1) * HEAD_DIM, pl.ds(off, tk)], ones], axis=0)
           for g in range(N_KV_HEADS)]
    ahead = 2
    pending = [_dot(kts[g], qa_sc[g]) for g in range(ahead)]
    for g in range(N_KV_HEADS):
        s = pending.pop(0)
        if g + ahead < N_KV_HEADS:
            pending.append(_dot(kts[g + ahead], qa_sc[g + ahead]))
        _t_step(post(s), vts[g], g, m_sc, acc_sc)


def _t_fill_q(qa_sc, h, tq, q, aug):
    g, jj = divmod(h, GROUP)
    qa_sc[g, :HEAD_DIM, jj * tq:(jj + 1) * tq] = q
    qa_sc[g, HEAD_DIM:, jj * tq:(jj + 1) * tq] = aug


def _t_finish(tq, acc_sc, ot_sc):
    for h in range(N_HEADS):
        g, jj = divmod(h, GROUP)
        cols = slice(jj * tq, (jj + 1) * tq)
        ot_sc[h * HEAD_DIM:(h + 1) * HEAD_DIM, :] = (
            acc_sc[g, :HEAD_DIM, cols] / acc_sc[g, HEAD_DIM:HEAD_DIM + 1, cols]).astype(bf16)


def _t_out(x_ref, gate_ref, wot_ref, ot_sc, o_ref):
    y_t = _dot(wot_ref[...], ot_sc[...])
    o_ref[...] = x_ref[...] + gate_ref[0] * y_t.T


def _fox_attn_kernel(x_ref, gate_ref, qt_ref, kpad_ref, vt_ref, cpad_ref, ct_ref, wot_ref,
                     o_ref, kaug_sc, m_sc, acc_sc, qa_sc, ot_sc, *, tq, tk, chunk):
    qi = pl.program_id(1)
    t = kaug_sc.shape[1]

    @pl.when(qi == 0)
    def _():
        r = lax.broadcasted_iota(i32, (LANES, LANES), 0)
        l = lax.broadcasted_iota(i32, (LANES, LANES), 1) - HEAD_DIM
        lane = lax.broadcasted_iota(i32, (1, LANES), 1) - HEAD_DIM
        ones = jnp.where((lane >= AUG_ONES) & (lane < AUG_ONES + 3), 1.0, 0.0)
        for g in range(N_KV_HEADS):
            perm = [((l >= 0) & (l < AUG_ONES) & ((l >> 2) == r - g * GROUP)
                     & ((l & 3) == p)).astype(bf16) for p in range(3)]

            def body(i, c, g=g, perm=perm):
                off = pl.multiple_of(i * chunk, chunk)
                pieces = _split3(cpad_ref[0, pl.ds(off, chunk), :] * -LOG2E)
                aug = sum(_dot(pc, pm) for pc, pm in zip(pieces, perm)) + ones
                kaug_sc[g, pl.ds(off, chunk), :] = (
                    kpad_ref[0, g, pl.ds(off, chunk), :].astype(f32) + aug).astype(bf16)
                return c

            lax.fori_loop(0, t // chunk, body, 0)

    ri = lax.broadcasted_iota(i32, (HEAD_DIM, tq), 0)
    wide = (tk, GROUP * tq)
    d_iota = (lax.broadcasted_iota(i32, wide, 0)
              - (lax.broadcasted_iota(i32, wide, 1) & (tq - 1)))
    for h in range(N_HEADS):
        jj = h % GROUP
        cq = _split3(ct_ref[0, h:h + 1, :] * LOG2E)
        aug = jnp.where((ri >= jj * AUG_STRIDE) & (ri < jj * AUG_STRIDE + 3), 1.0, 0.0)
        for p in range(3):
            aug = jnp.where(ri == AUG_ONES + p, cq[p].astype(f32), aug)
        _t_fill_q(qa_sc, h, tq, qt_ref[h * HEAD_DIM:(h + 1) * HEAD_DIM, :], aug.astype(bf16))
    _t_init(m_sc, acc_sc)

    def tile(j, post):
        off = pl.multiple_of(j * tk, tk)
        _t_tile(off, tk, lambda g: kaug_sc[g, pl.ds(off, tk), :], vt_ref, qa_sc, post,
                m_sc, acc_sc)

    def body(j, c):
        tile(j, lambda s: s)
        return c

    sub = tq // tk
    lax.fori_loop(0, qi * sub, body, 0)
    for dd in range(sub):
        tile(qi * sub + dd, lambda s, dd=dd: jnp.where(d_iota <= -dd * tk, s, NEG))
    _t_finish(tq, acc_sc, ot_sc)
    _t_out(x_ref, gate_ref, wot_ref, ot_sc, o_ref)


def _attn_specs(bsz, t, d, tq):
    nq = t // tq
    return dict(
        x=pl.BlockSpec((tq, d), lambda b, q: (b * nq + q, 0)),
        gate=pl.BlockSpec((1, 1, d), lambda b, q: (b, 0, 0)),
        qt=pl.BlockSpec((N_HEADS * HEAD_DIM, tq), lambda b, q: (0, b * nq + q)),
        kpad=pl.BlockSpec((1, N_KV_HEADS, t, LANES), lambda b, q: (b, 0, 0, 0)),
        vt=pl.BlockSpec((KV_COLS, t), lambda b, q: (0, b)),
        wot=pl.BlockSpec((d, N_HEADS * HEAD_DIM), lambda b, q: (0, 0)),
        scratch=[pltpu.VMEM((N_KV_HEADS, 1, GROUP * tq), f32),
                 pltpu.VMEM((N_KV_HEADS, ACC_ROWS, GROUP * tq), f32),
                 pltpu.VMEM((N_KV_HEADS, LANES, GROUP * tq), bf16),
                 pltpu.VMEM((N_HEADS * HEAD_DIM, tq), bf16)])


def _fox_attn(x, gate, qt, kpad, vt, c_pad, c_t, wot, tq):
    bsz, _, t, _ = kpad.shape
    m, d = x.shape
    sp = _attn_specs(bsz, t, d, tq)
    return pl.pallas_call(
        functools.partial(_fox_attn_kernel, tq=tq, tk=ATTN_TK, chunk=512),
        grid=(bsz, t // tq),
        in_specs=[sp["x"], sp["gate"], sp["qt"], sp["kpad"], sp["vt"],
                  pl.BlockSpec((1, t, LANES), lambda b, q: (b, 0, 0)),
                  pl.BlockSpec((1, N_HEADS, tq), lambda b, q: (b, 0, q)),
                  sp["wot"]],
        out_specs=sp["x"],
        out_shape=jax.ShapeDtypeStruct((m, d), f32),
        scratch_shapes=[pltpu.VMEM((N_KV_HEADS, t, LANES), bf16)] + sp["scratch"],
        compiler_params=_cparams("parallel", "arbitrary"),
        name="fox_attn",
    )(x, gate, qt, kpad, vt, c_pad, c_t, wot)


def _cumsum_t_kernel(lft_ref, ct_ref, carry_sc, *, tb):
    @pl.when(pl.program_id(1) == 0)
    def _():
        carry_sc[...] = jnp.zeros(carry_sc.shape, f32)

    tri = (lax.broadcasted_iota(i32, (tb, tb), 0)
           <= lax.broadcasted_iota(i32, (tb, tb), 1)).astype(bf16)
    lft = lft_ref[0]
    csum = sum(_dot(p, tri) for p in _split3(lft))
    ct_ref[0] = csum + carry_sc[...]
    carry_sc[...] = carry_sc[...] + jnp.sum(lft, axis=1, keepdims=True)


def _cumsum_t(lf_t, tb):
    b, nh, t = lf_t.shape
    spec = pl.BlockSpec((1, nh, tb), lambda bi, j: (bi, 0, j))
    return pl.pallas_call(
        functools.partial(_cumsum_t_kernel, tb=tb),
        grid=(b, t // tb),
        in_specs=[spec],
        out_specs=spec,
        out_shape=jax.ShapeDtypeStruct((b, nh, t), f32),
        scratch_shapes=[pltpu.VMEM((nh, 1), f32)],
        compiler_params=_cparams("parallel", "arbitrary"),
        name="cumsum_t",
    )(lf_t)


def _kth_threshold(count_ge, shape, n_sel):
    zero = jnp.zeros(shape, i32)
    ans = jnp.where(count_ge(zero) >= n_sel, zero, jnp.full(shape, INT_MIN, i32))

    def bit_body(i, ans):
        cand = ans | jnp.left_shift(jnp.int32(1), 30 - i)
        return jnp.where(count_ge(cand) >= n_sel, cand, ans)

    return lax.fori_loop(0, 31, bit_body, ans)


def _kth_threshold_early(count_ge, shape, n_sel, n_total):
    zero = jnp.zeros(shape, i32)
    c0 = count_ge(zero)
    ans = jnp.where(c0 >= n_sel, zero, jnp.full(shape, INT_MIN, i32))
    exact = jnp.where((c0 == n_sel) | ((c0 < n_sel) & (n_total == n_sel)), 1, 0)

    def cond(state):
        i, _, _, n_open = state
        return (i < 31) & (n_open > 0)

    def body(state):
        i, ans, exact, _ = state
        cand = ans | jnp.left_shift(jnp.int32(1), 30 - i)
        c = count_ge(cand)
        take = (exact == 0) & (c >= n_sel)
        ans = jnp.where(take, cand, ans)
        exact = jnp.where(take & (c == n_sel), 1, exact)
        return i + 1, ans, exact, jnp.sum(1 - exact)

    _, ans, exact, _ = lax.while_loop(cond, body, (jnp.int32(0), ans, exact, jnp.sum(1 - exact)))
    return ans, exact


def _tie_cut(count_tied_below, need, shape, n_bits):
    def body(i, p):
        cand = p | jnp.left_shift(jnp.int32(1), n_bits - 1 - i)
        return jnp.where(count_tied_below(cand) < need, cand, p)
    return lax.fori_loop(0, n_bits, body, jnp.zeros(shape, i32))


def _fold_lanes(ind, c):
    for cc in range(ind.shape[1] // LANES):
        c = c + ind[:, cc * LANES:(cc + 1) * LANES]
    return c


def _dsa_attn_kernel(x_ref, gate_ref, qt_ref, kpad_ref, vt_ref, iqt_ref, iwt_ref, ik_ref,
                     wot_ref, o_ref, key_sc, bias_sc, cut_sc, m_sc, acc_sc, qa_sc, ot_sc,
                     *, tq, n_sel, n_bits):
    qi = pl.program_id(1)
    n_k = qi + 1
    row = lax.broadcasted_iota(i32, (tq, tq), 0)
    d_iota = row - lax.broadcasted_iota(i32, (tq, tq), 1)

    def score_body(j, c):
        off = pl.multiple_of(j * tq, tq)
        ikt = ik_ref[0, pl.ds(off, tq), :]
        s = jnp.zeros((tq, tq), f32)
        for h in range(N_IDX_HEADS):
            dots = _dot(ikt, iqt_ref[h * IDX_DIM:(h + 1) * IDX_DIM, :])
            s = s + jnp.maximum(dots, 0.0) * iwt_ref[h:h + 1, :]
        key_sc[pl.ds(off, tq), :] = jnp.where(d_iota <= (qi - j) * tq, _sort_key(s),
                                              KEY_NEG_INF)
        return c

    lax.fori_loop(0, n_k, score_body, 0)

    def count(pred):
        def body(j, c):
            off = pl.multiple_of(j * tq, tq)
            ind = jnp.where(pred(key_sc[pl.ds(off, tq), :], j), 1, 0)
            for r in range(0, tq, SUBLANES):
                c = c + ind[r:r + SUBLANES]
            return c
        c = lax.fori_loop(0, n_k, body, jnp.zeros((SUBLANES, tq), i32))
        return jnp.sum(c, axis=0, keepdims=True)

    thr, exact = _kth_threshold_early(lambda cand: count(lambda kt, j: kt >= cand), (1, tq),
                                      n_sel, n_k * tq)
    real = thr > KEY_NEG_INF
    cut_sc[...] = jnp.where(real, INT_MAX, -1)

    @pl.when(jnp.max(jnp.where(real, 1 - exact, 0)) > 0)
    def _():
        need = n_sel - count(lambda kt, j: kt > thr)
        cut = _tie_cut(lambda cand: count(lambda kt, j: (kt == thr) & (row + j * tq < cand)),
                       need, (1, tq), n_bits)
        cut_sc[...] = jnp.where(real, jnp.where(exact == 1, INT_MAX, cut), -1)

    cut = cut_sc[...]
    floor = jnp.maximum(thr, KEY_NEG_INF)

    def bias_body(j, c):
        off = pl.multiple_of(j * tq, tq)
        kt = key_sc[pl.ds(off, tq), :]
        sel = (kt > floor) | ((kt == thr) & (row + j * tq <= cut))
        bias_sc[pl.ds(off, tq), :] = jnp.where(sel, 0.0, NEG)
        return c

    lax.fori_loop(0, n_k, bias_body, 0)

    for h in range(N_HEADS):
        _t_fill_q(qa_sc, h, tq, qt_ref[h * HEAD_DIM:(h + 1) * HEAD_DIM, :],
                  jnp.zeros((LANES - HEAD_DIM, tq), bf16))
    _t_init(m_sc, acc_sc)

    def body(j, c):
        off = pl.multiple_of(j * tq, tq)

        def masked(s):
            return jnp.concatenate([s[:, jj * tq:(jj + 1) * tq] + bias_sc[pl.ds(off, tq), :]
                                    for jj in range(GROUP)], axis=1)

        _t_tile(off, tq, lambda g: kpad_ref[0, g, pl.ds(off, tq), :], vt_ref, qa_sc,
                masked, m_sc, acc_sc)
        return c

    lax.fori_loop(0, n_k, body, 0)
    _t_finish(tq, acc_sc, ot_sc)
    _t_out(x_ref, gate_ref, wot_ref, ot_sc, o_ref)


def _dsa_attn(x, gate, qt, kpad, vt, iqt, iwt, ik, wot, tq, n_sel):
    bsz, _, t, _ = kpad.shape
    m, d = x.shape
    nq = t // tq
    sp = _attn_specs(bsz, t, d, tq)
    return pl.pallas_call(
        functools.partial(_dsa_attn_kernel, tq=tq, n_sel=n_sel, n_bits=(t - 1).bit_length()),
        grid=(bsz, nq),
        in_specs=[sp["x"], sp["gate"], sp["qt"], sp["kpad"], sp["vt"],
                  pl.BlockSpec((N_IDX_HEADS * IDX_DIM, tq), lambda b, q: (0, b * nq + q)),
                  pl.BlockSpec((2 * SUBLANES, tq), lambda b, q: (0, b * nq + q)),
                  pl.BlockSpec((1, t, IDX_DIM), lambda b, q: (b, 0, 0)),
                  sp["wot"]],
        out_specs=sp["x"],
        out_shape=jax.ShapeDtypeStruct((m, d), f32),
        scratch_shapes=[pltpu.VMEM((t, tq), i32), pltpu.VMEM((t, tq), f32),
                        pltpu.VMEM((1, tq), i32)] + sp["scratch"],
        compiler_params=_cparams("parallel", "arbitrary"),
        name="dsa_attn",
    )(x, gate, qt, kpad, vt, iqt, iwt, ik, wot)


def _out_proj_kernel(x_ref, o_ref, w_ref, gate_ref, y_ref):
    y_ref[...] = x_ref[...] + gate_ref[0] * _dot(o_ref[...], w_ref[...])


def _out_proj(x, o, w, gate, tm, rows_per_mod):
    m, d = x.shape
    kdim = o.shape[1]
    r = gate.shape[1]
    return pl.pallas_call(
        _out_proj_kernel,
        grid=(m // tm,),
        in_specs=[pl.BlockSpec((tm, d), lambda i: (i, 0)),
                  pl.BlockSpec((tm, kdim), lambda i: (i, 0)),
                  pl.BlockSpec((kdim, d), lambda i: (0, 0)),
                  pl.BlockSpec((1, r, d), lambda i: (i * tm // rows_per_mod, 0, 0))],
        out_specs=pl.BlockSpec((tm, d), lambda i: (i, 0)),
        out_shape=jax.ShapeDtypeStruct((m, d), f32),
        compiler_params=_cparams("parallel"),
        name="out_proj",
    )(x, o, w, gate)


FFN_CHUNK = 256
FFN_HALO = 2 * SUBLANES


def _ffn_kernel(x_ref, xh_ref, g_ref, shift_ref, scale_ref, gate_ref, wa_ref, wb_ref,
                cw_ref, cb_ref, wd_ref, o_ref, tail_ref, acc_sc, hext_sc, *, tm, tiles_per_seq):
    i = pl.program_id(0)
    x = x_ref[...]
    keep = jnp.where(i % tiles_per_seq == 0, 0.0, 1.0)
    hext_sc[:FFN_HALO, :] = (_norm_mod(xh_ref[...], g_ref[...], shift_ref[0], scale_ref[0])
                             * keep).astype(bf16)
    hext_sc[FFN_HALO:, :] = _norm_mod(x, g_ref[...], shift_ref[0], scale_ref[0]).astype(bf16)
    for c in range(D_FF // FFN_CHUNK):
        sl = slice(c * FFN_CHUNK, (c + 1) * FFN_CHUNK)
        a_ext = _dot(hext_sc[...], wa_ref[:, sl])
        b = _dot(hext_sc[FFN_HALO:, :], wb_ref[:, sl])
        a = a_ext[FFN_HALO:]
        p1 = pltpu.roll(a_ext, 1, 0)[FFN_HALO:]
        p2 = pltpu.roll(a_ext, 2, 0)[FFN_HALO:]
        conv = cb_ref[:, sl] + p2 * cw_ref[0:1, sl] + p1 * cw_ref[1:2, sl] + a * cw_ref[2:3, sl]
        y = _dot((_silu(conv) * b).astype(bf16), wd_ref[sl, :])
        if c == 0:
            acc_sc[...] = y
        else:
            acc_sc[...] += y
        tail_ref[0, :, sl] = a_ext[FFN_HALO + tm - SUBLANES:]
    o_ref[...] = x + gate_ref[0] * acc_sc[...]


def _ffn(x, g, shift, scale, gate, wa, wb, cw, cb, wd, tm, rows_per_seq):
    m, d = x.shape
    n_tiles = m // tm
    hb = tm // FFN_HALO
    mod_spec = pl.BlockSpec((1, 1, d), lambda i: (i * tm // rows_per_seq, 0, 0))
    const = lambda shape: pl.BlockSpec(shape, lambda i: (0,) * len(shape))
    return pl.pallas_call(
        functools.partial(_ffn_kernel, tm=tm, tiles_per_seq=rows_per_seq // tm),
        grid=(n_tiles,),
        in_specs=[pl.BlockSpec((tm, d), lambda i: (i, 0)),
                  pl.BlockSpec((FFN_HALO, d), lambda i: (jnp.maximum(i * hb - 1, 0), 0)),
                  const((1, d)), mod_spec, mod_spec, mod_spec,
                  const((d, D_FF)), const((d, D_FF)), const((SUBLANES, D_FF)),
                  const((1, D_FF)), const((D_FF, d))],
        out_specs=[pl.BlockSpec((tm, d), lambda i: (i, 0)),
                   pl.BlockSpec((1, SUBLANES, D_FF), lambda i: (i, 0, 0))],
        out_shape=[jax.ShapeDtypeStruct((m, d), f32),
                   jax.ShapeDtypeStruct((n_tiles, SUBLANES, D_FF), f32)],
        scratch_shapes=[pltpu.VMEM((tm, d), f32), pltpu.VMEM((FFN_HALO + tm, d), bf16)],
        compiler_params=_cparams("parallel"),
        name="ffn",
    )(x, x, g, shift, scale, gate, wa, wb, cw, cb, wd)


def _ffn_sample_kernel(x_ref, g_ref, shift_ref, scale_ref, gate_ref, p2_ref, p1_ref, wa_ref,
                       wb_ref, cw_ref, cb_ref, wd_ref, o_ref, a_ref, acc_sc):
    j = pl.program_id(0)

    @pl.when(j == 0)
    def _():
        acc_sc[...] = jnp.zeros(acc_sc.shape, f32)

    x = x_ref[...]
    h = _norm_mod(x, g_ref[...], shift_ref[0], scale_ref[0]).astype(bf16)
    a = _dot(h, wa_ref[...])
    b = _dot(h, wb_ref[...])
    conv = (cb_ref[...] + p2_ref[...] * cw_ref[0:1, :] + p1_ref[...] * cw_ref[1:2, :]
            + a * cw_ref[2:3, :])
    acc_sc[...] += _dot((_silu(conv) * b).astype(bf16), wd_ref[...])
    a_ref[...] = a

    @pl.when(j == pl.num_programs(0) - 1)
    def _():
        o_ref[...] = x + gate_ref[0] * acc_sc[...]


def _ffn_sample(x, g, shift, scale, gate, p2, p1, wa, wb, cw, cb, wd, tn):
    m, d = x.shape
    const = lambda shape: pl.BlockSpec(shape, lambda j: (0,) * len(shape))
    return pl.pallas_call(
        _ffn_sample_kernel,
        grid=(D_FF // tn,),
        in_specs=[const((m, d)), const((1, d)), const((1, m, d)), const((1, m, d)),
                  const((1, m, d)),
                  pl.BlockSpec((m, tn), lambda j: (0, j)), pl.BlockSpec((m, tn), lambda j: (0, j)),
                  pl.BlockSpec((d, tn), lambda j: (0, j)), pl.BlockSpec((d, tn), lambda j: (0, j)),
                  pl.BlockSpec((SUBLANES, tn), lambda j: (0, j)),
                  pl.BlockSpec((1, tn), lambda j: (0, j)),
                  pl.BlockSpec((tn, d), lambda j: (j, 0))],
        out_specs=[const((m, d)), pl.BlockSpec((m, tn), lambda j: (0, j))],
        out_shape=[jax.ShapeDtypeStruct((m, d), f32), jax.ShapeDtypeStruct((m, D_FF), f32)],
        scratch_shapes=[pltpu.VMEM((m, d), f32)],
        compiler_params=_cparams("arbitrary"),
        name="ffn_sample",
    )(x, g, shift, scale, gate, p2, p1, wa, wb, cw, cb, wd)


def _final_norm_kernel(x_ref, g_ref, o_ref):
    x = x_ref[...]
    o_ref[...] = x * lax.rsqrt(jnp.mean(x * x, axis=-1, keepdims=True) + EPS) * g_ref[...]


def _final_norm(x, g, tm):
    m, d = x.shape
    return pl.pallas_call(
        _final_norm_kernel,
        grid=(m // tm,),
        in_specs=[pl.BlockSpec((tm, d), lambda i: (i, 0)), pl.BlockSpec((1, d), lambda i: (0, 0))],
        out_specs=pl.BlockSpec((tm, d), lambda i: (i, 0)),
        out_shape=jax.ShapeDtypeStruct((m, d), f32),
        compiler_params=_cparams("parallel"),
        name="final_norm",
    )(x, g)


def _seq_page_copies(pt_ref, hbm, layer, seq, buf, slot, sem, n_pages):
    return [pltpu.make_async_copy(hbm.at[layer, pt_ref[seq, p]],
                                  buf.at[slot, :, pl.ds(p * PAGE_SIZE, PAGE_SIZE)], sem.at[slot])
            for p in range(n_pages)]


def _gather_seq(pt_ref, hbm, layer, buf, sem, n_pages):
    b = pl.program_id(0)
    slot = b % 2

    @pl.when(b == 0)
    def _():
        for cp in _seq_page_copies(pt_ref, hbm, layer, 0, buf, 0, sem, n_pages):
            cp.start()

    @pl.when(b + 1 < pl.num_programs(0))
    def _():
        for cp in _seq_page_copies(pt_ref, hbm, layer, b + 1, buf, 1 - slot, sem, n_pages):
            cp.start()

    for cp in _seq_page_copies(pt_ref, hbm, layer, b, buf, slot, sem, n_pages):
        cp.wait()
    return slot


def _sample_score_kernel(pt_ref, iq_ref, iw_ref, kidx_hbm, s_ref, buf, sem, *, layer, n_pages):
    slot = _gather_seq(pt_ref, kidx_hbm, layer, buf, sem, n_pages)
    d = _dot(iq_ref[0], buf[slot].astype(bf16))
    s_ref[0] = jnp.sum(jnp.maximum(d, 0.0) * iw_ref[0], axis=0, keepdims=True)


def _sample_scores(page_table, iq, iw, cache_kidx_t, layer):
    s, n_pages = page_table.shape
    length = n_pages * PAGE_SIZE
    grid_spec = pltpu.PrefetchScalarGridSpec(
        num_scalar_prefetch=1, grid=(s,),
        in_specs=[pl.BlockSpec((1, N_IDX_HEADS, IDX_DIM), lambda b, pt: (b, 0, 0)),
                  pl.BlockSpec((1, N_IDX_HEADS, 1), lambda b, pt: (b, 0, 0)),
                  pl.BlockSpec(memory_space=pl.ANY)],
        out_specs=pl.BlockSpec((1, 1, length), lambda b, pt: (b, 0, 0)),
        scratch_shapes=[pltpu.VMEM((2, IDX_DIM, length), f32), pltpu.SemaphoreType.DMA((2,))])
    out = pl.pallas_call(
        functools.partial(_sample_score_kernel, layer=layer, n_pages=n_pages),
        grid_spec=grid_spec,
        out_shape=jax.ShapeDtypeStruct((s, 1, length), f32),
        compiler_params=_cparams("arbitrary"),
        name="sample_scores",
    )(page_table, iq, iw, cache_kidx_t)
    return out.reshape(s, length)


SEL_TILE = 1024


def _sample_select_kernel(s_ref, iq_ref, ikt_ref, iw_ref, bias_ref, bnew_ref, key_sc,
                          *, n_sel, n_bits):
    rows, length = s_ref.shape
    n_tiles = length // SEL_TILE
    prod = iq_ref[...].astype(f32) * ikt_ref[...].astype(f32)
    grp = (lax.broadcasted_iota(i32, (prod.shape[1], LANES), 0) // IDX_DIM
           == lax.broadcasted_iota(i32, (prod.shape[1], LANES), 1)).astype(bf16)
    d_new = sum(_dot(p, grp) for p in _split3(prod))
    s_new = jnp.sum(jnp.maximum(d_new, 0.0) * iw_ref[...], axis=1, keepdims=True)
    key_new = _sort_key(s_new)
    key_sc[...] = _sort_key(s_ref[...])
    col = lax.broadcasted_iota(i32, (rows, SEL_TILE), 1)

    def count(pred, pred_new):
        def body(j, c):
            off = pl.multiple_of(j * SEL_TILE, SEL_TILE)
            return _fold_lanes(jnp.where(pred(key_sc[:, pl.ds(off, SEL_TILE)], j), 1, 0), c)
        c = lax.fori_loop(0, n_tiles, body, jnp.zeros((rows, LANES), i32))
        return jnp.sum(c, axis=1, keepdims=True) + jnp.where(pred_new, 1, 0)

    thr = _kth_threshold(lambda cand: count(lambda kt, j: kt >= cand, key_new >= cand),
                         (rows, 1), n_sel)
    need = n_sel - count(lambda kt, j: kt > thr, key_new > thr)
    cut = _tie_cut(lambda cand: count(lambda kt, j: (kt == thr) & (col + j * SEL_TILE < cand),
                                      (key_new == thr) & (length < cand)),
                   need, (rows, 1), n_bits)

    def bias_body(j, c):
        off = pl.multiple_of(j * SEL_TILE, SEL_TILE)
        kt = key_sc[:, pl.ds(off, SEL_TILE)]
        sel = (kt > thr) | ((kt == thr) & (col + j * SEL_TILE <= cut))
        bias_ref[:, pl.ds(off, SEL_TILE)] = jnp.where(sel, 0.0, NEG)
        return c

    lax.fori_loop(0, n_tiles, bias_body, 0)
    sel_new = (key_new > thr) | ((key_new == thr) & (length <= cut))
    bnew_ref[...] = jnp.broadcast_to(jnp.where(sel_new, 0.0, NEG), bnew_ref.shape)


def _sample_select(scores, iq, ik_tiled, iw, n_sel):
    rows, length = scores.shape
    return pl.pallas_call(
        functools.partial(_sample_select_kernel, n_sel=n_sel, n_bits=length.bit_length()),
        out_shape=[jax.ShapeDtypeStruct((rows, length), f32),
                   jax.ShapeDtypeStruct((rows, LANES), f32)],
        scratch_shapes=[pltpu.VMEM((rows, length), i32)],
        compiler_params=pltpu.CompilerParams(vmem_limit_bytes=VMEM_LIMIT),
        name="sample_select",
    )(scores, iq, ik_tiled, iw)


SUFFIX_CHUNK = 256


def _suffix_bias_kernel(pt_ref, lfnew_ref, lf_hbm, bias_ref, buf, sem, *, layer, n_pages):
    slot = _gather_seq(pt_ref, lf_hbm, layer, buf, sem, n_pages)
    w = SUFFIX_CHUNK
    tri = (lax.broadcasted_iota(i32, (w, w), 0)
           > lax.broadcasted_iota(i32, (w, w), 1)).astype(bf16)
    carry = lfnew_ref[0]
    for c in reversed(range(n_pages * PAGE_SIZE // w)):
        x = buf[slot, :, c * w:(c + 1) * w]
        bias_ref[0, :, c * w:(c + 1) * w] = sum(_dot(p, tri) for p in _split3(x)) + carry
        carry = carry + jnp.sum(x, axis=1, keepdims=True)


def _suffix_bias(page_table, lf_new, cache_logf_t, layer):
    s, n_pages = page_table.shape
    length = n_pages * PAGE_SIZE
    grid_spec = pltpu.PrefetchScalarGridSpec(
        num_scalar_prefetch=1, grid=(s,),
        in_specs=[pl.BlockSpec((1, N_HEADS, 1), lambda b, pt: (b, 0, 0)),
                  pl.BlockSpec(memory_space=pl.ANY)],
        out_specs=pl.BlockSpec((1, N_HEADS, length), lambda b, pt: (b, 0, 0)),
        scratch_shapes=[pltpu.VMEM((2, N_HEADS, length), f32), pltpu.SemaphoreType.DMA((2,))])
    return pl.pallas_call(
        functools.partial(_suffix_bias_kernel, layer=layer, n_pages=n_pages),
        grid_spec=grid_spec,
        out_shape=jax.ShapeDtypeStruct((s, N_HEADS, length), f32),
        compiler_params=_cparams("arbitrary"),
        name="suffix_bias",
    )(page_table, lf_new, cache_logf_t)


DEC_PAGES = 8


def _kv_chunk_copies(pt_ref, k_hbm, v_hbm, layer, seq, chunk, kbuf, vbuf, slot, sem):
    cps = []
    for i in range(DEC_PAGES):
        page = pt_ref[seq, chunk * DEC_PAGES + i]
        cps.append(pltpu.make_async_copy(k_hbm.at[layer, page], kbuf.at[slot, i], sem.at[0, slot]))
        cps.append(pltpu.make_async_copy(v_hbm.at[layer, page], vbuf.at[slot, i], sem.at[1, slot]))
    return cps


def _decode_attn_kernel(pt_ref, q_ref, knew_ref, vnew_ref, bnew_ref, bias_ref, k_hbm, v_hbm,
                        o_ref, kbuf, vbuf, sem, *, layer, n_chunks):
    b = pl.program_id(0)
    q = q_ref[0]
    width = DEC_PAGES * PAGE_SIZE
    copies = functools.partial(_kv_chunk_copies, pt_ref, k_hbm, v_hbm, layer)

    @pl.when(b == 0)
    def _():
        for cp in copies(0, 0, kbuf, vbuf, 0, sem):
            cp.start()

    kn = knew_ref[0].astype(bf16).astype(f32)
    m = jnp.sum(q.astype(f32) * kn, axis=1, keepdims=True) + bnew_ref[0][:, 0:1]
    l = jnp.ones_like(m)
    acc = jnp.broadcast_to(vnew_ref[0].astype(bf16).astype(f32), (N_HEADS, KV_COLS))

    for c in range(n_chunks):
        slot = c % 2
        if c + 1 < n_chunks:
            for cp in copies(b, c + 1, kbuf, vbuf, 1 - slot, sem):
                cp.start()
        else:
            @pl.when(b + 1 < pl.num_programs(0))
            def _():
                for cp in copies(b + 1, 0, kbuf, vbuf, 1 - slot, sem):
                    cp.start()
        for cp in copies(b, c, kbuf, vbuf, slot, sem):
            cp.wait()
        s = jnp.concatenate([_dot(q, kbuf[slot, i].astype(bf16)) for i in range(DEC_PAGES)],
                            axis=1) + bias_ref[0, :, c * width:(c + 1) * width]
        m_new = jnp.maximum(m, jnp.max(s, axis=1, keepdims=True))
        p = jnp.exp(s - m_new)
        alpha = jnp.exp(m - m_new)
        l = alpha * l + jnp.sum(p, axis=1, keepdims=True)
        pb = p.astype(bf16)
        pv = sum(_dot_nt(pb[:, i * PAGE_SIZE:(i + 1) * PAGE_SIZE], vbuf[slot, i].astype(bf16))
                 for i in range(DEC_PAGES))
        acc = alpha * acc + pv
        m = m_new
    o_ref[0] = acc / l


def _decode_attn(page_table, qmat, k_new, v_new, bias_new, bias, cache_kt, cache_vt, layer):
    s, n_pages = page_table.shape
    hb = bias.shape[1]
    n_chunks = n_pages // DEC_PAGES
    assert n_chunks * DEC_PAGES == n_pages and n_chunks % 2 == 0
    row = lambda shape: pl.BlockSpec(shape, lambda b, pt: (b, 0, 0))
    hbm = pl.BlockSpec(memory_space=pl.ANY)
    width = DEC_PAGES * PAGE_SIZE
    grid_spec = pltpu.PrefetchScalarGridSpec(
        num_scalar_prefetch=1, grid=(s,),
        in_specs=[row((1, N_HEADS, KV_COLS)), row((1, 1, KV_COLS)), row((1, 1, KV_COLS)),
                  row((1, hb, LANES)), row((1, hb, n_pages * PAGE_SIZE)), hbm, hbm],
        out_specs=row((1, N_HEADS, KV_COLS)),
        scratch_shapes=[pltpu.VMEM((2, DEC_PAGES, KV_COLS, PAGE_SIZE), f32),
                        pltpu.VMEM((2, DEC_PAGES, KV_COLS, PAGE_SIZE), f32),
                        pltpu.SemaphoreType.DMA((2, 2))])
    return pl.pallas_call(
        functools.partial(_decode_attn_kernel, layer=layer, n_chunks=n_chunks),
        grid_spec=grid_spec,
        out_shape=jax.ShapeDtypeStruct((s, N_HEADS, KV_COLS), f32),
        compiler_params=_cparams("arbitrary"),
        name="decode_attn",
    )(page_table, qmat, k_new, v_new, bias_new, bias, cache_kt, cache_vt)


def _pad_cols(w, n):
    return jnp.pad(w, ((0, 0), (0, n)))


def _decode_q(q_bf):
    s = q_bf.shape[0]
    onehot = (jnp.arange(N_HEADS)[:, None] // GROUP == jnp.arange(N_KV_HEADS)[None, :])
    q = q_bf.reshape(s, N_HEADS, 1, HEAD_DIM) * onehot[None, :, :, None].astype(q_bf.dtype)
    return q.reshape(s, N_HEADS, KV_COLS)


def _decode_o(o_full):
    s = o_full.shape[0]
    o = o_full.reshape(s, N_KV_HEADS, GROUP, N_KV_HEADS, HEAD_DIM)
    return jnp.stack([o[:, g, :, g, :] for g in range(N_KV_HEADS)], axis=1).reshape(s, -1)


def kernel(x_prompt, x_sample, cache_k_a, cache_v_a, cache_kidx_a, cache_k_b, cache_v_b,
           cache_logf_b, state_conv, page_table, c_prompt, c_sample, w_ada, b_ada, g_mix,
           g_ffn, w_in_a, w_o_a, w_in_b, b_f, w_o_b, w_up, conv_w, conv_b, w_down, g_final):
    bsz, t, d = x_prompt.shape
    s = x_sample.shape[0]
    depth = w_ada.shape[0]
    m = bsz * t
    past_len = page_table.shape[1] * PAGE_SIZE
    n_pool = cache_k_a.shape[1]
    tm = 512
    tq = 256

    pad_rows = (-(bsz + s)) % SUBLANES
    c_all = jnp.concatenate([c_prompt, c_sample, jnp.zeros((pad_rows, d), f32)], axis=0)
    mods = _adaln(c_all, w_ada, b_ada)
    mods_p = mods[:, :bsz].reshape(depth, bsz, 6, d)
    mods_s = mods[:, bsz:bsz + s].reshape(depth, s, 6, d)

    tabs_p = _rope_tables(jnp.arange(t))
    tabs_pt = _rope_tables_t(jnp.arange(t))
    tabs_s = _rope_tables(jnp.full((s,), past_len))
    zero_bf = jnp.zeros((1, LANES), f32)
    zero_col = jnp.zeros((N_HEADS, 1), f32)

    def rows_t(a):
        return a.reshape(bsz, N_KV_HEADS, HEAD_DIM, t).transpose(0, 3, 1, 2)

    xp = x_prompt.reshape(m, d)
    xs = x_sample.reshape(s, d)
    kv_shape_p = (bsz, t, N_KV_HEADS, HEAD_DIM)
    kv_shape_s = (s, 1, N_KV_HEADS, HEAD_DIM)
    ka_p, va_p, ia_p, kb_p, vb_p, lb_p, cv_p = [], [], [], [], [], [], []
    ka_s, va_s, ia_s, kb_s, vb_s, lb_s, cv_s = [], [], [], [], [], [], []

    def pages_t(cache):
        ct = jnp.moveaxis(cache, 2, -1)
        return ct.reshape(ct.shape[:2] + (-1, PAGE_SIZE))

    cache_k_a, cache_v_a, cache_kidx_a = pages_t(cache_k_a), pages_t(cache_v_a), pages_t(cache_kidx_a)
    cache_k_b, cache_v_b, cache_logf_b = pages_t(cache_k_b), pages_t(cache_v_b), pages_t(cache_logf_b)

    for i in range(depth):
        j = i // 2
        mp = [mods_p[i, :, c][:, None, :] for c in range(6)]
        ms = [mods_s[i, :, c][None] for c in range(6)]
        g_m = g_mix[i][None]
        if i % 2 == 0:
            w = w_in_a[j]
            n_qkvi = QKV_COLS + N_IDX_HEADS * IDX_DIM + IDX_DIM
            w_in = jnp.concatenate([_pad_cols(w[:, :n_qkvi], LANES - IDX_DIM),
                                    _pad_cols(w[:, n_qkvi:], LANES - N_IDX_HEADS)],
                                   axis=1).astype(bf16)
            w_o = w_o_a[j].astype(bf16)
            nq = N_HEADS * HEAD_DIM
            n_iq = QKV_COLS + N_IDX_HEADS * IDX_DIM
            wk = w[:, nq:nq + KV_COLS]
            w_row = jnp.concatenate([wk, _pad_cols(w[:, n_iq:n_qkvi], LANES - IDX_DIM)],
                                    axis=1).astype(bf16)
            w_t = jnp.concatenate([w[:, :nq], w[:, nq + KV_COLS:QKV_COLS], wk, w[:, QKV_COLS:n_qkvi],
                                   _pad_cols(w[:, n_qkvi:], IW_ROWS - N_IDX_HEADS)],
                                  axis=1).T.astype(bf16)
            qt, vt, kpad, kt32, vt32, iqt, iwt, ikb, ikt32 = _in_proj_t(
                xp, g_m, mp[0], mp[1], w_row, w_t, tabs_p, tabs_pt, zero_col, True, tm, t)
            xp = _dsa_attn(xp, mp[2], qt, kpad, vt, iqt, iwt, ikb.reshape(bsz, t, IDX_DIM),
                           w_o.T, tq, min(TOPK_MAX, t // 4))
            ka_p.append(rows_t(kt32))
            va_p.append(rows_t(vt32))
            ia_p.append(ikt32.transpose(0, 2, 1))
            q, k32, kb, v32, vb, iq, ik32, ikb, iw = _in_proj(
                xs, g_m, ms[0], ms[1], w_in, tabs_s, zero_bf, SEGS_A, OUTS_A, s, s)
            scores = _sample_scores(page_table, iq.reshape(s, N_IDX_HEADS, IDX_DIM),
                                    iw[:, :N_IDX_HEADS].reshape(s, N_IDX_HEADS, 1),
                                    cache_kidx_a, j)
            bias, bias_new = _sample_select(scores, iq, jnp.tile(ikb[:, :IDX_DIM], (1, N_IDX_HEADS)),
                                            iw, min(TOPK_MAX, (past_len + 1) // 4))
            o_full = _decode_attn(page_table, _decode_q(q), k32[:, None], v32[:, None],
                                  bias_new[:, None], bias[:, None], cache_k_a, cache_v_a, j)
            xs = _out_proj(xs, _decode_o(o_full).astype(bf16), w_o, ms[2], s, s)
            ka_s.append(k32.reshape(kv_shape_s))
            va_s.append(v32.reshape(kv_shape_s))
            ia_s.append(ik32[:, :IDX_DIM].reshape(s, 1, IDX_DIM))
        else:
            w_in = _pad_cols(w_in_b[j], LANES - N_HEADS).astype(bf16)
            w_o = w_o_b[j].astype(bf16)
            bf_pad = _pad_cols(b_f[j][None], LANES - N_HEADS)
            nq = N_HEADS * HEAD_DIM
            wk = w_in[:, nq:nq + KV_COLS]
            w_t = jnp.concatenate([w_in[:, :nq], w_in[:, nq + KV_COLS:QKV_COLS], wk,
                                   w_in[:, QKV_COLS:QKV_COLS + N_HEADS]], axis=1).T
            qt, vt, kpad, kt32, vt32, lft = _in_proj_t(
                xp, g_m, mp[0], mp[1], wk, w_t, tabs_p, tabs_pt, b_f[j][:, None], False, tm, t)
            c_t = _cumsum_t(lft, tq)
            c_pad = jnp.pad(c_t.transpose(0, 2, 1), ((0, 0), (0, 0), (0, LANES - N_HEADS)))
            xp = _fox_attn(xp, mp[2], qt, kpad, vt, c_pad, c_t, w_o.T, tq)
            kb_p.append(rows_t(kt32))
            vb_p.append(rows_t(vt32))
            lb_p.append(lft.transpose(0, 2, 1))
            q, k32, kb, v32, vb, lf = _in_proj(
                xs, g_m, ms[0], ms[1], w_in, tabs_s, bf_pad, SEGS_B, OUTS_B, s, s)
            bias = _suffix_bias(page_table, lf[:, :N_HEADS, None], cache_logf_b, j)
            o_full = _decode_attn(page_table, _decode_q(q), k32[:, None], v32[:, None],
                                  jnp.zeros((s, N_HEADS, LANES), f32), bias,
                                  cache_k_b, cache_v_b, j)
            xs = _out_proj(xs, _decode_o(o_full).astype(bf16), w_o, ms[2], s, s)
            kb_s.append(k32.reshape(kv_shape_s))
            vb_s.append(v32.reshape(kv_shape_s))
            lb_s.append(lf[:, :N_HEADS].reshape(s, 1, N_HEADS))

        wa = w_up[i][:, :D_FF].astype(bf16)
        wb = w_up[i][:, D_FF:].astype(bf16)
        wd = w_down[i].astype(bf16)
        cw = jnp.pad(conv_w[i], ((0, SUBLANES - CONV_W), (0, 0)))
        cb = conv_b[i][None]
        g_f = g_ffn[i][None]
        xp, tail = _ffn(xp, g_f, mp[3], mp[4], mp[5], wa, wb, cw, cb, wd, tm, t)
        cv_p.append(tail.reshape(bsz, t // tm, SUBLANES, D_FF)[:, -1, SUBLANES - (CONV_W - 1):])
        xs, a_s = _ffn_sample(xs, g_f, ms[3], ms[4], ms[5], state_conv[i][:, 0],
                              state_conv[i][:, 1], wa, wb, cw, cb, wd, D_FF // 2)
        cv_s.append(jnp.stack([state_conv[i][:, 1], a_s], axis=1))

    y_prompt = _final_norm(xp, g_final[None], tm).reshape(bsz, t, d)
    y_sample = _final_norm(xs, g_final[None], s).reshape(s, 1, d)
    return (y_prompt, y_sample,
            jnp.stack(ka_p), jnp.stack(va_p), jnp.stack(ia_p),
            jnp.stack(kb_p), jnp.stack(vb_p), jnp.stack(lb_p), jnp.stack(cv_p),
            jnp.stack(ka_s), jnp.stack(va_s), jnp.stack(ia_s),
            jnp.stack(kb_s), jnp.stack(vb_s), jnp.stack(lb_s), jnp.stack(cv_s))
```

```python
import functools

import jax
import jax.numpy as jnp
from jax import lax
from jax.experimental import pallas as pl
from jax.experimental.pallas import tpu as pltpu

f32 = jnp.float32
bf16 = jnp.bfloat16
i32 = jnp.int32

D_MODEL = 1024
N_HEADS = 16
HEAD_DIM = 64
N_KV_HEADS = 4
GROUP = N_HEADS // N_KV_HEADS
KV_COLS = N_KV_HEADS * HEAD_DIM
QKV_COLS = (N_HEADS + 2 * N_KV_HEADS) * HEAD_DIM
ROPE_DIMS = HEAD_DIM // 4
ROPE_HALF = ROPE_DIMS // 2
ROPE_THETA = 500000.0
N_IDX_HEADS = 8
IDX_DIM = 64
TOPK_MAX = 256
D_FF = 2816
CONV_W = 3
EPS = 1e-6
PAGE_SIZE = 128
Q_SCALE = HEAD_DIM ** -0.5
LOG2E = 1.4426950408889634

LANES = 128
SUBLANES = 8
VMEM_LIMIT = 56 * 1024 * 1024
NEG = -1e30
INT_MIN = -2 ** 31
INT_MAX = 2 ** 31 - 1
KEY_NEG_INF = (0xFF800000 ^ 0x7FFFFFFF) - 2 ** 32


def _cparams(*sem):
    return pltpu.CompilerParams(dimension_semantics=sem, vmem_limit_bytes=VMEM_LIMIT)


def _dot(a, b):
    return jnp.dot(a, b, preferred_element_type=f32)


def _dot_nt(a, b):
    return lax.dot_general(a, b, (((1,), (1,)), ((), ())), preferred_element_type=f32)


def _split3(x):
    hi = x.astype(bf16)
    r1 = x - hi.astype(f32)
    mid = r1.astype(bf16)
    lo = (r1 - mid.astype(f32)).astype(bf16)
    return hi, mid, lo


def _norm_mod(x, g, shift, scale):
    xn = x * lax.rsqrt(jnp.mean(x * x, axis=-1, keepdims=True) + EPS)
    return (xn * g) * (1.0 + scale) + shift


def _silu(x):
    return x * jax.nn.sigmoid(x)


def _sort_key(s):
    bits = pltpu.bitcast(s, i32)
    return bits ^ ((bits >> 31) & 0x7FFFFFFF)


def _adaln_kernel(c_ref, w_ref, b_ref, o_ref):
    a = _silu(c_ref[...]).astype(bf16)
    o_ref[0] = _dot(a, w_ref[0].astype(bf16)) + b_ref[0]


def _adaln(c_all, w_ada, b_ada):
    depth, d, n = w_ada.shape
    rows = c_all.shape[0]
    tn = 1536
    return pl.pallas_call(
        _adaln_kernel,
        grid=(depth, n // tn),
        in_specs=[pl.BlockSpec((rows, d), lambda l, j: (0, 0)),
                  pl.BlockSpec((1, d, tn), lambda l, j: (l, 0, j)),
                  pl.BlockSpec((1, 1, tn), lambda l, j: (l, 0, j))],
        out_specs=pl.BlockSpec((1, rows, tn), lambda l, j: (l, 0, j)),
        out_shape=jax.ShapeDtypeStruct((depth, rows, n), f32),
        compiler_params=_cparams("parallel", "parallel"),
        name="adaln",
    )(c_all, w_ada, b_ada.reshape(depth, 1, n))


SEGS_A = ((0, 1024, "rope", ((0, Q_SCALE),)),
          (1024, 256, "rope", ((1, 1.0), (2, 1.0))),
          (1280, 256, None, ((3, 1.0), (4, 1.0))),
          (1536, 512, "rope", ((5, 1.0),)),
          (2048, 128, "rope", ((6, 1.0), (7, 1.0))),
          (2176, 128, None, ((8, 1.0),)))
OUTS_A = ((1024, bf16), (256, f32), (256, bf16), (256, f32), (256, bf16),
          (512, bf16), (128, f32), (128, bf16), (128, f32))
SEGS_B = ((0, 1024, None, ((0, Q_SCALE),)),
          (1024, 256, None, ((1, 1.0), (2, 1.0))),
          (1280, 256, None, ((3, 1.0), (4, 1.0))),
          (1536, 128, "logf", ((5, 1.0),)))
OUTS_B = ((1024, bf16), (256, f32), (256, bf16), (256, f32), (256, bf16), (128, f32))


def _in_proj_kernel(x_ref, g_ref, shift_ref, scale_ref, w_ref, cos_ref, sa_ref, sb_ref,
                    bf_ref, *out_refs, segs):
    h = _norm_mod(x_ref[...], g_ref[...], shift_ref[0], scale_ref[0]).astype(bf16)
    for c0, width, epi, outs in segs:
        for c in range(width // LANES):
            lo = c0 + c * LANES
            a = _dot(h, w_ref[:, lo:lo + LANES])
            if epi == "rope":
                a = (a * cos_ref[...] + pltpu.roll(a, ROPE_HALF, 1) * sa_ref[...]
                     + pltpu.roll(a, LANES - ROPE_HALF, 1) * sb_ref[...])
            elif epi == "logf":
                z = a + bf_ref[...]
                a = jnp.minimum(z, 0.0) - jnp.log(1.0 + jnp.exp(-jnp.abs(z)))
            for oi, sc in outs:
                val = a if sc == 1.0 else a * sc
                out_refs[oi][:, c * LANES:(c + 1) * LANES] = val.astype(out_refs[oi].dtype)


def _in_proj(x, g, shift, scale, w, tabs, b_f, segs, outs, tm, rows_per_mod):
    m, d = x.shape
    n = w.shape[1]
    r = shift.shape[1]
    cos, sa, sb = tabs
    tab_blocks = cos.shape[0] // tm
    mod_spec = pl.BlockSpec((1, r, d), lambda i: (i * tm // rows_per_mod, 0, 0))
    tab_spec = pl.BlockSpec((tm, LANES), lambda i: (i % tab_blocks, 0))
    return pl.pallas_call(
        functools.partial(_in_proj_kernel, segs=segs),
        grid=(m // tm,),
        in_specs=[pl.BlockSpec((tm, d), lambda i: (i, 0)),
                  pl.BlockSpec((1, d), lambda i: (0, 0)),
                  mod_spec, mod_spec,
                  pl.BlockSpec((d, n), lambda i: (0, 0)),
                  tab_spec, tab_spec, tab_spec,
                  pl.BlockSpec((1, LANES), lambda i: (0, 0))],
        out_specs=[pl.BlockSpec((tm, wd), lambda i: (i, 0)) for wd, _ in outs],
        out_shape=[jax.ShapeDtypeStruct((m, wd), dt) for wd, dt in outs],
        compiler_params=_cparams("parallel"),
        name="in_proj",
    )(x, g, shift, scale, w, cos, sa, sb, b_f)


def _rope_tables(pos):
    inv_freq = ROPE_THETA ** (-jnp.arange(ROPE_HALF, dtype=f32) / ROPE_HALF)
    ang = pos.astype(f32)[:, None] * inv_freq[None, :]
    cos, sin = jnp.cos(ang), jnp.sin(ang)
    t = pos.shape[0]
    ones = jnp.ones((t, HEAD_DIM - ROPE_DIMS), f32)
    zeros = jnp.zeros((t, HEAD_DIM - ROPE_DIMS), f32)
    zh = jnp.zeros((t, ROPE_HALF), f32)
    c = jnp.concatenate([cos, cos, ones], axis=1)
    sa = jnp.concatenate([zh, sin, zeros], axis=1)
    sb = jnp.concatenate([-sin, zh, zeros], axis=1)
    rep = LANES // HEAD_DIM
    return tuple(jnp.tile(a, (1, rep)) for a in (c, sa, sb))


WT_Q = 0
WT_V = N_HEADS * HEAD_DIM
WT_K = WT_V + KV_COLS
WT_X = WT_K + KV_COLS
WT_IK = WT_X + N_IDX_HEADS * IDX_DIM
WT_IW = WT_IK + IDX_DIM
IW_ROWS = 2 * SUBLANES
ROW_BLOCK = 512


def _rope_rows(a, cos_t, sin_t):
    parts = []
    for b0 in range(0, a.shape[0], HEAD_DIM):
        x1 = a[b0:b0 + ROPE_HALF]
        x2 = a[b0 + ROPE_HALF:b0 + ROPE_DIMS]
        parts += [x1 * cos_t - x2 * sin_t, x1 * sin_t + x2 * cos_t, a[b0 + ROPE_DIMS:b0 + HEAD_DIM]]
    return jnp.concatenate(parts, axis=0)


def _in_proj_t_kernel(x_ref, g_ref, shift_ref, scale_ref, w_ref, wt_ref, cos_ref, sa_ref, sb_ref,
                      cost_ref, sint_ref, bf_ref, *outs, is_a):
    if is_a:
        qt_ref, vt_ref, kpad_ref, kt32_ref, vt32_ref, iqt_ref, iwt_ref, ikb_ref, ikt32_ref = outs
    else:
        qt_ref, vt_ref, kpad_ref, kt32_ref, vt32_ref, lft_ref = outs
    h = _norm_mod(x_ref[...], g_ref[...], shift_ref[0], scale_ref[0]).astype(bf16)
    cos_t, sin_t = cost_ref[...], sint_ref[...]

    def rows(r0, n):
        blk = _dot_nt(wt_ref[r0:r0 + n, :], h)
        return [blk[c * LANES:(c + 1) * LANES] for c in range(n // LANES)]

    def rope_cols(a):
        return (a * cos_ref[...] + pltpu.roll(a, ROPE_HALF, 1) * sa_ref[...]
                + pltpu.roll(a, LANES - ROPE_HALF, 1) * sb_ref[...])

    for r0 in range(0, N_HEADS * HEAD_DIM, ROW_BLOCK):
        for c, a in enumerate(rows(WT_Q + r0, ROW_BLOCK)):
            if is_a:
                a = _rope_rows(a, cos_t, sin_t)
            lo = r0 + c * LANES
            qt_ref[lo:lo + LANES, :] = (a * (Q_SCALE * LOG2E)).astype(bf16)
    for c, (v, k) in enumerate(zip(rows(WT_V, KV_COLS), rows(WT_K, KV_COLS))):
        sl = slice(c * LANES, (c + 1) * LANES)
        vt32_ref[0, sl, :] = v
        vt_ref[sl, :] = v.astype(bf16)
        kt32_ref[0, sl, :] = _rope_rows(k, cos_t, sin_t) if is_a else k
    if is_a:
        for c, a in enumerate(rows(WT_X, N_IDX_HEADS * IDX_DIM)):
            iqt_ref[c * LANES:(c + 1) * LANES, :] = _rope_rows(a, cos_t, sin_t).astype(bf16)
        ikt32_ref[0] = _rope_rows(_dot_nt(wt_ref[WT_IK:WT_IK + IDX_DIM, :], h), cos_t, sin_t)
        iwt_ref[...] = _dot_nt(wt_ref[WT_IW:WT_IW + IW_ROWS, :], h)
    else:
        z = _dot_nt(wt_ref[WT_X:WT_X + N_HEADS, :], h) + bf_ref[...]
        lft_ref[0] = jnp.minimum(z, 0.0) - jnp.log(1.0 + jnp.exp(-jnp.abs(z)))

    lane = lax.broadcasted_iota(i32, (1, LANES), 1)
    std = _dot(h, w_ref[...])
    for c in range(KV_COLS // LANES):
        a = std[:, c * LANES:(c + 1) * LANES]
        if is_a:
            a = rope_cols(a)
        kpad_ref[0, 2 * c] = jnp.where(lane < HEAD_DIM, a, 0.0).astype(bf16)
        kpad_ref[0, 2 * c + 1] = jnp.where(lane < HEAD_DIM, pltpu.roll(a, HEAD_DIM, 1),
                                           0.0).astype(bf16)
    if is_a:
        ikb_ref[...] = rope_cols(std[:, KV_COLS:KV_COLS + LANES])[:, :IDX_DIM].astype(bf16)


def _in_proj_t(x, g, shift, scale, w, wt, tabs, tabs_t, b_f, is_a, tm, t):
    m, d = x.shape
    bsz = m // t
    tpb = t // tm
    col = lambda rows_: pl.BlockSpec((rows_, tm), lambda i: (0, i))
    seq = lambda rows_: pl.BlockSpec((1, rows_, tm), lambda i: (i // tpb, 0, i % tpb))
    seq_shape = lambda rows_: jax.ShapeDtypeStruct((bsz, rows_, t), f32)
    const = lambda a: pl.BlockSpec(a.shape, lambda i: (0,) * a.ndim)
    mod_spec = pl.BlockSpec((1, 1, d), lambda i: (i // tpb, 0, 0))
    tab_spec = pl.BlockSpec((tm, LANES), lambda i: (i % tpb, 0))
    tabt_spec = pl.BlockSpec((ROPE_HALF, tm), lambda i: (0, i % tpb))
    nq = N_HEADS * HEAD_DIM
    out_specs = [col(nq), col(KV_COLS),
                 pl.BlockSpec((1, N_KV_HEADS, tm, LANES), lambda i: (i // tpb, 0, i % tpb, 0)),
                 seq(KV_COLS), seq(KV_COLS)]
    out_shape = [jax.ShapeDtypeStruct((nq, m), bf16), jax.ShapeDtypeStruct((KV_COLS, m), bf16),
                 jax.ShapeDtypeStruct((bsz, N_KV_HEADS, t, LANES), bf16),
                 seq_shape(KV_COLS), seq_shape(KV_COLS)]
    if is_a:
        out_specs += [col(N_IDX_HEADS * IDX_DIM), col(IW_ROWS),
                      pl.BlockSpec((tm, IDX_DIM), lambda i: (i, 0)), seq(IDX_DIM)]
        out_shape += [jax.ShapeDtypeStruct((N_IDX_HEADS * IDX_DIM, m), bf16),
                      jax.ShapeDtypeStruct((IW_ROWS, m), f32),
                      jax.ShapeDtypeStruct((m, IDX_DIM), bf16), seq_shape(IDX_DIM)]
    else:
        out_specs += [seq(N_HEADS)]
        out_shape += [seq_shape(N_HEADS)]
    return pl.pallas_call(
        functools.partial(_in_proj_t_kernel, is_a=is_a),
        grid=(m // tm,),
        in_specs=[pl.BlockSpec((tm, d), lambda i: (i, 0)), const(g), mod_spec, mod_spec,
                  const(w), const(wt), tab_spec, tab_spec, tab_spec, tabt_spec, tabt_spec,
                  const(b_f)],
        out_specs=out_specs, out_shape=out_shape,
        compiler_params=_cparams("parallel"),
        name="in_proj_t",
    )(x, g, shift, scale, w, wt, *tabs, *tabs_t, b_f)


def _rope_tables_t(pos):
    inv_freq = ROPE_THETA ** (-jnp.arange(ROPE_HALF, dtype=f32) / ROPE_HALF)
    ang = inv_freq[:, None] * pos.astype(f32)[None, :]
    return jnp.cos(ang), jnp.sin(ang)


AUG_STRIDE = 4
AUG_ONES = GROUP * AUG_STRIDE


ATTN_TK = 256
SUM_ROWS = 2 * SUBLANES
ACC_ROWS = HEAD_DIM + SUM_ROWS


def _t_init(m_sc, acc_sc):
    m_sc[...] = jnp.full(m_sc.shape, NEG, f32)
    acc_sc[...] = jnp.zeros(acc_sc.shape, f32)


def _t_step(s, vt1, h, m_sc, acc_sc):
    m_old = m_sc[h]
    m_new = jnp.maximum(m_old, jnp.max(s, axis=0, keepdims=True))
    p = jnp.exp2((s - m_new).astype(bf16))
    acc_sc[h] = jnp.exp2(m_old - m_new) * acc_sc[h] + _dot(vt1, p)
    m_sc[h] = m_new


def _t_tile(off, tk, kt_of, vt_ref, qa_sc, post, m_sc, acc_sc):
    kts = [kt_of(g) for g in range(N_KV_HEADS)]
    ones = jnp.ones((SUM_ROWS, tk), bf16)
    vts = [jnp.concatenate([vt_ref[g * HEAD_DIM:(g + 1) * HEAD_DIM, pl.ds(off, tk)], ones], axis=0)
           for g in range(N_KV_HEADS)]
    ahead = 2
    pending = [_dot(kts[g], qa_sc[g]) for g in range(ahead)]
    for g in range(N_KV_HEADS):
        s = pending.pop(0)
        if g + ahead < N_KV_HEADS:
            pending.append(_dot(kts[g + ahead], qa_sc[g + ahead]))
        _t_step(post(s), vts[g], g, m_sc, acc_sc)


def _t_fill_q(qa_sc, h, tq, q, aug):
    g, jj = divmod(h, GROUP)
    qa_sc[g, :HEAD_DIM, jj * tq:(jj + 1) * tq] = q
    qa_sc[g, HEAD_DIM:, jj * tq:(jj + 1) * tq] = aug


def _t_finish(tq, acc_sc, ot_sc):
    for h in range(N_HEADS):
        g, jj = divmod(h, GROUP)
        cols = slice(jj * tq, (jj + 1) * tq)
        ot_sc[h * HEAD_DIM:(h + 1) * HEAD_DIM, :] = (
            acc_sc[g, :HEAD_DIM, cols] / acc_sc[g, HEAD_DIM:HEAD_DIM + 1, cols]).astype(bf16)


def _t_out(x_ref, gate_ref, wot_ref, ot_sc, o_ref):
    y_t = _dot(wot_ref[...], ot_sc[...])
    o_ref[...] = x_ref[...] + gate_ref[0] * y_t.T


def _fox_attn_kernel(x_ref, gate_ref, qt_ref, kpad_ref, vt_ref, cpad_ref, ct_ref, wot_ref,
                     o_ref, kaug_sc, m_sc, acc_sc, qa_sc, ot_sc, *, tq, tk, chunk):
    qi = pl.program_id(1)
    t = kaug_sc.shape[1]

    @pl.when(qi == 0)
    def _():
        r = lax.broadcasted_iota(i32, (LANES, LANES), 0)
        l = lax.broadcasted_iota(i32, (LANES, LANES), 1) - HEAD_DIM
        lane = lax.broadcasted_iota(i32, (1, LANES), 1) - HEAD_DIM
        ones = jnp.where((lane >= AUG_ONES) & (lane < AUG_ONES + 3), 1.0, 0.0)
        for g in range(N_KV_HEADS):
            perm = [((l >= 0) & (l < AUG_ONES) & ((l >> 2) == r - g * GROUP)
                     & ((l & 3) == p)).astype(bf16) for p in range(3)]

            def body(i, c, g=g, perm=perm):
                off = pl.multiple_of(i * chunk, chunk)
                pieces = _split3(cpad_ref[0, pl.ds(off, chunk), :] * -LOG2E)
                aug = sum(_dot(pc, pm) for pc, pm in zip(pieces, perm)) + ones
                kaug_sc[g, pl.ds(off, chunk), :] = (
                    kpad_ref[0, g, pl.ds(off, chunk), :].astype(f32) + aug).astype(bf16)
                return c

            lax.fori_loop(0, t // chunk, body, 0)

    ri = lax.broadcasted_iota(i32, (HEAD_DIM, tq), 0)
    wide = (tk, GROUP * tq)
    d_iota = (lax.broadcasted_iota(i32, wide, 0)
              - (lax.broadcasted_iota(i32, wide, 1) & (tq - 1)))
    for h in range(N_HEADS):
        jj = h % GROUP
        cq = _split3(ct_ref[0, h:h + 1, :] * LOG2E)
        aug = jnp.where((ri >= jj * AUG_STRIDE) & (ri < jj * AUG_STRIDE + 3), 1.0, 0.0)
        for p in range(3):
            aug = jnp.where(ri == AUG_ONES + p, cq[p].astype(f32), aug)
        _t_fill_q(qa_sc, h, tq, qt_ref[h * HEAD_DIM:(h + 1) * HEAD_DIM, :], aug.astype(bf16))
    _t_init(m_sc, acc_sc)

    def tile(j, post):
        off = pl.multiple_of(j * tk, tk)
        _t_tile(off, tk, lambda g: kaug_sc[g, pl.ds(off, tk), :], vt_ref, qa_sc, post,
                m_sc, acc_sc)

    def body(j, c):
        tile(j, lambda s: s)
        return c

    sub = tq // tk
    lax.fori_loop(0, qi * sub, body, 0)
    for dd in range(sub):
        tile(qi * sub + dd, lambda s, dd=dd: jnp.where(d_iota <= -dd * tk, s, NEG))
    _t_finish(tq, acc_sc, ot_sc)
    _t_out(x_ref, gate_ref, wot_ref, ot_sc, o_ref)


def _attn_specs(bsz, t, d, tq):
    nq = t // tq
    return dict(
        x=pl.BlockSpec((tq, d), lambda b, q: (b * nq + q, 0)),
        gate=pl.BlockSpec((1, 1, d), lambda b, q: (b, 0, 0)),
        qt=pl.BlockSpec((N_HEADS * HEAD_DIM, tq), lambda b, q: (0, b * nq + q)),
        kpad=pl.BlockSpec((1, N_KV_HEADS, t, LANES), lambda b, q: (b, 0, 0, 0)),
        vt=pl.BlockSpec((KV_COLS, t), lambda b, q: (0, b)),
        wot=pl.BlockSpec((d, N_HEADS * HEAD_DIM), lambda b, q: (0, 0)),
        scratch=[pltpu.VMEM((N_KV_HEADS, 1, GROUP * tq), f32),
                 pltpu.VMEM((N_KV_HEADS, ACC_ROWS, GROUP * tq), f32),
                 pltpu.VMEM((N_KV_HEADS, LANES, GROUP * tq), bf16),
                 pltpu.VMEM((N_HEADS * HEAD_DIM, tq), bf16)])


def _fox_attn(x, gate, qt, kpad, vt, c_pad, c_t, wot, tq):
    bsz, _, t, _ = kpad.shape
    m, d = x.shape
    sp = _attn_specs(bsz, t, d, tq)
    return pl.pallas_call(
        functools.partial(_fox_attn_kernel, tq=tq, tk=ATTN_TK, chunk=512),
        grid=(bsz, t // tq),
        in_specs=[sp["x"], sp["gate"], sp["qt"], sp["kpad"], sp["vt"],
                  pl.BlockSpec((1, t, LANES), lambda b, q: (b, 0, 0)),
                  pl.BlockSpec((1, N_HEADS, tq), lambda b, q: (b, 0, q)),
                  sp["wot"]],
        out_specs=sp["x"],
        out_shape=jax.ShapeDtypeStruct((m, d), f32),
        scratch_shapes=[pltpu.VMEM((N_KV_HEADS, t, LANES), bf16)] + sp["scratch"],
        compiler_params=_cparams("parallel", "arbitrary"),
        name="fox_attn",
    )(x, gate, qt, kpad, vt, c_pad, c_t, wot)


def _cumsum_t_kernel(lft_ref, ct_ref, carry_sc, *, tb):
    @pl.when(pl.program_id(1) == 0)
    def _():
        carry_sc[...] = jnp.zeros(carry_sc.shape, f32)

    tri = (lax.broadcasted_iota(i32, (tb, tb), 0)
           <= lax.broadcasted_iota(i32, (tb, tb), 1)).astype(bf16)
    lft = lft_ref[0]
    csum = sum(_dot(p, tri) for p in _split3(lft))
    ct_ref[0] = csum + carry_sc[...]
    carry_sc[...] = carry_sc[...] + jnp.sum(lft, axis=1, keepdims=True)


def _cumsum_t(lf_t, tb):
    b, nh, t = lf_t.shape
    spec = pl.BlockSpec((1, nh, tb), lambda bi, j: (bi, 0, j))
    return pl.pallas_call(
        functools.partial(_cumsum_t_kernel, tb=tb),
        grid=(b, t // tb),
        in_specs=[spec],
        out_specs=spec,
        out_shape=jax.ShapeDtypeStruct((b, nh, t), f32),
        scratch_shapes=[pltpu.VMEM((nh, 1), f32)],
        compiler_params=_cparams("parallel", "arbitrary"),
        name="cumsum_t",
    )(lf_t)


def _kth_threshold(count_ge, shape, n_sel):
    zero = jnp.zeros(shape, i32)
    ans = jnp.where(count_ge(zero) >= n_sel, zero, jnp.full(shape, INT_MIN, i32))

    def bit_body(i, ans):
        cand = ans | jnp.left_shift(jnp.int32(1), 30 - i)
        return jnp.where(count_ge(cand) >= n_sel, cand, ans)

    return lax.fori_loop(0, 31, bit_body, ans)


def _kth_threshold_early(count_ge, shape, n_sel, n_total):
    zero = jnp.zeros(shape, i32)
    c0 = count_ge(zero)
    ans = jnp.where(c0 >= n_sel, zero, jnp.full(shape, INT_MIN, i32))
    exact = jnp.where((c0 == n_sel) | ((c0 < n_sel) & (n_total == n_sel)), 1, 0)

    def cond(state):
        i, _, _, n_open = state
        return (i < 31) & (n_open > 0)

    def body(state):
        i, ans, exact, _ = state
        cand = ans | jnp.left_shift(jnp.int32(1), 30 - i)
        c = count_ge(cand)
        take = (exact == 0) & (c >= n_sel)
        ans = jnp.where(take, cand, ans)
        exact = jnp.where(take & (c == n_sel), 1, exact)
        return i + 1, ans, exact, jnp.sum(1 - exact)

    _, ans, exact, _ = lax.while_loop(cond, body, (jnp.int32(0), ans, exact, jnp.sum(1 - exact)))
    return ans, exact


def _tie_cut(count_tied_below, need, shape, n_bits):
    def body(i, p):
        cand = p | jnp.left_shift(jnp.int32(1), n_bits - 1 - i)
        return jnp.where(count_tied_below(cand) < need, cand, p)
    return lax.fori_loop(0, n_bits, body, jnp.zeros(shape, i32))


def _fold_lanes(ind, c):
    for cc in range(ind.shape[1] // LANES):
        c = c + ind[:, cc * LANES:(cc + 1) * LANES]
    return c


def _dsa_attn_kernel(x_ref, gate_ref, qt_ref, kpad_ref, vt_ref, iqt_ref, iwt_ref, ik_ref,
                     wot_ref, o_ref, key_sc, bias_sc, cut_sc, m_sc, acc_sc, qa_sc, ot_sc,
                     *, tq, n_sel, n_bits):
    qi = pl.program_id(1)
    n_k = qi + 1
    row = lax.broadcasted_iota(i32, (tq, tq), 0)
    d_iota = row - lax.broadcasted_iota(i32, (tq, tq), 1)

    def score_body(j, c):
        off = pl.multiple_of(j * tq, tq)
        ikt = ik_ref[0, pl.ds(off, tq), :]
        s = jnp.zeros((tq, tq), f32)
        for h in range(N_IDX_HEADS):
            dots = _dot(ikt, iqt_ref[h * IDX_DIM:(h + 1) * IDX_DIM, :])
            s = s + jnp.maximum(dots, 0.0) * iwt_ref[h:h + 1, :]
        key_sc[pl.ds(off, tq), :] = jnp.where(d_iota <= (qi - j) * tq, _sort_key(s),
                                              KEY_NEG_INF)
        return c

    lax.fori_loop(0, n_k, score_body, 0)

    def count(pred):
        def body(j, c):
            off = pl.multiple_of(j * tq, tq)
            ind = jnp.where(pred(key_sc[pl.ds(off, tq), :], j), 1, 0)
            for r in range(0, tq, SUBLANES):
                c = c + ind[r:r + SUBLANES]
            return c
        c = lax.fori_loop(0, n_k, body, jnp.zeros((SUBLANES, tq), i32))
        return jnp.sum(c, axis=0, keepdims=True)

    thr, exact = _kth_threshold_early(lambda cand: count(lambda kt, j: kt >= cand), (1, tq),
                                      n_sel, n_k * tq)
    real = thr > KEY_NEG_INF
    cut_sc[...] = jnp.where(real, INT_MAX, -1)

    @pl.when(jnp.max(jnp.where(real, 1 - exact, 0)) > 0)
    def _():
        need = n_sel - count(lambda kt, j: kt > thr)
        cut = _tie_cut(lambda cand: count(lambda kt, j: (kt == thr) & (row + j * tq < cand)),
                       need, (1, tq), n_bits)
        cut_sc[...] = jnp.where(real, jnp.where(exact == 1, INT_MAX, cut), -1)

    cut = cut_sc[...]
    floor = jnp.maximum(thr, KEY_NEG_INF)

    def bias_body(j, c):
        off = pl.multiple_of(j * tq, tq)
        kt = key_sc[pl.ds(off, tq), :]
        sel = (kt > floor) | ((kt == thr) & (row + j * tq <= cut))
        bias_sc[pl.ds(off, tq), :] = jnp.where(sel, 0.0, NEG)
        return c

    lax.fori_loop(0, n_k, bias_body, 0)

    for h in range(N_HEADS):
        _t_fill_q(qa_sc, h, tq, qt_ref[h * HEAD_DIM:(h + 1) * HEAD_DIM, :],
                  jnp.zeros((LANES - HEAD_DIM, tq), bf16))
    _t_init(m_sc, acc_sc)

    def body(j, c):
        off = pl.multiple_of(j * tq, tq)

        def masked(s):
            return jnp.concatenate([s[:, jj * tq:(jj + 1) * tq] + bias_sc[pl.ds(off, tq), :]
                                    for jj in range(GROUP)], axis=1)

        _t_tile(off, tq, lambda g: kpad_ref[0, g, pl.ds(off, tq), :], vt_ref, qa_sc,
                masked, m_sc, acc_sc)
        return c

    lax.fori_loop(0, n_k, body, 0)
    _t_finish(tq, acc_sc, ot_sc)
    _t_out(x_ref, gate_ref, wot_ref, ot_sc, o_ref)


def _dsa_attn(x, gate, qt, kpad, vt, iqt, iwt, ik, wot, tq, n_sel):
    bsz, _, t, _ = kpad.shape
    m, d = x.shape
    nq = t // tq
    sp = _attn_specs(bsz, t, d, tq)
    return pl.pallas_call(
        functools.partial(_dsa_attn_kernel, tq=tq, n_sel=n_sel, n_bits=(t - 1).bit_length()),
        grid=(bsz, nq),
        in_specs=[sp["x"], sp["gate"], sp["qt"], sp["kpad"], sp["vt"],
                  pl.BlockSpec((N_IDX_HEADS * IDX_DIM, tq), lambda b, q: (0, b * nq + q)),
                  pl.BlockSpec((2 * SUBLANES, tq), lambda b, q: (0, b * nq + q)),
                  pl.BlockSpec((1, t, IDX_DIM), lambda b, q: (b, 0, 0)),
                  sp["wot"]],
        out_specs=sp["x"],
        out_shape=jax.ShapeDtypeStruct((m, d), f32),
        scratch_shapes=[pltpu.VMEM((t, tq), i32), pltpu.VMEM((t, tq), f32),
                        pltpu.VMEM((1, tq), i32)] + sp["scratch"],
        compiler_params=_cparams("parallel", "arbitrary"),
        name="dsa_attn",
    )(x, gate, qt, kpad, vt, iqt, iwt, ik, wot)


def _out_proj_kernel(x_ref, o_ref, w_ref, gate_ref, y_ref):
    y_ref[...] = x_ref[...] + gate_ref[0] * _dot(o_ref[...], w_ref[...])


def _out_proj(x, o, w, gate, tm, rows_per_mod):
    m, d = x.shape
    kdim = o.shape[1]
    r = gate.shape[1]
    return pl.pallas_call(
        _out_proj_kernel,
        grid=(m // tm,),
        in_specs=[pl.BlockSpec((tm, d), lambda i: (i, 0)),
                  pl.BlockSpec((tm, kdim), lambda i: (i, 0)),
                  pl.BlockSpec((kdim, d), lambda i: (0, 0)),
                  pl.BlockSpec((1, r, d), lambda i: (i * tm // rows_per_mod, 0, 0))],
        out_specs=pl.BlockSpec((tm, d), lambda i: (i, 0)),
        out_shape=jax.ShapeDtypeStruct((m, d), f32),
        compiler_params=_cparams("parallel"),
        name="out_proj",
    )(x, o, w, gate)


FFN_CHUNK = 256
FFN_HALO = 2 * SUBLANES


def _ffn_kernel(x_ref, xh_ref, g_ref, shift_ref, scale_ref, gate_ref, wa_ref, wb_ref,
                cw_ref, cb_ref, wd_ref, gfin_ref, o_ref, tail_ref, acc_sc, hext_sc,
                *, tm, tiles_per_seq, final):
    i = pl.program_id(0)
    x = x_ref[...]
    keep = jnp.where(i % tiles_per_seq == 0, 0.0, 1.0)
    hext_sc[:FFN_HALO, :] = (_norm_mod(xh_ref[...], g_ref[...], shift_ref[0], scale_ref[0])
                             * keep).astype(bf16)
    hext_sc[FFN_HALO:, :] = _norm_mod(x, g_ref[...], shift_ref[0], scale_ref[0]).astype(bf16)
    for c in range(D_FF // FFN_CHUNK):
        sl = slice(c * FFN_CHUNK, (c + 1) * FFN_CHUNK)
        a_ext = _dot(hext_sc[...], wa_ref[:, sl])
        b = _dot(hext_sc[FFN_HALO:, :], wb_ref[:, sl])
        a = a_ext[FFN_HALO:]
        p1 = pltpu.roll(a_ext, 1, 0)[FFN_HALO:]
        p2 = pltpu.roll(a_ext, 2, 0)[FFN_HALO:]
        conv = cb_ref[:, sl] + p2 * cw_ref[0:1, sl] + p1 * cw_ref[1:2, sl] + a * cw_ref[2:3, sl]
        y = _dot((_silu(conv) * b).astype(bf16), wd_ref[sl, :])
        if c == 0:
            acc_sc[...] = y
        else:
            acc_sc[...] += y
        tail_ref[0, :, sl] = a_ext[FFN_HALO + tm - SUBLANES:]
    y = x + gate_ref[0] * acc_sc[...]
    if final:
        y = y * lax.rsqrt(jnp.mean(y * y, axis=-1, keepdims=True) + EPS) * gfin_ref[...]
    o_ref[...] = y


def _ffn(x, g, shift, scale, gate, wa, wb, cw, cb, wd, g_final, final, tm, rows_per_seq):
    m, d = x.shape
    n_tiles = m // tm
    hb = tm // FFN_HALO
    mod_spec = pl.BlockSpec((1, 1, d), lambda i: (i * tm // rows_per_seq, 0, 0))
    const = lambda shape: pl.BlockSpec(shape, lambda i: (0,) * len(shape))
    return pl.pallas_call(
        functools.partial(_ffn_kernel, tm=tm, tiles_per_seq=rows_per_seq // tm, final=final),
        grid=(n_tiles,),
        in_specs=[pl.BlockSpec((tm, d), lambda i: (i, 0)),
                  pl.BlockSpec((FFN_HALO, d), lambda i: (jnp.maximum(i * hb - 1, 0), 0)),
                  const((1, d)), mod_spec, mod_spec, mod_spec,
                  const((d, D_FF)), const((d, D_FF)), const((SUBLANES, D_FF)),
                  const((1, D_FF)), const((D_FF, d)), const((1, d))],
        out_specs=[pl.BlockSpec((tm, d), lambda i: (i, 0)),
                   pl.BlockSpec((1, SUBLANES, D_FF), lambda i: (i, 0, 0))],
        out_shape=[jax.ShapeDtypeStruct((m, d), f32),
                   jax.ShapeDtypeStruct((n_tiles, SUBLANES, D_FF), f32)],
        scratch_shapes=[pltpu.VMEM((tm, d), f32), pltpu.VMEM((FFN_HALO + tm, d), bf16)],
        compiler_params=_cparams("parallel"),
        name="ffn",
    )(x, x, g, shift, scale, gate, wa, wb, cw, cb, wd, g_final)


def _ffn_sample_kernel(x_ref, g_ref, shift_ref, scale_ref, gate_ref, p2_ref, p1_ref, wa_ref,
                       wb_ref, cw_ref, cb_ref, wd_ref, o_ref, a_ref, acc_sc):
    j = pl.program_id(0)

    @pl.when(j == 0)
    def _():
        acc_sc[...] = jnp.zeros(acc_sc.shape, f32)

    x = x_ref[...]
    h = _norm_mod(x, g_ref[...], shift_ref[0], scale_ref[0]).astype(bf16)
    a = _dot(h, wa_ref[...])
    b = _dot(h, wb_ref[...])
    conv = (cb_ref[...] + p2_ref[...] * cw_ref[0:1, :] + p1_ref[...] * cw_ref[1:2, :]
            + a * cw_ref[2:3, :])
    acc_sc[...] += _dot((_silu(conv) * b).astype(bf16), wd_ref[...])
    a_ref[...] = a

    @pl.when(j == pl.num_programs(0) - 1)
    def _():
        o_ref[...] = x + gate_ref[0] * acc_sc[...]


def _ffn_sample(x, g, shift, scale, gate, p2, p1, wa, wb, cw, cb, wd, tn):
    m, d = x.shape
    const = lambda shape: pl.BlockSpec(shape, lambda j: (0,) * len(shape))
    return pl.pallas_call(
        _ffn_sample_kernel,
        grid=(D_FF // tn,),
        in_specs=[const((m, d)), const((1, d)), const((1, m, d)), const((1, m, d)),
                  const((1, m, d)),
                  pl.BlockSpec((m, tn), lambda j: (0, j)), pl.BlockSpec((m, tn), lambda j: (0, j)),
                  pl.BlockSpec((d, tn), lambda j: (0, j)), pl.BlockSpec((d, tn), lambda j: (0, j)),
                  pl.BlockSpec((SUBLANES, tn), lambda j: (0, j)),
                  pl.BlockSpec((1, tn), lambda j: (0, j)),
                  pl.BlockSpec((tn, d), lambda j: (j, 0))],
        out_specs=[const((m, d)), pl.BlockSpec((m, tn), lambda j: (0, j))],
        out_shape=[jax.ShapeDtypeStruct((m, d), f32), jax.ShapeDtypeStruct((m, D_FF), f32)],
        scratch_shapes=[pltpu.VMEM((m, d), f32)],
        compiler_params=_cparams("arbitrary"),
        name="ffn_sample",
    )(x, g, shift, scale, gate, p2, p1, wa, wb, cw, cb, wd)


def _final_norm_kernel(x_ref, g_ref, o_ref):
    x = x_ref[...]
    o_ref[...] = x * lax.rsqrt(jnp.mean(x * x, axis=-1, keepdims=True) + EPS) * g_ref[...]


def _final_norm(x, g, tm):
    m, d = x.shape
    return pl.pallas_call(
        _final_norm_kernel,
        grid=(m // tm,),
        in_specs=[pl.BlockSpec((tm, d), lambda i: (i, 0)), pl.BlockSpec((1, d), lambda i: (0, 0))],
        out_specs=pl.BlockSpec((tm, d), lambda i: (i, 0)),
        out_shape=jax.ShapeDtypeStruct((m, d), f32),
        compiler_params=_cparams("parallel"),
        name="final_norm",
    )(x, g)


def _seq_page_copies(pt_ref, hbm, layer, seq, buf, slot, sem, n_pages):
    return [pltpu.make_async_copy(hbm.at[layer, pt_ref[seq, p]],
                                  buf.at[slot, :, pl.ds(p * PAGE_SIZE, PAGE_SIZE)], sem.at[slot])
            for p in range(n_pages)]


def _gather_seq(pt_ref, hbm, layer, buf, sem, n_pages):
    b = pl.program_id(0)
    slot = b % 2

    @pl.when(b == 0)
    def _():
        for cp in _seq_page_copies(pt_ref, hbm, layer, 0, buf, 0, sem, n_pages):
            cp.start()

    @pl.when(b + 1 < pl.num_programs(0))
    def _():
        for cp in _seq_page_copies(pt_ref, hbm, layer, b + 1, buf, 1 - slot, sem, n_pages):
            cp.start()

    for cp in _seq_page_copies(pt_ref, hbm, layer, b, buf, slot, sem, n_pages):
        cp.wait()
    return slot


def _sample_score_kernel(pt_ref, iq_ref, iw_ref, kidx_hbm, s_ref, buf, sem, *, layer, n_pages):
    slot = _gather_seq(pt_ref, kidx_hbm, layer, buf, sem, n_pages)
    d = _dot(iq_ref[0], buf[slot].astype(bf16))
    s_ref[0] = jnp.sum(jnp.maximum(d, 0.0) * iw_ref[0], axis=0, keepdims=True)


def _sample_scores(page_table, iq, iw, cache_kidx_t, layer):
    s, n_pages = page_table.shape
    length = n_pages * PAGE_SIZE
    grid_spec = pltpu.PrefetchScalarGridSpec(
        num_scalar_prefetch=1, grid=(s,),
        in_specs=[pl.BlockSpec((1, N_IDX_HEADS, IDX_DIM), lambda b, pt: (b, 0, 0)),
                  pl.BlockSpec((1, N_IDX_HEADS, 1), lambda b, pt: (b, 0, 0)),
                  pl.BlockSpec(memory_space=pl.ANY)],
        out_specs=pl.BlockSpec((1, 1, length), lambda b, pt: (b, 0, 0)),
        scratch_shapes=[pltpu.VMEM((2, IDX_DIM, length), f32), pltpu.SemaphoreType.DMA((2,))])
    out = pl.pallas_call(
        functools.partial(_sample_score_kernel, layer=layer, n_pages=n_pages),
        grid_spec=grid_spec,
        out_shape=jax.ShapeDtypeStruct((s, 1, length), f32),
        compiler_params=_cparams("arbitrary"),
        name="sample_scores",
    )(page_table, iq, iw, cache_kidx_t)
    return out.reshape(s, length)


SEL_TILE = 1024


def _sample_select_kernel(s_ref, iq_ref, ikt_ref, iw_ref, bias_ref, bnew_ref, key_sc,
                          *, n_sel, n_bits):
    rows, length = s_ref.shape
    n_tiles = length // SEL_TILE
    prod = iq_ref[...].astype(f32) * ikt_ref[...].astype(f32)
    grp = (lax.broadcasted_iota(i32, (prod.shape[1], LANES), 0) // IDX_DIM
           == lax.broadcasted_iota(i32, (prod.shape[1], LANES), 1)).astype(bf16)
    d_new = sum(_dot(p, grp) for p in _split3(prod))
    s_new = jnp.sum(jnp.maximum(d_new, 0.0) * iw_ref[...], axis=1, keepdims=True)
    key_new = _sort_key(s_new)
    key_sc[...] = _sort_key(s_ref[...])
    col = lax.broadcasted_iota(i32, (rows, SEL_TILE), 1)

    def count(pred, pred_new):
        def body(j, c):
            off = pl.multiple_of(j * SEL_TILE, SEL_TILE)
            return _fold_lanes(jnp.where(pred(key_sc[:, pl.ds(off, SEL_TILE)], j), 1, 0), c)
        c = lax.fori_loop(0, n_tiles, body, jnp.zeros((rows, LANES), i32))
        return jnp.sum(c, axis=1, keepdims=True) + jnp.where(pred_new, 1, 0)

    thr = _kth_threshold(lambda cand: count(lambda kt, j: kt >= cand, key_new >= cand),
                         (rows, 1), n_sel)
    need = n_sel - count(lambda kt, j: kt > thr, key_new > thr)
    cut = _tie_cut(lambda cand: count(lambda kt, j: (kt == thr) & (col + j * SEL_TILE < cand),
                                      (key_new == thr) & (length < cand)),
                   need, (rows, 1), n_bits)

    def bias_body(j, c):
        off = pl.multiple_of(j * SEL_TILE, SEL_TILE)
        kt = key_sc[:, pl.ds(off, SEL_TILE)]
        sel = (kt > thr) | ((kt == thr) & (col + j * SEL_TILE <= cut))
        bias_ref[:, pl.ds(off, SEL_TILE)] = jnp.where(sel, 0.0, NEG)
        return c

    lax.fori_loop(0, n_tiles, bias_body, 0)
    sel_new = (key_new > thr) | ((key_new == thr) & (length <= cut))
    bnew_ref[...] = jnp.broadcast_to(jnp.where(sel_new, 0.0, NEG), bnew_ref.shape)


def _sample_select(scores, iq, ik_tiled, iw, n_sel):
    rows, length = scores.shape
    return pl.pallas_call(
        functools.partial(_sample_select_kernel, n_sel=n_sel, n_bits=length.bit_length()),
        out_shape=[jax.ShapeDtypeStruct((rows, length), f32),
                   jax.ShapeDtypeStruct((rows, LANES), f32)],
        scratch_shapes=[pltpu.VMEM((rows, length), i32)],
        compiler_params=pltpu.CompilerParams(vmem_limit_bytes=VMEM_LIMIT),
        name="sample_select",
    )(scores, iq, ik_tiled, iw)


SUFFIX_CHUNK = 256


def _suffix_bias_kernel(pt_ref, lfnew_ref, lf_hbm, bias_ref, buf, sem, *, layer, n_pages):
    slot = _gather_seq(pt_ref, lf_hbm, layer, buf, sem, n_pages)
    w = SUFFIX_CHUNK
    tri = (lax.broadcasted_iota(i32, (w, w), 0)
           > lax.broadcasted_iota(i32, (w, w), 1)).astype(bf16)
    carry = lfnew_ref[0]
    for c in reversed(range(n_pages * PAGE_SIZE // w)):
        x = buf[slot, :, c * w:(c + 1) * w]
        bias_ref[0, :, c * w:(c + 1) * w] = sum(_dot(p, tri) for p in _split3(x)) + carry
        carry = carry + jnp.sum(x, axis=1, keepdims=True)


def _suffix_bias(page_table, lf_new, cache_logf_t, layer):
    s, n_pages = page_table.shape
    length = n_pages * PAGE_SIZE
    grid_spec = pltpu.PrefetchScalarGridSpec(
        num_scalar_prefetch=1, grid=(s,),
        in_specs=[pl.BlockSpec((1, N_HEADS, 1), lambda b, pt: (b, 0, 0)),
                  pl.BlockSpec(memory_space=pl.ANY)],
        out_specs=pl.BlockSpec((1, N_HEADS, length), lambda b, pt: (b, 0, 0)),
        scratch_shapes=[pltpu.VMEM((2, N_HEADS, length), f32), pltpu.SemaphoreType.DMA((2,))])
    return pl.pallas_call(
        functools.partial(_suffix_bias_kernel, layer=layer, n_pages=n_pages),
        grid_spec=grid_spec,
        out_shape=jax.ShapeDtypeStruct((s, N_HEADS, length), f32),
        compiler_params=_cparams("arbitrary"),
        name="suffix_bias",
    )(page_table, lf_new, cache_logf_t)


DEC_PAGES = 8
DEC_SLOTS = 3


def _kv_chunk_copies(pt_ref, k_hbm, v_hbm, layer, seq, chunk, kbuf, vbuf, slot, sem):
    cps = []
    for i in range(DEC_PAGES):
        page = pt_ref[seq, chunk * DEC_PAGES + i]
        cps.append(pltpu.make_async_copy(k_hbm.at[layer, page], kbuf.at[slot, i], sem.at[0, slot]))
        cps.append(pltpu.make_async_copy(v_hbm.at[layer, page], vbuf.at[slot, i], sem.at[1, slot]))
    return cps


def _decode_attn_kernel(pt_ref, q_ref, knew_ref, vnew_ref, bnew_ref, bias_ref, k_hbm, v_hbm,
                        o_ref, kbuf, vbuf, sem, *, layer, n_chunks):
    b = pl.program_id(0)
    q = q_ref[0]
    width = DEC_PAGES * PAGE_SIZE
    copies = functools.partial(_kv_chunk_copies, pt_ref, k_hbm, v_hbm, layer)

    def start(seq, chunk, slot):
        for cp in copies(seq, chunk, kbuf, vbuf, slot, sem):
            cp.start()

    ahead = DEC_SLOTS - 1
    base = b * n_chunks

    @pl.when(b == 0)
    def _():
        for c in range(ahead):
            start(0, c, c)

    kn = knew_ref[0].astype(bf16).astype(f32)
    m = jnp.sum(q.astype(f32) * kn, axis=1, keepdims=True) + bnew_ref[0][:, 0:1]
    l = jnp.ones_like(m)
    acc = jnp.broadcast_to(vnew_ref[0].astype(bf16).astype(f32), (N_HEADS, KV_COLS))

    for c in range(n_chunks):
        slot = lax.rem(base + c, DEC_SLOTS)
        nxt = c + ahead
        nslot = lax.rem(base + nxt, DEC_SLOTS)
        if nxt < n_chunks:
            start(b, nxt, nslot)
        else:
            @pl.when(b + 1 < pl.num_programs(0))
            def _():
                start(b + 1, nxt - n_chunks, nslot)
        for cp in copies(b, c, kbuf, vbuf, slot, sem):
            cp.wait()
        s = jnp.concatenate([_dot(q, kbuf[slot, i].astype(bf16)) for i in range(DEC_PAGES)],
                            axis=1) + bias_ref[0, :, c * width:(c + 1) * width]
        m_new = jnp.maximum(m, jnp.max(s, axis=1, keepdims=True))
        p = jnp.exp(s - m_new)
        alpha = jnp.exp(m - m_new)
        l = alpha * l + jnp.sum(p, axis=1, keepdims=True)
        pb = p.astype(bf16)
        pv = sum(_dot_nt(pb[:, i * PAGE_SIZE:(i + 1) * PAGE_SIZE], vbuf[slot, i].astype(bf16))
                 for i in range(DEC_PAGES))
        acc = alpha * acc + pv
        m = m_new
    o_ref[0] = acc / l


def _decode_attn(page_table, qmat, k_new, v_new, bias_new, bias, cache_kt, cache_vt, layer):
    s, n_pages = page_table.shape
    hb = bias.shape[1]
    n_chunks = n_pages // DEC_PAGES
    assert n_chunks * DEC_PAGES == n_pages and n_chunks >= DEC_SLOTS - 1
    row = lambda shape: pl.BlockSpec(shape, lambda b, pt: (b, 0, 0))
    hbm = pl.BlockSpec(memory_space=pl.ANY)
    width = DEC_PAGES * PAGE_SIZE
    grid_spec = pltpu.PrefetchScalarGridSpec(
        num_scalar_prefetch=1, grid=(s,),
        in_specs=[row((1, N_HEADS, KV_COLS)), row((1, 1, KV_COLS)), row((1, 1, KV_COLS)),
                  row((1, hb, LANES)), row((1, hb, n_pages * PAGE_SIZE)), hbm, hbm],
        out_specs=row((1, N_HEADS, KV_COLS)),
        scratch_shapes=[pltpu.VMEM((DEC_SLOTS, DEC_PAGES, KV_COLS, PAGE_SIZE), f32),
                        pltpu.VMEM((DEC_SLOTS, DEC_PAGES, KV_COLS, PAGE_SIZE), f32),
                        pltpu.SemaphoreType.DMA((2, DEC_SLOTS))])
    return pl.pallas_call(
        functools.partial(_decode_attn_kernel, layer=layer, n_chunks=n_chunks),
        grid_spec=grid_spec,
        out_shape=jax.ShapeDtypeStruct((s, N_HEADS, KV_COLS), f32),
        compiler_params=_cparams("arbitrary"),
        name="decode_attn",
    )(page_table, qmat, k_new, v_new, bias_new, bias, cache_kt, cache_vt)


def _pad_cols(w, n):
    return jnp.pad(w, ((0, 0), (0, n)))


def _decode_q(q_bf):
    s = q_bf.shape[0]
    onehot = (jnp.arange(N_HEADS)[:, None] // GROUP == jnp.arange(N_KV_HEADS)[None, :])
    q = q_bf.reshape(s, N_HEADS, 1, HEAD_DIM) * onehot[None, :, :, None].astype(q_bf.dtype)
    return q.reshape(s, N_HEADS, KV_COLS)


def _decode_o(o_full):
    s = o_full.shape[0]
    o = o_full.reshape(s, N_KV_HEADS, GROUP, N_KV_HEADS, HEAD_DIM)
    return jnp.stack([o[:, g, :, g, :] for g in range(N_KV_HEADS)], axis=1).reshape(s, -1)


def kernel(x_prompt, x_sample, cache_k_a, cache_v_a, cache_kidx_a, cache_k_b, cache_v_b,
           cache_logf_b, state_conv, page_table, c_prompt, c_sample, w_ada, b_ada, g_mix,
           g_ffn, w_in_a, w_o_a, w_in_b, b_f, w_o_b, w_up, conv_w, conv_b, w_down, g_final):
    bsz, t, d = x_prompt.shape
    s = x_sample.shape[0]
    depth = w_ada.shape[0]
    m = bsz * t
    past_len = page_table.shape[1] * PAGE_SIZE
    n_pool = cache_k_a.shape[1]
    tm = 512
    tq = 256

    pad_rows = (-(bsz + s)) % SUBLANES
    c_all = jnp.concatenate([c_prompt, c_sample, jnp.zeros((pad_rows, d), f32)], axis=0)
    mods = _adaln(c_all, w_ada, b_ada)
    mods_p = mods[:, :bsz].reshape(depth, bsz, 6, d)
    mods_s = mods[:, bsz:bsz + s].reshape(depth, s, 6, d)

    tabs_p = _rope_tables(jnp.arange(t))
    tabs_pt = _rope_tables_t(jnp.arange(t))
    tabs_s = _rope_tables(jnp.full((s,), past_len))
    zero_bf = jnp.zeros((1, LANES), f32)
    zero_col = jnp.zeros((N_HEADS, 1), f32)

    def rows_t(a):
        return a.reshape(bsz, N_KV_HEADS, HEAD_DIM, t).transpose(0, 3, 1, 2)

    xp = x_prompt.reshape(m, d)
    xs = x_sample.reshape(s, d)
    kv_shape_p = (bsz, t, N_KV_HEADS, HEAD_DIM)
    kv_shape_s = (s, 1, N_KV_HEADS, HEAD_DIM)
    ka_p, va_p, ia_p, kb_p, vb_p, lb_p, cv_p = [], [], [], [], [], [], []
    ka_s, va_s, ia_s, kb_s, vb_s, lb_s, cv_s = [], [], [], [], [], [], []

    def pages_t(cache):
        ct = jnp.moveaxis(cache, 2, -1)
        return ct.reshape(ct.shape[:2] + (-1, PAGE_SIZE))

    cache_k_a, cache_v_a, cache_kidx_a = pages_t(cache_k_a), pages_t(cache_v_a), pages_t(cache_kidx_a)
    cache_k_b, cache_v_b, cache_logf_b = pages_t(cache_k_b), pages_t(cache_v_b), pages_t(cache_logf_b)

    for i in range(depth):
        j = i // 2
        mp = [mods_p[i, :, c][:, None, :] for c in range(6)]
        ms = [mods_s[i, :, c][None] for c in range(6)]
        g_m = g_mix[i][None]
        if i % 2 == 0:
            w = w_in_a[j]
            n_qkvi = QKV_COLS + N_IDX_HEADS * IDX_DIM + IDX_DIM
            w_in = jnp.concatenate([_pad_cols(w[:, :n_qkvi], LANES - IDX_DIM),
                                    _pad_cols(w[:, n_qkvi:], LANES - N_IDX_HEADS)],
                                   axis=1).astype(bf16)
            w_o = w_o_a[j].astype(bf16)
            nq = N_HEADS * HEAD_DIM
            n_iq = QKV_COLS + N_IDX_HEADS * IDX_DIM
            wk = w[:, nq:nq + KV_COLS]
            w_row = jnp.concatenate([wk, _pad_cols(w[:, n_iq:n_qkvi], LANES - IDX_DIM)],
                                    axis=1).astype(bf16)
            w_t = jnp.concatenate([w[:, :nq], w[:, nq + KV_COLS:QKV_COLS], wk, w[:, QKV_COLS:n_qkvi],
                                   _pad_cols(w[:, n_qkvi:], IW_ROWS - N_IDX_HEADS)],
                                  axis=1).T.astype(bf16)
            qt, vt, kpad, kt32, vt32, iqt, iwt, ikb, ikt32 = _in_proj_t(
                xp, g_m, mp[0], mp[1], w_row, w_t, tabs_p, tabs_pt, zero_col, True, tm, t)
            xp = _dsa_attn(xp, mp[2], qt, kpad, vt, iqt, iwt, ikb.reshape(bsz, t, IDX_DIM),
                           w_o.T, tq, min(TOPK_MAX, t // 4))
            ka_p.append(rows_t(kt32))
            va_p.append(rows_t(vt32))
            ia_p.append(ikt32.transpose(0, 2, 1))
            q, k32, kb, v32, vb, iq, ik32, ikb, iw = _in_proj(
                xs, g_m, ms[0], ms[1], w_in, tabs_s, zero_bf, SEGS_A, OUTS_A, s, s)
            scores = _sample_scores(page_table, iq.reshape(s, N_IDX_HEADS, IDX_DIM),
                                    iw[:, :N_IDX_HEADS].reshape(s, N_IDX_HEADS, 1),
                                    cache_kidx_a, j)
            bias, bias_new = _sample_select(scores, iq, jnp.tile(ikb[:, :IDX_DIM], (1, N_IDX_HEADS)),
                                            iw, min(TOPK_MAX, (past_len + 1) // 4))
            o_full = _decode_attn(page_table, _decode_q(q), k32[:, None], v32[:, None],
                                  bias_new[:, None], bias[:, None], cache_k_a, cache_v_a, j)
            xs = _out_proj(xs, _decode_o(o_full).astype(bf16), w_o, ms[2], s, s)
            ka_s.append(k32.reshape(kv_shape_s))
            va_s.append(v32.reshape(kv_shape_s))
            ia_s.append(ik32[:, :IDX_DIM].reshape(s, 1, IDX_DIM))
        else:
            w_in = _pad_cols(w_in_b[j], LANES - N_HEADS).astype(bf16)
            w_o = w_o_b[j].astype(bf16)
            bf_pad = _pad_cols(b_f[j][None], LANES - N_HEADS)
            nq = N_HEADS * HEAD_DIM
            wk = w_in[:, nq:nq + KV_COLS]
            w_t = jnp.concatenate([w_in[:, :nq], w_in[:, nq + KV_COLS:QKV_COLS], wk,
                                   w_in[:, QKV_COLS:QKV_COLS + N_HEADS]], axis=1).T
            qt, vt, kpad, kt32, vt32, lft = _in_proj_t(
                xp, g_m, mp[0], mp[1], wk, w_t, tabs_p, tabs_pt, b_f[j][:, None], False, tm, t)
            c_t = _cumsum_t(lft, tq)
            c_pad = jnp.pad(c_t.transpose(0, 2, 1), ((0, 0), (0, 0), (0, LANES - N_HEADS)))
            xp = _fox_attn(xp, mp[2], qt, kpad, vt, c_pad, c_t, w_o.T, tq)
            kb_p.append(rows_t(kt32))
            vb_p.append(rows_t(vt32))
            lb_p.append(lft.transpose(0, 2, 1))
            q, k32, kb, v32, vb, lf = _in_proj(
                xs, g_m, ms[0], ms[1], w_in, tabs_s, bf_pad, SEGS_B, OUTS_B, s, s)
            bias = _suffix_bias(page_table, lf[:, :N_HEADS, None], cache_logf_b, j)
            o_full = _decode_attn(page_table, _decode_q(q), k32[:, None], v32[:, None],
                                  jnp.zeros((s, N_HEADS, LANES), f32), bias,
                                  cache_k_b, cache_v_b, j)
            xs = _out_proj(xs, _decode_o(o_full).astype(bf16), w_o, ms[2], s, s)
            kb_s.append(k32.reshape(kv_shape_s))
            vb_s.append(v32.reshape(kv_shape_s))
            lb_s.append(lf[:, :N_HEADS].reshape(s, 1, N_HEADS))

        wa = w_up[i][:, :D_FF].astype(bf16)
        wb = w_up[i][:, D_FF:].astype(bf16)
        wd = w_down[i].astype(bf16)
        cw = jnp.pad(conv_w[i], ((0, SUBLANES - CONV_W), (0, 0)))
        cb = conv_b[i][None]
        g_f = g_ffn[i][None]
        xp, tail = _ffn(xp, g_f, mp[3], mp[4], mp[5], wa, wb, cw, cb, wd, g_final[None],
                        i == depth - 1, tm, t)
        cv_p.append(tail.reshape(bsz, t // tm, SUBLANES, D_FF)[:, -1, SUBLANES - (CONV_W - 1):])
        xs, a_s = _ffn_sample(xs, g_f, ms[3], ms[4], ms[5], state_conv[i][:, 0],
                              state_conv[i][:, 1], wa, wb, cw, cb, wd, D_FF // 2)
        cv_s.append(jnp.stack([state_conv[i][:, 1], a_s], axis=1))

    y_prompt = xp.reshape(bsz, t, d)
    y_sample = _final_norm(xs, g_final[None], s).reshape(s, 1, d)
    return (y_prompt, y_sample,
            jnp.stack(ka_p), jnp.stack(va_p), jnp.stack(ia_p),
            jnp.stack(kb_p), jnp.stack(vb_p), jnp.stack(lb_p), jnp.stack(cv_p),
            jnp.stack(ka_s), jnp.stack(va_s), jnp.stack(ia_s),
            jnp.stack(kb_s), jnp.stack(vb_s), jnp.stack(lb_s), jnp.stack(cv_s))
```

```python
import functools

import jax
import jax.numpy as jnp
from jax import lax
from jax.experimental import pallas as pl
from jax.experimental.pallas import tpu as pltpu

f32 = jnp.float32
bf16 = jnp.bfloat16
i32 = jnp.int32

D_MODEL = 1024
N_HEADS = 16
HEAD_DIM = 64
N_KV_HEADS = 4
GROUP = N_HEADS // N_KV_HEADS
KV_COLS = N_KV_HEADS * HEAD_DIM
QKV_COLS = (N_HEADS + 2 * N_KV_HEADS) * HEAD_DIM
ROPE_DIMS = HEAD_DIM // 4
ROPE_HALF = ROPE_DIMS // 2
ROPE_THETA = 500000.0
N_IDX_HEADS = 8
IDX_DIM = 64
TOPK_MAX = 256
D_FF = 2816
CONV_W = 3
EPS = 1e-6
PAGE_SIZE = 128
Q_SCALE = HEAD_DIM ** -0.5
LOG2E = 1.4426950408889634

LANES = 128
SUBLANES = 8
VMEM_LIMIT = 56 * 1024 * 1024
NEG = -1e30
INT_MIN = -2 ** 31
INT_MAX = 2 ** 31 - 1
KEY_NEG_INF = (0xFF800000 ^ 0x7FFFFFFF) - 2 ** 32


def _cparams(*sem):
    return pltpu.CompilerParams(dimension_semantics=sem, vmem_limit_bytes=VMEM_LIMIT)


def _dot(a, b):
    return jnp.dot(a, b, preferred_element_type=f32)


def _dot_nt(a, b):
    return lax.dot_general(a, b, (((1,), (1,)), ((), ())), preferred_element_type=f32)


def _split3(x):
    hi = x.astype(bf16)
    r1 = x - hi.astype(f32)
    mid = r1.astype(bf16)
    lo = (r1 - mid.astype(f32)).astype(bf16)
    return hi, mid, lo


def _norm_mod(x, g, shift, scale):
    xn = x * lax.rsqrt(jnp.mean(x * x, axis=-1, keepdims=True) + EPS)
    return (xn * g) * (1.0 + scale) + shift


def _silu(x):
    return x * jax.nn.sigmoid(x)


def _sort_key(s):
    bits = pltpu.bitcast(s, i32)
    return bits ^ ((bits >> 31) & 0x7FFFFFFF)


def _adaln_kernel(c_ref, w_ref, b_ref, o_ref):
    a = _silu(c_ref[...]).astype(bf16)
    o_ref[0] = _dot(a, w_ref[0].astype(bf16)) + b_ref[0]


def _adaln(c_all, w_ada, b_ada):
    depth, d, n = w_ada.shape
    rows = c_all.shape[0]
    tn = 1536
    return pl.pallas_call(
        _adaln_kernel,
        grid=(depth, n // tn),
        in_specs=[pl.BlockSpec((rows, d), lambda l, j: (0, 0)),
                  pl.BlockSpec((1, d, tn), lambda l, j: (l, 0, j)),
                  pl.BlockSpec((1, 1, tn), lambda l, j: (l, 0, j))],
        out_specs=pl.BlockSpec((1, rows, tn), lambda l, j: (l, 0, j)),
        out_shape=jax.ShapeDtypeStruct((depth, rows, n), f32),
        compiler_params=_cparams("parallel", "parallel"),
        name="adaln",
    )(c_all, w_ada, b_ada.reshape(depth, 1, n))


SEGS_A = ((0, 1024, "rope", ((0, Q_SCALE),)),
          (1024, 256, "rope", ((1, 1.0), (2, 1.0))),
          (1280, 256, None, ((3, 1.0), (4, 1.0))),
          (1536, 512, "rope", ((5, 1.0),)),
          (2048, 128, "rope", ((6, 1.0), (7, 1.0))),
          (2176, 128, None, ((8, 1.0),)))
OUTS_A = ((1024, bf16), (256, f32), (256, bf16), (256, f32), (256, bf16),
          (512, bf16), (128, f32), (128, bf16), (128, f32))
SEGS_B = ((0, 1024, None, ((0, Q_SCALE),)),
          (1024, 256, None, ((1, 1.0), (2, 1.0))),
          (1280, 256, None, ((3, 1.0), (4, 1.0))),
          (1536, 128, "logf", ((5, 1.0),)))
OUTS_B = ((1024, bf16), (256, f32), (256, bf16), (256, f32), (256, bf16), (128, f32))


def _in_proj_kernel(x_ref, g_ref, shift_ref, scale_ref, w_ref, cos_ref, sa_ref, sb_ref,
                    bf_ref, *out_refs, segs):
    h = _norm_mod(x_ref[...], g_ref[...], shift_ref[0], scale_ref[0]).astype(bf16)
    for c0, width, epi, outs in segs:
        for c in range(width // LANES):
            lo = c0 + c * LANES
            a = _dot(h, w_ref[:, lo:lo + LANES])
            if epi == "rope":
                a = (a * cos_ref[...] + pltpu.roll(a, ROPE_HALF, 1) * sa_ref[...]
                     + pltpu.roll(a, LANES - ROPE_HALF, 1) * sb_ref[...])
            elif epi == "logf":
                z = a + bf_ref[...]
                a = jnp.minimum(z, 0.0) - jnp.log(1.0 + jnp.exp(-jnp.abs(z)))
            for oi, sc in outs:
                val = a if sc == 1.0 else a * sc
                out_refs[oi][:, c * LANES:(c + 1) * LANES] = val.astype(out_refs[oi].dtype)


def _in_proj(x, g, shift, scale, w, tabs, b_f, segs, outs, tm, rows_per_mod):
    m, d = x.shape
    n = w.shape[1]
    r = shift.shape[1]
    cos, sa, sb = tabs
    tab_blocks = cos.shape[0] // tm
    mod_spec = pl.BlockSpec((1, r, d), lambda i: (i * tm // rows_per_mod, 0, 0))
    tab_spec = pl.BlockSpec((tm, LANES), lambda i: (i % tab_blocks, 0))
    return pl.pallas_call(
        functools.partial(_in_proj_kernel, segs=segs),
        grid=(m // tm,),
        in_specs=[pl.BlockSpec((tm, d), lambda i: (i, 0)),
                  pl.BlockSpec((1, d), lambda i: (0, 0)),
                  mod_spec, mod_spec,
                  pl.BlockSpec((d, n), lambda i: (0, 0)),
                  tab_spec, tab_spec, tab_spec,
                  pl.BlockSpec((1, LANES), lambda i: (0, 0))],
        out_specs=[pl.BlockSpec((tm, wd), lambda i: (i, 0)) for wd, _ in outs],
        out_shape=[jax.ShapeDtypeStruct((m, wd), dt) for wd, dt in outs],
        compiler_params=_cparams("parallel"),
        name="in_proj",
    )(x, g, shift, scale, w, cos, sa, sb, b_f)


def _rope_tables(pos):
    inv_freq = ROPE_THETA ** (-jnp.arange(ROPE_HALF, dtype=f32) / ROPE_HALF)
    ang = pos.astype(f32)[:, None] * inv_freq[None, :]
    cos, sin = jnp.cos(ang), jnp.sin(ang)
    t = pos.shape[0]
    ones = jnp.ones((t, HEAD_DIM - ROPE_DIMS), f32)
    zeros = jnp.zeros((t, HEAD_DIM - ROPE_DIMS), f32)
    zh = jnp.zeros((t, ROPE_HALF), f32)
    c = jnp.concatenate([cos, cos, ones], axis=1)
    sa = jnp.concatenate([zh, sin, zeros], axis=1)
    sb = jnp.concatenate([-sin, zh, zeros], axis=1)
    rep = LANES // HEAD_DIM
    return tuple(jnp.tile(a, (1, rep)) for a in (c, sa, sb))


WT_Q = 0
WT_V = N_HEADS * HEAD_DIM
WT_K = WT_V + KV_COLS
WT_X = WT_K + KV_COLS
WT_IK = WT_X + N_IDX_HEADS * IDX_DIM
WT_IW = WT_IK + IDX_DIM
IW_ROWS = 2 * SUBLANES
ROW_BLOCK = 512


def _rope_rows(a, cos_t, sin_t):
    parts = []
    for b0 in range(0, a.shape[0], HEAD_DIM):
        x1 = a[b0:b0 + ROPE_HALF]
        x2 = a[b0 + ROPE_HALF:b0 + ROPE_DIMS]
        parts += [x1 * cos_t - x2 * sin_t, x1 * sin_t + x2 * cos_t, a[b0 + ROPE_DIMS:b0 + HEAD_DIM]]
    return jnp.concatenate(parts, axis=0)


def _in_proj_t_kernel(x_ref, g_ref, shift_ref, scale_ref, w_ref, wt_ref, cos_ref, sa_ref, sb_ref,
                      cost_ref, sint_ref, bf_ref, *outs, is_a):
    if is_a:
        qt_ref, vt_ref, kpad_ref, kt32_ref, vt32_ref, iqt_ref, iwt_ref, ikb_ref, ikt32_ref = outs
    else:
        qt_ref, vt_ref, kpad_ref, kt32_ref, vt32_ref, lft_ref = outs
    h = _norm_mod(x_ref[...], g_ref[...], shift_ref[0], scale_ref[0]).astype(bf16)
    cos_t, sin_t = cost_ref[...], sint_ref[...]

    def rows(r0, n):
        blk = _dot_nt(wt_ref[r0:r0 + n, :], h)
        return [blk[c * LANES:(c + 1) * LANES] for c in range(n // LANES)]

    def rope_cols(a):
        return (a * cos_ref[...] + pltpu.roll(a, ROPE_HALF, 1) * sa_ref[...]
                + pltpu.roll(a, LANES - ROPE_HALF, 1) * sb_ref[...])

    for r0 in range(0, N_HEADS * HEAD_DIM, ROW_BLOCK):
        for c, a in enumerate(rows(WT_Q + r0, ROW_BLOCK)):
            if is_a:
                a = _rope_rows(a, cos_t, sin_t)
            lo = r0 + c * LANES
            qt_ref[lo:lo + LANES, :] = (a * (Q_SCALE * LOG2E)).astype(bf16)
    for c, (v, k) in enumerate(zip(rows(WT_V, KV_COLS), rows(WT_K, KV_COLS))):
        sl = slice(c * LANES, (c + 1) * LANES)
        vt32_ref[0, sl, :] = v
        vt_ref[sl, :] = v.astype(bf16)
        kt32_ref[0, sl, :] = _rope_rows(k, cos_t, sin_t) if is_a else k
    if is_a:
        for c, a in enumerate(rows(WT_X, N_IDX_HEADS * IDX_DIM)):
            iqt_ref[c * LANES:(c + 1) * LANES, :] = _rope_rows(a, cos_t, sin_t).astype(bf16)
        ikt32_ref[0] = _rope_rows(_dot_nt(wt_ref[WT_IK:WT_IK + IDX_DIM, :], h), cos_t, sin_t)
        iwt_ref[...] = _dot_nt(wt_ref[WT_IW:WT_IW + IW_ROWS, :], h)
    else:
        z = _dot_nt(wt_ref[WT_X:WT_X + N_HEADS, :], h) + bf_ref[...]
        lft_ref[0] = jnp.minimum(z, 0.0) - jnp.log(1.0 + jnp.exp(-jnp.abs(z)))

    lane = lax.broadcasted_iota(i32, (1, LANES), 1)
    std = _dot(h, w_ref[...])
    for c in range(KV_COLS // LANES):
        a = std[:, c * LANES:(c + 1) * LANES]
        if is_a:
            a = rope_cols(a)
        kpad_ref[0, 2 * c] = jnp.where(lane < HEAD_DIM, a, 0.0).astype(bf16)
        kpad_ref[0, 2 * c + 1] = jnp.where(lane < HEAD_DIM, pltpu.roll(a, HEAD_DIM, 1),
                                           0.0).astype(bf16)
    if is_a:
        ikb_ref[...] = rope_cols(std[:, KV_COLS:KV_COLS + LANES])[:, :IDX_DIM].astype(bf16)


def _in_proj_t(x, g, shift, scale, w, wt, tabs, tabs_t, b_f, is_a, tm, t):
    m, d = x.shape
    bsz = m // t
    tpb = t // tm
    col = lambda rows_: pl.BlockSpec((rows_, tm), lambda i: (0, i))
    seq = lambda rows_: pl.BlockSpec((1, rows_, tm), lambda i: (i // tpb, 0, i % tpb))
    seq_shape = lambda rows_: jax.ShapeDtypeStruct((bsz, rows_, t), f32)
    const = lambda a: pl.BlockSpec(a.shape, lambda i: (0,) * a.ndim)
    mod_spec = pl.BlockSpec((1, 1, d), lambda i: (i // tpb, 0, 0))
    tab_spec = pl.BlockSpec((tm, LANES), lambda i: (i % tpb, 0))
    tabt_spec = pl.BlockSpec((ROPE_HALF, tm), lambda i: (0, i % tpb))
    nq = N_HEADS * HEAD_DIM
    out_specs = [col(nq), col(KV_COLS),
                 pl.BlockSpec((1, N_KV_HEADS, tm, LANES), lambda i: (i // tpb, 0, i % tpb, 0)),
                 seq(KV_COLS), seq(KV_COLS)]
    out_shape = [jax.ShapeDtypeStruct((nq, m), bf16), jax.ShapeDtypeStruct((KV_COLS, m), bf16),
                 jax.ShapeDtypeStruct((bsz, N_KV_HEADS, t, LANES), bf16),
                 seq_shape(KV_COLS), seq_shape(KV_COLS)]
    if is_a:
        out_specs += [col(N_IDX_HEADS * IDX_DIM), col(IW_ROWS),
                      pl.BlockSpec((tm, IDX_DIM), lambda i: (i, 0)), seq(IDX_DIM)]
        out_shape += [jax.ShapeDtypeStruct((N_IDX_HEADS * IDX_DIM, m), bf16),
                      jax.ShapeDtypeStruct((IW_ROWS, m), f32),
                      jax.ShapeDtypeStruct((m, IDX_DIM), bf16), seq_shape(IDX_DIM)]
    else:
        out_specs += [seq(N_HEADS)]
        out_shape += [seq_shape(N_HEADS)]
    return pl.pallas_call(
        functools.partial(_in_proj_t_kernel, is_a=is_a),
        grid=(m // tm,),
        in_specs=[pl.BlockSpec((tm, d), lambda i: (i, 0)), const(g), mod_spec, mod_spec,
                  const(w), const(wt), tab_spec, tab_spec, tab_spec, tabt_spec, tabt_spec,
                  const(b_f)],
        out_specs=out_specs, out_shape=out_shape,
        compiler_params=_cparams("parallel"),
        name="in_proj_t",
    )(x, g, shift, scale, w, wt, *tabs, *tabs_t, b_f)


def _rope_tables_t(pos):
    inv_freq = ROPE_THETA ** (-jnp.arange(ROPE_HALF, dtype=f32) / ROPE_HALF)
    ang = inv_freq[:, None] * pos.astype(f32)[None, :]
    return jnp.cos(ang), jnp.sin(ang)


AUG_STRIDE = 4
AUG_ONES = GROUP * AUG_STRIDE


ATTN_TK = 256
SUM_ROWS = 2 * SUBLANES
ACC_ROWS = HEAD_DIM + SUM_ROWS


def _t_init(m_sc, acc_sc):
    m_sc[...] = jnp.full(m_sc.shape, NEG, f32)
    acc_sc[...] = jnp.zeros(acc_sc.shape, f32)


def _t_step(s, vt1, h, m_sc, acc_sc):
    m_old = m_sc[h]
    m_new = jnp.maximum(m_old, jnp.max(s, axis=0, keepdims=True))
    p = jnp.exp2((s - m_new).astype(bf16))
    acc_sc[h] = jnp.exp2(m_old - m_new) * acc_sc[h] + _dot(vt1, p)
    m_sc[h] = m_new


STALE_LIMIT = 64.0


def _t_step_stale(s, vt1, h, m_sc, acc_sc, rise_sc):
    m_old = m_sc[h]
    p = jnp.exp2((s - m_old).astype(bf16))
    s_max = jnp.max(s, axis=0, keepdims=True)
    m_new = jnp.maximum(m_old, s_max)
    acc_sc[h] = (acc_sc[h] + _dot(vt1, p)) * jnp.exp2(m_old - m_new)
    m_sc[h] = m_new
    rise_sc[h] = jnp.maximum(rise_sc[h], s_max - m_old)


def _t_tile(off, tk, kt_of, vt_ref, qa_sc, post, step):
    kts = [kt_of(g) for g in range(N_KV_HEADS)]
    ones = jnp.ones((SUM_ROWS, tk), bf16)
    vts = [jnp.concatenate([vt_ref[g * HEAD_DIM:(g + 1) * HEAD_DIM, pl.ds(off, tk)], ones], axis=0)
           for g in range(N_KV_HEADS)]
    ahead = 2
    pending = [_dot(kts[g], qa_sc[g]) for g in range(ahead)]
    for g in range(N_KV_HEADS):
        s = pending.pop(0)
        if g + ahead < N_KV_HEADS:
            pending.append(_dot(kts[g + ahead], qa_sc[g + ahead]))
        step(post(s), vts[g], g)


def _t_sweep(first, rest, m_sc, acc_sc, rise_sc):
    def exact(s, vt1, g):
        _t_step(s, vt1, g, m_sc, acc_sc)

    def stale(s, vt1, g):
        _t_step_stale(s, vt1, g, m_sc, acc_sc, rise_sc)

    rise_sc[...] = jnp.full(rise_sc.shape, NEG, f32)
    _t_init(m_sc, acc_sc)
    first(exact)
    rest(stale)

    @pl.when(jnp.max(rise_sc[...]) > STALE_LIMIT)
    def _():
        _t_init(m_sc, acc_sc)
        first(exact)
        rest(exact)


def _t_fill_q(qa_sc, h, tq, q, aug):
    g, jj = divmod(h, GROUP)
    qa_sc[g, :HEAD_DIM, jj * tq:(jj + 1) * tq] = q
    qa_sc[g, HEAD_DIM:, jj * tq:(jj + 1) * tq] = aug


def _t_finish(tq, acc_sc, ot_sc):
    for h in range(N_HEADS):
        g, jj = divmod(h, GROUP)
        cols = slice(jj * tq, (jj + 1) * tq)
        ot_sc[h * HEAD_DIM:(h + 1) * HEAD_DIM, :] = (
            acc_sc[g, :HEAD_DIM, cols] / acc_sc[g, HEAD_DIM:HEAD_DIM + 1, cols]).astype(bf16)


def _t_out(x_ref, gate_ref, wot_ref, ot_sc, o_ref):
    y_t = _dot(wot_ref[...], ot_sc[...])
    o_ref[...] = x_ref[...] + gate_ref[0] * y_t.T


def _fox_attn_kernel(x_ref, gate_ref, qt_ref, kpad_ref, vt_ref, cpad_ref, ct_ref, wot_ref,
                     o_ref, kaug_sc, m_sc, rise_sc, acc_sc, qa_sc, ot_sc, *, tq, tk, chunk):
    qi = pl.program_id(1)
    t = kaug_sc.shape[1]

    @pl.when(qi == 0)
    def _():
        r = lax.broadcasted_iota(i32, (LANES, LANES), 0)
        l = lax.broadcasted_iota(i32, (LANES, LANES), 1) - HEAD_DIM
        lane = lax.broadcasted_iota(i32, (1, LANES), 1) - HEAD_DIM
        ones = jnp.where((lane >= AUG_ONES) & (lane < AUG_ONES + 3), 1.0, 0.0)
        for g in range(N_KV_HEADS):
            perm = [((l >= 0) & (l < AUG_ONES) & ((l >> 2) == r - g * GROUP)
                     & ((l & 3) == p)).astype(bf16) for p in range(3)]

            def body(i, c, g=g, perm=perm):
                off = pl.multiple_of(i * chunk, chunk)
                pieces = _split3(cpad_ref[0, pl.ds(off, chunk), :] * -LOG2E)
                aug = sum(_dot(pc, pm) for pc, pm in zip(pieces, perm)) + ones
                kaug_sc[g, pl.ds(off, chunk), :] = (
                    kpad_ref[0, g, pl.ds(off, chunk), :].astype(f32) + aug).astype(bf16)
                return c

            lax.fori_loop(0, t // chunk, body, 0)

    ri = lax.broadcasted_iota(i32, (HEAD_DIM, tq), 0)
    wide = (tk, GROUP * tq)
    d_iota = (lax.broadcasted_iota(i32, wide, 0)
              - (lax.broadcasted_iota(i32, wide, 1) & (tq - 1)))
    for h in range(N_HEADS):
        jj = h % GROUP
        cq = _split3(ct_ref[0, h:h + 1, :] * LOG2E)
        aug = jnp.where((ri >= jj * AUG_STRIDE) & (ri < jj * AUG_STRIDE + 3), 1.0, 0.0)
        for p in range(3):
            aug = jnp.where(ri == AUG_ONES + p, cq[p].astype(f32), aug)
        _t_fill_q(qa_sc, h, tq, qt_ref[h * HEAD_DIM:(h + 1) * HEAD_DIM, :], aug.astype(bf16))
    def tile(j, post, step):
        off = pl.multiple_of(j * tk, tk)
        _t_tile(off, tk, lambda g: kaug_sc[g, pl.ds(off, tk), :], vt_ref, qa_sc, post, step)

    sub = tq // tk

    def diagonal(step):
        for dd in range(sub):
            tile(qi * sub + dd, lambda s, dd=dd: jnp.where(d_iota <= -dd * tk, s, NEG), step)

    def earlier(step):
        def body(i, c):
            tile(qi * sub - 1 - i, lambda s: s, step)
            return c
        lax.fori_loop(0, qi * sub, body, 0)

    _t_sweep(diagonal, earlier, m_sc, acc_sc, rise_sc)
    _t_finish(tq, acc_sc, ot_sc)
    _t_out(x_ref, gate_ref, wot_ref, ot_sc, o_ref)


def _attn_specs(bsz, t, d, tq):
    nq = t // tq
    return dict(
        x=pl.BlockSpec((tq, d), lambda b, q: (b * nq + q, 0)),
        gate=pl.BlockSpec((1, 1, d), lambda b, q: (b, 0, 0)),
        qt=pl.BlockSpec((N_HEADS * HEAD_DIM, tq), lambda b, q: (0, b * nq + q)),
        kpad=pl.BlockSpec((1, N_KV_HEADS, t, LANES), lambda b, q: (b, 0, 0, 0)),
        vt=pl.BlockSpec((KV_COLS, t), lambda b, q: (0, b)),
        wot=pl.BlockSpec((d, N_HEADS * HEAD_DIM), lambda b, q: (0, 0)),
        scratch=[pltpu.VMEM((N_KV_HEADS, 1, GROUP * tq), f32),
                 pltpu.VMEM((N_KV_HEADS, 1, GROUP * tq), f32),
                 pltpu.VMEM((N_KV_HEADS, ACC_ROWS, GROUP * tq), f32),
                 pltpu.VMEM((N_KV_HEADS, LANES, GROUP * tq), bf16),
                 pltpu.VMEM((N_HEADS * HEAD_DIM, tq), bf16)])


def _fox_attn(x, gate, qt, kpad, vt, c_pad, c_t, wot, tq):
    bsz, _, t, _ = kpad.shape
    m, d = x.shape
    sp = _attn_specs(bsz, t, d, tq)
    return pl.pallas_call(
        functools.partial(_fox_attn_kernel, tq=tq, tk=ATTN_TK, chunk=512),
        grid=(bsz, t // tq),
        in_specs=[sp["x"], sp["gate"], sp["qt"], sp["kpad"], sp["vt"],
                  pl.BlockSpec((1, t, LANES), lambda b, q: (b, 0, 0)),
                  pl.BlockSpec((1, N_HEADS, tq), lambda b, q: (b, 0, q)),
                  sp["wot"]],
        out_specs=sp["x"],
        out_shape=jax.ShapeDtypeStruct((m, d), f32),
        scratch_shapes=[pltpu.VMEM((N_KV_HEADS, t, LANES), bf16)] + sp["scratch"],
        compiler_params=_cparams("parallel", "arbitrary"),
        name="fox_attn",
    )(x, gate, qt, kpad, vt, c_pad, c_t, wot)


def _cumsum_t_kernel(lft_ref, ct_ref, carry_sc, *, tb):
    @pl.when(pl.program_id(1) == 0)
    def _():
        carry_sc[...] = jnp.zeros(carry_sc.shape, f32)

    tri = (lax.broadcasted_iota(i32, (tb, tb), 0)
           <= lax.broadcasted_iota(i32, (tb, tb), 1)).astype(bf16)
    lft = lft_ref[0]
    csum = sum(_dot(p, tri) for p in _split3(lft))
    ct_ref[0] = csum + carry_sc[...]
    carry_sc[...] = carry_sc[...] + jnp.sum(lft, axis=1, keepdims=True)


def _cumsum_t(lf_t, tb):
    b, nh, t = lf_t.shape
    spec = pl.BlockSpec((1, nh, tb), lambda bi, j: (bi, 0, j))
    return pl.pallas_call(
        functools.partial(_cumsum_t_kernel, tb=tb),
        grid=(b, t // tb),
        in_specs=[spec],
        out_specs=spec,
        out_shape=jax.ShapeDtypeStruct((b, nh, t), f32),
        scratch_shapes=[pltpu.VMEM((nh, 1), f32)],
        compiler_params=_cparams("parallel", "arbitrary"),
        name="cumsum_t",
    )(lf_t)


def _kth_threshold(count_ge, shape, n_sel):
    zero = jnp.zeros(shape, i32)
    ans = jnp.where(count_ge(zero) >= n_sel, zero, jnp.full(shape, INT_MIN, i32))

    def bit_body(i, ans):
        cand = ans | jnp.left_shift(jnp.int32(1), 30 - i)
        return jnp.where(count_ge(cand) >= n_sel, cand, ans)

    return lax.fori_loop(0, 31, bit_body, ans)


def _kth_threshold_early(count_ge, shape, n_sel, n_total):
    zero = jnp.zeros(shape, i32)
    c0 = count_ge(zero)
    ans = jnp.where(c0 >= n_sel, zero, jnp.full(shape, INT_MIN, i32))
    exact = jnp.where((c0 == n_sel) | ((c0 < n_sel) & (n_total == n_sel)), 1, 0)

    def cond(state):
        i, _, _, n_open = state
        return (i < 31) & (n_open > 0)

    def body(state):
        i, ans, exact, _ = state
        cand = ans | jnp.left_shift(jnp.int32(1), 30 - i)
        c = count_ge(cand)
        take = (exact == 0) & (c >= n_sel)
        ans = jnp.where(take, cand, ans)
        exact = jnp.where(take & (c == n_sel), 1, exact)
        return i + 1, ans, exact, jnp.sum(1 - exact)

    _, ans, exact, _ = lax.while_loop(cond, body, (jnp.int32(0), ans, exact, jnp.sum(1 - exact)))
    return ans, exact


def _tie_cut(count_tied_below, need, shape, n_bits):
    def body(i, p):
        cand = p | jnp.left_shift(jnp.int32(1), n_bits - 1 - i)
        return jnp.where(count_tied_below(cand) < need, cand, p)
    return lax.fori_loop(0, n_bits, body, jnp.zeros(shape, i32))


def _fold_lanes(ind, c):
    for cc in range(ind.shape[1] // LANES):
        c = c + ind[:, cc * LANES:(cc + 1) * LANES]
    return c


def _dsa_attn_kernel(x_ref, gate_ref, qt_ref, kpad_ref, vt_ref, iqt_ref, iwt_ref, ik_ref,
                     wot_ref, o_ref, key_sc, bias_sc, cut_sc, m_sc, rise_sc, acc_sc, qa_sc, ot_sc,
                     *, tq, n_sel, n_bits):
    qi = pl.program_id(1)
    n_k = qi + 1
    row = lax.broadcasted_iota(i32, (tq, tq), 0)
    d_iota = row - lax.broadcasted_iota(i32, (tq, tq), 1)

    def score_body(j, c):
        off = pl.multiple_of(j * tq, tq)
        ikt = ik_ref[0, pl.ds(off, tq), :]
        s = jnp.zeros((tq, tq), f32)
        for h in range(N_IDX_HEADS):
            dots = _dot(ikt, iqt_ref[h * IDX_DIM:(h + 1) * IDX_DIM, :])
            s = s + jnp.maximum(dots, 0.0) * iwt_ref[h:h + 1, :]
        key_sc[pl.ds(off, tq), :] = jnp.where(d_iota <= (qi - j) * tq, _sort_key(s),
                                              KEY_NEG_INF)
        return c

    lax.fori_loop(0, n_k, score_body, 0)

    def count(pred):
        def body(j, c):
            off = pl.multiple_of(j * tq, tq)
            ind = jnp.where(pred(key_sc[pl.ds(off, tq), :], j), 1, 0)
            for r in range(0, tq, SUBLANES):
                c = c + ind[r:r + SUBLANES]
            return c
        c = lax.fori_loop(0, n_k, body, jnp.zeros((SUBLANES, tq), i32))
        return jnp.sum(c, axis=0, keepdims=True)

    thr, exact = _kth_threshold_early(lambda cand: count(lambda kt, j: kt >= cand), (1, tq),
                                      n_sel, n_k * tq)
    real = thr > KEY_NEG_INF
    cut_sc[...] = jnp.where(real, INT_MAX, -1)

    @pl.when(jnp.max(jnp.where(real, 1 - exact, 0)) > 0)
    def _():
        need = n_sel - count(lambda kt, j: kt > thr)
        cut = _tie_cut(lambda cand: count(lambda kt, j: (kt == thr) & (row + j * tq < cand)),
                       need, (1, tq), n_bits)
        cut_sc[...] = jnp.where(real, jnp.where(exact == 1, INT_MAX, cut), -1)

    cut = cut_sc[...]
    floor = jnp.maximum(thr, KEY_NEG_INF)

    def bias_body(j, c):
        off = pl.multiple_of(j * tq, tq)
        kt = key_sc[pl.ds(off, tq), :]
        sel = (kt > floor) | ((kt == thr) & (row + j * tq <= cut))
        bias_sc[pl.ds(off, tq), :] = jnp.where(sel, 0.0, NEG)
        return c

    lax.fori_loop(0, n_k, bias_body, 0)

    for h in range(N_HEADS):
        _t_fill_q(qa_sc, h, tq, qt_ref[h * HEAD_DIM:(h + 1) * HEAD_DIM, :],
                  jnp.zeros((LANES - HEAD_DIM, tq), bf16))
    def tile(j, step):
        off = pl.multiple_of(j * tq, tq)

        def masked(s):
            return jnp.concatenate([s[:, jj * tq:(jj + 1) * tq] + bias_sc[pl.ds(off, tq), :]
                                    for jj in range(GROUP)], axis=1)

        _t_tile(off, tq, lambda g: kpad_ref[0, g, pl.ds(off, tq), :], vt_ref, qa_sc, masked, step)

    def oldest(step):
        tile(0, step)

    def later(step):
        def body(j, c):
            tile(j, step)
            return c
        lax.fori_loop(1, n_k, body, 0)

    _t_sweep(oldest, later, m_sc, acc_sc, rise_sc)
    _t_finish(tq, acc_sc, ot_sc)
    _t_out(x_ref, gate_ref, wot_ref, ot_sc, o_ref)


def _dsa_attn(x, gate, qt, kpad, vt, iqt, iwt, ik, wot, tq, n_sel):
    bsz, _, t, _ = kpad.shape
    m, d = x.shape
    nq = t // tq
    sp = _attn_specs(bsz, t, d, tq)
    return pl.pallas_call(
        functools.partial(_dsa_attn_kernel, tq=tq, n_sel=n_sel, n_bits=(t - 1).bit_length()),
        grid=(bsz, nq),
        in_specs=[sp["x"], sp["gate"], sp["qt"], sp["kpad"], sp["vt"],
                  pl.BlockSpec((N_IDX_HEADS * IDX_DIM, tq), lambda b, q: (0, b * nq + q)),
                  pl.BlockSpec((2 * SUBLANES, tq), lambda b, q: (0, b * nq + q)),
                  pl.BlockSpec((1, t, IDX_DIM), lambda b, q: (b, 0, 0)),
                  sp["wot"]],
        out_specs=sp["x"],
        out_shape=jax.ShapeDtypeStruct((m, d), f32),
        scratch_shapes=[pltpu.VMEM((t, tq), i32), pltpu.VMEM((t, tq), f32),
                        pltpu.VMEM((1, tq), i32)] + sp["scratch"],
        compiler_params=_cparams("parallel", "arbitrary"),
        name="dsa_attn",
    )(x, gate, qt, kpad, vt, iqt, iwt, ik, wot)


def _out_proj_kernel(x_ref, o_ref, w_ref, gate_ref, y_ref):
    y_ref[...] = x_ref[...] + gate_ref[0] * _dot(o_ref[...], w_ref[...])


def _out_proj(x, o, w, gate, tm, rows_per_mod):
    m, d = x.shape
    kdim = o.shape[1]
    r = gate.shape[1]
    return pl.pallas_call(
        _out_proj_kernel,
        grid=(m // tm,),
        in_specs=[pl.BlockSpec((tm, d), lambda i: (i, 0)),
                  pl.BlockSpec((tm, kdim), lambda i: (i, 0)),
                  pl.BlockSpec((kdim, d), lambda i: (0, 0)),
                  pl.BlockSpec((1, r, d), lambda i: (i * tm // rows_per_mod, 0, 0))],
        out_specs=pl.BlockSpec((tm, d), lambda i: (i, 0)),
        out_shape=jax.ShapeDtypeStruct((m, d), f32),
        compiler_params=_cparams("parallel"),
        name="out_proj",
    )(x, o, w, gate)


FFN_CHUNK = 256
FFN_HALO = 2 * SUBLANES


def _ffn_kernel(x_ref, xh_ref, g_ref, shift_ref, scale_ref, gate_ref, wa_ref, wb_ref,
                cw_ref, cb_ref, wd_ref, gfin_ref, o_ref, tail_ref, acc_sc, hext_sc,
                *, tm, tiles_per_seq, final):
    i = pl.program_id(0)
    x = x_ref[...]
    keep = jnp.where(i % tiles_per_seq == 0, 0.0, 1.0)
    hext_sc[:FFN_HALO, :] = (_norm_mod(xh_ref[...], g_ref[...], shift_ref[0], scale_ref[0])
                             * keep).astype(bf16)
    hext_sc[FFN_HALO:, :] = _norm_mod(x, g_ref[...], shift_ref[0], scale_ref[0]).astype(bf16)
    for c in range(D_FF // FFN_CHUNK):
        sl = slice(c * FFN_CHUNK, (c + 1) * FFN_CHUNK)
        a_ext = _dot(hext_sc[...], wa_ref[:, sl])
        b = _dot(hext_sc[FFN_HALO:, :], wb_ref[:, sl])
        a = a_ext[FFN_HALO:]
        p1 = pltpu.roll(a_ext, 1, 0)[FFN_HALO:]
        p2 = pltpu.roll(a_ext, 2, 0)[FFN_HALO:]
        conv = cb_ref[:, sl] + p2 * cw_ref[0:1, sl] + p1 * cw_ref[1:2, sl] + a * cw_ref[2:3, sl]
        y = _dot((_silu(conv) * b).astype(bf16), wd_ref[sl, :])
        if c == 0:
            acc_sc[...] = y
        else:
            acc_sc[...] += y
        tail_ref[0, :, sl] = a_ext[FFN_HALO + tm - SUBLANES:]
    y = x + gate_ref[0] * acc_sc[...]
    if final:
        y = y * lax.rsqrt(jnp.mean(y * y, axis=-1, keepdims=True) + EPS) * gfin_ref[...]
    o_ref[...] = y


def _ffn(x, g, shift, scale, gate, wa, wb, cw, cb, wd, g_final, final, tm, rows_per_seq):
    m, d = x.shape
    n_tiles = m // tm
    hb = tm // FFN_HALO
    mod_spec = pl.BlockSpec((1, 1, d), lambda i: (i * tm // rows_per_seq, 0, 0))
    const = lambda shape: pl.BlockSpec(shape, lambda i: (0,) * len(shape))
    return pl.pallas_call(
        functools.partial(_ffn_kernel, tm=tm, tiles_per_seq=rows_per_seq // tm, final=final),
        grid=(n_tiles,),
        in_specs=[pl.BlockSpec((tm, d), lambda i: (i, 0)),
                  pl.BlockSpec((FFN_HALO, d), lambda i: (jnp.maximum(i * hb - 1, 0), 0)),
                  const((1, d)), mod_spec, mod_spec, mod_spec,
                  const((d, D_FF)), const((d, D_FF)), const((SUBLANES, D_FF)),
                  const((1, D_FF)), const((D_FF, d)), const((1, d))],
        out_specs=[pl.BlockSpec((tm, d), lambda i: (i, 0)),
                   pl.BlockSpec((1, SUBLANES, D_FF), lambda i: (i, 0, 0))],
        out_shape=[jax.ShapeDtypeStruct((m, d), f32),
                   jax.ShapeDtypeStruct((n_tiles, SUBLANES, D_FF), f32)],
        scratch_shapes=[pltpu.VMEM((tm, d), f32), pltpu.VMEM((FFN_HALO + tm, d), bf16)],
        compiler_params=_cparams("parallel"),
        name="ffn",
    )(x, x, g, shift, scale, gate, wa, wb, cw, cb, wd, g_final)


def _ffn_sample_kernel(x_ref, g_ref, shift_ref, scale_ref, gate_ref, p2_ref, p1_ref, wa_ref,
                       wb_ref, cw_ref, cb_ref, wd_ref, o_ref, a_ref, acc_sc):
    j = pl.program_id(0)

    @pl.when(j == 0)
    def _():
        acc_sc[...] = jnp.zeros(acc_sc.shape, f32)

    x = x_ref[...]
    h = _norm_mod(x, g_ref[...], shift_ref[0], scale_ref[0]).astype(bf16)
    a = _dot(h, wa_ref[...])
    b = _dot(h, wb_ref[...])
    conv = (cb_ref[...] + p2_ref[...] * cw_ref[0:1, :] + p1_ref[...] * cw_ref[1:2, :]
            + a * cw_ref[2:3, :])
    acc_sc[...] += _dot((_silu(conv) * b).astype(bf16), wd_ref[...])
    a_ref[...] = a

    @pl.when(j == pl.num_programs(0) - 1)
    def _():
        o_ref[...] = x + gate_ref[0] * acc_sc[...]


def _ffn_sample(x, g, shift, scale, gate, p2, p1, wa, wb, cw, cb, wd, tn):
    m, d = x.shape
    const = lambda shape: pl.BlockSpec(shape, lambda j: (0,) * len(shape))
    return pl.pallas_call(
        _ffn_sample_kernel,
        grid=(D_FF // tn,),
        in_specs=[const((m, d)), const((1, d)), const((1, m, d)), const((1, m, d)),
                  const((1, m, d)),
                  pl.BlockSpec((m, tn), lambda j: (0, j)), pl.BlockSpec((m, tn), lambda j: (0, j)),
                  pl.BlockSpec((d, tn), lambda j: (0, j)), pl.BlockSpec((d, tn), lambda j: (0, j)),
                  pl.BlockSpec((SUBLANES, tn), lambda j: (0, j)),
                  pl.BlockSpec((1, tn), lambda j: (0, j)),
                  pl.BlockSpec((tn, d), lambda j: (j, 0))],
        out_specs=[const((m, d)), pl.BlockSpec((m, tn), lambda j: (0, j))],
        out_shape=[jax.ShapeDtypeStruct((m, d), f32), jax.ShapeDtypeStruct((m, D_FF), f32)],
        scratch_shapes=[pltpu.VMEM((m, d), f32)],
        compiler_params=_cparams("arbitrary"),
        name="ffn_sample",
    )(x, g, shift, scale, gate, p2, p1, wa, wb, cw, cb, wd)


def _final_norm_kernel(x_ref, g_ref, o_ref):
    x = x_ref[...]
    o_ref[...] = x * lax.rsqrt(jnp.mean(x * x, axis=-1, keepdims=True) + EPS) * g_ref[...]


def _final_norm(x, g, tm):
    m, d = x.shape
    return pl.pallas_call(
        _final_norm_kernel,
        grid=(m // tm,),
        in_specs=[pl.BlockSpec((tm, d), lambda i: (i, 0)), pl.BlockSpec((1, d), lambda i: (0, 0))],
        out_specs=pl.BlockSpec((tm, d), lambda i: (i, 0)),
        out_shape=jax.ShapeDtypeStruct((m, d), f32),
        compiler_params=_cparams("parallel"),
        name="final_norm",
    )(x, g)


def _seq_page_copies(pt_ref, hbm, layer, seq, buf, slot, sem, n_pages):
    return [pltpu.make_async_copy(hbm.at[layer, pt_ref[seq, p]],
                                  buf.at[slot, :, pl.ds(p * PAGE_SIZE, PAGE_SIZE)], sem.at[slot])
            for p in range(n_pages)]


def _gather_seq(pt_ref, hbm, layer, buf, sem, n_pages):
    b = pl.program_id(0)
    slot = b % 2

    @pl.when(b == 0)
    def _():
        for cp in _seq_page_copies(pt_ref, hbm, layer, 0, buf, 0, sem, n_pages):
            cp.start()

    @pl.when(b + 1 < pl.num_programs(0))
    def _():
        for cp in _seq_page_copies(pt_ref, hbm, layer, b + 1, buf, 1 - slot, sem, n_pages):
            cp.start()

    for cp in _seq_page_copies(pt_ref, hbm, layer, b, buf, slot, sem, n_pages):
        cp.wait()
    return slot


def _sample_score_kernel(pt_ref, iq_ref, iw_ref, kidx_hbm, s_ref, buf, sem, *, layer, n_pages):
    slot = _gather_seq(pt_ref, kidx_hbm, layer, buf, sem, n_pages)
    d = _dot(iq_ref[0], buf[slot].astype(bf16))
    s_ref[0] = jnp.sum(jnp.maximum(d, 0.0) * iw_ref[0], axis=0, keepdims=True)


def _sample_scores(page_table, iq, iw, cache_kidx_t, layer):
    s, n_pages = page_table.shape
    length = n_pages * PAGE_SIZE
    grid_spec = pltpu.PrefetchScalarGridSpec(
        num_scalar_prefetch=1, grid=(s,),
        in_specs=[pl.BlockSpec((1, N_IDX_HEADS, IDX_DIM), lambda b, pt: (b, 0, 0)),
                  pl.BlockSpec((1, N_IDX_HEADS, 1), lambda b, pt: (b, 0, 0)),
                  pl.BlockSpec(memory_space=pl.ANY)],
        out_specs=pl.BlockSpec((1, 1, length), lambda b, pt: (b, 0, 0)),
        scratch_shapes=[pltpu.VMEM((2, IDX_DIM, length), f32), pltpu.SemaphoreType.DMA((2,))])
    out = pl.pallas_call(
        functools.partial(_sample_score_kernel, layer=layer, n_pages=n_pages),
        grid_spec=grid_spec,
        out_shape=jax.ShapeDtypeStruct((s, 1, length), f32),
        compiler_params=_cparams("arbitrary"),
        name="sample_scores",
    )(page_table, iq, iw, cache_kidx_t)
    return out.reshape(s, length)


SEL_TILE = 1024


def _sample_select_kernel(s_ref, iq_ref, ikt_ref, iw_ref, bias_ref, bnew_ref, key_sc,
                          *, n_sel, n_bits):
    rows, length = s_ref.shape
    n_tiles = length // SEL_TILE
    prod = iq_ref[...].astype(f32) * ikt_ref[...].astype(f32)
    grp = (lax.broadcasted_iota(i32, (prod.shape[1], LANES), 0) // IDX_DIM
           == lax.broadcasted_iota(i32, (prod.shape[1], LANES), 1)).astype(bf16)
    d_new = sum(_dot(p, grp) for p in _split3(prod))
    s_new = jnp.sum(jnp.maximum(d_new, 0.0) * iw_ref[...], axis=1, keepdims=True)
    key_new = _sort_key(s_new)
    key_sc[...] = _sort_key(s_ref[...])
    col = lax.broadcasted_iota(i32, (rows, SEL_TILE), 1)

    def count(pred, pred_new):
        def body(j, c):
            off = pl.multiple_of(j * SEL_TILE, SEL_TILE)
            return _fold_lanes(jnp.where(pred(key_sc[:, pl.ds(off, SEL_TILE)], j), 1, 0), c)
        c = lax.fori_loop(0, n_tiles, body, jnp.zeros((rows, LANES), i32))
        return jnp.sum(c, axis=1, keepdims=True) + jnp.where(pred_new, 1, 0)

    thr = _kth_threshold(lambda cand: count(lambda kt, j: kt >= cand, key_new >= cand),
                         (rows, 1), n_sel)
    need = n_sel - count(lambda kt, j: kt > thr, key_new > thr)
    cut = _tie_cut(lambda cand: count(lambda kt, j: (kt == thr) & (col + j * SEL_TILE < cand),
                                      (key_new == thr) & (length < cand)),
                   need, (rows, 1), n_bits)

    def bias_body(j, c):
        off = pl.multiple_of(j * SEL_TILE, SEL_TILE)
        kt = key_sc[:, pl.ds(off, SEL_TILE)]
        sel = (kt > thr) | ((kt == thr) & (col + j * SEL_TILE <= cut))
        bias_ref[:, pl.ds(off, SEL_TILE)] = jnp.where(sel, 0.0, NEG)
        return c

    lax.fori_loop(0, n_tiles, bias_body, 0)
    sel_new = (key_new > thr) | ((key_new == thr) & (length <= cut))
    bnew_ref[...] = jnp.broadcast_to(jnp.where(sel_new, 0.0, NEG), bnew_ref.shape)


def _sample_select(scores, iq, ik_tiled, iw, n_sel):
    rows, length = scores.shape
    return pl.pallas_call(
        functools.partial(_sample_select_kernel, n_sel=n_sel, n_bits=length.bit_length()),
        out_shape=[jax.ShapeDtypeStruct((rows, length), f32),
                   jax.ShapeDtypeStruct((rows, LANES), f32)],
        scratch_shapes=[pltpu.VMEM((rows, length), i32)],
        compiler_params=pltpu.CompilerParams(vmem_limit_bytes=VMEM_LIMIT),
        name="sample_select",
    )(scores, iq, ik_tiled, iw)


SUFFIX_CHUNK = 256


def _suffix_bias_kernel(pt_ref, lfnew_ref, lf_hbm, bias_ref, buf, sem, *, layer, n_pages):
    slot = _gather_seq(pt_ref, lf_hbm, layer, buf, sem, n_pages)
    w = SUFFIX_CHUNK
    tri = (lax.broadcasted_iota(i32, (w, w), 0)
           > lax.broadcasted_iota(i32, (w, w), 1)).astype(bf16)
    carry = lfnew_ref[0]
    for c in reversed(range(n_pages * PAGE_SIZE // w)):
        x = buf[slot, :, c * w:(c + 1) * w]
        bias_ref[0, :, c * w:(c + 1) * w] = sum(_dot(p, tri) for p in _split3(x)) + carry
        carry = carry + jnp.sum(x, axis=1, keepdims=True)


def _suffix_bias(page_table, lf_new, cache_logf_t, layer):
    s, n_pages = page_table.shape
    length = n_pages * PAGE_SIZE
    grid_spec = pltpu.PrefetchScalarGridSpec(
        num_scalar_prefetch=1, grid=(s,),
        in_specs=[pl.BlockSpec((1, N_HEADS, 1), lambda b, pt: (b, 0, 0)),
                  pl.BlockSpec(memory_space=pl.ANY)],
        out_specs=pl.BlockSpec((1, N_HEADS, length), lambda b, pt: (b, 0, 0)),
        scratch_shapes=[pltpu.VMEM((2, N_HEADS, length), f32), pltpu.SemaphoreType.DMA((2,))])
    return pl.pallas_call(
        functools.partial(_suffix_bias_kernel, layer=layer, n_pages=n_pages),
        grid_spec=grid_spec,
        out_shape=jax.ShapeDtypeStruct((s, N_HEADS, length), f32),
        compiler_params=_cparams("arbitrary"),
        name="suffix_bias",
    )(page_table, lf_new, cache_logf_t)


DEC_PAGES = 8
DEC_SLOTS = 3


def _kv_chunk_copies(pt_ref, k_hbm, v_hbm, layer, seq, chunk, kbuf, vbuf, slot, sem):
    cps = []
    for i in range(DEC_PAGES):
        page = pt_ref[seq, chunk * DEC_PAGES + i]
        cps.append(pltpu.make_async_copy(k_hbm.at[layer, page], kbuf.at[slot, i], sem.at[0, slot]))
        cps.append(pltpu.make_async_copy(v_hbm.at[layer, page], vbuf.at[slot, i], sem.at[1, slot]))
    return cps


def _decode_attn_kernel(pt_ref, q_ref, knew_ref, vnew_ref, bnew_ref, bias_ref, k_hbm, v_hbm,
                        o_ref, kbuf, vbuf, sem, *, layer, n_chunks):
    b = pl.program_id(0)
    q = q_ref[0]
    width = DEC_PAGES * PAGE_SIZE
    copies = functools.partial(_kv_chunk_copies, pt_ref, k_hbm, v_hbm, layer)

    def start(seq, chunk, slot):
        for cp in copies(seq, chunk, kbuf, vbuf, slot, sem):
            cp.start()

    ahead = DEC_SLOTS - 1
    base = b * n_chunks

    @pl.when(b == 0)
    def _():
        for c in range(ahead):
            start(0, c, c)

    kn = knew_ref[0].astype(bf16).astype(f32)
    m = jnp.sum(q.astype(f32) * kn, axis=1, keepdims=True) + bnew_ref[0][:, 0:1]
    l = jnp.ones_like(m)
    acc = jnp.broadcast_to(vnew_ref[0].astype(bf16).astype(f32), (N_HEADS, KV_COLS))

    for c in range(n_chunks):
        slot = lax.rem(base + c, DEC_SLOTS)
        nxt = c + ahead
        nslot = lax.rem(base + nxt, DEC_SLOTS)
        if nxt < n_chunks:
            start(b, nxt, nslot)
        else:
            @pl.when(b + 1 < pl.num_programs(0))
            def _():
                start(b + 1, nxt - n_chunks, nslot)
        for cp in copies(b, c, kbuf, vbuf, slot, sem):
            cp.wait()
        s = jnp.concatenate([_dot(q, kbuf[slot, i].astype(bf16)) for i in range(DEC_PAGES)],
                            axis=1) + bias_ref[0, :, c * width:(c + 1) * width]
        m_new = jnp.maximum(m, jnp.max(s, axis=1, keepdims=True))
        p = jnp.exp(s - m_new)
        alpha = jnp.exp(m - m_new)
        l = alpha * l + jnp.sum(p, axis=1, keepdims=True)
        pb = p.astype(bf16)
        pv = sum(_dot_nt(pb[:, i * PAGE_SIZE:(i + 1) * PAGE_SIZE], vbuf[slot, i].astype(bf16))
                 for i in range(DEC_PAGES))
        acc = alpha * acc + pv
        m = m_new
    o_ref[0] = acc / l


def _decode_attn(page_table, qmat, k_new, v_new, bias_new, bias, cache_kt, cache_vt, layer):
    s, n_pages = page_table.shape
    hb = bias.shape[1]
    n_chunks = n_pages // DEC_PAGES
    assert n_chunks * DEC_PAGES == n_pages and n_chunks >= DEC_SLOTS - 1
    row = lambda shape: pl.BlockSpec(shape, lambda b, pt: (b, 0, 0))
    hbm = pl.BlockSpec(memory_space=pl.ANY)
    width = DEC_PAGES * PAGE_SIZE
    grid_spec = pltpu.PrefetchScalarGridSpec(
        num_scalar_prefetch=1, grid=(s,),
        in_specs=[row((1, N_HEADS, KV_COLS)), row((1, 1, KV_COLS)), row((1, 1, KV_COLS)),
                  row((1, hb, LANES)), row((1, hb, n_pages * PAGE_SIZE)), hbm, hbm],
        out_specs=row((1, N_HEADS, KV_COLS)),
        scratch_shapes=[pltpu.VMEM((DEC_SLOTS, DEC_PAGES, KV_COLS, PAGE_SIZE), f32),
                        pltpu.VMEM((DEC_SLOTS, DEC_PAGES, KV_COLS, PAGE_SIZE), f32),
                        pltpu.SemaphoreType.DMA((2, DEC_SLOTS))])
    return pl.pallas_call(
        functools.partial(_decode_attn_kernel, layer=layer, n_chunks=n_chunks),
        grid_spec=grid_spec,
        out_shape=jax.ShapeDtypeStruct((s, N_HEADS, KV_COLS), f32),
        compiler_params=_cparams("arbitrary"),
        name="decode_attn",
    )(page_table, qmat, k_new, v_new, bias_new, bias, cache_kt, cache_vt)


def _pad_cols(w, n):
    return jnp.pad(w, ((0, 0), (0, n)))


def _decode_q(q_bf):
    s = q_bf.shape[0]
    onehot = (jnp.arange(N_HEADS)[:, None] // GROUP == jnp.arange(N_KV_HEADS)[None, :])
    q = q_bf.reshape(s, N_HEADS, 1, HEAD_DIM) * onehot[None, :, :, None].astype(q_bf.dtype)
    return q.reshape(s, N_HEADS, KV_COLS)


def _decode_o(o_full):
    s = o_full.shape[0]
    o = o_full.reshape(s, N_KV_HEADS, GROUP, N_KV_HEADS, HEAD_DIM)
    return jnp.stack([o[:, g, :, g, :] for g in range(N_KV_HEADS)], axis=1).reshape(s, -1)


def kernel(x_prompt, x_sample, cache_k_a, cache_v_a, cache_kidx_a, cache_k_b, cache_v_b,
           cache_logf_b, state_conv, page_table, c_prompt, c_sample, w_ada, b_ada, g_mix,
           g_ffn, w_in_a, w_o_a, w_in_b, b_f, w_o_b, w_up, conv_w, conv_b, w_down, g_final):
    bsz, t, d = x_prompt.shape
    s = x_sample.shape[0]
    depth = w_ada.shape[0]
    m = bsz * t
    past_len = page_table.shape[1] * PAGE_SIZE
    n_pool = cache_k_a.shape[1]
    tm = 512
    tq = 256

    pad_rows = (-(bsz + s)) % SUBLANES
    c_all = jnp.concatenate([c_prompt, c_sample, jnp.zeros((pad_rows, d), f32)], axis=0)
    mods = _adaln(c_all, w_ada, b_ada)
    mods_p = mods[:, :bsz].reshape(depth, bsz, 6, d)
    mods_s = mods[:, bsz:bsz + s].reshape(depth, s, 6, d)

    tabs_p = _rope_tables(jnp.arange(t))
    tabs_pt = _rope_tables_t(jnp.arange(t))
    tabs_s = _rope_tables(jnp.full((s,), past_len))
    zero_bf = jnp.zeros((1, LANES), f32)
    zero_col = jnp.zeros((N_HEADS, 1), f32)

    def rows_t(a):
        return a.reshape(bsz, N_KV_HEADS, HEAD_DIM, t).transpose(0, 3, 1, 2)

    xp = x_prompt.reshape(m, d)
    xs = x_sample.reshape(s, d)
    kv_shape_p = (bsz, t, N_KV_HEADS, HEAD_DIM)
    kv_shape_s = (s, 1, N_KV_HEADS, HEAD_DIM)
    ka_p, va_p, ia_p, kb_p, vb_p, lb_p, cv_p = [], [], [], [], [], [], []
    ka_s, va_s, ia_s, kb_s, vb_s, lb_s, cv_s = [], [], [], [], [], [], []

    def pages_t(cache):
        ct = jnp.moveaxis(cache, 2, -1)
        return ct.reshape(ct.shape[:2] + (-1, PAGE_SIZE))

    cache_k_a, cache_v_a, cache_kidx_a = pages_t(cache_k_a), pages_t(cache_v_a), pages_t(cache_kidx_a)
    cache_k_b, cache_v_b, cache_logf_b = pages_t(cache_k_b), pages_t(cache_v_b), pages_t(cache_logf_b)

    for i in range(depth):
        j = i // 2
        mp = [mods_p[i, :, c][:, None, :] for c in range(6)]
        ms = [mods_s[i, :, c][None] for c in range(6)]
        g_m = g_mix[i][None]
        if i % 2 == 0:
            w = w_in_a[j]
            n_qkvi = QKV_COLS + N_IDX_HEADS * IDX_DIM + IDX_DIM
            w_in = jnp.concatenate([_pad_cols(w[:, :n_qkvi], LANES - IDX_DIM),
                                    _pad_cols(w[:, n_qkvi:], LANES - N_IDX_HEADS)],
                                   axis=1).astype(bf16)
            w_o = w_o_a[j].astype(bf16)
            nq = N_HEADS * HEAD_DIM
            n_iq = QKV_COLS + N_IDX_HEADS * IDX_DIM
            wk = w[:, nq:nq + KV_COLS]
            w_row = jnp.concatenate([wk, _pad_cols(w[:, n_iq:n_qkvi], LANES - IDX_DIM)],
                                    axis=1).astype(bf16)
            w_t = jnp.concatenate([w[:, :nq], w[:, nq + KV_COLS:QKV_COLS], wk, w[:, QKV_COLS:n_qkvi],
                                   _pad_cols(w[:, n_qkvi:], IW_ROWS - N_IDX_HEADS)],
                                  axis=1).T.astype(bf16)
            qt, vt, kpad, kt32, vt32, iqt, iwt, ikb, ikt32 = _in_proj_t(
                xp, g_m, mp[0], mp[1], w_row, w_t, tabs_p, tabs_pt, zero_col, True, tm, t)
            xp = _dsa_attn(xp, mp[2], qt, kpad, vt, iqt, iwt, ikb.reshape(bsz, t, IDX_DIM),
                           w_o.T, tq, min(TOPK_MAX, t // 4))
            ka_p.append(rows_t(kt32))
            va_p.append(rows_t(vt32))
            ia_p.append(ikt32.transpose(0, 2, 1))
            q, k32, kb, v32, vb, iq, ik32, ikb, iw = _in_proj(
                xs, g_m, ms[0], ms[1], w_in, tabs_s, zero_bf, SEGS_A, OUTS_A, s, s)
            scores = _sample_scores(page_table, iq.reshape(s, N_IDX_HEADS, IDX_DIM),
                                    iw[:, :N_IDX_HEADS].reshape(s, N_IDX_HEADS, 1),
                                    cache_kidx_a, j)
            bias, bias_new = _sample_select(scores, iq, jnp.tile(ikb[:, :IDX_DIM], (1, N_IDX_HEADS)),
                                            iw, min(TOPK_MAX, (past_len + 1) // 4))
            o_full = _decode_attn(page_table, _decode_q(q), k32[:, None], v32[:, None],
                                  bias_new[:, None], bias[:, None], cache_k_a, cache_v_a, j)
            xs = _out_proj(xs, _decode_o(o_full).astype(bf16), w_o, ms[2], s, s)
            ka_s.append(k32.reshape(kv_shape_s))
            va_s.append(v32.reshape(kv_shape_s))
            ia_s.append(ik32[:, :IDX_DIM].reshape(s, 1, IDX_DIM))
        else:
            w_in = _pad_cols(w_in_b[j], LANES - N_HEADS).astype(bf16)
            w_o = w_o_b[j].astype(bf16)
            bf_pad = _pad_cols(b_f[j][None], LANES - N_HEADS)
            nq = N_HEADS * HEAD_DIM
            wk = w_in[:, nq:nq + KV_COLS]
            w_t = jnp.concatenate([w_in[:, :nq], w_in[:, nq + KV_COLS:QKV_COLS], wk,
                                   w_in[:, QKV_COLS:QKV_COLS + N_HEADS]], axis=1).T
            qt, vt, kpad, kt32, vt32, lft = _in_proj_t(
                xp, g_m, mp[0], mp[1], wk, w_t, tabs_p, tabs_pt, b_f[j][:, None], False, tm, t)
            c_t = _cumsum_t(lft, tq)
            c_pad = jnp.pad(c_t.transpose(0, 2, 1), ((0, 0), (0, 0), (0, LANES - N_HEADS)))
            xp = _fox_attn(xp, mp[2], qt, kpad, vt, c_pad, c_t, w_o.T, tq)
            kb_p.append(rows_t(kt32))
            vb_p.append(rows_t(vt32))
            lb_p.append(lft.transpose(0, 2, 1))
            q, k32, kb, v32, vb, lf = _in_proj(
                xs, g_m, ms[0], ms[1], w_in, tabs_s, bf_pad, SEGS_B, OUTS_B, s, s)
            bias = _suffix_bias(page_table, lf[:, :N_HEADS, None], cache_logf_b, j)
            o_full = _decode_attn(page_table, _decode_q(q), k32[:, None], v32[:, None],
                                  jnp.zeros((s, N_HEADS, LANES), f32), bias,
                                  cache_k_b, cache_v_b, j)
            xs = _out_proj(xs, _decode_o(o_full).astype(bf16), w_o, ms[2], s, s)
            kb_s.append(k32.reshape(kv_shape_s))
            vb_s.append(v32.reshape(kv_shape_s))
            lb_s.append(lf[:, :N_HEADS].reshape(s, 1, N_HEADS))

        wa = w_up[i][:, :D_FF].astype(bf16)
        wb = w_up[i][:, D_FF:].astype(bf16)
        wd = w_down[i].astype(bf16)
        cw = jnp.pad(conv_w[i], ((0, SUBLANES - CONV_W), (0, 0)))
        cb = conv_b[i][None]
        g_f = g_ffn[i][None]
        xp, tail = _ffn(xp, g_f, mp[3], mp[4], mp[5], wa, wb, cw, cb, wd, g_final[None],
                        i == depth - 1, tm, t)
        cv_p.append(tail.reshape(bsz, t // tm, SUBLANES, D_FF)[:, -1, SUBLANES - (CONV_W - 1):])
        xs, a_s = _ffn_sample(xs, g_f, ms[3], ms[4], ms[5], state_conv[i][:, 0],
                              state_conv[i][:, 1], wa, wb, cw, cb, wd, D_FF // 2)
        cv_s.append(jnp.stack([state_conv[i][:, 1], a_s], axis=1))

    y_prompt = xp.reshape(bsz, t, d)
    y_sample = _final_norm(xs, g_final[None], s).reshape(s, 1, d)
    return (y_prompt, y_sample,
            jnp.stack(ka_p), jnp.stack(va_p), jnp.stack(ia_p),
            jnp.stack(kb_p), jnp.stack(vb_p), jnp.stack(lb_p), jnp.stack(cv_p),
            jnp.stack(ka_s), jnp.stack(va_s), jnp.stack(ia_s),
            jnp.stack(kb_s), jnp.stack(vb_s), jnp.stack(lb_s), jnp.stack(cv_s))
```

```python
import functools

import jax
import jax.numpy as jnp
from jax import lax
from jax.experimental import pallas as pl
from jax.experimental.pallas import tpu as pltpu

f32 = jnp.float32
bf16 = jnp.bfloat16
i32 = jnp.int32

D_MODEL = 1024
N_HEADS = 16
HEAD_DIM = 64
N_KV_HEADS = 4
GROUP = N_HEADS // N_KV_HEADS
KV_COLS = N_KV_HEADS * HEAD_DIM
QKV_COLS = (N_HEADS + 2 * N_KV_HEADS) * HEAD_DIM
ROPE_DIMS = HEAD_DIM // 4
ROPE_HALF = ROPE_DIMS // 2
ROPE_THETA = 500000.0
N_IDX_HEADS = 8
IDX_DIM = 64
TOPK_MAX = 256
D_FF = 2816
CONV_W = 3
EPS = 1e-6
PAGE_SIZE = 128
Q_SCALE = HEAD_DIM ** -0.5
LOG2E = 1.4426950408889634

LANES = 128
SUBLANES = 8
VMEM_LIMIT = 56 * 1024 * 1024
NEG = -1e30
INT_MIN = -2 ** 31
INT_MAX = 2 ** 31 - 1
KEY_NEG_INF = (0xFF800000 ^ 0x7FFFFFFF) - 2 ** 32


def _cparams(*sem):
    return pltpu.CompilerParams(dimension_semantics=sem, vmem_limit_bytes=VMEM_LIMIT)


def _dot(a, b):
    return jnp.dot(a, b, preferred_element_type=f32)


def _dot_nt(a, b):
    return lax.dot_general(a, b, (((1,), (1,)), ((), ())), preferred_element_type=f32)


def _split3(x):
    hi = x.astype(bf16)
    r1 = x - hi.astype(f32)
    mid = r1.astype(bf16)
    lo = (r1 - mid.astype(f32)).astype(bf16)
    return hi, mid, lo


def _norm_mod(x, g, shift, scale):
    xn = x * lax.rsqrt(jnp.mean(x * x, axis=-1, keepdims=True) + EPS)
    return (xn * g) * (1.0 + scale) + shift


def _silu(x):
    return x * jax.nn.sigmoid(x)


def _sort_key(s):
    bits = pltpu.bitcast(s, i32)
    return bits ^ ((bits >> 31) & 0x7FFFFFFF)


def _adaln_kernel(c_ref, w_ref, b_ref, o_ref):
    a = _silu(c_ref[...]).astype(bf16)
    o_ref[0] = _dot(a, w_ref[0].astype(bf16)) + b_ref[0]


def _adaln(c_all, w_ada, b_ada):
    depth, d, n = w_ada.shape
    rows = c_all.shape[0]
    tn = 1536
    return pl.pallas_call(
        _adaln_kernel,
        grid=(depth, n // tn),
        in_specs=[pl.BlockSpec((rows, d), lambda l, j: (0, 0)),
                  pl.BlockSpec((1, d, tn), lambda l, j: (l, 0, j)),
                  pl.BlockSpec((1, 1, tn), lambda l, j: (l, 0, j))],
        out_specs=pl.BlockSpec((1, rows, tn), lambda l, j: (l, 0, j)),
        out_shape=jax.ShapeDtypeStruct((depth, rows, n), f32),
        compiler_params=_cparams("parallel", "parallel"),
        name="adaln",
    )(c_all, w_ada, b_ada.reshape(depth, 1, n))


SEGS_A = ((0, 1024, "rope", ((0, Q_SCALE),)),
          (1024, 256, "rope", ((1, 1.0), (2, 1.0))),
          (1280, 256, None, ((3, 1.0), (4, 1.0))),
          (1536, 512, "rope", ((5, 1.0),)),
          (2048, 128, "rope", ((6, 1.0), (7, 1.0))),
          (2176, 128, None, ((8, 1.0),)))
OUTS_A = ((1024, bf16), (256, f32), (256, bf16), (256, f32), (256, bf16),
          (512, bf16), (128, f32), (128, bf16), (128, f32))
SEGS_B = ((0, 1024, None, ((0, Q_SCALE),)),
          (1024, 256, None, ((1, 1.0), (2, 1.0))),
          (1280, 256, None, ((3, 1.0), (4, 1.0))),
          (1536, 128, "logf", ((5, 1.0),)))
OUTS_B = ((1024, bf16), (256, f32), (256, bf16), (256, f32), (256, bf16), (128, f32))


def _in_proj_kernel(x_ref, g_ref, shift_ref, scale_ref, w_ref, cos_ref, sa_ref, sb_ref,
                    bf_ref, *out_refs, segs):
    h = _norm_mod(x_ref[...], g_ref[...], shift_ref[0], scale_ref[0]).astype(bf16)
    for c0, width, epi, outs in segs:
        for c in range(width // LANES):
            lo = c0 + c * LANES
            a = _dot(h, w_ref[:, lo:lo + LANES])
            if epi == "rope":
                a = (a * cos_ref[...] + pltpu.roll(a, ROPE_HALF, 1) * sa_ref[...]
                     + pltpu.roll(a, LANES - ROPE_HALF, 1) * sb_ref[...])
            elif epi == "logf":
                z = a + bf_ref[...]
                a = jnp.minimum(z, 0.0) - jnp.log(1.0 + jnp.exp(-jnp.abs(z)))
            for oi, sc in outs:
                val = a if sc == 1.0 else a * sc
                out_refs[oi][:, c * LANES:(c + 1) * LANES] = val.astype(out_refs[oi].dtype)


def _in_proj(x, g, shift, scale, w, tabs, b_f, segs, outs, tm, rows_per_mod):
    m, d = x.shape
    n = w.shape[1]
    r = shift.shape[1]
    cos, sa, sb = tabs
    tab_blocks = cos.shape[0] // tm
    mod_spec = pl.BlockSpec((1, r, d), lambda i: (i * tm // rows_per_mod, 0, 0))
    tab_spec = pl.BlockSpec((tm, LANES), lambda i: (i % tab_blocks, 0))
    return pl.pallas_call(
        functools.partial(_in_proj_kernel, segs=segs),
        grid=(m // tm,),
        in_specs=[pl.BlockSpec((tm, d), lambda i: (i, 0)),
                  pl.BlockSpec((1, d), lambda i: (0, 0)),
                  mod_spec, mod_spec,
                  pl.BlockSpec((d, n), lambda i: (0, 0)),
                  tab_spec, tab_spec, tab_spec,
                  pl.BlockSpec((1, LANES), lambda i: (0, 0))],
        out_specs=[pl.BlockSpec((tm, wd), lambda i: (i, 0)) for wd, _ in outs],
        out_shape=[jax.ShapeDtypeStruct((m, wd), dt) for wd, dt in outs],
        compiler_params=_cparams("parallel"),
        name="in_proj",
    )(x, g, shift, scale, w, cos, sa, sb, b_f)


def _rope_tables(pos):
    inv_freq = ROPE_THETA ** (-jnp.arange(ROPE_HALF, dtype=f32) / ROPE_HALF)
    ang = pos.astype(f32)[:, None] * inv_freq[None, :]
    cos, sin = jnp.cos(ang), jnp.sin(ang)
    t = pos.shape[0]
    ones = jnp.ones((t, HEAD_DIM - ROPE_DIMS), f32)
    zeros = jnp.zeros((t, HEAD_DIM - ROPE_DIMS), f32)
    zh = jnp.zeros((t, ROPE_HALF), f32)
    c = jnp.concatenate([cos, cos, ones], axis=1)
    sa = jnp.concatenate([zh, sin, zeros], axis=1)
    sb = jnp.concatenate([-sin, zh, zeros], axis=1)
    rep = LANES // HEAD_DIM
    return tuple(jnp.tile(a, (1, rep)) for a in (c, sa, sb))


WT_Q = 0
WT_V = N_HEADS * HEAD_DIM
WT_K = WT_V + KV_COLS
WT_X = WT_K + KV_COLS
WT_IK = WT_X + N_IDX_HEADS * IDX_DIM
WT_IW = WT_IK + IDX_DIM
IW_ROWS = 2 * SUBLANES
ROW_BLOCK = 512


def _rope_rows(a, cos_t, sin_t):
    parts = []
    for b0 in range(0, a.shape[0], HEAD_DIM):
        x1 = a[b0:b0 + ROPE_HALF]
        x2 = a[b0 + ROPE_HALF:b0 + ROPE_DIMS]
        parts += [x1 * cos_t - x2 * sin_t, x1 * sin_t + x2 * cos_t, a[b0 + ROPE_DIMS:b0 + HEAD_DIM]]
    return jnp.concatenate(parts, axis=0)


def _in_proj_t_kernel(x_ref, g_ref, shift_ref, scale_ref, w_ref, wt_ref, cos_ref, sa_ref, sb_ref,
                      cost_ref, sint_ref, bf_ref, *outs, is_a):
    if is_a:
        qt_ref, vt_ref, kpad_ref, kt32_ref, vt32_ref, iqt_ref, iwt_ref, ikb_ref, ikt32_ref = outs
    else:
        qt_ref, vt_ref, kpad_ref, kt32_ref, vt32_ref, lft_ref = outs
    h = _norm_mod(x_ref[...], g_ref[...], shift_ref[0], scale_ref[0]).astype(bf16)
    cos_t, sin_t = cost_ref[...], sint_ref[...]

    def rows(r0, n):
        blk = _dot_nt(wt_ref[r0:r0 + n, :], h)
        return [blk[c * LANES:(c + 1) * LANES] for c in range(n // LANES)]

    def rope_cols(a):
        return (a * cos_ref[...] + pltpu.roll(a, ROPE_HALF, 1) * sa_ref[...]
                + pltpu.roll(a, LANES - ROPE_HALF, 1) * sb_ref[...])

    for r0 in range(0, N_HEADS * HEAD_DIM, ROW_BLOCK):
        for c, a in enumerate(rows(WT_Q + r0, ROW_BLOCK)):
            if is_a:
                a = _rope_rows(a, cos_t, sin_t)
            lo = r0 + c * LANES
            qt_ref[lo:lo + LANES, :] = (a * (Q_SCALE * LOG2E)).astype(bf16)
    for c, (v, k) in enumerate(zip(rows(WT_V, KV_COLS), rows(WT_K, KV_COLS))):
        sl = slice(c * LANES, (c + 1) * LANES)
        vt32_ref[0, sl, :] = v
        vt_ref[sl, :] = v.astype(bf16)
        kt32_ref[0, sl, :] = _rope_rows(k, cos_t, sin_t) if is_a else k
    if is_a:
        for c, a in enumerate(rows(WT_X, N_IDX_HEADS * IDX_DIM)):
            iqt_ref[c * LANES:(c + 1) * LANES, :] = _rope_rows(a, cos_t, sin_t).astype(bf16)
        ikt32_ref[0] = _rope_rows(_dot_nt(wt_ref[WT_IK:WT_IK + IDX_DIM, :], h), cos_t, sin_t)
        iwt_ref[...] = _dot_nt(wt_ref[WT_IW:WT_IW + IW_ROWS, :], h)
    else:
        z = _dot_nt(wt_ref[WT_X:WT_X + N_HEADS, :], h) + bf_ref[...]
        lft_ref[0] = jnp.minimum(z, 0.0) - jnp.log(1.0 + jnp.exp(-jnp.abs(z)))

    lane = lax.broadcasted_iota(i32, (1, LANES), 1)
    std = _dot(h, w_ref[...])
    for c in range(KV_COLS // LANES):
        a = std[:, c * LANES:(c + 1) * LANES]
        if is_a:
            a = rope_cols(a)
        kpad_ref[0, 2 * c] = jnp.where(lane < HEAD_DIM, a, 0.0).astype(bf16)
        kpad_ref[0, 2 * c + 1] = jnp.where(lane < HEAD_DIM, pltpu.roll(a, HEAD_DIM, 1),
                                           0.0).astype(bf16)
    if is_a:
        ikb_ref[...] = rope_cols(std[:, KV_COLS:KV_COLS + LANES])[:, :IDX_DIM].astype(bf16)


def _in_proj_t(x, g, shift, scale, w, wt, tabs, tabs_t, b_f, is_a, tm, t):
    m, d = x.shape
    bsz = m // t
    tpb = t // tm
    col = lambda rows_: pl.BlockSpec((rows_, tm), lambda i: (0, i))
    seq = lambda rows_: pl.BlockSpec((1, rows_, tm), lambda i: (i // tpb, 0, i % tpb))
    seq_shape = lambda rows_: jax.ShapeDtypeStruct((bsz, rows_, t), f32)
    const = lambda a: pl.BlockSpec(a.shape, lambda i: (0,) * a.ndim)
    mod_spec = pl.BlockSpec((1, 1, d), lambda i: (i // tpb, 0, 0))
    tab_spec = pl.BlockSpec((tm, LANES), lambda i: (i % tpb, 0))
    tabt_spec = pl.BlockSpec((ROPE_HALF, tm), lambda i: (0, i % tpb))
    nq = N_HEADS * HEAD_DIM
    out_specs = [col(nq), col(KV_COLS),
                 pl.BlockSpec((1, N_KV_HEADS, tm, LANES), lambda i: (i // tpb, 0, i % tpb, 0)),
                 seq(KV_COLS), seq(KV_COLS)]
    out_shape = [jax.ShapeDtypeStruct((nq, m), bf16), jax.ShapeDtypeStruct((KV_COLS, m), bf16),
                 jax.ShapeDtypeStruct((bsz, N_KV_HEADS, t, LANES), bf16),
                 seq_shape(KV_COLS), seq_shape(KV_COLS)]
    if is_a:
        out_specs += [col(N_IDX_HEADS * IDX_DIM), col(IW_ROWS),
                      pl.BlockSpec((tm, IDX_DIM), lambda i: (i, 0)), seq(IDX_DIM)]
        out_shape += [jax.ShapeDtypeStruct((N_IDX_HEADS * IDX_DIM, m), bf16),
                      jax.ShapeDtypeStruct((IW_ROWS, m), f32),
                      jax.ShapeDtypeStruct((m, IDX_DIM), bf16), seq_shape(IDX_DIM)]
    else:
        out_specs += [seq(N_HEADS)]
        out_shape += [seq_shape(N_HEADS)]
    return pl.pallas_call(
        functools.partial(_in_proj_t_kernel, is_a=is_a),
        grid=(m // tm,),
        in_specs=[pl.BlockSpec((tm, d), lambda i: (i, 0)), const(g), mod_spec, mod_spec,
                  const(w), const(wt), tab_spec, tab_spec, tab_spec, tabt_spec, tabt_spec,
                  const(b_f)],
        out_specs=out_specs, out_shape=out_shape,
        compiler_params=_cparams("parallel"),
        name="in_proj_t",
    )(x, g, shift, scale, w, wt, *tabs, *tabs_t, b_f)


def _rope_tables_t(pos):
    inv_freq = ROPE_THETA ** (-jnp.arange(ROPE_HALF, dtype=f32) / ROPE_HALF)
    ang = inv_freq[:, None] * pos.astype(f32)[None, :]
    return jnp.cos(ang), jnp.sin(ang)


AUG_STRIDE = 4
AUG_ONES = GROUP * AUG_STRIDE


ATTN_TK = 256
SUM_ROWS = 2 * SUBLANES
ACC_ROWS = HEAD_DIM + SUM_ROWS


def _t_init(m_sc, acc_sc):
    m_sc[...] = jnp.full(m_sc.shape, NEG, f32)
    acc_sc[...] = jnp.zeros(acc_sc.shape, f32)


def _t_step(s, vt1, h, m_sc, acc_sc):
    m_old = m_sc[h]
    m_new = jnp.maximum(m_old, jnp.max(s, axis=0, keepdims=True))
    p = jnp.exp2((s - m_new).astype(bf16))
    acc_sc[h] = jnp.exp2(m_old - m_new) * acc_sc[h] + _dot(vt1, p)
    m_sc[h] = m_new


STALE_LIMIT = 64.0


def _t_step_stale(s, vt1, h, m_sc, acc_sc, rise_sc):
    m_old = m_sc[h]
    p = jnp.exp2((s - m_old).astype(bf16))
    s_max = jnp.max(s, axis=0, keepdims=True)
    m_new = jnp.maximum(m_old, s_max)
    acc_sc[h] = (acc_sc[h] + _dot(vt1, p)) * jnp.exp2(m_old - m_new)
    m_sc[h] = m_new
    rise_sc[h] = jnp.maximum(rise_sc[h], s_max - m_old)


def _t_tile(off, tk, kt_of, vt_ref, qa_sc, post, step):
    kts = [kt_of(g) for g in range(N_KV_HEADS)]
    ones = jnp.ones((SUM_ROWS, tk), bf16)
    vts = [jnp.concatenate([vt_ref[g * HEAD_DIM:(g + 1) * HEAD_DIM, pl.ds(off, tk)], ones], axis=0)
           for g in range(N_KV_HEADS)]
    ahead = 2
    pending = [_dot(kts[g], qa_sc[g]) for g in range(ahead)]
    for g in range(N_KV_HEADS):
        s = pending.pop(0)
        if g + ahead < N_KV_HEADS:
            pending.append(_dot(kts[g + ahead], qa_sc[g + ahead]))
        step(post(s), vts[g], g)


def _t_sweep(first, rest, m_sc, acc_sc, rise_sc):
    def exact(s, vt1, g):
        _t_step(s, vt1, g, m_sc, acc_sc)

    def stale(s, vt1, g):
        _t_step_stale(s, vt1, g, m_sc, acc_sc, rise_sc)

    rise_sc[...] = jnp.full(rise_sc.shape, NEG, f32)
    _t_init(m_sc, acc_sc)
    first(exact)
    rest(stale)

    @pl.when(jnp.max(rise_sc[...]) > STALE_LIMIT)
    def _():
        _t_init(m_sc, acc_sc)
        first(exact)
        rest(exact)


def _t_fill_q(qa_sc, h, tq, q, aug):
    g, jj = divmod(h, GROUP)
    qa_sc[g, :HEAD_DIM, jj * tq:(jj + 1) * tq] = q
    qa_sc[g, HEAD_DIM:, jj * tq:(jj + 1) * tq] = aug


def _t_finish(tq, acc_sc, ot_sc):
    for h in range(N_HEADS):
        g, jj = divmod(h, GROUP)
        cols = slice(jj * tq, (jj + 1) * tq)
        ot_sc[h * HEAD_DIM:(h + 1) * HEAD_DIM, :] = (
            acc_sc[g, :HEAD_DIM, cols] / acc_sc[g, HEAD_DIM:HEAD_DIM + 1, cols]).astype(bf16)


def _t_out(x_ref, gate_ref, wot_ref, ot_sc, o_ref):
    y_t = _dot(wot_ref[...], ot_sc[...])
    o_ref[...] = x_ref[...] + gate_ref[0] * y_t.T


def _fox_attn_kernel(x_ref, gate_ref, qt_ref, kpad_ref, vt_ref, cpad_ref, ct_ref, wot_ref,
                     o_ref, kaug_sc, m_sc, rise_sc, acc_sc, qa_sc, ot_sc, *, tq, tk, chunk):
    qi = pl.program_id(1)
    t = kaug_sc.shape[1]

    @pl.when(qi == 0)
    def _():
        r = lax.broadcasted_iota(i32, (LANES, LANES), 0)
        l = lax.broadcasted_iota(i32, (LANES, LANES), 1) - HEAD_DIM
        lane = lax.broadcasted_iota(i32, (1, LANES), 1) - HEAD_DIM
        ones = jnp.where((lane >= AUG_ONES) & (lane < AUG_ONES + 3), 1.0, 0.0)
        for g in range(N_KV_HEADS):
            perm = [((l >= 0) & (l < AUG_ONES) & ((l >> 2) == r - g * GROUP)
                     & ((l & 3) == p)).astype(bf16) for p in range(3)]

            def body(i, c, g=g, perm=perm):
                off = pl.multiple_of(i * chunk, chunk)
                pieces = _split3(cpad_ref[0, pl.ds(off, chunk), :] * -LOG2E)
                aug = sum(_dot(pc, pm) for pc, pm in zip(pieces, perm)) + ones
                kaug_sc[g, pl.ds(off, chunk), :] = (
                    kpad_ref[0, g, pl.ds(off, chunk), :].astype(f32) + aug).astype(bf16)
                return c

            lax.fori_loop(0, t // chunk, body, 0)

    ri = lax.broadcasted_iota(i32, (HEAD_DIM, tq), 0)
    wide = (tk, GROUP * tq)
    d_iota = (lax.broadcasted_iota(i32, wide, 0)
              - (lax.broadcasted_iota(i32, wide, 1) & (tq - 1)))
    for h in range(N_HEADS):
        jj = h % GROUP
        cq = _split3(ct_ref[0, h:h + 1, :] * LOG2E)
        aug = jnp.where((ri >= jj * AUG_STRIDE) & (ri < jj * AUG_STRIDE + 3), 1.0, 0.0)
        for p in range(3):
            aug = jnp.where(ri == AUG_ONES + p, cq[p].astype(f32), aug)
        _t_fill_q(qa_sc, h, tq, qt_ref[h * HEAD_DIM:(h + 1) * HEAD_DIM, :], aug.astype(bf16))
    def tile(j, post, step):
        off = pl.multiple_of(j * tk, tk)
        _t_tile(off, tk, lambda g: kaug_sc[g, pl.ds(off, tk), :], vt_ref, qa_sc, post, step)

    sub = tq // tk

    def diagonal(step):
        for dd in range(sub):
            tile(qi * sub + dd, lambda s, dd=dd: jnp.where(d_iota <= -dd * tk, s, NEG), step)

    def earlier(step):
        def body(i, c):
            tile(qi * sub - 1 - i, lambda s: s, step)
            return c
        lax.fori_loop(0, qi * sub, body, 0)

    _t_sweep(diagonal, earlier, m_sc, acc_sc, rise_sc)
    _t_finish(tq, acc_sc, ot_sc)
    _t_out(x_ref, gate_ref, wot_ref, ot_sc, o_ref)


def _attn_specs(bsz, t, d, tq):
    nq = t // tq
    return dict(
        x=pl.BlockSpec((tq, d), lambda b, q: (b * nq + q, 0)),
        gate=pl.BlockSpec((1, 1, d), lambda b, q: (b, 0, 0)),
        qt=pl.BlockSpec((N_HEADS * HEAD_DIM, tq), lambda b, q: (0, b * nq + q)),
        kpad=pl.BlockSpec((1, N_KV_HEADS, t, LANES), lambda b, q: (b, 0, 0, 0)),
        vt=pl.BlockSpec((KV_COLS, t), lambda b, q: (0, b)),
        wot=pl.BlockSpec((d, N_HEADS * HEAD_DIM), lambda b, q: (0, 0)),
        scratch=[pltpu.VMEM((N_KV_HEADS, 1, GROUP * tq), f32),
                 pltpu.VMEM((N_KV_HEADS, 1, GROUP * tq), f32),
                 pltpu.VMEM((N_KV_HEADS, ACC_ROWS, GROUP * tq), f32),
                 pltpu.VMEM((N_KV_HEADS, LANES, GROUP * tq), bf16),
                 pltpu.VMEM((N_HEADS * HEAD_DIM, tq), bf16)])


def _fox_attn(x, gate, qt, kpad, vt, c_pad, c_t, wot, tq):
    bsz, _, t, _ = kpad.shape
    m, d = x.shape
    sp = _attn_specs(bsz, t, d, tq)
    return pl.pallas_call(
        functools.partial(_fox_attn_kernel, tq=tq, tk=ATTN_TK, chunk=512),
        grid=(bsz, t // tq),
        in_specs=[sp["x"], sp["gate"], sp["qt"], sp["kpad"], sp["vt"],
                  pl.BlockSpec((1, t, LANES), lambda b, q: (b, 0, 0)),
                  pl.BlockSpec((1, N_HEADS, tq), lambda b, q: (b, 0, q)),
                  sp["wot"]],
        out_specs=sp["x"],
        out_shape=jax.ShapeDtypeStruct((m, d), f32),
        scratch_shapes=[pltpu.VMEM((N_KV_HEADS, t, LANES), bf16)] + sp["scratch"],
        compiler_params=_cparams("parallel", "arbitrary"),
        name="fox_attn",
    )(x, gate, qt, kpad, vt, c_pad, c_t, wot)


def _cumsum_t_kernel(lft_ref, ct_ref, carry_sc, *, tb):
    @pl.when(pl.program_id(1) == 0)
    def _():
        carry_sc[...] = jnp.zeros(carry_sc.shape, f32)

    tri = (lax.broadcasted_iota(i32, (tb, tb), 0)
           <= lax.broadcasted_iota(i32, (tb, tb), 1)).astype(bf16)
    lft = lft_ref[0]
    csum = sum(_dot(p, tri) for p in _split3(lft))
    ct_ref[0] = csum + carry_sc[...]
    carry_sc[...] = carry_sc[...] + jnp.sum(lft, axis=1, keepdims=True)


def _cumsum_t(lf_t, tb):
    b, nh, t = lf_t.shape
    spec = pl.BlockSpec((1, nh, tb), lambda bi, j: (bi, 0, j))
    return pl.pallas_call(
        functools.partial(_cumsum_t_kernel, tb=tb),
        grid=(b, t // tb),
        in_specs=[spec],
        out_specs=spec,
        out_shape=jax.ShapeDtypeStruct((b, nh, t), f32),
        scratch_shapes=[pltpu.VMEM((nh, 1), f32)],
        compiler_params=_cparams("parallel", "arbitrary"),
        name="cumsum_t",
    )(lf_t)


def _kth_threshold(count_ge, shape, n_sel):
    zero = jnp.zeros(shape, i32)
    ans = jnp.where(count_ge(zero) >= n_sel, zero, jnp.full(shape, INT_MIN, i32))

    def bit_body(i, ans):
        cand = ans | jnp.left_shift(jnp.int32(1), 30 - i)
        return jnp.where(count_ge(cand) >= n_sel, cand, ans)

    return lax.fori_loop(0, 31, bit_body, ans)


BF16_SUBLANES = 2 * SUBLANES


def _kth_two_level(count_coarse_ge, count_ge, shape, n_sel):
    def coarse_value(a):
        pattern = a ^ ((a >> 31) & 0x7FFF)
        return pltpu.bitcast(pattern << 16, f32).astype(bf16)

    zero = jnp.zeros(shape, i32)
    ans = jnp.where(count_coarse_ge(coarse_value(zero)) >= n_sel, zero,
                    jnp.full(shape, -2 ** 15, i32))

    def bit_body(i, ans):
        cand = ans | jnp.left_shift(jnp.int32(1), 14 - i)
        return jnp.where(count_coarse_ge(coarse_value(cand)) >= n_sel, cand, ans)

    ans = lax.fori_loop(0, 15, bit_body, ans)
    pattern = (ans ^ ((ans >> 31) & 0x7FFF)) << 16
    centre = pattern ^ ((pattern >> 31) & 0x7FFFFFFF)
    half = 2 ** 15

    def cond(state):
        i, _, _, _, n_open = state
        return (i <= 17) & (n_open > 0)

    def body(state):
        i, lo, hi, exact, _ = state
        live = (exact == 0) & (lo < hi)
        mid = lo + ((hi - lo + 1) >> 1)
        c = count_ge(mid)
        ge = c >= n_sel
        lo = jnp.where(live & ge, mid, lo)
        hi = jnp.where(live & jnp.logical_not(ge), mid - 1, hi)
        exact = jnp.where(live & (c == n_sel), 1, exact)
        n_open = jnp.sum(jnp.where((exact == 0) & (lo < hi), 1, 0))
        return i + 1, lo, hi, exact, n_open

    state = (jnp.int32(0), centre - half, centre + half, zero, jnp.int32(1))
    _, lo, _, exact, _ = lax.while_loop(cond, body, state)
    return lo, exact


def _tie_cut(count_tied_below, need, shape, n_bits):
    def body(i, p):
        cand = p | jnp.left_shift(jnp.int32(1), n_bits - 1 - i)
        return jnp.where(count_tied_below(cand) < need, cand, p)
    return lax.fori_loop(0, n_bits, body, jnp.zeros(shape, i32))


def _fold_lanes(ind, c):
    for cc in range(ind.shape[1] // LANES):
        c = c + ind[:, cc * LANES:(cc + 1) * LANES]
    return c


def _dsa_attn_kernel(x_ref, gate_ref, qt_ref, kpad_ref, vt_ref, iqt_ref, iwt_ref, ik_ref,
                     wot_ref, o_ref, key_sc, coarse_sc, bias_sc, cut_sc, m_sc, rise_sc, acc_sc,
                     qa_sc, ot_sc,
                     *, tq, n_sel, n_bits):
    qi = pl.program_id(1)
    n_k = qi + 1
    row = lax.broadcasted_iota(i32, (tq, tq), 0)
    d_iota = row - lax.broadcasted_iota(i32, (tq, tq), 1)

    def score_body(j, c):
        off = pl.multiple_of(j * tq, tq)
        ikt = ik_ref[0, pl.ds(off, tq), :]
        s = jnp.zeros((tq, tq), f32)
        for h in range(N_IDX_HEADS):
            dots = _dot(ikt, iqt_ref[h * IDX_DIM:(h + 1) * IDX_DIM, :])
            s = s + jnp.maximum(dots, 0.0) * iwt_ref[h:h + 1, :]
        causal = d_iota <= (qi - j) * tq
        key_sc[pl.ds(off, tq), :] = jnp.where(causal, _sort_key(s), KEY_NEG_INF)
        coarse_sc[pl.ds(off, tq), :] = jnp.where(causal, s, -jnp.inf).astype(bf16)
        return c

    lax.fori_loop(0, n_k, score_body, 0)

    def count_coarse_ge(v):
        def body(j, c):
            off = pl.multiple_of(j * tq, tq)
            ind = jnp.where(coarse_sc[pl.ds(off, tq), :] >= v, jnp.ones((), bf16),
                            jnp.zeros((), bf16))
            for r in range(0, tq, BF16_SUBLANES):
                c = c + ind[r:r + BF16_SUBLANES]
            return c
        c = lax.fori_loop(0, n_k, body, jnp.zeros((BF16_SUBLANES, tq), bf16))
        return jnp.sum(c.astype(f32), axis=0, keepdims=True)

    def count(pred):
        def body(j, c):
            off = pl.multiple_of(j * tq, tq)
            ind = jnp.where(pred(key_sc[pl.ds(off, tq), :], j), 1, 0)
            for r in range(0, tq, SUBLANES):
                c = c + ind[r:r + SUBLANES]
            return c
        c = lax.fori_loop(0, n_k, body, jnp.zeros((SUBLANES, tq), i32))
        return jnp.sum(c, axis=0, keepdims=True)

    thr, exact = _kth_two_level(count_coarse_ge, lambda cand: count(lambda kt, j: kt >= cand),
                                (1, tq), n_sel)
    keep_all = n_k * tq <= n_sel
    thr = jnp.where(keep_all, INT_MIN, thr)
    exact = jnp.where(keep_all, 1, exact)
    real = thr > KEY_NEG_INF
    cut_sc[...] = jnp.where(real, INT_MAX, -1)

    @pl.when(jnp.max(jnp.where(real, 1 - exact, 0)) > 0)
    def _():
        need = n_sel - count(lambda kt, j: kt > thr)
        cut = _tie_cut(lambda cand: count(lambda kt, j: (kt == thr) & (row + j * tq < cand)),
                       need, (1, tq), n_bits)
        cut_sc[...] = jnp.where(real, jnp.where(exact == 1, INT_MAX, cut), -1)

    cut = cut_sc[...]
    floor = jnp.maximum(thr, KEY_NEG_INF)

    def bias_body(j, c):
        off = pl.multiple_of(j * tq, tq)
        kt = key_sc[pl.ds(off, tq), :]
        sel = (kt > floor) | ((kt == thr) & (row + j * tq <= cut))
        bias_sc[pl.ds(off, tq), :] = jnp.where(sel, 0.0, NEG)
        return c

    lax.fori_loop(0, n_k, bias_body, 0)

    for h in range(N_HEADS):
        _t_fill_q(qa_sc, h, tq, qt_ref[h * HEAD_DIM:(h + 1) * HEAD_DIM, :],
                  jnp.zeros((LANES - HEAD_DIM, tq), bf16))
    def tile(j, step):
        off = pl.multiple_of(j * tq, tq)

        def masked(s):
            return jnp.concatenate([s[:, jj * tq:(jj + 1) * tq] + bias_sc[pl.ds(off, tq), :]
                                    for jj in range(GROUP)], axis=1)

        _t_tile(off, tq, lambda g: kpad_ref[0, g, pl.ds(off, tq), :], vt_ref, qa_sc, masked, step)

    def oldest(step):
        tile(0, step)

    def later(step):
        def body(j, c):
            tile(j, step)
            return c
        lax.fori_loop(1, n_k, body, 0)

    _t_sweep(oldest, later, m_sc, acc_sc, rise_sc)
    _t_finish(tq, acc_sc, ot_sc)
    _t_out(x_ref, gate_ref, wot_ref, ot_sc, o_ref)


def _dsa_attn(x, gate, qt, kpad, vt, iqt, iwt, ik, wot, tq, n_sel):
    bsz, _, t, _ = kpad.shape
    m, d = x.shape
    nq = t // tq
    sp = _attn_specs(bsz, t, d, tq)
    assert tq >= n_sel and t // BF16_SUBLANES <= 256
    return pl.pallas_call(
        functools.partial(_dsa_attn_kernel, tq=tq, n_sel=n_sel, n_bits=(t - 1).bit_length()),
        grid=(bsz, nq),
        in_specs=[sp["x"], sp["gate"], sp["qt"], sp["kpad"], sp["vt"],
                  pl.BlockSpec((N_IDX_HEADS * IDX_DIM, tq), lambda b, q: (0, b * nq + q)),
                  pl.BlockSpec((2 * SUBLANES, tq), lambda b, q: (0, b * nq + q)),
                  pl.BlockSpec((1, t, IDX_DIM), lambda b, q: (b, 0, 0)),
                  sp["wot"]],
        out_specs=sp["x"],
        out_shape=jax.ShapeDtypeStruct((m, d), f32),
        scratch_shapes=[pltpu.VMEM((t, tq), i32), pltpu.VMEM((t, tq), bf16),
                        pltpu.VMEM((t, tq), f32), pltpu.VMEM((1, tq), i32)] + sp["scratch"],
        compiler_params=_cparams("parallel", "arbitrary"),
        name="dsa_attn",
    )(x, gate, qt, kpad, vt, iqt, iwt, ik, wot)


def _out_proj_kernel(x_ref, o_ref, w_ref, gate_ref, y_ref):
    y_ref[...] = x_ref[...] + gate_ref[0] * _dot(o_ref[...], w_ref[...])


def _out_proj(x, o, w, gate, tm, rows_per_mod):
    m, d = x.shape
    kdim = o.shape[1]
    r = gate.shape[1]
    return pl.pallas_call(
        _out_proj_kernel,
        grid=(m // tm,),
        in_specs=[pl.BlockSpec((tm, d), lambda i: (i, 0)),
                  pl.BlockSpec((tm, kdim), lambda i: (i, 0)),
                  pl.BlockSpec((kdim, d), lambda i: (0, 0)),
                  pl.BlockSpec((1, r, d), lambda i: (i * tm // rows_per_mod, 0, 0))],
        out_specs=pl.BlockSpec((tm, d), lambda i: (i, 0)),
        out_shape=jax.ShapeDtypeStruct((m, d), f32),
        compiler_params=_cparams("parallel"),
        name="out_proj",
    )(x, o, w, gate)


FFN_CHUNK = 256
FFN_HALO = 2 * SUBLANES


def _ffn_kernel(x_ref, xh_ref, g_ref, shift_ref, scale_ref, gate_ref, wa_ref, wb_ref,
                cw_ref, cb_ref, wd_ref, gfin_ref, o_ref, tail_ref, acc_sc, hext_sc,
                *, tm, tiles_per_seq, final):
    i = pl.program_id(0)
    x = x_ref[...]
    keep = jnp.where(i % tiles_per_seq == 0, 0.0, 1.0)
    hext_sc[:FFN_HALO, :] = (_norm_mod(xh_ref[...], g_ref[...], shift_ref[0], scale_ref[0])
                             * keep).astype(bf16)
    hext_sc[FFN_HALO:, :] = _norm_mod(x, g_ref[...], shift_ref[0], scale_ref[0]).astype(bf16)
    for c in range(D_FF // FFN_CHUNK):
        sl = slice(c * FFN_CHUNK, (c + 1) * FFN_CHUNK)
        a_ext = _dot(hext_sc[...], wa_ref[:, sl])
        b = _dot(hext_sc[FFN_HALO:, :], wb_ref[:, sl])
        a = a_ext[FFN_HALO:]
        p1 = pltpu.roll(a_ext, 1, 0)[FFN_HALO:]
        p2 = pltpu.roll(a_ext, 2, 0)[FFN_HALO:]
        conv = cb_ref[:, sl] + p2 * cw_ref[0:1, sl] + p1 * cw_ref[1:2, sl] + a * cw_ref[2:3, sl]
        y = _dot((_silu(conv) * b).astype(bf16), wd_ref[sl, :])
        if c == 0:
            acc_sc[...] = y
        else:
            acc_sc[...] += y
        tail_ref[0, :, sl] = a_ext[FFN_HALO + tm - SUBLANES:]
    y = x + gate_ref[0] * acc_sc[...]
    if final:
        y = y * lax.rsqrt(jnp.mean(y * y, axis=-1, keepdims=True) + EPS) * gfin_ref[...]
    o_ref[...] = y


def _ffn(x, g, shift, scale, gate, wa, wb, cw, cb, wd, g_final, final, tm, rows_per_seq):
    m, d = x.shape
    n_tiles = m // tm
    hb = tm // FFN_HALO
    mod_spec = pl.BlockSpec((1, 1, d), lambda i: (i * tm // rows_per_seq, 0, 0))
    const = lambda shape: pl.BlockSpec(shape, lambda i: (0,) * len(shape))
    return pl.pallas_call(
        functools.partial(_ffn_kernel, tm=tm, tiles_per_seq=rows_per_seq // tm, final=final),
        grid=(n_tiles,),
        in_specs=[pl.BlockSpec((tm, d), lambda i: (i, 0)),
                  pl.BlockSpec((FFN_HALO, d), lambda i: (jnp.maximum(i * hb - 1, 0), 0)),
                  const((1, d)), mod_spec, mod_spec, mod_spec,
                  const((d, D_FF)), const((d, D_FF)), const((SUBLANES, D_FF)),
                  const((1, D_FF)), const((D_FF, d)), const((1, d))],
        out_specs=[pl.BlockSpec((tm, d), lambda i: (i, 0)),
                   pl.BlockSpec((1, SUBLANES, D_FF), lambda i: (i, 0, 0))],
        out_shape=[jax.ShapeDtypeStruct((m, d), f32),
                   jax.ShapeDtypeStruct((n_tiles, SUBLANES, D_FF), f32)],
        scratch_shapes=[pltpu.VMEM((tm, d), f32), pltpu.VMEM((FFN_HALO + tm, d), bf16)],
        compiler_params=_cparams("parallel"),
        name="ffn",
    )(x, x, g, shift, scale, gate, wa, wb, cw, cb, wd, g_final)


def _ffn_sample_kernel(x_ref, g_ref, shift_ref, scale_ref, gate_ref, p2_ref, p1_ref, wa_ref,
                       wb_ref, cw_ref, cb_ref, wd_ref, o_ref, a_ref, acc_sc):
    j = pl.program_id(0)

    @pl.when(j == 0)
    def _():
        acc_sc[...] = jnp.zeros(acc_sc.shape, f32)

    x = x_ref[...]
    h = _norm_mod(x, g_ref[...], shift_ref[0], scale_ref[0]).astype(bf16)
    a = _dot(h, wa_ref[...])
    b = _dot(h, wb_ref[...])
    conv = (cb_ref[...] + p2_ref[...] * cw_ref[0:1, :] + p1_ref[...] * cw_ref[1:2, :]
            + a * cw_ref[2:3, :])
    acc_sc[...] += _dot((_silu(conv) * b).astype(bf16), wd_ref[...])
    a_ref[...] = a

    @pl.when(j == pl.num_programs(0) - 1)
    def _():
        o_ref[...] = x + gate_ref[0] * acc_sc[...]


def _ffn_sample(x, g, shift, scale, gate, p2, p1, wa, wb, cw, cb, wd, tn):
    m, d = x.shape
    const = lambda shape: pl.BlockSpec(shape, lambda j: (0,) * len(shape))
    return pl.pallas_call(
        _ffn_sample_kernel,
        grid=(D_FF // tn,),
        in_specs=[const((m, d)), const((1, d)), const((1, m, d)), const((1, m, d)),
                  const((1, m, d)),
                  pl.BlockSpec((m, tn), lambda j: (0, j)), pl.BlockSpec((m, tn), lambda j: (0, j)),
                  pl.BlockSpec((d, tn), lambda j: (0, j)), pl.BlockSpec((d, tn), lambda j: (0, j)),
                  pl.BlockSpec((SUBLANES, tn), lambda j: (0, j)),
                  pl.BlockSpec((1, tn), lambda j: (0, j)),
                  pl.BlockSpec((tn, d), lambda j: (j, 0))],
        out_specs=[const((m, d)), pl.BlockSpec((m, tn), lambda j: (0, j))],
        out_shape=[jax.ShapeDtypeStruct((m, d), f32), jax.ShapeDtypeStruct((m, D_FF), f32)],
        scratch_shapes=[pltpu.VMEM((m, d), f32)],
        compiler_params=_cparams("arbitrary"),
        name="ffn_sample",
    )(x, g, shift, scale, gate, p2, p1, wa, wb, cw, cb, wd)


def _final_norm_kernel(x_ref, g_ref, o_ref):
    x = x_ref[...]
    o_ref[...] = x * lax.rsqrt(jnp.mean(x * x, axis=-1, keepdims=True) + EPS) * g_ref[...]


def _final_norm(x, g, tm):
    m, d = x.shape
    return pl.pallas_call(
        _final_norm_kernel,
        grid=(m // tm,),
        in_specs=[pl.BlockSpec((tm, d), lambda i: (i, 0)), pl.BlockSpec((1, d), lambda i: (0, 0))],
        out_specs=pl.BlockSpec((tm, d), lambda i: (i, 0)),
        out_shape=jax.ShapeDtypeStruct((m, d), f32),
        compiler_params=_cparams("parallel"),
        name="final_norm",
    )(x, g)


def _seq_page_copies(pt_ref, hbm, layer, seq, buf, slot, sem, n_pages):
    return [pltpu.make_async_copy(hbm.at[layer, pt_ref[seq, p]],
                                  buf.at[slot, :, pl.ds(p * PAGE_SIZE, PAGE_SIZE)], sem.at[slot])
            for p in range(n_pages)]


def _gather_seq(pt_ref, hbm, layer, buf, sem, n_pages):
    b = pl.program_id(0)
    slot = b % 2

    @pl.when(b == 0)
    def _():
        for cp in _seq_page_copies(pt_ref, hbm, layer, 0, buf, 0, sem, n_pages):
            cp.start()

    @pl.when(b + 1 < pl.num_programs(0))
    def _():
        for cp in _seq_page_copies(pt_ref, hbm, layer, b + 1, buf, 1 - slot, sem, n_pages):
            cp.start()

    for cp in _seq_page_copies(pt_ref, hbm, layer, b, buf, slot, sem, n_pages):
        cp.wait()
    return slot


def _sample_score_kernel(pt_ref, iq_ref, iw_ref, kidx_hbm, s_ref, buf, sem, *, layer, n_pages):
    slot = _gather_seq(pt_ref, kidx_hbm, layer, buf, sem, n_pages)
    d = _dot(iq_ref[0], buf[slot].astype(bf16))
    s_ref[0] = jnp.sum(jnp.maximum(d, 0.0) * iw_ref[0], axis=0, keepdims=True)


def _sample_scores(page_table, iq, iw, cache_kidx_t, layer):
    s, n_pages = page_table.shape
    length = n_pages * PAGE_SIZE
    grid_spec = pltpu.PrefetchScalarGridSpec(
        num_scalar_prefetch=1, grid=(s,),
        in_specs=[pl.BlockSpec((1, N_IDX_HEADS, IDX_DIM), lambda b, pt: (b, 0, 0)),
                  pl.BlockSpec((1, N_IDX_HEADS, 1), lambda b, pt: (b, 0, 0)),
                  pl.BlockSpec(memory_space=pl.ANY)],
        out_specs=pl.BlockSpec((1, 1, length), lambda b, pt: (b, 0, 0)),
        scratch_shapes=[pltpu.VMEM((2, IDX_DIM, length), f32), pltpu.SemaphoreType.DMA((2,))])
    out = pl.pallas_call(
        functools.partial(_sample_score_kernel, layer=layer, n_pages=n_pages),
        grid_spec=grid_spec,
        out_shape=jax.ShapeDtypeStruct((s, 1, length), f32),
        compiler_params=_cparams("arbitrary"),
        name="sample_scores",
    )(page_table, iq, iw, cache_kidx_t)
    return out.reshape(s, length)


SEL_TILE = 1024


def _sample_select_kernel(s_ref, iq_ref, ikt_ref, iw_ref, bias_ref, bnew_ref, key_sc,
                          *, n_sel, n_bits):
    rows, length = s_ref.shape
    n_tiles = length // SEL_TILE
    prod = iq_ref[...].astype(f32) * ikt_ref[...].astype(f32)
    grp = (lax.broadcasted_iota(i32, (prod.shape[1], LANES), 0) // IDX_DIM
           == lax.broadcasted_iota(i32, (prod.shape[1], LANES), 1)).astype(bf16)
    d_new = sum(_dot(p, grp) for p in _split3(prod))
    s_new = jnp.sum(jnp.maximum(d_new, 0.0) * iw_ref[...], axis=1, keepdims=True)
    key_new = _sort_key(s_new)
    key_sc[...] = _sort_key(s_ref[...])
    col = lax.broadcasted_iota(i32, (rows, SEL_TILE), 1)

    def count(pred, pred_new):
        def body(j, c):
            off = pl.multiple_of(j * SEL_TILE, SEL_TILE)
            return _fold_lanes(jnp.where(pred(key_sc[:, pl.ds(off, SEL_TILE)], j), 1, 0), c)
        c = lax.fori_loop(0, n_tiles, body, jnp.zeros((rows, LANES), i32))
        return jnp.sum(c, axis=1, keepdims=True) + jnp.where(pred_new, 1, 0)

    thr = _kth_threshold(lambda cand: count(lambda kt, j: kt >= cand, key_new >= cand),
                         (rows, 1), n_sel)
    need = n_sel - count(lambda kt, j: kt > thr, key_new > thr)
    cut = _tie_cut(lambda cand: count(lambda kt, j: (kt == thr) & (col + j * SEL_TILE < cand),
                                      (key_new == thr) & (length < cand)),
                   need, (rows, 1), n_bits)

    def bias_body(j, c):
        off = pl.multiple_of(j * SEL_TILE, SEL_TILE)
        kt = key_sc[:, pl.ds(off, SEL_TILE)]
        sel = (kt > thr) | ((kt == thr) & (col + j * SEL_TILE <= cut))
        bias_ref[:, pl.ds(off, SEL_TILE)] = jnp.where(sel, 0.0, NEG)
        return c

    lax.fori_loop(0, n_tiles, bias_body, 0)
    sel_new = (key_new > thr) | ((key_new == thr) & (length <= cut))
    bnew_ref[...] = jnp.broadcast_to(jnp.where(sel_new, 0.0, NEG), bnew_ref.shape)


def _sample_select(scores, iq, ik_tiled, iw, n_sel):
    rows, length = scores.shape
    return pl.pallas_call(
        functools.partial(_sample_select_kernel, n_sel=n_sel, n_bits=length.bit_length()),
        out_shape=[jax.ShapeDtypeStruct((rows, length), f32),
                   jax.ShapeDtypeStruct((rows, LANES), f32)],
        scratch_shapes=[pltpu.VMEM((rows, length), i32)],
        compiler_params=pltpu.CompilerParams(vmem_limit_bytes=VMEM_LIMIT),
        name="sample_select",
    )(scores, iq, ik_tiled, iw)


SUFFIX_CHUNK = 256


def _suffix_bias_kernel(pt_ref, lfnew_ref, lf_hbm, bias_ref, buf, sem, *, layer, n_pages):
    slot = _gather_seq(pt_ref, lf_hbm, layer, buf, sem, n_pages)
    w = SUFFIX_CHUNK
    tri = (lax.broadcasted_iota(i32, (w, w), 0)
           > lax.broadcasted_iota(i32, (w, w), 1)).astype(bf16)
    carry = lfnew_ref[0]
    for c in reversed(range(n_pages * PAGE_SIZE // w)):
        x = buf[slot, :, c * w:(c + 1) * w]
        bias_ref[0, :, c * w:(c + 1) * w] = sum(_dot(p, tri) for p in _split3(x)) + carry
        carry = carry + jnp.sum(x, axis=1, keepdims=True)


def _suffix_bias(page_table, lf_new, cache_logf_t, layer):
    s, n_pages = page_table.shape
    length = n_pages * PAGE_SIZE
    grid_spec = pltpu.PrefetchScalarGridSpec(
        num_scalar_prefetch=1, grid=(s,),
        in_specs=[pl.BlockSpec((1, N_HEADS, 1), lambda b, pt: (b, 0, 0)),
                  pl.BlockSpec(memory_space=pl.ANY)],
        out_specs=pl.BlockSpec((1, N_HEADS, length), lambda b, pt: (b, 0, 0)),
        scratch_shapes=[pltpu.VMEM((2, N_HEADS, length), f32), pltpu.SemaphoreType.DMA((2,))])
    return pl.pallas_call(
        functools.partial(_suffix_bias_kernel, layer=layer, n_pages=n_pages),
        grid_spec=grid_spec,
        out_shape=jax.ShapeDtypeStruct((s, N_HEADS, length), f32),
        compiler_params=_cparams("arbitrary"),
        name="suffix_bias",
    )(page_table, lf_new, cache_logf_t)


DEC_PAGES = 8
DEC_SLOTS = 3


def _kv_chunk_copies(pt_ref, k_hbm, v_hbm, layer, seq, chunk, kbuf, vbuf, slot, sem):
    cps = []
    for i in range(DEC_PAGES):
        page = pt_ref[seq, chunk * DEC_PAGES + i]
        cps.append(pltpu.make_async_copy(k_hbm.at[layer, page], kbuf.at[slot, i], sem.at[0, slot]))
        cps.append(pltpu.make_async_copy(v_hbm.at[layer, page], vbuf.at[slot, i], sem.at[1, slot]))
    return cps


def _decode_attn_kernel(pt_ref, q_ref, knew_ref, vnew_ref, bnew_ref, bias_ref, k_hbm, v_hbm,
                        o_ref, kbuf, vbuf, sem, *, layer, n_chunks):
    b = pl.program_id(0)
    q = q_ref[0]
    width = DEC_PAGES * PAGE_SIZE
    copies = functools.partial(_kv_chunk_copies, pt_ref, k_hbm, v_hbm, layer)

    def start(seq, chunk, slot):
        for cp in copies(seq, chunk, kbuf, vbuf, slot, sem):
            cp.start()

    ahead = DEC_SLOTS - 1
    base = b * n_chunks

    @pl.when(b == 0)
    def _():
        for c in range(ahead):
            start(0, c, c)

    kn = knew_ref[0].astype(bf16).astype(f32)
    m = jnp.sum(q.astype(f32) * kn, axis=1, keepdims=True) + bnew_ref[0][:, 0:1]
    l = jnp.ones_like(m)
    acc = jnp.broadcast_to(vnew_ref[0].astype(bf16).astype(f32), (N_HEADS, KV_COLS))

    for c in range(n_chunks):
        slot = lax.rem(base + c, DEC_SLOTS)
        nxt = c + ahead
        nslot = lax.rem(base + nxt, DEC_SLOTS)
        if nxt < n_chunks:
            start(b, nxt, nslot)
        else:
            @pl.when(b + 1 < pl.num_programs(0))
            def _():
                start(b + 1, nxt - n_chunks, nslot)
        for cp in copies(b, c, kbuf, vbuf, slot, sem):
            cp.wait()
        s = jnp.concatenate([_dot(q, kbuf[slot, i].astype(bf16)) for i in range(DEC_PAGES)],
                            axis=1) + bias_ref[0, :, c * width:(c + 1) * width]
        m_new = jnp.maximum(m, jnp.max(s, axis=1, keepdims=True))
        p = jnp.exp(s - m_new)
        alpha = jnp.exp(m - m_new)
        l = alpha * l + jnp.sum(p, axis=1, keepdims=True)
        pb = p.astype(bf16)
        pv = sum(_dot_nt(pb[:, i * PAGE_SIZE:(i + 1) * PAGE_SIZE], vbuf[slot, i].astype(bf16))
                 for i in range(DEC_PAGES))
        acc = alpha * acc + pv
        m = m_new
    o_ref[0] = acc / l


def _decode_attn(page_table, qmat, k_new, v_new, bias_new, bias, cache_kt, cache_vt, layer):
    s, n_pages = page_table.shape
    hb = bias.shape[1]
    n_chunks = n_pages // DEC_PAGES
    assert n_chunks * DEC_PAGES == n_pages and n_chunks >= DEC_SLOTS - 1
    row = lambda shape: pl.BlockSpec(shape, lambda b, pt: (b, 0, 0))
    hbm = pl.BlockSpec(memory_space=pl.ANY)
    width = DEC_PAGES * PAGE_SIZE
    grid_spec = pltpu.PrefetchScalarGridSpec(
        num_scalar_prefetch=1, grid=(s,),
        in_specs=[row((1, N_HEADS, KV_COLS)), row((1, 1, KV_COLS)), row((1, 1, KV_COLS)),
                  row((1, hb, LANES)), row((1, hb, n_pages * PAGE_SIZE)), hbm, hbm],
        out_specs=row((1, N_HEADS, KV_COLS)),
        scratch_shapes=[pltpu.VMEM((DEC_SLOTS, DEC_PAGES, KV_COLS, PAGE_SIZE), f32),
                        pltpu.VMEM((DEC_SLOTS, DEC_PAGES, KV_COLS, PAGE_SIZE), f32),
                        pltpu.SemaphoreType.DMA((2, DEC_SLOTS))])
    return pl.pallas_call(
        functools.partial(_decode_attn_kernel, layer=layer, n_chunks=n_chunks),
        grid_spec=grid_spec,
        out_shape=jax.ShapeDtypeStruct((s, N_HEADS, KV_COLS), f32),
        compiler_params=_cparams("arbitrary"),
        name="decode_attn",
    )(page_table, qmat, k_new, v_new, bias_new, bias, cache_kt, cache_vt)


def _pad_cols(w, n):
    return jnp.pad(w, ((0, 0), (0, n)))


def _decode_q(q_bf):
    s = q_bf.shape[0]
    onehot = (jnp.arange(N_HEADS)[:, None] // GROUP == jnp.arange(N_KV_HEADS)[None, :])
    q = q_bf.reshape(s, N_HEADS, 1, HEAD_DIM) * onehot[None, :, :, None].astype(q_bf.dtype)
    return q.reshape(s, N_HEADS, KV_COLS)


def _decode_o(o_full):
    s = o_full.shape[0]
    o = o_full.reshape(s, N_KV_HEADS, GROUP, N_KV_HEADS, HEAD_DIM)
    return jnp.stack([o[:, g, :, g, :] for g in range(N_KV_HEADS)], axis=1).reshape(s, -1)


def kernel(x_prompt, x_sample, cache_k_a, cache_v_a, cache_kidx_a, cache_k_b, cache_v_b,
           cache_logf_b, state_conv, page_table, c_prompt, c_sample, w_ada, b_ada, g_mix,
           g_ffn, w_in_a, w_o_a, w_in_b, b_f, w_o_b, w_up, conv_w, conv_b, w_down, g_final):
    bsz, t, d = x_prompt.shape
    s = x_sample.shape[0]
    depth = w_ada.shape[0]
    m = bsz * t
    past_len = page_table.shape[1] * PAGE_SIZE
    n_pool = cache_k_a.shape[1]
    tm = 512
    tq = 256

    pad_rows = (-(bsz + s)) % SUBLANES
    c_all = jnp.concatenate([c_prompt, c_sample, jnp.zeros((pad_rows, d), f32)], axis=0)
    mods = _adaln(c_all, w_ada, b_ada)
    mods_p = mods[:, :bsz].reshape(depth, bsz, 6, d)
    mods_s = mods[:, bsz:bsz + s].reshape(depth, s, 6, d)

    tabs_p = _rope_tables(jnp.arange(t))
    tabs_pt = _rope_tables_t(jnp.arange(t))
    tabs_s = _rope_tables(jnp.full((s,), past_len))
    zero_bf = jnp.zeros((1, LANES), f32)
    zero_col = jnp.zeros((N_HEADS, 1), f32)

    def rows_t(a):
        return a.reshape(bsz, N_KV_HEADS, HEAD_DIM, t).transpose(0, 3, 1, 2)

    xp = x_prompt.reshape(m, d)
    xs = x_sample.reshape(s, d)
    kv_shape_p = (bsz, t, N_KV_HEADS, HEAD_DIM)
    kv_shape_s = (s, 1, N_KV_HEADS, HEAD_DIM)
    ka_p, va_p, ia_p, kb_p, vb_p, lb_p, cv_p = [], [], [], [], [], [], []
    ka_s, va_s, ia_s, kb_s, vb_s, lb_s, cv_s = [], [], [], [], [], [], []

    def pages_t(cache):
        ct = jnp.moveaxis(cache, 2, -1)
        return ct.reshape(ct.shape[:2] + (-1, PAGE_SIZE))

    cache_k_a, cache_v_a, cache_kidx_a = pages_t(cache_k_a), pages_t(cache_v_a), pages_t(cache_kidx_a)
    cache_k_b, cache_v_b, cache_logf_b = pages_t(cache_k_b), pages_t(cache_v_b), pages_t(cache_logf_b)

    for i in range(depth):
        j = i // 2
        mp = [mods_p[i, :, c][:, None, :] for c in range(6)]
        ms = [mods_s[i, :, c][None] for c in range(6)]
        g_m = g_mix[i][None]
        if i % 2 == 0:
            w = w_in_a[j]
            n_qkvi = QKV_COLS + N_IDX_HEADS * IDX_DIM + IDX_DIM
            w_in = jnp.concatenate([_pad_cols(w[:, :n_qkvi], LANES - IDX_DIM),
                                    _pad_cols(w[:, n_qkvi:], LANES - N_IDX_HEADS)],
                                   axis=1).astype(bf16)
            w_o = w_o_a[j].astype(bf16)
            nq = N_HEADS * HEAD_DIM
            n_iq = QKV_COLS + N_IDX_HEADS * IDX_DIM
            wk = w[:, nq:nq + KV_COLS]
            w_row = jnp.concatenate([wk, _pad_cols(w[:, n_iq:n_qkvi], LANES - IDX_DIM)],
                                    axis=1).astype(bf16)
            w_t = jnp.concatenate([w[:, :nq], w[:, nq + KV_COLS:QKV_COLS], wk, w[:, QKV_COLS:n_qkvi],
                                   _pad_cols(w[:, n_qkvi:], IW_ROWS - N_IDX_HEADS)],
                                  axis=1).T.astype(bf16)
            qt, vt, kpad, kt32, vt32, iqt, iwt, ikb, ikt32 = _in_proj_t(
                xp, g_m, mp[0], mp[1], w_row, w_t, tabs_p, tabs_pt, zero_col, True, tm, t)
            xp = _dsa_attn(xp, mp[2], qt, kpad, vt, iqt, iwt, ikb.reshape(bsz, t, IDX_DIM),
                           w_o.T, tq, min(TOPK_MAX, t // 4))
            ka_p.append(rows_t(kt32))
            va_p.append(rows_t(vt32))
            ia_p.append(ikt32.transpose(0, 2, 1))
            q, k32, kb, v32, vb, iq, ik32, ikb, iw = _in_proj(
                xs, g_m, ms[0], ms[1], w_in, tabs_s, zero_bf, SEGS_A, OUTS_A, s, s)
            scores = _sample_scores(page_table, iq.reshape(s, N_IDX_HEADS, IDX_DIM),
                                    iw[:, :N_IDX_HEADS].reshape(s, N_IDX_HEADS, 1),
                                    cache_kidx_a, j)
            bias, bias_new = _sample_select(scores, iq, jnp.tile(ikb[:, :IDX_DIM], (1, N_IDX_HEADS)),
                                            iw, min(TOPK_MAX, (past_len + 1) // 4))
            o_full = _decode_attn(page_table, _decode_q(q), k32[:, None], v32[:, None],
                                  bias_new[:, None], bias[:, None], cache_k_a, cache_v_a, j)
            xs = _out_proj(xs, _decode_o(o_full).astype(bf16), w_o, ms[2], s, s)
            ka_s.append(k32.reshape(kv_shape_s))
            va_s.append(v32.reshape(kv_shape_s))
            ia_s.append(ik32[:, :IDX_DIM].reshape(s, 1, IDX_DIM))
        else:
            w_in = _pad_cols(w_in_b[j], LANES - N_HEADS).astype(bf16)
            w_o = w_o_b[j].astype(bf16)
            bf_pad = _pad_cols(b_f[j][None], LANES - N_HEADS)
            nq = N_HEADS * HEAD_DIM
            wk = w_in[:, nq:nq + KV_COLS]
            w_t = jnp.concatenate([w_in[:, :nq], w_in[:, nq + KV_COLS:QKV_COLS], wk,
                                   w_in[:, QKV_COLS:QKV_COLS + N_HEADS]], axis=1).T
            qt, vt, kpad, kt32, vt32, lft = _in_proj_t(
                xp, g_m, mp[0], mp[1], wk, w_t, tabs_p, tabs_pt, b_f[j][:, None], False, tm, t)
            c_t = _cumsum_t(lft, tq)
            c_pad = jnp.pad(c_t.transpose(0, 2, 1), ((0, 0), (0, 0), (0, LANES - N_HEADS)))
            xp = _fox_attn(xp, mp[2], qt, kpad, vt, c_pad, c_t, w_o.T, tq)
            kb_p.append(rows_t(kt32))
            vb_p.append(rows_t(vt32))
            lb_p.append(lft.transpose(0, 2, 1))
            q, k32, kb, v32, vb, lf = _in_proj(
                xs, g_m, ms[0], ms[1], w_in, tabs_s, bf_pad, SEGS_B, OUTS_B, s, s)
            bias = _suffix_bias(page_table, lf[:, :N_HEADS, None], cache_logf_b, j)
            o_full = _decode_attn(page_table, _decode_q(q), k32[:, None], v32[:, None],
                                  jnp.zeros((s, N_HEADS, LANES), f32), bias,
                                  cache_k_b, cache_v_b, j)
            xs = _out_proj(xs, _decode_o(o_full).astype(bf16), w_o, ms[2], s, s)
            kb_s.append(k32.reshape(kv_shape_s))
            vb_s.append(v32.reshape(kv_shape_s))
            lb_s.append(lf[:, :N_HEADS].reshape(s, 1, N_HEADS))

        wa = w_up[i][:, :D_FF].astype(bf16)
        wb = w_up[i][:, D_FF:].astype(bf16)
        wd = w_down[i].astype(bf16)
        cw = jnp.pad(conv_w[i], ((0, SUBLANES - CONV_W), (0, 0)))
        cb = conv_b[i][None]
        g_f = g_ffn[i][None]
        xp, tail = _ffn(xp, g_f, mp[3], mp[4], mp[5], wa, wb, cw, cb, wd, g_final[None],
                        i == depth - 1, tm, t)
        cv_p.append(tail.reshape(bsz, t // tm, SUBLANES, D_FF)[:, -1, SUBLANES - (CONV_W - 1):])
        xs, a_s = _ffn_sample(xs, g_f, ms[3], ms[4], ms[5], state_conv[i][:, 0],
                              state_conv[i][:, 1], wa, wb, cw, cb, wd, D_FF // 2)
        cv_s.append(jnp.stack([state_conv[i][:, 1], a_s], axis=1))

    y_prompt = xp.reshape(bsz, t, d)
    y_sample = _final_norm(xs, g_final[None], s).reshape(s, 1, d)
    return (y_prompt, y_sample,
            jnp.stack(ka_p), jnp.stack(va_p), jnp.stack(ia_p),
            jnp.stack(kb_p), jnp.stack(vb_p), jnp.stack(lb_p), jnp.stack(cv_p),
            jnp.stack(ka_s), jnp.stack(va_s), jnp.stack(ia_s),
            jnp.stack(kb_s), jnp.stack(vb_s), jnp.stack(lb_s), jnp.stack(cv_s))
```

```python
import functools

import jax
import jax.numpy as jnp
from jax import lax
from jax.experimental import pallas as pl
from jax.experimental.pallas import tpu as pltpu

f32 = jnp.float32
bf16 = jnp.bfloat16
i32 = jnp.int32

D_MODEL = 1024
N_HEADS = 16
HEAD_DIM = 64
N_KV_HEADS = 4
GROUP = N_HEADS // N_KV_HEADS
KV_COLS = N_KV_HEADS * HEAD_DIM
QKV_COLS = (N_HEADS + 2 * N_KV_HEADS) * HEAD_DIM
ROPE_DIMS = HEAD_DIM // 4
ROPE_HALF = ROPE_DIMS // 2
ROPE_THETA = 500000.0
N_IDX_HEADS = 8
IDX_DIM = 64
TOPK_MAX = 256
D_FF = 2816
CONV_W = 3
EPS = 1e-6
PAGE_SIZE = 128
Q_SCALE = HEAD_DIM ** -0.5
LOG2E = 1.4426950408889634

LANES = 128
SUBLANES = 8
VMEM_LIMIT = 56 * 1024 * 1024
NEG = -1e30
INT_MIN = -2 ** 31
INT_MAX = 2 ** 31 - 1
KEY_NEG_INF = (0xFF800000 ^ 0x7FFFFFFF) - 2 ** 32


def _cparams(*sem):
    return pltpu.CompilerParams(dimension_semantics=sem, vmem_limit_bytes=VMEM_LIMIT)


def _dot(a, b):
    return jnp.dot(a, b, preferred_element_type=f32)


def _dot_nt(a, b):
    return lax.dot_general(a, b, (((1,), (1,)), ((), ())), preferred_element_type=f32)


def _split3(x):
    hi = x.astype(bf16)
    r1 = x - hi.astype(f32)
    mid = r1.astype(bf16)
    lo = (r1 - mid.astype(f32)).astype(bf16)
    return hi, mid, lo


def _norm_mod(x, g, shift, scale):
    xn = x * lax.rsqrt(jnp.mean(x * x, axis=-1, keepdims=True) + EPS)
    return (xn * g) * (1.0 + scale) + shift


def _silu(x):
    return x * jax.nn.sigmoid(x)


def _sort_key(s):
    bits = pltpu.bitcast(s, i32)
    return bits ^ ((bits >> 31) & 0x7FFFFFFF)


def _adaln_kernel(c_ref, w_ref, b_ref, o_ref):
    a = _silu(c_ref[...]).astype(bf16)
    o_ref[0] = _dot(a, w_ref[0].astype(bf16)) + b_ref[0]


def _adaln(c_all, w_ada, b_ada):
    depth, d, n = w_ada.shape
    rows = c_all.shape[0]
    tn = 1536
    return pl.pallas_call(
        _adaln_kernel,
        grid=(depth, n // tn),
        in_specs=[pl.BlockSpec((rows, d), lambda l, j: (0, 0)),
                  pl.BlockSpec((1, d, tn), lambda l, j: (l, 0, j)),
                  pl.BlockSpec((1, 1, tn), lambda l, j: (l, 0, j))],
        out_specs=pl.BlockSpec((1, rows, tn), lambda l, j: (l, 0, j)),
        out_shape=jax.ShapeDtypeStruct((depth, rows, n), f32),
        compiler_params=_cparams("parallel", "parallel"),
        name="adaln",
    )(c_all, w_ada, b_ada.reshape(depth, 1, n))


SEGS_A = ((0, 1024, "rope", ((0, Q_SCALE),)),
          (1024, 256, "rope", ((1, 1.0), (2, 1.0))),
          (1280, 256, None, ((3, 1.0), (4, 1.0))),
          (1536, 512, "rope", ((5, 1.0),)),
          (2048, 128, "rope", ((6, 1.0), (7, 1.0))),
          (2176, 128, None, ((8, 1.0),)))
OUTS_A = ((1024, bf16), (256, f32), (256, bf16), (256, f32), (256, bf16),
          (512, bf16), (128, f32), (128, bf16), (128, f32))
SEGS_B = ((0, 1024, None, ((0, Q_SCALE),)),
          (1024, 256, None, ((1, 1.0), (2, 1.0))),
          (1280, 256, None, ((3, 1.0), (4, 1.0))),
          (1536, 128, "logf", ((5, 1.0),)))
OUTS_B = ((1024, bf16), (256, f32), (256, bf16), (256, f32), (256, bf16), (128, f32))


def _in_proj_kernel(x_ref, g_ref, shift_ref, scale_ref, w_ref, cos_ref, sa_ref, sb_ref,
                    bf_ref, *out_refs, segs):
    h = _norm_mod(x_ref[...], g_ref[...], shift_ref[0], scale_ref[0]).astype(bf16)
    for c0, width, epi, outs in segs:
        for c in range(width // LANES):
            lo = c0 + c * LANES
            a = _dot(h, w_ref[:, lo:lo + LANES])
            if epi == "rope":
                a = (a * cos_ref[...] + pltpu.roll(a, ROPE_HALF, 1) * sa_ref[...]
                     + pltpu.roll(a, LANES - ROPE_HALF, 1) * sb_ref[...])
            elif epi == "logf":
                z = a + bf_ref[...]
                a = jnp.minimum(z, 0.0) - jnp.log(1.0 + jnp.exp(-jnp.abs(z)))
            for oi, sc in outs:
                val = a if sc == 1.0 else a * sc
                out_refs[oi][:, c * LANES:(c + 1) * LANES] = val.astype(out_refs[oi].dtype)


def _in_proj(x, g, shift, scale, w, tabs, b_f, segs, outs, tm, rows_per_mod):
    m, d = x.shape
    n = w.shape[1]
    r = shift.shape[1]
    cos, sa, sb = tabs
    tab_blocks = cos.shape[0] // tm
    mod_spec = pl.BlockSpec((1, r, d), lambda i: (i * tm // rows_per_mod, 0, 0))
    tab_spec = pl.BlockSpec((tm, LANES), lambda i: (i % tab_blocks, 0))
    return pl.pallas_call(
        functools.partial(_in_proj_kernel, segs=segs),
        grid=(m // tm,),
        in_specs=[pl.BlockSpec((tm, d), lambda i: (i, 0)),
                  pl.BlockSpec((1, d), lambda i: (0, 0)),
                  mod_spec, mod_spec,
                  pl.BlockSpec((d, n), lambda i: (0, 0)),
                  tab_spec, tab_spec, tab_spec,
                  pl.BlockSpec((1, LANES), lambda i: (0, 0))],
        out_specs=[pl.BlockSpec((tm, wd), lambda i: (i, 0)) for wd, _ in outs],
        out_shape=[jax.ShapeDtypeStruct((m, wd), dt) for wd, dt in outs],
        compiler_params=_cparams("parallel"),
        name="in_proj",
    )(x, g, shift, scale, w, cos, sa, sb, b_f)


def _rope_tables(pos):
    inv_freq = ROPE_THETA ** (-jnp.arange(ROPE_HALF, dtype=f32) / ROPE_HALF)
    ang = pos.astype(f32)[:, None] * inv_freq[None, :]
    cos, sin = jnp.cos(ang), jnp.sin(ang)
    t = pos.shape[0]
    ones = jnp.ones((t, HEAD_DIM - ROPE_DIMS), f32)
    zeros = jnp.zeros((t, HEAD_DIM - ROPE_DIMS), f32)
    zh = jnp.zeros((t, ROPE_HALF), f32)
    c = jnp.concatenate([cos, cos, ones], axis=1)
    sa = jnp.concatenate([zh, sin, zeros], axis=1)
    sb = jnp.concatenate([-sin, zh, zeros], axis=1)
    rep = LANES // HEAD_DIM
    return tuple(jnp.tile(a, (1, rep)) for a in (c, sa, sb))


WT_Q = 0
WT_V = N_HEADS * HEAD_DIM
WT_K = WT_V + KV_COLS
WT_X = WT_K + KV_COLS
WT_IK = WT_X + N_IDX_HEADS * IDX_DIM
WT_IW = WT_IK + IDX_DIM
IW_ROWS = 2 * SUBLANES
ROW_BLOCK = 512


def _rope_rows(a, cos_t, sin_t):
    parts = []
    for b0 in range(0, a.shape[0], HEAD_DIM):
        x1 = a[b0:b0 + ROPE_HALF]
        x2 = a[b0 + ROPE_HALF:b0 + ROPE_DIMS]
        parts += [x1 * cos_t - x2 * sin_t, x1 * sin_t + x2 * cos_t, a[b0 + ROPE_DIMS:b0 + HEAD_DIM]]
    return jnp.concatenate(parts, axis=0)


def _in_proj_t_kernel(x_ref, g_ref, shift_ref, scale_ref, w_ref, wt_ref, cos_ref, sa_ref, sb_ref,
                      cost_ref, sint_ref, bf_ref, *outs, is_a):
    if is_a:
        qt_ref, vt_ref, kpad_ref, kt32_ref, vt32_ref, iqt_ref, iwt_ref, ikb_ref, ikt32_ref = outs
    else:
        qt_ref, vt_ref, kpad_ref, kt32_ref, vt32_ref, lft_ref = outs
    h = _norm_mod(x_ref[...], g_ref[...], shift_ref[0], scale_ref[0]).astype(bf16)
    cos_t, sin_t = cost_ref[...], sint_ref[...]

    def rows(r0, n):
        blk = _dot_nt(wt_ref[r0:r0 + n, :], h)
        return [blk[c * LANES:(c + 1) * LANES] for c in range(n // LANES)]

    def rope_cols(a):
        return (a * cos_ref[...] + pltpu.roll(a, ROPE_HALF, 1) * sa_ref[...]
                + pltpu.roll(a, LANES - ROPE_HALF, 1) * sb_ref[...])

    for r0 in range(0, N_HEADS * HEAD_DIM, ROW_BLOCK):
        for c, a in enumerate(rows(WT_Q + r0, ROW_BLOCK)):
            if is_a:
                a = _rope_rows(a, cos_t, sin_t)
            lo = r0 + c * LANES
            qt_ref[lo:lo + LANES, :] = (a * (Q_SCALE * LOG2E)).astype(bf16)
    for c, (v, k) in enumerate(zip(rows(WT_V, KV_COLS), rows(WT_K, KV_COLS))):
        sl = slice(c * LANES, (c + 1) * LANES)
        vt32_ref[0, sl, :] = v
        vt_ref[sl, :] = v.astype(bf16)
        kt32_ref[0, sl, :] = _rope_rows(k, cos_t, sin_t) if is_a else k
    if is_a:
        for c, a in enumerate(rows(WT_X, N_IDX_HEADS * IDX_DIM)):
            iqt_ref[c * LANES:(c + 1) * LANES, :] = _rope_rows(a, cos_t, sin_t).astype(bf16)
        ikt32_ref[0] = _rope_rows(_dot_nt(wt_ref[WT_IK:WT_IK + IDX_DIM, :], h), cos_t, sin_t)
        iwt_ref[...] = _dot_nt(wt_ref[WT_IW:WT_IW + IW_ROWS, :], h)
    else:
        z = _dot_nt(wt_ref[WT_X:WT_X + N_HEADS, :], h) + bf_ref[...]
        lft_ref[0] = jnp.minimum(z, 0.0) - jnp.log(1.0 + jnp.exp(-jnp.abs(z)))

    lane = lax.broadcasted_iota(i32, (1, LANES), 1)
    std = _dot(h, w_ref[...])
    for c in range(KV_COLS // LANES):
        a = std[:, c * LANES:(c + 1) * LANES]
        if is_a:
            a = rope_cols(a)
        kpad_ref[0, 2 * c] = jnp.where(lane < HEAD_DIM, a, 0.0).astype(bf16)
        kpad_ref[0, 2 * c + 1] = jnp.where(lane < HEAD_DIM, pltpu.roll(a, HEAD_DIM, 1),
                                           0.0).astype(bf16)
    if is_a:
        ikb_ref[...] = rope_cols(std[:, KV_COLS:KV_COLS + LANES])[:, :IDX_DIM].astype(bf16)


def _in_proj_t(x, g, shift, scale, w, wt, tabs, tabs_t, b_f, is_a, tm, t):
    m, d = x.shape
    bsz = m // t
    tpb = t // tm
    col = lambda rows_: pl.BlockSpec((rows_, tm), lambda i: (0, i))
    seq = lambda rows_: pl.BlockSpec((1, rows_, tm), lambda i: (i // tpb, 0, i % tpb))
    seq_shape = lambda rows_: jax.ShapeDtypeStruct((bsz, rows_, t), f32)
    const = lambda a: pl.BlockSpec(a.shape, lambda i: (0,) * a.ndim)
    mod_spec = pl.BlockSpec((1, 1, d), lambda i: (i // tpb, 0, 0))
    tab_spec = pl.BlockSpec((tm, LANES), lambda i: (i % tpb, 0))
    tabt_spec = pl.BlockSpec((ROPE_HALF, tm), lambda i: (0, i % tpb))
    nq = N_HEADS * HEAD_DIM
    out_specs = [col(nq), col(KV_COLS),
                 pl.BlockSpec((1, N_KV_HEADS, tm, LANES), lambda i: (i // tpb, 0, i % tpb, 0)),
                 seq(KV_COLS), seq(KV_COLS)]
    out_shape = [jax.ShapeDtypeStruct((nq, m), bf16), jax.ShapeDtypeStruct((KV_COLS, m), bf16),
                 jax.ShapeDtypeStruct((bsz, N_KV_HEADS, t, LANES), bf16),
                 seq_shape(KV_COLS), seq_shape(KV_COLS)]
    if is_a:
        out_specs += [col(N_IDX_HEADS * IDX_DIM), col(IW_ROWS),
                      pl.BlockSpec((tm, IDX_DIM), lambda i: (i, 0)), seq(IDX_DIM)]
        out_shape += [jax.ShapeDtypeStruct((N_IDX_HEADS * IDX_DIM, m), bf16),
                      jax.ShapeDtypeStruct((IW_ROWS, m), f32),
                      jax.ShapeDtypeStruct((m, IDX_DIM), bf16), seq_shape(IDX_DIM)]
    else:
        out_specs += [seq(N_HEADS)]
        out_shape += [seq_shape(N_HEADS)]
    return pl.pallas_call(
        functools.partial(_in_proj_t_kernel, is_a=is_a),
        grid=(m // tm,),
        in_specs=[pl.BlockSpec((tm, d), lambda i: (i, 0)), const(g), mod_spec, mod_spec,
                  const(w), const(wt), tab_spec, tab_spec, tab_spec, tabt_spec, tabt_spec,
                  const(b_f)],
        out_specs=out_specs, out_shape=out_shape,
        compiler_params=_cparams("parallel"),
        name="in_proj_t",
    )(x, g, shift, scale, w, wt, *tabs, *tabs_t, b_f)


def _rope_tables_t(pos):
    inv_freq = ROPE_THETA ** (-jnp.arange(ROPE_HALF, dtype=f32) / ROPE_HALF)
    ang = inv_freq[:, None] * pos.astype(f32)[None, :]
    return jnp.cos(ang), jnp.sin(ang)


AUG_STRIDE = 4
AUG_ONES = GROUP * AUG_STRIDE


ATTN_TK = 256
SUM_ROWS = 2 * SUBLANES
ACC_ROWS = HEAD_DIM + SUM_ROWS


def _t_init(m_sc, acc_sc):
    m_sc[...] = jnp.full(m_sc.shape, NEG, f32)
    acc_sc[...] = jnp.zeros(acc_sc.shape, f32)


def _t_step(s, vt1, h, m_sc, acc_sc):
    m_old = m_sc[h]
    m_new = jnp.maximum(m_old, jnp.max(s, axis=0, keepdims=True))
    p = jnp.exp2((s - m_new).astype(bf16))
    acc_sc[h] = jnp.exp2(m_old - m_new) * acc_sc[h] + _dot(vt1, p)
    m_sc[h] = m_new


STALE_LIMIT = 64.0


def _t_step_stale(s, vt1, h, m_sc, acc_sc, rise_sc):
    m_old = m_sc[h]
    p = jnp.exp2((s - m_old).astype(bf16))
    s_max = jnp.max(s, axis=0, keepdims=True)
    m_new = jnp.maximum(m_old, s_max)
    acc_sc[h] = (acc_sc[h] + _dot(vt1, p)) * jnp.exp2(m_old - m_new)
    m_sc[h] = m_new
    rise_sc[h] = jnp.maximum(rise_sc[h], s_max - m_old)


def _t_tile(off, tk, kt_of, vt_ref, qa_sc, post, step):
    kts = [kt_of(g) for g in range(N_KV_HEADS)]
    ones = jnp.ones((SUM_ROWS, tk), bf16)
    vts = [jnp.concatenate([vt_ref[g * HEAD_DIM:(g + 1) * HEAD_DIM, pl.ds(off, tk)], ones], axis=0)
           for g in range(N_KV_HEADS)]
    ahead = 2
    pending = [_dot(kts[g], qa_sc[g]) for g in range(ahead)]
    for g in range(N_KV_HEADS):
        s = pending.pop(0)
        if g + ahead < N_KV_HEADS:
            pending.append(_dot(kts[g + ahead], qa_sc[g + ahead]))
        step(post(s), vts[g], g)


def _t_sweep(first, rest, m_sc, acc_sc, rise_sc):
    def exact(s, vt1, g):
        _t_step(s, vt1, g, m_sc, acc_sc)

    def stale(s, vt1, g):
        _t_step_stale(s, vt1, g, m_sc, acc_sc, rise_sc)

    rise_sc[...] = jnp.full(rise_sc.shape, NEG, f32)
    _t_init(m_sc, acc_sc)
    first(exact)
    rest(stale)

    @pl.when(jnp.max(rise_sc[...]) > STALE_LIMIT)
    def _():
        _t_init(m_sc, acc_sc)
        first(exact)
        rest(exact)


def _t_fill_q(qa_sc, h, tq, q, aug):
    g, jj = divmod(h, GROUP)
    qa_sc[g, :HEAD_DIM, jj * tq:(jj + 1) * tq] = q
    qa_sc[g, HEAD_DIM:, jj * tq:(jj + 1) * tq] = aug


def _t_finish(tq, acc_sc, ot_sc):
    for h in range(N_HEADS):
        g, jj = divmod(h, GROUP)
        cols = slice(jj * tq, (jj + 1) * tq)
        ot_sc[h * HEAD_DIM:(h + 1) * HEAD_DIM, :] = (
            acc_sc[g, :HEAD_DIM, cols] / acc_sc[g, HEAD_DIM:HEAD_DIM + 1, cols]).astype(bf16)


def _t_out(x_ref, gate_ref, wot_ref, ot_sc, o_ref):
    y_t = _dot(wot_ref[...], ot_sc[...])
    o_ref[...] = x_ref[...] + gate_ref[0] * y_t.T


def _fox_attn_kernel(x_ref, gate_ref, qt_ref, kpad_ref, vt_ref, cpad_ref, ct_ref, wot_ref,
                     o_ref, kaug_sc, m_sc, rise_sc, acc_sc, qa_sc, ot_sc, *, tq, tk, chunk):
    qi = pl.program_id(1)
    t = kaug_sc.shape[1]

    @pl.when(qi == 0)
    def _():
        r = lax.broadcasted_iota(i32, (LANES, LANES), 0)
        l = lax.broadcasted_iota(i32, (LANES, LANES), 1) - HEAD_DIM
        lane = lax.broadcasted_iota(i32, (1, LANES), 1) - HEAD_DIM
        ones = jnp.where((lane >= AUG_ONES) & (lane < AUG_ONES + 3), 1.0, 0.0)
        for g in range(N_KV_HEADS):
            perm = [((l >= 0) & (l < AUG_ONES) & ((l >> 2) == r - g * GROUP)
                     & ((l & 3) == p)).astype(bf16) for p in range(3)]

            def body(i, c, g=g, perm=perm):
                off = pl.multiple_of(i * chunk, chunk)
                pieces = _split3(cpad_ref[0, pl.ds(off, chunk), :] * -LOG2E)
                aug = sum(_dot(pc, pm) for pc, pm in zip(pieces, perm)) + ones
                kaug_sc[g, pl.ds(off, chunk), :] = (
                    kpad_ref[0, g, pl.ds(off, chunk), :].astype(f32) + aug).astype(bf16)
                return c

            lax.fori_loop(0, t // chunk, body, 0)

    ri = lax.broadcasted_iota(i32, (HEAD_DIM, tq), 0)
    wide = (tk, GROUP * tq)
    d_iota = (lax.broadcasted_iota(i32, wide, 0)
              - (lax.broadcasted_iota(i32, wide, 1) & (tq - 1)))
    for h in range(N_HEADS):
        jj = h % GROUP
        cq = _split3(ct_ref[0, h:h + 1, :] * LOG2E)
        aug = jnp.where((ri >= jj * AUG_STRIDE) & (ri < jj * AUG_STRIDE + 3), 1.0, 0.0)
        for p in range(3):
            aug = jnp.where(ri == AUG_ONES + p, cq[p].astype(f32), aug)
        _t_fill_q(qa_sc, h, tq, qt_ref[h * HEAD_DIM:(h + 1) * HEAD_DIM, :], aug.astype(bf16))
    def tile(j, post, step):
        off = pl.multiple_of(j * tk, tk)
        _t_tile(off, tk, lambda g: kaug_sc[g, pl.ds(off, tk), :], vt_ref, qa_sc, post, step)

    sub = tq // tk

    def diagonal(step):
        for dd in range(sub):
            tile(qi * sub + dd, lambda s, dd=dd: jnp.where(d_iota <= -dd * tk, s, NEG), step)

    def earlier(step):
        def body(i, c):
            tile(qi * sub - 1 - i, lambda s: s, step)
            return c
        lax.fori_loop(0, qi * sub, body, 0)

    _t_sweep(diagonal, earlier, m_sc, acc_sc, rise_sc)
    _t_finish(tq, acc_sc, ot_sc)
    _t_out(x_ref, gate_ref, wot_ref, ot_sc, o_ref)


def _attn_specs(bsz, t, d, tq):
    nq = t // tq
    return dict(
        x=pl.BlockSpec((tq, d), lambda b, q: (b * nq + q, 0)),
        gate=pl.BlockSpec((1, 1, d), lambda b, q: (b, 0, 0)),
        qt=pl.BlockSpec((N_HEADS * HEAD_DIM, tq), lambda b, q: (0, b * nq + q)),
        kpad=pl.BlockSpec((1, N_KV_HEADS, t, LANES), lambda b, q: (b, 0, 0, 0)),
        vt=pl.BlockSpec((KV_COLS, t), lambda b, q: (0, b)),
        wot=pl.BlockSpec((d, N_HEADS * HEAD_DIM), lambda b, q: (0, 0)),
        scratch=[pltpu.VMEM((N_KV_HEADS, 1, GROUP * tq), f32),
                 pltpu.VMEM((N_KV_HEADS, 1, GROUP * tq), f32),
                 pltpu.VMEM((N_KV_HEADS, ACC_ROWS, GROUP * tq), f32),
                 pltpu.VMEM((N_KV_HEADS, LANES, GROUP * tq), bf16),
                 pltpu.VMEM((N_HEADS * HEAD_DIM, tq), bf16)])


def _fox_attn(x, gate, qt, kpad, vt, c_pad, c_t, wot, tq):
    bsz, _, t, _ = kpad.shape
    m, d = x.shape
    sp = _attn_specs(bsz, t, d, tq)
    return pl.pallas_call(
        functools.partial(_fox_attn_kernel, tq=tq, tk=ATTN_TK, chunk=512),
        grid=(bsz, t // tq),
        in_specs=[sp["x"], sp["gate"], sp["qt"], sp["kpad"], sp["vt"],
                  pl.BlockSpec((1, t, LANES), lambda b, q: (b, 0, 0)),
                  pl.BlockSpec((1, N_HEADS, tq), lambda b, q: (b, 0, q)),
                  sp["wot"]],
        out_specs=sp["x"],
        out_shape=jax.ShapeDtypeStruct((m, d), f32),
        scratch_shapes=[pltpu.VMEM((N_KV_HEADS, t, LANES), bf16)] + sp["scratch"],
        compiler_params=_cparams("parallel", "arbitrary"),
        name="fox_attn",
    )(x, gate, qt, kpad, vt, c_pad, c_t, wot)


def _cumsum_t_kernel(lft_ref, ct_ref, carry_sc, *, tb):
    @pl.when(pl.program_id(1) == 0)
    def _():
        carry_sc[...] = jnp.zeros(carry_sc.shape, f32)

    tri = (lax.broadcasted_iota(i32, (tb, tb), 0)
           <= lax.broadcasted_iota(i32, (tb, tb), 1)).astype(bf16)
    lft = lft_ref[0]
    csum = sum(_dot(p, tri) for p in _split3(lft))
    ct_ref[0] = csum + carry_sc[...]
    carry_sc[...] = carry_sc[...] + jnp.sum(lft, axis=1, keepdims=True)


def _cumsum_t(lf_t, tb):
    b, nh, t = lf_t.shape
    spec = pl.BlockSpec((1, nh, tb), lambda bi, j: (bi, 0, j))
    return pl.pallas_call(
        functools.partial(_cumsum_t_kernel, tb=tb),
        grid=(b, t // tb),
        in_specs=[spec],
        out_specs=spec,
        out_shape=jax.ShapeDtypeStruct((b, nh, t), f32),
        scratch_shapes=[pltpu.VMEM((nh, 1), f32)],
        compiler_params=_cparams("parallel", "arbitrary"),
        name="cumsum_t",
    )(lf_t)


def _kth_threshold(count_ge, shape, n_sel):
    zero = jnp.zeros(shape, i32)
    ans = jnp.where(count_ge(zero) >= n_sel, zero, jnp.full(shape, INT_MIN, i32))

    def bit_body(i, ans):
        cand = ans | jnp.left_shift(jnp.int32(1), 30 - i)
        return jnp.where(count_ge(cand) >= n_sel, cand, ans)

    return lax.fori_loop(0, 31, bit_body, ans)


BF16_SUBLANES = 2 * SUBLANES


def _kth_two_level(count_coarse_ge, count_ge, shape, n_sel):
    def coarse_value(a):
        pattern = a ^ ((a >> 31) & 0x7FFF)
        return pltpu.bitcast(pattern << 16, f32).astype(bf16)

    zero = jnp.zeros(shape, i32)
    ans = jnp.where(count_coarse_ge(coarse_value(zero)) >= n_sel, zero,
                    jnp.full(shape, -2 ** 15, i32))

    def bit_body(i, ans):
        cand = ans | jnp.left_shift(jnp.int32(1), 14 - i)
        return jnp.where(count_coarse_ge(coarse_value(cand)) >= n_sel, cand, ans)

    ans = lax.fori_loop(0, 15, bit_body, ans)
    pattern = (ans ^ ((ans >> 31) & 0x7FFF)) << 16
    centre = pattern ^ ((pattern >> 31) & 0x7FFFFFFF)
    tiny = (ans >= -2 ** 7) & (ans < 2 ** 7)
    centre = jnp.where(tiny, 0, centre)
    half = jnp.where(tiny, 2 ** 23, 2 ** 15)

    def cond(state):
        i, _, _, _, n_open = state
        return (i <= 24) & (n_open > 0)

    def body(state):
        i, lo, hi, exact, _ = state
        live = (exact == 0) & (lo < hi)
        mid = lo + ((hi - lo + 1) >> 1)
        c = count_ge(mid)
        ge = c >= n_sel
        lo = jnp.where(live & ge, mid, lo)
        hi = jnp.where(live & jnp.logical_not(ge), mid - 1, hi)
        exact = jnp.where(live & (c == n_sel), 1, exact)
        n_open = jnp.sum(jnp.where((exact == 0) & (lo < hi), 1, 0))
        return i + 1, lo, hi, exact, n_open

    state = (jnp.int32(0), centre - half, centre + half, zero, jnp.int32(1))
    _, lo, _, exact, _ = lax.while_loop(cond, body, state)
    return lo, exact


def _tie_cut(count_tied_below, need, shape, n_bits):
    def body(i, p):
        cand = p | jnp.left_shift(jnp.int32(1), n_bits - 1 - i)
        return jnp.where(count_tied_below(cand) < need, cand, p)
    return lax.fori_loop(0, n_bits, body, jnp.zeros(shape, i32))


def _fold_lanes(ind, c):
    for cc in range(ind.shape[1] // LANES):
        c = c + ind[:, cc * LANES:(cc + 1) * LANES]
    return c


def _dsa_attn_kernel(x_ref, gate_ref, qt_ref, kpad_ref, vt_ref, iqt_ref, iwt_ref, ik_ref,
                     wot_ref, o_ref, key_sc, coarse_sc, bias_sc, cut_sc, m_sc, rise_sc, acc_sc,
                     qa_sc, ot_sc,
                     *, tq, n_sel, n_bits):
    qi = pl.program_id(1)
    n_k = qi + 1
    row = lax.broadcasted_iota(i32, (tq, tq), 0)
    d_iota = row - lax.broadcasted_iota(i32, (tq, tq), 1)

    def score_body(j, c):
        off = pl.multiple_of(j * tq, tq)
        ikt = ik_ref[0, pl.ds(off, tq), :]
        s = jnp.zeros((tq, tq), f32)
        for h in range(N_IDX_HEADS):
            dots = _dot(ikt, iqt_ref[h * IDX_DIM:(h + 1) * IDX_DIM, :])
            s = s + jnp.maximum(dots, 0.0) * iwt_ref[h:h + 1, :]
        causal = d_iota <= (qi - j) * tq
        key_sc[pl.ds(off, tq), :] = jnp.where(causal, _sort_key(s), KEY_NEG_INF)
        coarse_sc[pl.ds(off, tq), :] = jnp.where(causal, s, -jnp.inf).astype(bf16)
        return c

    lax.fori_loop(0, n_k, score_body, 0)

    def count_coarse_ge(v):
        def body(j, c):
            off = pl.multiple_of(j * tq, tq)
            ind = jnp.where(coarse_sc[pl.ds(off, tq), :] >= v, jnp.ones((), bf16),
                            jnp.zeros((), bf16))
            for r in range(0, tq, BF16_SUBLANES):
                c = c + ind[r:r + BF16_SUBLANES]
            return c
        c = lax.fori_loop(0, n_k, body, jnp.zeros((BF16_SUBLANES, tq), bf16))
        return jnp.sum(c.astype(f32), axis=0, keepdims=True)

    def count(pred):
        def body(j, c):
            off = pl.multiple_of(j * tq, tq)
            ind = jnp.where(pred(key_sc[pl.ds(off, tq), :], j), 1, 0)
            for r in range(0, tq, SUBLANES):
                c = c + ind[r:r + SUBLANES]
            return c
        c = lax.fori_loop(0, n_k, body, jnp.zeros((SUBLANES, tq), i32))
        return jnp.sum(c, axis=0, keepdims=True)

    thr, exact = _kth_two_level(count_coarse_ge, lambda cand: count(lambda kt, j: kt >= cand),
                                (1, tq), n_sel)
    keep_all = n_k * tq <= n_sel
    thr = jnp.where(keep_all, INT_MIN, thr)
    exact = jnp.where(keep_all, 1, exact)
    real = thr > KEY_NEG_INF
    cut_sc[...] = jnp.where(real, INT_MAX, -1)

    @pl.when(jnp.max(jnp.where(real, 1 - exact, 0)) > 0)
    def _():
        need = n_sel - count(lambda kt, j: kt > thr)
        cut = _tie_cut(lambda cand: count(lambda kt, j: (kt == thr) & (row + j * tq < cand)),
                       need, (1, tq), n_bits)
        cut_sc[...] = jnp.where(real, jnp.where(exact == 1, INT_MAX, cut), -1)

    cut = cut_sc[...]
    floor = jnp.maximum(thr, KEY_NEG_INF)

    def bias_body(j, c):
        off = pl.multiple_of(j * tq, tq)
        kt = key_sc[pl.ds(off, tq), :]
        sel = (kt > floor) | ((kt == thr) & (row + j * tq <= cut))
        bias_sc[pl.ds(off, tq), :] = jnp.where(sel, 0.0, NEG)
        return c

    lax.fori_loop(0, n_k, bias_body, 0)

    for h in range(N_HEADS):
        _t_fill_q(qa_sc, h, tq, qt_ref[h * HEAD_DIM:(h + 1) * HEAD_DIM, :],
                  jnp.zeros((LANES - HEAD_DIM, tq), bf16))
    def tile(j, step):
        off = pl.multiple_of(j * tq, tq)

        def masked(s):
            return jnp.concatenate([s[:, jj * tq:(jj + 1) * tq] + bias_sc[pl.ds(off, tq), :]
                                    for jj in range(GROUP)], axis=1)

        _t_tile(off, tq, lambda g: kpad_ref[0, g, pl.ds(off, tq), :], vt_ref, qa_sc, masked, step)

    def oldest(step):
        tile(0, step)

    def later(step):
        def body(j, c):
            tile(j, step)
            return c
        lax.fori_loop(1, n_k, body, 0)

    _t_sweep(oldest, later, m_sc, acc_sc, rise_sc)
    _t_finish(tq, acc_sc, ot_sc)
    _t_out(x_ref, gate_ref, wot_ref, ot_sc, o_ref)


def _dsa_attn(x, gate, qt, kpad, vt, iqt, iwt, ik, wot, tq, n_sel):
    bsz, _, t, _ = kpad.shape
    m, d = x.shape
    nq = t // tq
    sp = _attn_specs(bsz, t, d, tq)
    assert tq >= n_sel and t // BF16_SUBLANES <= 256
    return pl.pallas_call(
        functools.partial(_dsa_attn_kernel, tq=tq, n_sel=n_sel, n_bits=(t - 1).bit_length()),
        grid=(bsz, nq),
        in_specs=[sp["x"], sp["gate"], sp["qt"], sp["kpad"], sp["vt"],
                  pl.BlockSpec((N_IDX_HEADS * IDX_DIM, tq), lambda b, q: (0, b * nq + q)),
                  pl.BlockSpec((2 * SUBLANES, tq), lambda b, q: (0, b * nq + q)),
                  pl.BlockSpec((1, t, IDX_DIM), lambda b, q: (b, 0, 0)),
                  sp["wot"]],
        out_specs=sp["x"],
        out_shape=jax.ShapeDtypeStruct((m, d), f32),
        scratch_shapes=[pltpu.VMEM((t, tq), i32), pltpu.VMEM((t, tq), bf16),
                        pltpu.VMEM((t, tq), f32), pltpu.VMEM((1, tq), i32)] + sp["scratch"],
        compiler_params=_cparams("parallel", "arbitrary"),
        name="dsa_attn",
    )(x, gate, qt, kpad, vt, iqt, iwt, ik, wot)


def _out_proj_kernel(x_ref, o_ref, w_ref, gate_ref, y_ref):
    y_ref[...] = x_ref[...] + gate_ref[0] * _dot(o_ref[...], w_ref[...])


def _out_proj(x, o, w, gate, tm, rows_per_mod):
    m, d = x.shape
    kdim = o.shape[1]
    r = gate.shape[1]
    return pl.pallas_call(
        _out_proj_kernel,
        grid=(m // tm,),
        in_specs=[pl.BlockSpec((tm, d), lambda i: (i, 0)),
                  pl.BlockSpec((tm, kdim), lambda i: (i, 0)),
                  pl.BlockSpec((kdim, d), lambda i: (0, 0)),
                  pl.BlockSpec((1, r, d), lambda i: (i * tm // rows_per_mod, 0, 0))],
        out_specs=pl.BlockSpec((tm, d), lambda i: (i, 0)),
        out_shape=jax.ShapeDtypeStruct((m, d), f32),
        compiler_params=_cparams("parallel"),
        name="out_proj",
    )(x, o, w, gate)


FFN_CHUNK = 256
FFN_HALO = 2 * SUBLANES


def _ffn_kernel(x_ref, xh_ref, g_ref, shift_ref, scale_ref, gate_ref, wa_ref, wb_ref,
                cw_ref, cb_ref, wd_ref, gfin_ref, o_ref, tail_ref, acc_sc, hext_sc,
                *, tm, tiles_per_seq, final):
    i = pl.program_id(0)
    x = x_ref[...]
    keep = jnp.where(i % tiles_per_seq == 0, 0.0, 1.0)
    hext_sc[:FFN_HALO, :] = (_norm_mod(xh_ref[...], g_ref[...], shift_ref[0], scale_ref[0])
                             * keep).astype(bf16)
    hext_sc[FFN_HALO:, :] = _norm_mod(x, g_ref[...], shift_ref[0], scale_ref[0]).astype(bf16)
    for c in range(D_FF // FFN_CHUNK):
        sl = slice(c * FFN_CHUNK, (c + 1) * FFN_CHUNK)
        a_ext = _dot(hext_sc[...], wa_ref[:, sl])
        b = _dot(hext_sc[FFN_HALO:, :], wb_ref[:, sl])
        a = a_ext[FFN_HALO:]
        p1 = pltpu.roll(a_ext, 1, 0)[FFN_HALO:]
        p2 = pltpu.roll(a_ext, 2, 0)[FFN_HALO:]
        conv = cb_ref[:, sl] + p2 * cw_ref[0:1, sl] + p1 * cw_ref[1:2, sl] + a * cw_ref[2:3, sl]
        y = _dot((_silu(conv) * b).astype(bf16), wd_ref[sl, :])
        if c == 0:
            acc_sc[...] = y
        else:
            acc_sc[...] += y
        tail_ref[0, :, sl] = a_ext[FFN_HALO + tm - SUBLANES:]
    y = x + gate_ref[0] * acc_sc[...]
    if final:
        y = y * lax.rsqrt(jnp.mean(y * y, axis=-1, keepdims=True) + EPS) * gfin_ref[...]
    o_ref[...] = y


def _ffn(x, g, shift, scale, gate, wa, wb, cw, cb, wd, g_final, final, tm, rows_per_seq):
    m, d = x.shape
    n_tiles = m // tm
    hb = tm // FFN_HALO
    mod_spec = pl.BlockSpec((1, 1, d), lambda i: (i * tm // rows_per_seq, 0, 0))
    const = lambda shape: pl.BlockSpec(shape, lambda i: (0,) * len(shape))
    return pl.pallas_call(
        functools.partial(_ffn_kernel, tm=tm, tiles_per_seq=rows_per_seq // tm, final=final),
        grid=(n_tiles,),
        in_specs=[pl.BlockSpec((tm, d), lambda i: (i, 0)),
                  pl.BlockSpec((FFN_HALO, d), lambda i: (jnp.maximum(i * hb - 1, 0), 0)),
                  const((1, d)), mod_spec, mod_spec, mod_spec,
                  const((d, D_FF)), const((d, D_FF)), const((SUBLANES, D_FF)),
                  const((1, D_FF)), const((D_FF, d)), const((1, d))],
        out_specs=[pl.BlockSpec((tm, d), lambda i: (i, 0)),
                   pl.BlockSpec((1, SUBLANES, D_FF), lambda i: (i, 0, 0))],
        out_shape=[jax.ShapeDtypeStruct((m, d), f32),
                   jax.ShapeDtypeStruct((n_tiles, SUBLANES, D_FF), f32)],
        scratch_shapes=[pltpu.VMEM((tm, d), f32), pltpu.VMEM((FFN_HALO + tm, d), bf16)],
        compiler_params=_cparams("parallel"),
        name="ffn",
    )(x, x, g, shift, scale, gate, wa, wb, cw, cb, wd, g_final)


def _ffn_sample_kernel(x_ref, g_ref, shift_ref, scale_ref, gate_ref, p2_ref, p1_ref, wa_ref,
                       wb_ref, cw_ref, cb_ref, wd_ref, o_ref, a_ref, acc_sc):
    j = pl.program_id(0)

    @pl.when(j == 0)
    def _():
        acc_sc[...] = jnp.zeros(acc_sc.shape, f32)

    x = x_ref[...]
    h = _norm_mod(x, g_ref[...], shift_ref[0], scale_ref[0]).astype(bf16)
    a = _dot(h, wa_ref[...])
    b = _dot(h, wb_ref[...])
    conv = (cb_ref[...] + p2_ref[...] * cw_ref[0:1, :] + p1_ref[...] * cw_ref[1:2, :]
            + a * cw_ref[2:3, :])
    acc_sc[...] += _dot((_silu(conv) * b).astype(bf16), wd_ref[...])
    a_ref[...] = a

    @pl.when(j == pl.num_programs(0) - 1)
    def _():
        o_ref[...] = x + gate_ref[0] * acc_sc[...]


def _ffn_sample(x, g, shift, scale, gate, p2, p1, wa, wb, cw, cb, wd, tn):
    m, d = x.shape
    const = lambda shape: pl.BlockSpec(shape, lambda j: (0,) * len(shape))
    return pl.pallas_call(
        _ffn_sample_kernel,
        grid=(D_FF // tn,),
        in_specs=[const((m, d)), const((1, d)), const((1, m, d)), const((1, m, d)),
                  const((1, m, d)),
                  pl.BlockSpec((m, tn), lambda j: (0, j)), pl.BlockSpec((m, tn), lambda j: (0, j)),
                  pl.BlockSpec((d, tn), lambda j: (0, j)), pl.BlockSpec((d, tn), lambda j: (0, j)),
                  pl.BlockSpec((SUBLANES, tn), lambda j: (0, j)),
                  pl.BlockSpec((1, tn), lambda j: (0, j)),
                  pl.BlockSpec((tn, d), lambda j: (j, 0))],
        out_specs=[const((m, d)), pl.BlockSpec((m, tn), lambda j: (0, j))],
        out_shape=[jax.ShapeDtypeStruct((m, d), f32), jax.ShapeDtypeStruct((m, D_FF), f32)],
        scratch_shapes=[pltpu.VMEM((m, d), f32)],
        compiler_params=_cparams("arbitrary"),
        name="ffn_sample",
    )(x, g, shift, scale, gate, p2, p1, wa, wb, cw, cb, wd)


def _final_norm_kernel(x_ref, g_ref, o_ref):
    x = x_ref[...]
    o_ref[...] = x * lax.rsqrt(jnp.mean(x * x, axis=-1, keepdims=True) + EPS) * g_ref[...]


def _final_norm(x, g, tm):
    m, d = x.shape
    return pl.pallas_call(
        _final_norm_kernel,
        grid=(m // tm,),
        in_specs=[pl.BlockSpec((tm, d), lambda i: (i, 0)), pl.BlockSpec((1, d), lambda i: (0, 0))],
        out_specs=pl.BlockSpec((tm, d), lambda i: (i, 0)),
        out_shape=jax.ShapeDtypeStruct((m, d), f32),
        compiler_params=_cparams("parallel"),
        name="final_norm",
    )(x, g)


def _seq_page_copies(pt_ref, hbm, layer, seq, buf, slot, sem, n_pages):
    return [pltpu.make_async_copy(hbm.at[layer, pt_ref[seq, p]],
                                  buf.at[slot, :, pl.ds(p * PAGE_SIZE, PAGE_SIZE)], sem.at[slot])
            for p in range(n_pages)]


def _gather_seq(pt_ref, hbm, layer, buf, sem, n_pages):
    b = pl.program_id(0)
    slot = b % 2

    @pl.when(b == 0)
    def _():
        for cp in _seq_page_copies(pt_ref, hbm, layer, 0, buf, 0, sem, n_pages):
            cp.start()

    @pl.when(b + 1 < pl.num_programs(0))
    def _():
        for cp in _seq_page_copies(pt_ref, hbm, layer, b + 1, buf, 1 - slot, sem, n_pages):
            cp.start()

    for cp in _seq_page_copies(pt_ref, hbm, layer, b, buf, slot, sem, n_pages):
        cp.wait()
    return slot


def _sample_score_kernel(pt_ref, iq_ref, iw_ref, kidx_hbm, s_ref, buf, sem, *, layer, n_pages):
    slot = _gather_seq(pt_ref, kidx_hbm, layer, buf, sem, n_pages)
    d = _dot(iq_ref[0], buf[slot].astype(bf16))
    s_ref[0] = jnp.sum(jnp.maximum(d, 0.0) * iw_ref[0], axis=0, keepdims=True)


def _sample_scores(page_table, iq, iw, cache_kidx_t, layer):
    s, n_pages = page_table.shape
    length = n_pages * PAGE_SIZE
    grid_spec = pltpu.PrefetchScalarGridSpec(
        num_scalar_prefetch=1, grid=(s,),
        in_specs=[pl.BlockSpec((1, N_IDX_HEADS, IDX_DIM), lambda b, pt: (b, 0, 0)),
                  pl.BlockSpec((1, N_IDX_HEADS, 1), lambda b, pt: (b, 0, 0)),
                  pl.BlockSpec(memory_space=pl.ANY)],
        out_specs=pl.BlockSpec((1, 1, length), lambda b, pt: (b, 0, 0)),
        scratch_shapes=[pltpu.VMEM((2, IDX_DIM, length), f32), pltpu.SemaphoreType.DMA((2,))])
    out = pl.pallas_call(
        functools.partial(_sample_score_kernel, layer=layer, n_pages=n_pages),
        grid_spec=grid_spec,
        out_shape=jax.ShapeDtypeStruct((s, 1, length), f32),
        compiler_params=_cparams("arbitrary"),
        name="sample_scores",
    )(page_table, iq, iw, cache_kidx_t)
    return out.reshape(s, length)


SEL_TILE = 1024


def _sample_select_kernel(s_ref, iq_ref, ikt_ref, iw_ref, bias_ref, bnew_ref, key_sc,
                          *, n_sel, n_bits):
    rows, length = s_ref.shape
    n_tiles = length // SEL_TILE
    prod = iq_ref[...].astype(f32) * ikt_ref[...].astype(f32)
    grp = (lax.broadcasted_iota(i32, (prod.shape[1], LANES), 0) // IDX_DIM
           == lax.broadcasted_iota(i32, (prod.shape[1], LANES), 1)).astype(bf16)
    d_new = sum(_dot(p, grp) for p in _split3(prod))
    s_new = jnp.sum(jnp.maximum(d_new, 0.0) * iw_ref[...], axis=1, keepdims=True)
    key_new = _sort_key(s_new)
    key_sc[...] = _sort_key(s_ref[...])
    col = lax.broadcasted_iota(i32, (rows, SEL_TILE), 1)

    def count(pred, pred_new):
        def body(j, c):
            off = pl.multiple_of(j * SEL_TILE, SEL_TILE)
            return _fold_lanes(jnp.where(pred(key_sc[:, pl.ds(off, SEL_TILE)], j), 1, 0), c)
        c = lax.fori_loop(0, n_tiles, body, jnp.zeros((rows, LANES), i32))
        return jnp.sum(c, axis=1, keepdims=True) + jnp.where(pred_new, 1, 0)

    thr = _kth_threshold(lambda cand: count(lambda kt, j: kt >= cand, key_new >= cand),
                         (rows, 1), n_sel)
    need = n_sel - count(lambda kt, j: kt > thr, key_new > thr)
    cut = _tie_cut(lambda cand: count(lambda kt, j: (kt == thr) & (col + j * SEL_TILE < cand),
                                      (key_new == thr) & (length < cand)),
                   need, (rows, 1), n_bits)

    def bias_body(j, c):
        off = pl.multiple_of(j * SEL_TILE, SEL_TILE)
        kt = key_sc[:, pl.ds(off, SEL_TILE)]
        sel = (kt > thr) | ((kt == thr) & (col + j * SEL_TILE <= cut))
        bias_ref[:, pl.ds(off, SEL_TILE)] = jnp.where(sel, 0.0, NEG)
        return c

    lax.fori_loop(0, n_tiles, bias_body, 0)
    sel_new = (key_new > thr) | ((key_new == thr) & (length <= cut))
    bnew_ref[...] = jnp.broadcast_to(jnp.where(sel_new, 0.0, NEG), bnew_ref.shape)


def _sample_select(scores, iq, ik_tiled, iw, n_sel):
    rows, length = scores.shape
    return pl.pallas_call(
        functools.partial(_sample_select_kernel, n_sel=n_sel, n_bits=length.bit_length()),
        out_shape=[jax.ShapeDtypeStruct((rows, length), f32),
                   jax.ShapeDtypeStruct((rows, LANES), f32)],
        scratch_shapes=[pltpu.VMEM((rows, length), i32)],
        compiler_params=pltpu.CompilerParams(vmem_limit_bytes=VMEM_LIMIT),
        name="sample_select",
    )(scores, iq, ik_tiled, iw)


SUFFIX_CHUNK = 256


def _suffix_bias_kernel(pt_ref, lfnew_ref, lf_hbm, bias_ref, buf, sem, *, layer, n_pages):
    slot = _gather_seq(pt_ref, lf_hbm, layer, buf, sem, n_pages)
    w = SUFFIX_CHUNK
    tri = (lax.broadcasted_iota(i32, (w, w), 0)
           > lax.broadcasted_iota(i32, (w, w), 1)).astype(bf16)
    carry = lfnew_ref[0]
    for c in reversed(range(n_pages * PAGE_SIZE // w)):
        x = buf[slot, :, c * w:(c + 1) * w]
        bias_ref[0, :, c * w:(c + 1) * w] = sum(_dot(p, tri) for p in _split3(x)) + carry
        carry = carry + jnp.sum(x, axis=1, keepdims=True)


def _suffix_bias(page_table, lf_new, cache_logf_t, layer):
    s, n_pages = page_table.shape
    length = n_pages * PAGE_SIZE
    grid_spec = pltpu.PrefetchScalarGridSpec(
        num_scalar_prefetch=1, grid=(s,),
        in_specs=[pl.BlockSpec((1, N_HEADS, 1), lambda b, pt: (b, 0, 0)),
                  pl.BlockSpec(memory_space=pl.ANY)],
        out_specs=pl.BlockSpec((1, N_HEADS, length), lambda b, pt: (b, 0, 0)),
        scratch_shapes=[pltpu.VMEM((2, N_HEADS, length), f32), pltpu.SemaphoreType.DMA((2,))])
    return pl.pallas_call(
        functools.partial(_suffix_bias_kernel, layer=layer, n_pages=n_pages),
        grid_spec=grid_spec,
        out_shape=jax.ShapeDtypeStruct((s, N_HEADS, length), f32),
        compiler_params=_cparams("arbitrary"),
        name="suffix_bias",
    )(page_table, lf_new, cache_logf_t)


DEC_PAGES = 8
DEC_SLOTS = 3


def _kv_chunk_copies(pt_ref, k_hbm, v_hbm, layer, seq, chunk, kbuf, vbuf, slot, sem):
    cps = []
    for i in range(DEC_PAGES):
        page = pt_ref[seq, chunk * DEC_PAGES + i]
        cps.append(pltpu.make_async_copy(k_hbm.at[layer, page], kbuf.at[slot, i], sem.at[0, slot]))
        cps.append(pltpu.make_async_copy(v_hbm.at[layer, page], vbuf.at[slot, i], sem.at[1, slot]))
    return cps


def _decode_attn_kernel(pt_ref, q_ref, knew_ref, vnew_ref, bnew_ref, bias_ref, k_hbm, v_hbm,
                        o_ref, kbuf, vbuf, sem, *, layer, n_chunks):
    b = pl.program_id(0)
    q = q_ref[0]
    width = DEC_PAGES * PAGE_SIZE
    copies = functools.partial(_kv_chunk_copies, pt_ref, k_hbm, v_hbm, layer)

    def start(seq, chunk, slot):
        for cp in copies(seq, chunk, kbuf, vbuf, slot, sem):
            cp.start()

    ahead = DEC_SLOTS - 1
    base = b * n_chunks

    @pl.when(b == 0)
    def _():
        for c in range(ahead):
            start(0, c, c)

    kn = knew_ref[0].astype(bf16).astype(f32)
    m = jnp.sum(q.astype(f32) * kn, axis=1, keepdims=True) + bnew_ref[0][:, 0:1]
    l = jnp.ones_like(m)
    acc = jnp.broadcast_to(vnew_ref[0].astype(bf16).astype(f32), (N_HEADS, KV_COLS))

    for c in range(n_chunks):
        slot = lax.rem(base + c, DEC_SLOTS)
        nxt = c + ahead
        nslot = lax.rem(base + nxt, DEC_SLOTS)
        if nxt < n_chunks:
            start(b, nxt, nslot)
        else:
            @pl.when(b + 1 < pl.num_programs(0))
            def _():
                start(b + 1, nxt - n_chunks, nslot)
        for cp in copies(b, c, kbuf, vbuf, slot, sem):
            cp.wait()
        s = jnp.concatenate([_dot(q, kbuf[slot, i].astype(bf16)) for i in range(DEC_PAGES)],
                            axis=1) + bias_ref[0, :, c * width:(c + 1) * width]
        m_new = jnp.maximum(m, jnp.max(s, axis=1, keepdims=True))
        p = jnp.exp(s - m_new)
        alpha = jnp.exp(m - m_new)
        l = alpha * l + jnp.sum(p, axis=1, keepdims=True)
        pb = p.astype(bf16)
        pv = sum(_dot_nt(pb[:, i * PAGE_SIZE:(i + 1) * PAGE_SIZE], vbuf[slot, i].astype(bf16))
                 for i in range(DEC_PAGES))
        acc = alpha * acc + pv
        m = m_new
    o_ref[0] = acc / l


def _decode_attn(page_table, qmat, k_new, v_new, bias_new, bias, cache_kt, cache_vt, layer):
    s, n_pages = page_table.shape
    hb = bias.shape[1]
    n_chunks = n_pages // DEC_PAGES
    assert n_chunks * DEC_PAGES == n_pages and n_chunks >= DEC_SLOTS - 1
    row = lambda shape: pl.BlockSpec(shape, lambda b, pt: (b, 0, 0))
    hbm = pl.BlockSpec(memory_space=pl.ANY)
    width = DEC_PAGES * PAGE_SIZE
    grid_spec = pltpu.PrefetchScalarGridSpec(
        num_scalar_prefetch=1, grid=(s,),
        in_specs=[row((1, N_HEADS, KV_COLS)), row((1, 1, KV_COLS)), row((1, 1, KV_COLS)),
                  row((1, hb, LANES)), row((1, hb, n_pages * PAGE_SIZE)), hbm, hbm],
        out_specs=row((1, N_HEADS, KV_COLS)),
        scratch_shapes=[pltpu.VMEM((DEC_SLOTS, DEC_PAGES, KV_COLS, PAGE_SIZE), f32),
                        pltpu.VMEM((DEC_SLOTS, DEC_PAGES, KV_COLS, PAGE_SIZE), f32),
                        pltpu.SemaphoreType.DMA((2, DEC_SLOTS))])
    return pl.pallas_call(
        functools.partial(_decode_attn_kernel, layer=layer, n_chunks=n_chunks),
        grid_spec=grid_spec,
        out_shape=jax.ShapeDtypeStruct((s, N_HEADS, KV_COLS), f32),
        compiler_params=_cparams("arbitrary"),
        name="decode_attn",
    )(page_table, qmat, k_new, v_new, bias_new, bias, cache_kt, cache_vt)


def _pad_cols(w, n):
    return jnp.pad(w, ((0, 0), (0, n)))


def _decode_q(q_bf):
    s = q_bf.shape[0]
    onehot = (jnp.arange(N_HEADS)[:, None] // GROUP == jnp.arange(N_KV_HEADS)[None, :])
    q = q_bf.reshape(s, N_HEADS, 1, HEAD_DIM) * onehot[None, :, :, None].astype(q_bf.dtype)
    return q.reshape(s, N_HEADS, KV_COLS)


def _decode_o(o_full):
    s = o_full.shape[0]
    o = o_full.reshape(s, N_KV_HEADS, GROUP, N_KV_HEADS, HEAD_DIM)
    return jnp.stack([o[:, g, :, g, :] for g in range(N_KV_HEADS)], axis=1).reshape(s, -1)


def kernel(x_prompt, x_sample, cache_k_a, cache_v_a, cache_kidx_a, cache_k_b, cache_v_b,
           cache_logf_b, state_conv, page_table, c_prompt, c_sample, w_ada, b_ada, g_mix,
           g_ffn, w_in_a, w_o_a, w_in_b, b_f, w_o_b, w_up, conv_w, conv_b, w_down, g_final):
    bsz, t, d = x_prompt.shape
    s = x_sample.shape[0]
    depth = w_ada.shape[0]
    m = bsz * t
    past_len = page_table.shape[1] * PAGE_SIZE
    n_pool = cache_k_a.shape[1]
    tm = 512
    tq = 256

    pad_rows = (-(bsz + s)) % SUBLANES
    c_all = jnp.concatenate([c_prompt, c_sample, jnp.zeros((pad_rows, d), f32)], axis=0)
    mods = _adaln(c_all, w_ada, b_ada)
    mods_p = mods[:, :bsz].reshape(depth, bsz, 6, d)
    mods_s = mods[:, bsz:bsz + s].reshape(depth, s, 6, d)

    tabs_p = _rope_tables(jnp.arange(t))
    tabs_pt = _rope_tables_t(jnp.arange(t))
    tabs_s = _rope_tables(jnp.full((s,), past_len))
    zero_bf = jnp.zeros((1, LANES), f32)
    zero_col = jnp.zeros((N_HEADS, 1), f32)

    def rows_t(a):
        return a.reshape(bsz, N_KV_HEADS, HEAD_DIM, t).transpose(0, 3, 1, 2)

    xp = x_prompt.reshape(m, d)
    xs = x_sample.reshape(s, d)
    kv_shape_p = (bsz, t, N_KV_HEADS, HEAD_DIM)
    kv_shape_s = (s, 1, N_KV_HEADS, HEAD_DIM)
    ka_p, va_p, ia_p, kb_p, vb_p, lb_p, cv_p = [], [], [], [], [], [], []
    ka_s, va_s, ia_s, kb_s, vb_s, lb_s, cv_s = [], [], [], [], [], [], []

    def pages_t(cache):
        ct = jnp.moveaxis(cache, 2, -1)
        return ct.reshape(ct.shape[:2] + (-1, PAGE_SIZE))

    cache_k_a, cache_v_a, cache_kidx_a = pages_t(cache_k_a), pages_t(cache_v_a), pages_t(cache_kidx_a)
    cache_k_b, cache_v_b, cache_logf_b = pages_t(cache_k_b), pages_t(cache_v_b), pages_t(cache_logf_b)

    for i in range(depth):
        j = i // 2
        mp = [mods_p[i, :, c][:, None, :] for c in range(6)]
        ms = [mods_s[i, :, c][None] for c in range(6)]
        g_m = g_mix[i][None]
        if i % 2 == 0:
            w = w_in_a[j]
            n_qkvi = QKV_COLS + N_IDX_HEADS * IDX_DIM + IDX_DIM
            w_in = jnp.concatenate([_pad_cols(w[:, :n_qkvi], LANES - IDX_DIM),
                                    _pad_cols(w[:, n_qkvi:], LANES - N_IDX_HEADS)],
                                   axis=1).astype(bf16)
            w_o = w_o_a[j].astype(bf16)
            nq = N_HEADS * HEAD_DIM
            n_iq = QKV_COLS + N_IDX_HEADS * IDX_DIM
            wk = w[:, nq:nq + KV_COLS]
            w_row = jnp.concatenate([wk, _pad_cols(w[:, n_iq:n_qkvi], LANES - IDX_DIM)],
                                    axis=1).astype(bf16)
            w_t = jnp.concatenate([w[:, :nq], w[:, nq + KV_COLS:QKV_COLS], wk, w[:, QKV_COLS:n_qkvi],
                                   _pad_cols(w[:, n_qkvi:], IW_ROWS - N_IDX_HEADS)],
                                  axis=1).T.astype(bf16)
            qt, vt, kpad, kt32, vt32, iqt, iwt, ikb, ikt32 = _in_proj_t(
                xp, g_m, mp[0], mp[1], w_row, w_t, tabs_p, tabs_pt, zero_col, True, tm, t)
            xp = _dsa_attn(xp, mp[2], qt, kpad, vt, iqt, iwt, ikb.reshape(bsz, t, IDX_DIM),
                           w_o.T, tq, min(TOPK_MAX, t // 4))
            ka_p.append(rows_t(kt32))
            va_p.append(rows_t(vt32))
            ia_p.append(ikt32.transpose(0, 2, 1))
            q, k32, kb, v32, vb, iq, ik32, ikb, iw = _in_proj(
                xs, g_m, ms[0], ms[1], w_in, tabs_s, zero_bf, SEGS_A, OUTS_A, s, s)
            scores = _sample_scores(page_table, iq.reshape(s, N_IDX_HEADS, IDX_DIM),
                                    iw[:, :N_IDX_HEADS].reshape(s, N_IDX_HEADS, 1),
                                    cache_kidx_a, j)
            bias, bias_new = _sample_select(scores, iq, jnp.tile(ikb[:, :IDX_DIM], (1, N_IDX_HEADS)),
                                            iw, min(TOPK_MAX, (past_len + 1) // 4))
            o_full = _decode_attn(page_table, _decode_q(q), k32[:, None], v32[:, None],
                                  bias_new[:, None], bias[:, None], cache_k_a, cache_v_a, j)
            xs = _out_proj(xs, _decode_o(o_full).astype(bf16), w_o, ms[2], s, s)
            ka_s.append(k32.reshape(kv_shape_s))
            va_s.append(v32.reshape(kv_shape_s))
            ia_s.append(ik32[:, :IDX_DIM].reshape(s, 1, IDX_DIM))
        else:
            w_in = _pad_cols(w_in_b[j], LANES - N_HEADS).astype(bf16)
            w_o = w_o_b[j].astype(bf16)
            bf_pad = _pad_cols(b_f[j][None], LANES - N_HEADS)
            nq = N_HEADS * HEAD_DIM
            wk = w_in[:, nq:nq + KV_COLS]
            w_t = jnp.concatenate([w_in[:, :nq], w_in[:, nq + KV_COLS:QKV_COLS], wk,
                                   w_in[:, QKV_COLS:QKV_COLS + N_HEADS]], axis=1).T
            qt, vt, kpad, kt32, vt32, lft = _in_proj_t(
                xp, g_m, mp[0], mp[1], wk, w_t, tabs_p, tabs_pt, b_f[j][:, None], False, tm, t)
            c_t = _cumsum_t(lft, tq)
            c_pad = jnp.pad(c_t.transpose(0, 2, 1), ((0, 0), (0, 0), (0, LANES - N_HEADS)))
            xp = _fox_attn(xp, mp[2], qt, kpad, vt, c_pad, c_t, w_o.T, tq)
            kb_p.append(rows_t(kt32))
            vb_p.append(rows_t(vt32))
            lb_p.append(lft.transpose(0, 2, 1))
            q, k32, kb, v32, vb, lf = _in_proj(
                xs, g_m, ms[0], ms[1], w_in, tabs_s, bf_pad, SEGS_B, OUTS_B, s, s)
            bias = _suffix_bias(page_table, lf[:, :N_HEADS, None], cache_logf_b, j)
            o_full = _decode_attn(page_table, _decode_q(q), k32[:, None], v32[:, None],
                                  jnp.zeros((s, N_HEADS, LANES), f32), bias,
                                  cache_k_b, cache_v_b, j)
            xs = _out_proj(xs, _decode_o(o_full).astype(bf16), w_o, ms[2], s, s)
            kb_s.append(k32.reshape(kv_shape_s))
            vb_s.append(v32.reshape(kv_shape_s))
            lb_s.append(lf[:, :N_HEADS].reshape(s, 1, N_HEADS))

        wa = w_up[i][:, :D_FF].astype(bf16)
        wb = w_up[i][:, D_FF:].astype(bf16)
        wd = w_down[i].astype(bf16)
        cw = jnp.pad(conv_w[i], ((0, SUBLANES - CONV_W), (0, 0)))
        cb = conv_b[i][None]
        g_f = g_ffn[i][None]
        xp, tail = _ffn(xp, g_f, mp[3], mp[4], mp[5], wa, wb, cw, cb, wd, g_final[None],
                        i == depth - 1, tm, t)
        cv_p.append(tail.reshape(bsz, t // tm, SUBLANES, D_FF)[:, -1, SUBLANES - (CONV_W - 1):])
        xs, a_s = _ffn_sample(xs, g_f, ms[3], ms[4], ms[5], state_conv[i][:, 0],
                              state_conv[i][:, 1], wa, wb, cw, cb, wd, D_FF // 2)
        cv_s.append(jnp.stack([state_conv[i][:, 1], a_s], axis=1))

    y_prompt = xp.reshape(bsz, t, d)
    y_sample = _final_norm(xs, g_final[None], s).reshape(s, 1, d)
    return (y_prompt, y_sample,
            jnp.stack(ka_p), jnp.stack(va_p), jnp.stack(ia_p),
            jnp.stack(kb_p), jnp.stack(vb_p), jnp.stack(lb_p), jnp.stack(cv_p),
            jnp.stack(ka_s), jnp.stack(va_s), jnp.stack(ia_s),
            jnp.stack(kb_s), jnp.stack(vb_s), jnp.stack(lb_s), jnp.stack(cv_s))
```

```python
import functools

import jax
import jax.numpy as jnp
from jax import lax
from jax.experimental import pallas as pl
from jax.experimental.pallas import tpu as pltpu

f32 = jnp.float32
bf16 = jnp.bfloat16
i32 = jnp.int32

D_MODEL = 1024
N_HEADS = 16
HEAD_DIM = 64
N_KV_HEADS = 4
GROUP = N_HEADS // N_KV_HEADS
KV_COLS = N_KV_HEADS * HEAD_DIM
QKV_COLS = (N_HEADS + 2 * N_KV_HEADS) * HEAD_DIM
ROPE_DIMS = HEAD_DIM // 4
ROPE_HALF = ROPE_DIMS // 2
ROPE_THETA = 500000.0
N_IDX_HEADS = 8
IDX_DIM = 64
TOPK_MAX = 256
D_FF = 2816
CONV_W = 3
EPS = 1e-6
PAGE_SIZE = 128
Q_SCALE = HEAD_DIM ** -0.5
LOG2E = 1.4426950408889634

LANES = 128
SUBLANES = 8
VMEM_LIMIT = 56 * 1024 * 1024
NEG = -1e30
INT_MIN = -2 ** 31
INT_MAX = 2 ** 31 - 1
KEY_NEG_INF = (0xFF800000 ^ 0x7FFFFFFF) - 2 ** 32


def _cparams(*sem):
    return pltpu.CompilerParams(dimension_semantics=sem, vmem_limit_bytes=VMEM_LIMIT)


def _dot(a, b):
    return jnp.dot(a, b, preferred_element_type=f32)


def _dot_nt(a, b):
    return lax.dot_general(a, b, (((1,), (1,)), ((), ())), preferred_element_type=f32)


def _split3(x):
    hi = x.astype(bf16)
    r1 = x - hi.astype(f32)
    mid = r1.astype(bf16)
    lo = (r1 - mid.astype(f32)).astype(bf16)
    return hi, mid, lo


def _norm_mod(x, g, shift, scale):
    xn = x * lax.rsqrt(jnp.mean(x * x, axis=-1, keepdims=True) + EPS)
    return (xn * g) * (1.0 + scale) + shift


def _silu(x):
    return x * jax.nn.sigmoid(x)


def _sort_key(s):
    bits = pltpu.bitcast(s, i32)
    return bits ^ ((bits >> 31) & 0x7FFFFFFF)


def _adaln_kernel(c_ref, w_ref, b_ref, o_ref):
    a = _silu(c_ref[...]).astype(bf16)
    o_ref[0] = _dot(a, w_ref[0].astype(bf16)) + b_ref[0]


def _adaln(c_all, w_ada, b_ada):
    depth, d, n = w_ada.shape
    rows = c_all.shape[0]
    tn = 1536
    return pl.pallas_call(
        _adaln_kernel,
        grid=(depth, n // tn),
        in_specs=[pl.BlockSpec((rows, d), lambda l, j: (0, 0)),
                  pl.BlockSpec((1, d, tn), lambda l, j: (l, 0, j)),
                  pl.BlockSpec((1, 1, tn), lambda l, j: (l, 0, j))],
        out_specs=pl.BlockSpec((1, rows, tn), lambda l, j: (l, 0, j)),
        out_shape=jax.ShapeDtypeStruct((depth, rows, n), f32),
        compiler_params=_cparams("parallel", "parallel"),
        name="adaln",
    )(c_all, w_ada, b_ada.reshape(depth, 1, n))


SEGS_A = ((0, 1024, "rope", ((0, Q_SCALE),)),
          (1024, 256, "rope", ((1, 1.0), (2, 1.0))),
          (1280, 256, None, ((3, 1.0), (4, 1.0))),
          (1536, 512, "rope", ((5, 1.0),)),
          (2048, 128, "rope", ((6, 1.0), (7, 1.0))),
          (2176, 128, None, ((8, 1.0),)))
OUTS_A = ((1024, bf16), (256, f32), (256, bf16), (256, f32), (256, bf16),
          (512, bf16), (128, f32), (128, bf16), (128, f32))
SEGS_B = ((0, 1024, None, ((0, Q_SCALE),)),
          (1024, 256, None, ((1, 1.0), (2, 1.0))),
          (1280, 256, None, ((3, 1.0), (4, 1.0))),
          (1536, 128, "logf", ((5, 1.0),)))
OUTS_B = ((1024, bf16), (256, f32), (256, bf16), (256, f32), (256, bf16), (128, f32))


def _in_proj_kernel(x_ref, g_ref, shift_ref, scale_ref, w_ref, cos_ref, sa_ref, sb_ref,
                    bf_ref, *out_refs, segs):
    h = _norm_mod(x_ref[...], g_ref[...], shift_ref[0], scale_ref[0]).astype(bf16)
    for c0, width, epi, outs in segs:
        for c in range(width // LANES):
            lo = c0 + c * LANES
            a = _dot(h, w_ref[:, lo:lo + LANES])
            if epi == "rope":
                a = (a * cos_ref[...] + pltpu.roll(a, ROPE_HALF, 1) * sa_ref[...]
                     + pltpu.roll(a, LANES - ROPE_HALF, 1) * sb_ref[...])
            elif epi == "logf":
                z = a + bf_ref[...]
                a = jnp.minimum(z, 0.0) - jnp.log(1.0 + jnp.exp(-jnp.abs(z)))
            for oi, sc in outs:
                val = a if sc == 1.0 else a * sc
                out_refs[oi][:, c * LANES:(c + 1) * LANES] = val.astype(out_refs[oi].dtype)


def _in_proj(x, g, shift, scale, w, tabs, b_f, segs, outs, tm, rows_per_mod):
    m, d = x.shape
    n = w.shape[1]
    r = shift.shape[1]
    cos, sa, sb = tabs
    tab_blocks = cos.shape[0] // tm
    mod_spec = pl.BlockSpec((1, r, d), lambda i: (i * tm // rows_per_mod, 0, 0))
    tab_spec = pl.BlockSpec((tm, LANES), lambda i: (i % tab_blocks, 0))
    return pl.pallas_call(
        functools.partial(_in_proj_kernel, segs=segs),
        grid=(m // tm,),
        in_specs=[pl.BlockSpec((tm, d), lambda i: (i, 0)),
                  pl.BlockSpec((1, d), lambda i: (0, 0)),
                  mod_spec, mod_spec,
                  pl.BlockSpec((d, n), lambda i: (0, 0)),
                  tab_spec, tab_spec, tab_spec,
                  pl.BlockSpec((1, LANES), lambda i: (0, 0))],
        out_specs=[pl.BlockSpec((tm, wd), lambda i: (i, 0)) for wd, _ in outs],
        out_shape=[jax.ShapeDtypeStruct((m, wd), dt) for wd, dt in outs],
        compiler_params=_cparams("parallel"),
        name="in_proj",
    )(x, g, shift, scale, w, cos, sa, sb, b_f)


def _rope_tables(pos):
    inv_freq = ROPE_THETA ** (-jnp.arange(ROPE_HALF, dtype=f32) / ROPE_HALF)
    ang = pos.astype(f32)[:, None] * inv_freq[None, :]
    cos, sin = jnp.cos(ang), jnp.sin(ang)
    t = pos.shape[0]
    ones = jnp.ones((t, HEAD_DIM - ROPE_DIMS), f32)
    zeros = jnp.zeros((t, HEAD_DIM - ROPE_DIMS), f32)
    zh = jnp.zeros((t, ROPE_HALF), f32)
    c = jnp.concatenate([cos, cos, ones], axis=1)
    sa = jnp.concatenate([zh, sin, zeros], axis=1)
    sb = jnp.concatenate([-sin, zh, zeros], axis=1)
    rep = LANES // HEAD_DIM
    return tuple(jnp.tile(a, (1, rep)) for a in (c, sa, sb))


WT_Q = 0
WT_V = N_HEADS * HEAD_DIM
WT_K = WT_V + KV_COLS
WT_X = WT_K + KV_COLS
WT_IK = WT_X + N_IDX_HEADS * IDX_DIM
WT_IW = WT_IK + IDX_DIM
IW_ROWS = 2 * SUBLANES
ROW_BLOCK = 512


def _rope_rows(a, cos_t, sin_t):
    parts = []
    for b0 in range(0, a.shape[0], HEAD_DIM):
        x1 = a[b0:b0 + ROPE_HALF]
        x2 = a[b0 + ROPE_HALF:b0 + ROPE_DIMS]
        parts += [x1 * cos_t - x2 * sin_t, x1 * sin_t + x2 * cos_t, a[b0 + ROPE_DIMS:b0 + HEAD_DIM]]
    return jnp.concatenate(parts, axis=0)


def _in_proj_t_kernel(x_ref, g_ref, shift_ref, scale_ref, w_ref, wt_ref, cos_ref, sa_ref, sb_ref,
                      cost_ref, sint_ref, bf_ref, *outs, is_a):
    if is_a:
        qt_ref, vt_ref, kpad_ref, kt32_ref, vt32_ref, iqt_ref, iwt_ref, ikb_ref, ikt32_ref = outs
    else:
        qt_ref, vt_ref, kpad_ref, kt32_ref, vt32_ref, lft_ref = outs
    h = _norm_mod(x_ref[...], g_ref[...], shift_ref[0], scale_ref[0]).astype(bf16)
    cos_t, sin_t = cost_ref[...], sint_ref[...]

    def rows(r0, n):
        blk = _dot_nt(wt_ref[r0:r0 + n, :], h)
        return [blk[c * LANES:(c + 1) * LANES] for c in range(n // LANES)]

    def rope_cols(a):
        return (a * cos_ref[...] + pltpu.roll(a, ROPE_HALF, 1) * sa_ref[...]
                + pltpu.roll(a, LANES - ROPE_HALF, 1) * sb_ref[...])

    for r0 in range(0, N_HEADS * HEAD_DIM, ROW_BLOCK):
        for c, a in enumerate(rows(WT_Q + r0, ROW_BLOCK)):
            if is_a:
                a = _rope_rows(a, cos_t, sin_t)
            lo = r0 + c * LANES
            qt_ref[lo:lo + LANES, :] = (a * (Q_SCALE * LOG2E)).astype(bf16)
    for c, (v, k) in enumerate(zip(rows(WT_V, KV_COLS), rows(WT_K, KV_COLS))):
        sl = slice(c * LANES, (c + 1) * LANES)
        vt32_ref[0, sl, :] = v
        vt_ref[sl, :] = v.astype(bf16)
        kt32_ref[0, sl, :] = _rope_rows(k, cos_t, sin_t) if is_a else k
    if is_a:
        for c, a in enumerate(rows(WT_X, N_IDX_HEADS * IDX_DIM)):
            iqt_ref[c * LANES:(c + 1) * LANES, :] = _rope_rows(a, cos_t, sin_t).astype(bf16)
        ikt32_ref[0] = _rope_rows(_dot_nt(wt_ref[WT_IK:WT_IK + IDX_DIM, :], h), cos_t, sin_t)
        iwt_ref[...] = _dot_nt(wt_ref[WT_IW:WT_IW + IW_ROWS, :], h)
    else:
        z = _dot_nt(wt_ref[WT_X:WT_X + N_HEADS, :], h) + bf_ref[...]
        lft_ref[0] = jnp.minimum(z, 0.0) - jnp.log(1.0 + jnp.exp(-jnp.abs(z)))

    lane = lax.broadcasted_iota(i32, (1, LANES), 1)
    std = _dot(h, w_ref[...])
    for c in range(KV_COLS // LANES):
        a = std[:, c * LANES:(c + 1) * LANES]
        if is_a:
            a = rope_cols(a)
        kpad_ref[0, 2 * c] = jnp.where(lane < HEAD_DIM, a, 0.0).astype(bf16)
        kpad_ref[0, 2 * c + 1] = jnp.where(lane < HEAD_DIM, pltpu.roll(a, HEAD_DIM, 1),
                                           0.0).astype(bf16)
    if is_a:
        ikb_ref[...] = rope_cols(std[:, KV_COLS:KV_COLS + LANES])[:, :IDX_DIM].astype(bf16)


def _in_proj_t(x, g, shift, scale, w, wt, tabs, tabs_t, b_f, is_a, tm, t):
    m, d = x.shape
    bsz = m // t
    tpb = t // tm
    col = lambda rows_: pl.BlockSpec((rows_, tm), lambda i: (0, i))
    seq = lambda rows_: pl.BlockSpec((1, rows_, tm), lambda i: (i // tpb, 0, i % tpb))
    seq_shape = lambda rows_: jax.ShapeDtypeStruct((bsz, rows_, t), f32)
    const = lambda a: pl.BlockSpec(a.shape, lambda i: (0,) * a.ndim)
    mod_spec = pl.BlockSpec((1, 1, d), lambda i: (i // tpb, 0, 0))
    tab_spec = pl.BlockSpec((tm, LANES), lambda i: (i % tpb, 0))
    tabt_spec = pl.BlockSpec((ROPE_HALF, tm), lambda i: (0, i % tpb))
    nq = N_HEADS * HEAD_DIM
    out_specs = [col(nq), col(KV_COLS),
                 pl.BlockSpec((1, N_KV_HEADS, tm, LANES), lambda i: (i // tpb, 0, i % tpb, 0)),
                 seq(KV_COLS), seq(KV_COLS)]
    out_shape = [jax.ShapeDtypeStruct((nq, m), bf16), jax.ShapeDtypeStruct((KV_COLS, m), bf16),
                 jax.ShapeDtypeStruct((bsz, N_KV_HEADS, t, LANES), bf16),
                 seq_shape(KV_COLS), seq_shape(KV_COLS)]
    if is_a:
        out_specs += [col(N_IDX_HEADS * IDX_DIM), col(IW_ROWS),
                      pl.BlockSpec((tm, IDX_DIM), lambda i: (i, 0)), seq(IDX_DIM)]
        out_shape += [jax.ShapeDtypeStruct((N_IDX_HEADS * IDX_DIM, m), bf16),
                      jax.ShapeDtypeStruct((IW_ROWS, m), f32),
                      jax.ShapeDtypeStruct((m, IDX_DIM), bf16), seq_shape(IDX_DIM)]
    else:
        out_specs += [seq(N_HEADS)]
        out_shape += [seq_shape(N_HEADS)]
    return pl.pallas_call(
        functools.partial(_in_proj_t_kernel, is_a=is_a),
        grid=(m // tm,),
        in_specs=[pl.BlockSpec((tm, d), lambda i: (i, 0)), const(g), mod_spec, mod_spec,
                  const(w), const(wt), tab_spec, tab_spec, tab_spec, tabt_spec, tabt_spec,
                  const(b_f)],
        out_specs=out_specs, out_shape=out_shape,
        compiler_params=_cparams("parallel"),
        name="in_proj_t",
    )(x, g, shift, scale, w, wt, *tabs, *tabs_t, b_f)


def _rope_tables_t(pos):
    inv_freq = ROPE_THETA ** (-jnp.arange(ROPE_HALF, dtype=f32) / ROPE_HALF)
    ang = inv_freq[:, None] * pos.astype(f32)[None, :]
    return jnp.cos(ang), jnp.sin(ang)


AUG_STRIDE = 4
AUG_ONES = GROUP * AUG_STRIDE


ATTN_TK = 256
SUM_ROWS = 2 * SUBLANES
ACC_ROWS = HEAD_DIM + SUM_ROWS


def _t_init(m_sc, acc_sc):
    m_sc[...] = jnp.full(m_sc.shape, NEG, f32)
    acc_sc[...] = jnp.zeros(acc_sc.shape, f32)


def _t_step(s, vt1, h, m_sc, acc_sc):
    m_old = m_sc[h]
    m_new = jnp.maximum(m_old, jnp.max(s, axis=0, keepdims=True))
    p = jnp.exp2((s - m_new).astype(bf16))
    acc_sc[h] = jnp.exp2(m_old - m_new) * acc_sc[h] + _dot(vt1, p)
    m_sc[h] = m_new


STALE_LIMIT = 64.0


def _t_step_stale(s, vt1, h, m_sc, acc_sc, rise_sc):
    m_old = m_sc[h]
    p = jnp.exp2((s - m_old).astype(bf16))
    s_max = jnp.max(s, axis=0, keepdims=True)
    m_new = jnp.maximum(m_old, s_max)
    acc_sc[h] = (acc_sc[h] + _dot(vt1, p)) * jnp.exp2(m_old - m_new)
    m_sc[h] = m_new
    rise_sc[h] = jnp.maximum(rise_sc[h], s_max - m_old)


def _t_tile(off, tk, kt_of, vt_ref, qa_sc, post, step):
    kts = [kt_of(g) for g in range(N_KV_HEADS)]
    ones = jnp.ones((SUM_ROWS, tk), bf16)
    vts = [jnp.concatenate([vt_ref[g * HEAD_DIM:(g + 1) * HEAD_DIM, pl.ds(off, tk)], ones], axis=0)
           for g in range(N_KV_HEADS)]
    ahead = 2
    pending = [_dot(kts[g], qa_sc[g]) for g in range(ahead)]
    for g in range(N_KV_HEADS):
        s = pending.pop(0)
        if g + ahead < N_KV_HEADS:
            pending.append(_dot(kts[g + ahead], qa_sc[g + ahead]))
        step(post(s), vts[g], g)


def _t_sweep(first, rest, m_sc, acc_sc, rise_sc):
    def exact(s, vt1, g):
        _t_step(s, vt1, g, m_sc, acc_sc)

    def stale(s, vt1, g):
        _t_step_stale(s, vt1, g, m_sc, acc_sc, rise_sc)

    rise_sc[...] = jnp.full(rise_sc.shape, NEG, f32)
    _t_init(m_sc, acc_sc)
    first(exact)
    rest(stale)

    @pl.when(jnp.max(rise_sc[...]) > STALE_LIMIT)
    def _():
        _t_init(m_sc, acc_sc)
        first(exact)
        rest(exact)


def _t_fill_q(qa_sc, h, tq, q, aug):
    g, jj = divmod(h, GROUP)
    qa_sc[g, :HEAD_DIM, jj * tq:(jj + 1) * tq] = q
    qa_sc[g, HEAD_DIM:, jj * tq:(jj + 1) * tq] = aug


def _t_finish(tq, acc_sc, ot_sc):
    for h in range(N_HEADS):
        g, jj = divmod(h, GROUP)
        cols = slice(jj * tq, (jj + 1) * tq)
        ot_sc[h * HEAD_DIM:(h + 1) * HEAD_DIM, :] = (
            acc_sc[g, :HEAD_DIM, cols] / acc_sc[g, HEAD_DIM:HEAD_DIM + 1, cols]).astype(bf16)


def _t_out(x_ref, gate_ref, wot_ref, ot_sc, o_ref):
    y_t = _dot(wot_ref[...], ot_sc[...])
    o_ref[...] = x_ref[...] + gate_ref[0] * y_t.T


def _fox_attn_kernel(x_ref, gate_ref, qt_ref, kpad_ref, vt_ref, cpad_ref, ct_ref, wot_ref,
                     o_ref, kaug_sc, m_sc, rise_sc, acc_sc, qa_sc, ot_sc, *, tq, tk, chunk):
    qi = pl.program_id(1)
    t = kaug_sc.shape[1]

    @pl.when(qi == 0)
    def _():
        r = lax.broadcasted_iota(i32, (LANES, LANES), 0)
        l = lax.broadcasted_iota(i32, (LANES, LANES), 1) - HEAD_DIM
        lane = lax.broadcasted_iota(i32, (1, LANES), 1) - HEAD_DIM
        ones = jnp.where((lane >= AUG_ONES) & (lane < AUG_ONES + 3), 1.0, 0.0)
        for g in range(N_KV_HEADS):
            perm = [((l >= 0) & (l < AUG_ONES) & ((l >> 2) == r - g * GROUP)
                     & ((l & 3) == p)).astype(bf16) for p in range(3)]

            def body(i, c, g=g, perm=perm):
                off = pl.multiple_of(i * chunk, chunk)
                pieces = _split3(cpad_ref[0, pl.ds(off, chunk), :] * -LOG2E)
                aug = sum(_dot(pc, pm) for pc, pm in zip(pieces, perm)) + ones
                kaug_sc[g, pl.ds(off, chunk), :] = (
                    kpad_ref[0, g, pl.ds(off, chunk), :].astype(f32) + aug).astype(bf16)
                return c

            lax.fori_loop(0, t // chunk, body, 0)

    ri = lax.broadcasted_iota(i32, (HEAD_DIM, tq), 0)
    wide = (tk, GROUP * tq)
    d_iota = (lax.broadcasted_iota(i32, wide, 0)
              - (lax.broadcasted_iota(i32, wide, 1) & (tq - 1)))
    for h in range(N_HEADS):
        jj = h % GROUP
        cq = _split3(ct_ref[0, h:h + 1, :] * LOG2E)
        aug = jnp.where((ri >= jj * AUG_STRIDE) & (ri < jj * AUG_STRIDE + 3), 1.0, 0.0)
        for p in range(3):
            aug = jnp.where(ri == AUG_ONES + p, cq[p].astype(f32), aug)
        _t_fill_q(qa_sc, h, tq, qt_ref[h * HEAD_DIM:(h + 1) * HEAD_DIM, :], aug.astype(bf16))
    def tile(j, post, step):
        off = pl.multiple_of(j * tk, tk)
        _t_tile(off, tk, lambda g: kaug_sc[g, pl.ds(off, tk), :], vt_ref, qa_sc, post, step)

    sub = tq // tk

    def diagonal(step):
        for dd in range(sub):
            tile(qi * sub + dd, lambda s, dd=dd: jnp.where(d_iota <= -dd * tk, s, NEG), step)

    def earlier(step):
        def body(i, c):
            tile(qi * sub - 1 - i, lambda s: s, step)
            return c
        lax.fori_loop(0, qi * sub, body, 0)

    _t_sweep(diagonal, earlier, m_sc, acc_sc, rise_sc)
    _t_finish(tq, acc_sc, ot_sc)
    _t_out(x_ref, gate_ref, wot_ref, ot_sc, o_ref)


def _attn_specs(bsz, t, d, tq):
    nq = t // tq
    return dict(
        x=pl.BlockSpec((tq, d), lambda b, q: (b * nq + q, 0)),
        gate=pl.BlockSpec((1, 1, d), lambda b, q: (b, 0, 0)),
        qt=pl.BlockSpec((N_HEADS * HEAD_DIM, tq), lambda b, q: (0, b * nq + q)),
        kpad=pl.BlockSpec((1, N_KV_HEADS, t, LANES), lambda b, q: (b, 0, 0, 0)),
        vt=pl.BlockSpec((KV_COLS, t), lambda b, q: (0, b)),
        wot=pl.BlockSpec((d, N_HEADS * HEAD_DIM), lambda b, q: (0, 0)),
        scratch=[pltpu.VMEM((N_KV_HEADS, 1, GROUP * tq), f32),
                 pltpu.VMEM((N_KV_HEADS, 1, GROUP * tq), f32),
                 pltpu.VMEM((N_KV_HEADS, ACC_ROWS, GROUP * tq), f32),
                 pltpu.VMEM((N_KV_HEADS, LANES, GROUP * tq), bf16),
                 pltpu.VMEM((N_HEADS * HEAD_DIM, tq), bf16)])


def _fox_attn(x, gate, qt, kpad, vt, c_pad, c_t, wot, tq):
    bsz, _, t, _ = kpad.shape
    m, d = x.shape
    sp = _attn_specs(bsz, t, d, tq)
    return pl.pallas_call(
        functools.partial(_fox_attn_kernel, tq=tq, tk=ATTN_TK, chunk=512),
        grid=(bsz, t // tq),
        in_specs=[sp["x"], sp["gate"], sp["qt"], sp["kpad"], sp["vt"],
                  pl.BlockSpec((1, t, LANES), lambda b, q: (b, 0, 0)),
                  pl.BlockSpec((1, N_HEADS, tq), lambda b, q: (b, 0, q)),
                  sp["wot"]],
        out_specs=sp["x"],
        out_shape=jax.ShapeDtypeStruct((m, d), f32),
        scratch_shapes=[pltpu.VMEM((N_KV_HEADS, t, LANES), bf16)] + sp["scratch"],
        compiler_params=_cparams("parallel", "arbitrary"),
        name="fox_attn",
    )(x, gate, qt, kpad, vt, c_pad, c_t, wot)


def _cumsum_t_kernel(lft_ref, ct_ref, carry_sc, *, tb):
    @pl.when(pl.program_id(1) == 0)
    def _():
        carry_sc[...] = jnp.zeros(carry_sc.shape, f32)

    tri = (lax.broadcasted_iota(i32, (tb, tb), 0)
           <= lax.broadcasted_iota(i32, (tb, tb), 1)).astype(bf16)
    lft = lft_ref[0]
    csum = sum(_dot(p, tri) for p in _split3(lft))
    ct_ref[0] = csum + carry_sc[...]
    carry_sc[...] = carry_sc[...] + jnp.sum(lft, axis=1, keepdims=True)


def _cumsum_t(lf_t, tb):
    b, nh, t = lf_t.shape
    spec = pl.BlockSpec((1, nh, tb), lambda bi, j: (bi, 0, j))
    return pl.pallas_call(
        functools.partial(_cumsum_t_kernel, tb=tb),
        grid=(b, t // tb),
        in_specs=[spec],
        out_specs=spec,
        out_shape=jax.ShapeDtypeStruct((b, nh, t), f32),
        scratch_shapes=[pltpu.VMEM((nh, 1), f32)],
        compiler_params=_cparams("parallel", "arbitrary"),
        name="cumsum_t",
    )(lf_t)


def _kth_threshold(count_ge, shape, n_sel):
    zero = jnp.zeros(shape, i32)
    ans = jnp.where(count_ge(zero) >= n_sel, zero, jnp.full(shape, INT_MIN, i32))

    def bit_body(i, ans):
        cand = ans | jnp.left_shift(jnp.int32(1), 30 - i)
        return jnp.where(count_ge(cand) >= n_sel, cand, ans)

    return lax.fori_loop(0, 31, bit_body, ans)


BF16_SUBLANES = 2 * SUBLANES


def _kth_two_level(count_coarse_ge, count_ge, shape, n_sel):
    def coarse_value(a):
        pattern = a ^ ((a >> 31) & 0x7FFF)
        return pltpu.bitcast(pattern << 16, f32).astype(bf16)

    zero = jnp.zeros(shape, i32)
    ans = jnp.where(count_coarse_ge(coarse_value(zero)) >= n_sel, zero,
                    jnp.full(shape, -2 ** 15, i32))

    def bit_body(i, ans):
        cand = ans | jnp.left_shift(jnp.int32(1), 14 - i)
        return jnp.where(count_coarse_ge(coarse_value(cand)) >= n_sel, cand, ans)

    ans = lax.fori_loop(0, 15, bit_body, ans)
    pattern = (ans ^ ((ans >> 31) & 0x7FFF)) << 16
    centre = pattern ^ ((pattern >> 31) & 0x7FFFFFFF)
    tiny = (ans >= -2 ** 7) & (ans < 2 ** 7)
    centre = jnp.where(tiny, 0, centre)
    half = jnp.where(tiny, 2 ** 23, 2 ** 15)

    def cond(state):
        i, _, _, _, n_open = state
        return (i <= 24) & (n_open > 0)

    def body(state):
        i, lo, hi, exact, _ = state
        live = (exact == 0) & (lo < hi)
        mid = lo + ((hi - lo + 1) >> 1)
        mid = jnp.where(tiny & (lo == 0) & (i == 1), 1, mid)
        c = count_ge(mid)
        ge = c >= n_sel
        lo = jnp.where(live & ge, mid, lo)
        hi = jnp.where(live & jnp.logical_not(ge), mid - 1, hi)
        exact = jnp.where(live & (c == n_sel), 1, exact)
        n_open = jnp.sum(jnp.where((exact == 0) & (lo < hi), 1, 0))
        return i + 1, lo, hi, exact, n_open

    state = (jnp.int32(0), centre - half, centre + half, zero, jnp.int32(1))
    _, lo, _, exact, _ = lax.while_loop(cond, body, state)
    return lo, exact


def _tie_cut(count_tied_below, need, shape, n_bits):
    def body(i, p):
        cand = p | jnp.left_shift(jnp.int32(1), n_bits - 1 - i)
        return jnp.where(count_tied_below(cand) < need, cand, p)
    return lax.fori_loop(0, n_bits, body, jnp.zeros(shape, i32))


def _fold_lanes(ind, c):
    for cc in range(ind.shape[1] // LANES):
        c = c + ind[:, cc * LANES:(cc + 1) * LANES]
    return c


def _dsa_attn_kernel(x_ref, gate_ref, qt_ref, kpad_ref, vt_ref, iqt_ref, iwt_ref, ik_ref,
                     wot_ref, o_ref, key_sc, coarse_sc, bias_sc, cut_sc, m_sc, rise_sc, acc_sc,
                     qa_sc, ot_sc,
                     *, tq, n_sel, n_bits):
    qi = pl.program_id(1)
    n_k = qi + 1
    row = lax.broadcasted_iota(i32, (tq, tq), 0)
    d_iota = row - lax.broadcasted_iota(i32, (tq, tq), 1)

    def score_body(j, c):
        off = pl.multiple_of(j * tq, tq)
        ikt = ik_ref[0, pl.ds(off, tq), :]
        dots = [_dot(ikt, iqt_ref[h * IDX_DIM:(h + 1) * IDX_DIM, :]) for h in range(N_IDX_HEADS)]
        s = jnp.zeros((tq, tq), f32)
        for h in range(N_IDX_HEADS):
            s = s + jnp.maximum(dots[h], 0.0) * iwt_ref[h:h + 1, :]
        causal = d_iota <= (qi - j) * tq
        key_sc[pl.ds(off, tq), :] = jnp.where(causal, _sort_key(s), KEY_NEG_INF)
        coarse_sc[pl.ds(off, tq), :] = jnp.where(causal, s, -jnp.inf).astype(bf16)
        return c

    lax.fori_loop(0, n_k, score_body, 0)

    def count_coarse_ge(v):
        def body(j, c):
            off = pl.multiple_of(j * tq, tq)
            ind = jnp.where(coarse_sc[pl.ds(off, tq), :] >= v, jnp.ones((), bf16),
                            jnp.zeros((), bf16))
            for r in range(0, tq, BF16_SUBLANES):
                c = c + ind[r:r + BF16_SUBLANES]
            return c
        c = lax.fori_loop(0, n_k, body, jnp.zeros((BF16_SUBLANES, tq), bf16))
        return jnp.sum(c.astype(f32), axis=0, keepdims=True)

    def count(pred):
        def body(j, c):
            off = pl.multiple_of(j * tq, tq)
            ind = jnp.where(pred(key_sc[pl.ds(off, tq), :], j), 1, 0)
            for r in range(0, tq, SUBLANES):
                c = c + ind[r:r + SUBLANES]
            return c
        c = lax.fori_loop(0, n_k, body, jnp.zeros((SUBLANES, tq), i32))
        return jnp.sum(c, axis=0, keepdims=True)

    thr, exact = _kth_two_level(count_coarse_ge, lambda cand: count(lambda kt, j: kt >= cand),
                                (1, tq), n_sel)
    keep_all = n_k * tq <= n_sel
    thr = jnp.where(keep_all, INT_MIN, thr)
    exact = jnp.where(keep_all, 1, exact)
    real = thr > KEY_NEG_INF
    cut_sc[...] = jnp.where(real, INT_MAX, -1)

    @pl.when(jnp.max(jnp.where(real, 1 - exact, 0)) > 0)
    def _():
        need = n_sel - count(lambda kt, j: kt > thr)
        cut = _tie_cut(lambda cand: count(lambda kt, j: (kt == thr) & (row + j * tq < cand)),
                       need, (1, tq), n_bits)
        cut_sc[...] = jnp.where(real, jnp.where(exact == 1, INT_MAX, cut), -1)

    cut = cut_sc[...]
    floor = jnp.maximum(thr, KEY_NEG_INF)

    def bias_body(j, c):
        off = pl.multiple_of(j * tq, tq)
        kt = key_sc[pl.ds(off, tq), :]
        sel = (kt > floor) | ((kt == thr) & (row + j * tq <= cut))
        bias_sc[pl.ds(off, tq), :] = jnp.where(sel, 0.0, NEG)
        return c

    lax.fori_loop(0, n_k, bias_body, 0)

    for h in range(N_HEADS):
        _t_fill_q(qa_sc, h, tq, qt_ref[h * HEAD_DIM:(h + 1) * HEAD_DIM, :],
                  jnp.zeros((LANES - HEAD_DIM, tq), bf16))
    def tile(j, step):
        off = pl.multiple_of(j * tq, tq)

        def masked(s):
            return jnp.concatenate([s[:, jj * tq:(jj + 1) * tq] + bias_sc[pl.ds(off, tq), :]
                                    for jj in range(GROUP)], axis=1)

        _t_tile(off, tq, lambda g: kpad_ref[0, g, pl.ds(off, tq), :], vt_ref, qa_sc, masked, step)

    def oldest(step):
        tile(0, step)

    def later(step):
        def body(j, c):
            tile(j, step)
            return c
        lax.fori_loop(1, n_k, body, 0)

    _t_sweep(oldest, later, m_sc, acc_sc, rise_sc)
    _t_finish(tq, acc_sc, ot_sc)
    _t_out(x_ref, gate_ref, wot_ref, ot_sc, o_ref)


def _dsa_attn(x, gate, qt, kpad, vt, iqt, iwt, ik, wot, tq, n_sel):
    bsz, _, t, _ = kpad.shape
    m, d = x.shape
    nq = t // tq
    sp = _attn_specs(bsz, t, d, tq)
    assert tq >= n_sel and t // BF16_SUBLANES <= 256
    return pl.pallas_call(
        functools.partial(_dsa_attn_kernel, tq=tq, n_sel=n_sel, n_bits=(t - 1).bit_length()),
        grid=(bsz, nq),
        in_specs=[sp["x"], sp["gate"], sp["qt"], sp["kpad"], sp["vt"],
                  pl.BlockSpec((N_IDX_HEADS * IDX_DIM, tq), lambda b, q: (0, b * nq + q)),
                  pl.BlockSpec((2 * SUBLANES, tq), lambda b, q: (0, b * nq + q)),
                  pl.BlockSpec((1, t, IDX_DIM), lambda b, q: (b, 0, 0)),
                  sp["wot"]],
        out_specs=sp["x"],
        out_shape=jax.ShapeDtypeStruct((m, d), f32),
        scratch_shapes=[pltpu.VMEM((t, tq), i32), pltpu.VMEM((t, tq), bf16),
                        pltpu.VMEM((t, tq), f32), pltpu.VMEM((1, tq), i32)] + sp["scratch"],
        compiler_params=_cparams("parallel", "arbitrary"),
        name="dsa_attn",
    )(x, gate, qt, kpad, vt, iqt, iwt, ik, wot)


def _out_proj_kernel(x_ref, o_ref, w_ref, gate_ref, y_ref):
    y_ref[...] = x_ref[...] + gate_ref[0] * _dot(o_ref[...], w_ref[...])


def _out_proj(x, o, w, gate, tm, rows_per_mod):
    m, d = x.shape
    kdim = o.shape[1]
    r = gate.shape[1]
    return pl.pallas_call(
        _out_proj_kernel,
        grid=(m // tm,),
        in_specs=[pl.BlockSpec((tm, d), lambda i: (i, 0)),
                  pl.BlockSpec((tm, kdim), lambda i: (i, 0)),
                  pl.BlockSpec((kdim, d), lambda i: (0, 0)),
                  pl.BlockSpec((1, r, d), lambda i: (i * tm // rows_per_mod, 0, 0))],
        out_specs=pl.BlockSpec((tm, d), lambda i: (i, 0)),
        out_shape=jax.ShapeDtypeStruct((m, d), f32),
        compiler_params=_cparams("parallel"),
        name="out_proj",
    )(x, o, w, gate)


FFN_CHUNK = 256
FFN_HALO = 2 * SUBLANES


def _ffn_kernel(x_ref, xh_ref, g_ref, shift_ref, scale_ref, gate_ref, wa_ref, wb_ref,
                cw_ref, cb_ref, wd_ref, gfin_ref, o_ref, tail_ref, acc_sc, hext_sc,
                *, tm, tiles_per_seq, final):
    i = pl.program_id(0)
    x = x_ref[...]
    keep = jnp.where(i % tiles_per_seq == 0, 0.0, 1.0)
    hext_sc[:FFN_HALO, :] = (_norm_mod(xh_ref[...], g_ref[...], shift_ref[0], scale_ref[0])
                             * keep).astype(bf16)
    hext_sc[FFN_HALO:, :] = _norm_mod(x, g_ref[...], shift_ref[0], scale_ref[0]).astype(bf16)
    for c in range(D_FF // FFN_CHUNK):
        sl = slice(c * FFN_CHUNK, (c + 1) * FFN_CHUNK)
        a_ext = _dot(hext_sc[...], wa_ref[:, sl])
        b = _dot(hext_sc[FFN_HALO:, :], wb_ref[:, sl])
        a = a_ext[FFN_HALO:]
        p1 = pltpu.roll(a_ext, 1, 0)[FFN_HALO:]
        p2 = pltpu.roll(a_ext, 2, 0)[FFN_HALO:]
        conv = cb_ref[:, sl] + p2 * cw_ref[0:1, sl] + p1 * cw_ref[1:2, sl] + a * cw_ref[2:3, sl]
        y = _dot((_silu(conv) * b).astype(bf16), wd_ref[sl, :])
        if c == 0:
            acc_sc[...] = y
        else:
            acc_sc[...] += y
        tail_ref[0, :, sl] = a_ext[FFN_HALO + tm - SUBLANES:]
    y = x + gate_ref[0] * acc_sc[...]
    if final:
        y = y * lax.rsqrt(jnp.mean(y * y, axis=-1, keepdims=True) + EPS) * gfin_ref[...]
    o_ref[...] = y


def _ffn(x, g, shift, scale, gate, wa, wb, cw, cb, wd, g_final, final, tm, rows_per_seq):
    m, d = x.shape
    n_tiles = m // tm
    hb = tm // FFN_HALO
    mod_spec = pl.BlockSpec((1, 1, d), lambda i: (i * tm // rows_per_seq, 0, 0))
    const = lambda shape: pl.BlockSpec(shape, lambda i: (0,) * len(shape))
    return pl.pallas_call(
        functools.partial(_ffn_kernel, tm=tm, tiles_per_seq=rows_per_seq // tm, final=final),
        grid=(n_tiles,),
        in_specs=[pl.BlockSpec((tm, d), lambda i: (i, 0)),
                  pl.BlockSpec((FFN_HALO, d), lambda i: (jnp.maximum(i * hb - 1, 0), 0)),
                  const((1, d)), mod_spec, mod_spec, mod_spec,
                  const((d, D_FF)), const((d, D_FF)), const((SUBLANES, D_FF)),
                  const((1, D_FF)), const((D_FF, d)), const((1, d))],
        out_specs=[pl.BlockSpec((tm, d), lambda i: (i, 0)),
                   pl.BlockSpec((1, SUBLANES, D_FF), lambda i: (i, 0, 0))],
        out_shape=[jax.ShapeDtypeStruct((m, d), f32),
                   jax.ShapeDtypeStruct((n_tiles, SUBLANES, D_FF), f32)],
        scratch_shapes=[pltpu.VMEM((tm, d), f32), pltpu.VMEM((FFN_HALO + tm, d), bf16)],
        compiler_params=_cparams("parallel"),
        name="ffn",
    )(x, x, g, shift, scale, gate, wa, wb, cw, cb, wd, g_final)


def _ffn_sample_kernel(x_ref, g_ref, shift_ref, scale_ref, gate_ref, p2_ref, p1_ref, wa_ref,
                       wb_ref, cw_ref, cb_ref, wd_ref, o_ref, a_ref, acc_sc):
    j = pl.program_id(0)

    @pl.when(j == 0)
    def _():
        acc_sc[...] = jnp.zeros(acc_sc.shape, f32)

    x = x_ref[...]
    h = _norm_mod(x, g_ref[...], shift_ref[0], scale_ref[0]).astype(bf16)
    a = _dot(h, wa_ref[...])
    b = _dot(h, wb_ref[...])
    conv = (cb_ref[...] + p2_ref[...] * cw_ref[0:1, :] + p1_ref[...] * cw_ref[1:2, :]
            + a * cw_ref[2:3, :])
    acc_sc[...] += _dot((_silu(conv) * b).astype(bf16), wd_ref[...])
    a_ref[...] = a

    @pl.when(j == pl.num_programs(0) - 1)
    def _():
        o_ref[...] = x + gate_ref[0] * acc_sc[...]


def _ffn_sample(x, g, shift, scale, gate, p2, p1, wa, wb, cw, cb, wd, tn):
    m, d = x.shape
    const = lambda shape: pl.BlockSpec(shape, lambda j: (0,) * len(shape))
    return pl.pallas_call(
        _ffn_sample_kernel,
        grid=(D_FF // tn,),
        in_specs=[const((m, d)), const((1, d)), const((1, m, d)), const((1, m, d)),
                  const((1, m, d)),
                  pl.BlockSpec((m, tn), lambda j: (0, j)), pl.BlockSpec((m, tn), lambda j: (0, j)),
                  pl.BlockSpec((d, tn), lambda j: (0, j)), pl.BlockSpec((d, tn), lambda j: (0, j)),
                  pl.BlockSpec((SUBLANES, tn), lambda j: (0, j)),
                  pl.BlockSpec((1, tn), lambda j: (0, j)),
                  pl.BlockSpec((tn, d), lambda j: (j, 0))],
        out_specs=[const((m, d)), pl.BlockSpec((m, tn), lambda j: (0, j))],
        out_shape=[jax.ShapeDtypeStruct((m, d), f32), jax.ShapeDtypeStruct((m, D_FF), f32)],
        scratch_shapes=[pltpu.VMEM((m, d), f32)],
        compiler_params=_cparams("arbitrary"),
        name="ffn_sample",
    )(x, g, shift, scale, gate, p2, p1, wa, wb, cw, cb, wd)


def _final_norm_kernel(x_ref, g_ref, o_ref):
    x = x_ref[...]
    o_ref[...] = x * lax.rsqrt(jnp.mean(x * x, axis=-1, keepdims=True) + EPS) * g_ref[...]


def _final_norm(x, g, tm):
    m, d = x.shape
    return pl.pallas_call(
        _final_norm_kernel,
        grid=(m // tm,),
        in_specs=[pl.BlockSpec((tm, d), lambda i: (i, 0)), pl.BlockSpec((1, d), lambda i: (0, 0))],
        out_specs=pl.BlockSpec((tm, d), lambda i: (i, 0)),
        out_shape=jax.ShapeDtypeStruct((m, d), f32),
        compiler_params=_cparams("parallel"),
        name="final_norm",
    )(x, g)


def _seq_page_copies(pt_ref, hbm, layer, seq, buf, slot, sem, n_pages):
    return [pltpu.make_async_copy(hbm.at[layer, pt_ref[seq, p]],
                                  buf.at[slot, :, pl.ds(p * PAGE_SIZE, PAGE_SIZE)], sem.at[slot])
            for p in range(n_pages)]


def _gather_seq(pt_ref, hbm, layer, buf, sem, n_pages):
    b = pl.program_id(0)
    slot = b % 2

    @pl.when(b == 0)
    def _():
        for cp in _seq_page_copies(pt_ref, hbm, layer, 0, buf, 0, sem, n_pages):
            cp.start()

    @pl.when(b + 1 < pl.num_programs(0))
    def _():
        for cp in _seq_page_copies(pt_ref, hbm, layer, b + 1, buf, 1 - slot, sem, n_pages):
            cp.start()

    for cp in _seq_page_copies(pt_ref, hbm, layer, b, buf, slot, sem, n_pages):
        cp.wait()
    return slot


def _sample_score_kernel(pt_ref, iq_ref, iw_ref, kidx_hbm, s_ref, buf, sem, *, layer, n_pages):
    slot = _gather_seq(pt_ref, kidx_hbm, layer, buf, sem, n_pages)
    d = _dot(iq_ref[0], buf[slot].astype(bf16))
    s_ref[0] = jnp.sum(jnp.maximum(d, 0.0) * iw_ref[0], axis=0, keepdims=True)


def _sample_scores(page_table, iq, iw, cache_kidx_t, layer):
    s, n_pages = page_table.shape
    length = n_pages * PAGE_SIZE
    grid_spec = pltpu.PrefetchScalarGridSpec(
        num_scalar_prefetch=1, grid=(s,),
        in_specs=[pl.BlockSpec((1, N_IDX_HEADS, IDX_DIM), lambda b, pt: (b, 0, 0)),
                  pl.BlockSpec((1, N_IDX_HEADS, 1), lambda b, pt: (b, 0, 0)),
                  pl.BlockSpec(memory_space=pl.ANY)],
        out_specs=pl.BlockSpec((1, 1, length), lambda b, pt: (b, 0, 0)),
        scratch_shapes=[pltpu.VMEM((2, IDX_DIM, length), f32), pltpu.SemaphoreType.DMA((2,))])
    out = pl.pallas_call(
        functools.partial(_sample_score_kernel, layer=layer, n_pages=n_pages),
        grid_spec=grid_spec,
        out_shape=jax.ShapeDtypeStruct((s, 1, length), f32),
        compiler_params=_cparams("arbitrary"),
        name="sample_scores",
    )(page_table, iq, iw, cache_kidx_t)
    return out.reshape(s, length)


SEL_TILE = 1024


def _sample_select_kernel(s_ref, iq_ref, ikt_ref, iw_ref, bias_ref, bnew_ref, key_sc,
                          *, n_sel, n_bits):
    rows, length = s_ref.shape
    n_tiles = length // SEL_TILE
    prod = iq_ref[...].astype(f32) * ikt_ref[...].astype(f32)
    grp = (lax.broadcasted_iota(i32, (prod.shape[1], LANES), 0) // IDX_DIM
           == lax.broadcasted_iota(i32, (prod.shape[1], LANES), 1)).astype(bf16)
    d_new = sum(_dot(p, grp) for p in _split3(prod))
    s_new = jnp.sum(jnp.maximum(d_new, 0.0) * iw_ref[...], axis=1, keepdims=True)
    key_new = _sort_key(s_new)
    key_sc[...] = _sort_key(s_ref[...])
    col = lax.broadcasted_iota(i32, (rows, SEL_TILE), 1)

    def count(pred, pred_new):
        def body(j, c):
            off = pl.multiple_of(j * SEL_TILE, SEL_TILE)
            return _fold_lanes(jnp.where(pred(key_sc[:, pl.ds(off, SEL_TILE)], j), 1, 0), c)
        c = lax.fori_loop(0, n_tiles, body, jnp.zeros((rows, LANES), i32))
        return jnp.sum(c, axis=1, keepdims=True) + jnp.where(pred_new, 1, 0)

    thr = _kth_threshold(lambda cand: count(lambda kt, j: kt >= cand, key_new >= cand),
                         (rows, 1), n_sel)
    need = n_sel - count(lambda kt, j: kt > thr, key_new > thr)
    cut = _tie_cut(lambda cand: count(lambda kt, j: (kt == thr) & (col + j * SEL_TILE < cand),
                                      (key_new == thr) & (length < cand)),
                   need, (rows, 1), n_bits)

    def bias_body(j, c):
        off = pl.multiple_of(j * SEL_TILE, SEL_TILE)
        kt = key_sc[:, pl.ds(off, SEL_TILE)]
        sel = (kt > thr) | ((kt == thr) & (col + j * SEL_TILE <= cut))
        bias_ref[:, pl.ds(off, SEL_TILE)] = jnp.where(sel, 0.0, NEG)
        return c

    lax.fori_loop(0, n_tiles, bias_body, 0)
    sel_new = (key_new > thr) | ((key_new == thr) & (length <= cut))
    bnew_ref[...] = jnp.broadcast_to(jnp.where(sel_new, 0.0, NEG), bnew_ref.shape)


def _sample_select(scores, iq, ik_tiled, iw, n_sel):
    rows, length = scores.shape
    return pl.pallas_call(
        functools.partial(_sample_select_kernel, n_sel=n_sel, n_bits=length.bit_length()),
        out_shape=[jax.ShapeDtypeStruct((rows, length), f32),
                   jax.ShapeDtypeStruct((rows, LANES), f32)],
        scratch_shapes=[pltpu.VMEM((rows, length), i32)],
        compiler_params=pltpu.CompilerParams(vmem_limit_bytes=VMEM_LIMIT),
        name="sample_select",
    )(scores, iq, ik_tiled, iw)


SUFFIX_CHUNK = 256


def _suffix_bias_kernel(pt_ref, lfnew_ref, lf_hbm, bias_ref, buf, sem, *, layer, n_pages):
    slot = _gather_seq(pt_ref, lf_hbm, layer, buf, sem, n_pages)
    w = SUFFIX_CHUNK
    tri = (lax.broadcasted_iota(i32, (w, w), 0)
           > lax.broadcasted_iota(i32, (w, w), 1)).astype(bf16)
    n_chunks = n_pages * PAGE_SIZE // w
    chunk = lambda c: buf[slot, :, c * w:(c + 1) * w]
    totals = [jnp.sum(chunk(c), axis=1, keepdims=True) for c in range(n_chunks)]
    carry = lfnew_ref[0]
    for c in reversed(range(n_chunks)):
        bias_ref[0, :, c * w:(c + 1) * w] = sum(_dot(p, tri) for p in _split3(chunk(c))) + carry
        carry = carry + totals[c]


def _suffix_bias(page_table, lf_new, cache_logf_t, layer):
    s, n_pages = page_table.shape
    length = n_pages * PAGE_SIZE
    grid_spec = pltpu.PrefetchScalarGridSpec(
        num_scalar_prefetch=1, grid=(s,),
        in_specs=[pl.BlockSpec((1, N_HEADS, 1), lambda b, pt: (b, 0, 0)),
                  pl.BlockSpec(memory_space=pl.ANY)],
        out_specs=pl.BlockSpec((1, N_HEADS, length), lambda b, pt: (b, 0, 0)),
        scratch_shapes=[pltpu.VMEM((2, N_HEADS, length), f32), pltpu.SemaphoreType.DMA((2,))])
    return pl.pallas_call(
        functools.partial(_suffix_bias_kernel, layer=layer, n_pages=n_pages),
        grid_spec=grid_spec,
        out_shape=jax.ShapeDtypeStruct((s, N_HEADS, length), f32),
        compiler_params=_cparams("arbitrary"),
        name="suffix_bias",
    )(page_table, lf_new, cache_logf_t)


DEC_PAGES = 8
DEC_SLOTS = 3


def _kv_chunk_copies(pt_ref, k_hbm, v_hbm, layer, seq, chunk, kbuf, vbuf, slot, sem):
    cps = []
    for i in range(DEC_PAGES):
        page = pt_ref[seq, chunk * DEC_PAGES + i]
        cps.append(pltpu.make_async_copy(k_hbm.at[layer, page], kbuf.at[slot, i], sem.at[0, slot]))
        cps.append(pltpu.make_async_copy(v_hbm.at[layer, page], vbuf.at[slot, i], sem.at[1, slot]))
    return cps


def _decode_attn_kernel(pt_ref, q_ref, knew_ref, vnew_ref, bnew_ref, bias_ref, k_hbm, v_hbm,
                        o_ref, kbuf, vbuf, sem, *, layer, n_chunks):
    b = pl.program_id(0)
    q = q_ref[0]
    width = DEC_PAGES * PAGE_SIZE
    copies = functools.partial(_kv_chunk_copies, pt_ref, k_hbm, v_hbm, layer)

    def start(seq, chunk, slot):
        for cp in copies(seq, chunk, kbuf, vbuf, slot, sem):
            cp.start()

    ahead = DEC_SLOTS - 1
    base = b * n_chunks

    @pl.when(b == 0)
    def _():
        for c in range(ahead):
            start(0, c, c)

    kn = knew_ref[0].astype(bf16).astype(f32)
    m = jnp.sum(q.astype(f32) * kn, axis=1, keepdims=True) + bnew_ref[0][:, 0:1]
    l = jnp.ones_like(m)
    acc = jnp.broadcast_to(vnew_ref[0].astype(bf16).astype(f32), (N_HEADS, KV_COLS))

    for c in range(n_chunks):
        slot = lax.rem(base + c, DEC_SLOTS)
        nxt = c + ahead
        nslot = lax.rem(base + nxt, DEC_SLOTS)
        if nxt < n_chunks:
            start(b, nxt, nslot)
        else:
            @pl.when(b + 1 < pl.num_programs(0))
            def _():
                start(b + 1, nxt - n_chunks, nslot)
        for cp in copies(b, c, kbuf, vbuf, slot, sem):
            cp.wait()
        s = jnp.concatenate([_dot(q, kbuf[slot, i].astype(bf16)) for i in range(DEC_PAGES)],
                            axis=1) + bias_ref[0, :, c * width:(c + 1) * width]
        m_new = jnp.maximum(m, jnp.max(s, axis=1, keepdims=True))
        p = jnp.exp(s - m_new)
        alpha = jnp.exp(m - m_new)
        l = alpha * l + jnp.sum(p, axis=1, keepdims=True)
        pb = p.astype(bf16)
        pv = sum(_dot_nt(pb[:, i * PAGE_SIZE:(i + 1) * PAGE_SIZE], vbuf[slot, i].astype(bf16))
                 for i in range(DEC_PAGES))
        acc = alpha * acc + pv
        m = m_new
    o_ref[0] = acc / l


def _decode_attn(page_table, qmat, k_new, v_new, bias_new, bias, cache_kt, cache_vt, layer):
    s, n_pages = page_table.shape
    hb = bias.shape[1]
    n_chunks = n_pages // DEC_PAGES
    assert n_chunks * DEC_PAGES == n_pages and n_chunks >= DEC_SLOTS - 1
    row = lambda shape: pl.BlockSpec(shape, lambda b, pt: (b, 0, 0))
    hbm = pl.BlockSpec(memory_space=pl.ANY)
    width = DEC_PAGES * PAGE_SIZE
    grid_spec = pltpu.PrefetchScalarGridSpec(
        num_scalar_prefetch=1, grid=(s,),
        in_specs=[row((1, N_HEADS, KV_COLS)), row((1, 1, KV_COLS)), row((1, 1, KV_COLS)),
                  row((1, hb, LANES)), row((1, hb, n_pages * PAGE_SIZE)), hbm, hbm],
        out_specs=row((1, N_HEADS, KV_COLS)),
        scratch_shapes=[pltpu.VMEM((DEC_SLOTS, DEC_PAGES, KV_COLS, PAGE_SIZE), f32),
                        pltpu.VMEM((DEC_SLOTS, DEC_PAGES, KV_COLS, PAGE_SIZE), f32),
                        pltpu.SemaphoreType.DMA((2, DEC_SLOTS))])
    return pl.pallas_call(
        functools.partial(_decode_attn_kernel, layer=layer, n_chunks=n_chunks),
        grid_spec=grid_spec,
        out_shape=jax.ShapeDtypeStruct((s, N_HEADS, KV_COLS), f32),
        compiler_params=_cparams("arbitrary"),
        name="decode_attn",
    )(page_table, qmat, k_new, v_new, bias_new, bias, cache_kt, cache_vt)


def _pad_cols(w, n):
    return jnp.pad(w, ((0, 0), (0, n)))


def _decode_q(q_bf):
    s = q_bf.shape[0]
    onehot = (jnp.arange(N_HEADS)[:, None] // GROUP == jnp.arange(N_KV_HEADS)[None, :])
    q = q_bf.reshape(s, N_HEADS, 1, HEAD_DIM) * onehot[None, :, :, None].astype(q_bf.dtype)
    return q.reshape(s, N_HEADS, KV_COLS)


def _decode_o(o_full):
    s = o_full.shape[0]
    o = o_full.reshape(s, N_KV_HEADS, GROUP, N_KV_HEADS, HEAD_DIM)
    return jnp.stack([o[:, g, :, g, :] for g in range(N_KV_HEADS)], axis=1).reshape(s, -1)


def kernel(x_prompt, x_sample, cache_k_a, cache_v_a, cache_kidx_a, cache_k_b, cache_v_b,
           cache_logf_b, state_conv, page_table, c_prompt, c_sample, w_ada, b_ada, g_mix,
           g_ffn, w_in_a, w_o_a, w_in_b, b_f, w_o_b, w_up, conv_w, conv_b, w_down, g_final):
    bsz, t, d = x_prompt.shape
    s = x_sample.shape[0]
    depth = w_ada.shape[0]
    m = bsz * t
    past_len = page_table.shape[1] * PAGE_SIZE
    n_pool = cache_k_a.shape[1]
    tm = 512
    tq = 256

    pad_rows = (-(bsz + s)) % SUBLANES
    c_all = jnp.concatenate([c_prompt, c_sample, jnp.zeros((pad_rows, d), f32)], axis=0)
    mods = _adaln(c_all, w_ada, b_ada)
    mods_p = mods[:, :bsz].reshape(depth, bsz, 6, d)
    mods_s = mods[:, bsz:bsz + s].reshape(depth, s, 6, d)

    tabs_p = _rope_tables(jnp.arange(t))
    tabs_pt = _rope_tables_t(jnp.arange(t))
    tabs_s = _rope_tables(jnp.full((s,), past_len))
    zero_bf = jnp.zeros((1, LANES), f32)
    zero_col = jnp.zeros((N_HEADS, 1), f32)

    def rows_t(a):
        return a.reshape(bsz, N_KV_HEADS, HEAD_DIM, t).transpose(0, 3, 1, 2)

    xp = x_prompt.reshape(m, d)
    xs = x_sample.reshape(s, d)
    kv_shape_p = (bsz, t, N_KV_HEADS, HEAD_DIM)
    kv_shape_s = (s, 1, N_KV_HEADS, HEAD_DIM)
    ka_p, va_p, ia_p, kb_p, vb_p, lb_p, cv_p = [], [], [], [], [], [], []
    ka_s, va_s, ia_s, kb_s, vb_s, lb_s, cv_s = [], [], [], [], [], [], []

    def pages_t(cache):
        ct = jnp.moveaxis(cache, 2, -1)
        return ct.reshape(ct.shape[:2] + (-1, PAGE_SIZE))

    cache_k_a, cache_v_a, cache_kidx_a = pages_t(cache_k_a), pages_t(cache_v_a), pages_t(cache_kidx_a)
    cache_k_b, cache_v_b, cache_logf_b = pages_t(cache_k_b), pages_t(cache_v_b), pages_t(cache_logf_b)

    for i in range(depth):
        j = i // 2
        mp = [mods_p[i, :, c][:, None, :] for c in range(6)]
        ms = [mods_s[i, :, c][None] for c in range(6)]
        g_m = g_mix[i][None]
        if i % 2 == 0:
            w = w_in_a[j]
            n_qkvi = QKV_COLS + N_IDX_HEADS * IDX_DIM + IDX_DIM
            w_in = jnp.concatenate([_pad_cols(w[:, :n_qkvi], LANES - IDX_DIM),
                                    _pad_cols(w[:, n_qkvi:], LANES - N_IDX_HEADS)],
                                   axis=1).astype(bf16)
            w_o = w_o_a[j].astype(bf16)
            nq = N_HEADS * HEAD_DIM
            n_iq = QKV_COLS + N_IDX_HEADS * IDX_DIM
            wk = w[:, nq:nq + KV_COLS]
            w_row = jnp.concatenate([wk, _pad_cols(w[:, n_iq:n_qkvi], LANES - IDX_DIM)],
                                    axis=1).astype(bf16)
            w_t = jnp.concatenate([w[:, :nq], w[:, nq + KV_COLS:QKV_COLS], wk, w[:, QKV_COLS:n_qkvi],
                                   _pad_cols(w[:, n_qkvi:], IW_ROWS - N_IDX_HEADS)],
                                  axis=1).T.astype(bf16)
            qt, vt, kpad, kt32, vt32, iqt, iwt, ikb, ikt32 = _in_proj_t(
                xp, g_m, mp[0], mp[1], w_row, w_t, tabs_p, tabs_pt, zero_col, True, tm, t)
            xp = _dsa_attn(xp, mp[2], qt, kpad, vt, iqt, iwt, ikb.reshape(bsz, t, IDX_DIM),
                           w_o.T, tq, min(TOPK_MAX, t // 4))
            ka_p.append(rows_t(kt32))
            va_p.append(rows_t(vt32))
            ia_p.append(ikt32.transpose(0, 2, 1))
            q, k32, kb, v32, vb, iq, ik32, ikb, iw = _in_proj(
                xs, g_m, ms[0], ms[1], w_in, tabs_s, zero_bf, SEGS_A, OUTS_A, s, s)
            scores = _sample_scores(page_table, iq.reshape(s, N_IDX_HEADS, IDX_DIM),
                                    iw[:, :N_IDX_HEADS].reshape(s, N_IDX_HEADS, 1),
                                    cache_kidx_a, j)
            bias, bias_new = _sample_select(scores, iq, jnp.tile(ikb[:, :IDX_DIM], (1, N_IDX_HEADS)),
                                            iw, min(TOPK_MAX, (past_len + 1) // 4))
            o_full = _decode_attn(page_table, _decode_q(q), k32[:, None], v32[:, None],
                                  bias_new[:, None], bias[:, None], cache_k_a, cache_v_a, j)
            xs = _out_proj(xs, _decode_o(o_full).astype(bf16), w_o, ms[2], s, s)
            ka_s.append(k32.reshape(kv_shape_s))
            va_s.append(v32.reshape(kv_shape_s))
            ia_s.append(ik32[:, :IDX_DIM].reshape(s, 1, IDX_DIM))
        else:
            w_in = _pad_cols(w_in_b[j], LANES - N_HEADS).astype(bf16)
            w_o = w_o_b[j].astype(bf16)
            bf_pad = _pad_cols(b_f[j][None], LANES - N_HEADS)
            nq = N_HEADS * HEAD_DIM
            wk = w_in[:, nq:nq + KV_COLS]
            w_t = jnp.concatenate([w_in[:, :nq], w_in[:, nq + KV_COLS:QKV_COLS], wk,
                                   w_in[:, QKV_COLS:QKV_COLS + N_HEADS]], axis=1).T
            qt, vt, kpad, kt32, vt32, lft = _in_proj_t(
                xp, g_m, mp[0], mp[1], wk, w_t, tabs_p, tabs_pt, b_f[j][:, None], False, tm, t)
            c_t = _cumsum_t(lft, tq)
            c_pad = jnp.pad(c_t.transpose(0, 2, 1), ((0, 0), (0, 0), (0, LANES - N_HEADS)))
            xp = _fox_attn(xp, mp[2], qt, kpad, vt, c_pad, c_t, w_o.T, tq)
            kb_p.append(rows_t(kt32))
            vb_p.append(rows_t(vt32))
            lb_p.append(lft.transpose(0, 2, 1))
            q, k32, kb, v32, vb, lf = _in_proj(
                xs, g_m, ms[0], ms[1], w_in, tabs_s, bf_pad, SEGS_B, OUTS_B, s, s)
            bias = _suffix_bias(page_table, lf[:, :N_HEADS, None], cache_logf_b, j)
            o_full = _decode_attn(page_table, _decode_q(q), k32[:, None], v32[:, None],
                                  jnp.zeros((s, N_HEADS, LANES), f32), bias,
                                  cache_k_b, cache_v_b, j)
            xs = _out_proj(xs, _decode_o(o_full).astype(bf16), w_o, ms[2], s, s)
            kb_s.append(k32.reshape(kv_shape_s))
            vb_s.append(v32.reshape(kv_shape_s))
            lb_s.append(lf[:, :N_HEADS].reshape(s, 1, N_HEADS))

        wa = w_up[i][:, :D_FF].astype(bf16)
        wb = w_up[i][:, D_FF:].astype(bf16)
        wd = w_down[i].astype(bf16)
        cw = jnp.pad(conv_w[i], ((0, SUBLANES - CONV_W), (0, 0)))
        cb = conv_b[i][None]
        g_f = g_ffn[i][None]
        xp, tail = _ffn(xp, g_f, mp[3], mp[4], mp[5], wa, wb, cw, cb, wd, g_final[None],
                        i == depth - 1, tm, t)
        cv_p.append(tail.reshape(bsz, t // tm, SUBLANES, D_FF)[:, -1, SUBLANES - (CONV_W - 1):])
        xs, a_s = _ffn_sample(xs, g_f, ms[3], ms[4], ms[5], state_conv[i][:, 0],
                              state_conv[i][:, 1], wa, wb, cw, cb, wd, D_FF // 2)
        cv_s.append(jnp.stack([state_conv[i][:, 1], a_s], axis=1))

    y_prompt = xp.reshape(bsz, t, d)
    y_sample = _final_norm(xs, g_final[None], s).reshape(s, 1, d)
    return (y_prompt, y_sample,
            jnp.stack(ka_p), jnp.stack(va_p), jnp.stack(ia_p),
            jnp.stack(kb_p), jnp.stack(vb_p), jnp.stack(lb_p), jnp.stack(cv_p),
            jnp.stack(ka_s), jnp.stack(va_s), jnp.stack(ia_s),
            jnp.stack(kb_s), jnp.stack(vb_s), jnp.stack(lb_s), jnp.stack(cv_s))
```

```python
import functools

import jax
import jax.numpy as jnp
from jax import lax
from jax.experimental import pallas as pl
from jax.experimental.pallas import tpu as pltpu

f32 = jnp.float32
bf16 = jnp.bfloat16
i32 = jnp.int32

D_MODEL = 1024
N_HEADS = 16
HEAD_DIM = 64
N_KV_HEADS = 4
GROUP = N_HEADS // N_KV_HEADS
KV_COLS = N_KV_HEADS * HEAD_DIM
QKV_COLS = (N_HEADS + 2 * N_KV_HEADS) * HEAD_DIM
ROPE_DIMS = HEAD_DIM // 4
ROPE_HALF = ROPE_DIMS // 2
ROPE_THETA = 500000.0
N_IDX_HEADS = 8
IDX_DIM = 64
TOPK_MAX = 256
D_FF = 2816
CONV_W = 3
EPS = 1e-6
PAGE_SIZE = 128
Q_SCALE = HEAD_DIM ** -0.5
LOG2E = 1.4426950408889634

LANES = 128
SUBLANES = 8
VMEM_LIMIT = 56 * 1024 * 1024
NEG = -1e30
INT_MIN = -2 ** 31
INT_MAX = 2 ** 31 - 1
KEY_NEG_INF = (0xFF800000 ^ 0x7FFFFFFF) - 2 ** 32


def _cparams(*sem):
    return pltpu.CompilerParams(dimension_semantics=sem, vmem_limit_bytes=VMEM_LIMIT)


def _dot(a, b):
    return jnp.dot(a, b, preferred_element_type=f32)


def _dot_nt(a, b):
    return lax.dot_general(a, b, (((1,), (1,)), ((), ())), preferred_element_type=f32)


def _split3(x):
    hi = x.astype(bf16)
    r1 = x - hi.astype(f32)
    mid = r1.astype(bf16)
    lo = (r1 - mid.astype(f32)).astype(bf16)
    return hi, mid, lo


def _norm_mod(x, g, shift, scale):
    xn = x * lax.rsqrt(jnp.mean(x * x, axis=-1, keepdims=True) + EPS)
    return (xn * g) * (1.0 + scale) + shift


def _silu(x):
    return x * jax.nn.sigmoid(x)


def _sort_key(s):
    bits = pltpu.bitcast(s, i32)
    return bits ^ ((bits >> 31) & 0x7FFFFFFF)


def _adaln_kernel(c_ref, w_ref, b_ref, o_ref):
    a = _silu(c_ref[...]).astype(bf16)
    o_ref[0] = _dot(a, w_ref[0].astype(bf16)) + b_ref[0]


def _adaln(c_all, w_ada, b_ada):
    depth, d, n = w_ada.shape
    rows = c_all.shape[0]
    tn = 1536
    return pl.pallas_call(
        _adaln_kernel,
        grid=(depth, n // tn),
        in_specs=[pl.BlockSpec((rows, d), lambda l, j: (0, 0)),
                  pl.BlockSpec((1, d, tn), lambda l, j: (l, 0, j)),
                  pl.BlockSpec((1, 1, tn), lambda l, j: (l, 0, j))],
        out_specs=pl.BlockSpec((1, rows, tn), lambda l, j: (l, 0, j)),
        out_shape=jax.ShapeDtypeStruct((depth, rows, n), f32),
        compiler_params=_cparams("parallel", "parallel"),
        name="adaln",
    )(c_all, w_ada, b_ada.reshape(depth, 1, n))


SEGS_A = ((0, 1024, "rope", ((0, Q_SCALE),)),
          (1024, 256, "rope", ((1, 1.0), (2, 1.0))),
          (1280, 256, None, ((3, 1.0), (4, 1.0))),
          (1536, 512, "rope", ((5, 1.0),)),
          (2048, 128, "rope", ((6, 1.0), (7, 1.0))),
          (2176, 128, None, ((8, 1.0),)))
OUTS_A = ((1024, bf16), (256, f32), (256, bf16), (256, f32), (256, bf16),
          (512, bf16), (128, f32), (128, bf16), (128, f32))
SEGS_B = ((0, 1024, None, ((0, Q_SCALE),)),
          (1024, 256, None, ((1, 1.0), (2, 1.0))),
          (1280, 256, None, ((3, 1.0), (4, 1.0))),
          (1536, 128, "logf", ((5, 1.0),)))
OUTS_B = ((1024, bf16), (256, f32), (256, bf16), (256, f32), (256, bf16), (128, f32))


def _in_proj_kernel(x_ref, g_ref, shift_ref, scale_ref, w_ref, cos_ref, sa_ref, sb_ref,
                    bf_ref, *out_refs, segs):
    h = _norm_mod(x_ref[...], g_ref[...], shift_ref[0], scale_ref[0]).astype(bf16)
    for c0, width, epi, outs in segs:
        for c in range(width // LANES):
            lo = c0 + c * LANES
            a = _dot(h, w_ref[:, lo:lo + LANES])
            if epi == "rope":
                a = (a * cos_ref[...] + pltpu.roll(a, ROPE_HALF, 1) * sa_ref[...]
                     + pltpu.roll(a, LANES - ROPE_HALF, 1) * sb_ref[...])
            elif epi == "logf":
                z = a + bf_ref[...]
                a = jnp.minimum(z, 0.0) - jnp.log(1.0 + jnp.exp(-jnp.abs(z)))
            for oi, sc in outs:
                val = a if sc == 1.0 else a * sc
                out_refs[oi][:, c * LANES:(c + 1) * LANES] = val.astype(out_refs[oi].dtype)


def _in_proj(x, g, shift, scale, w, tabs, b_f, segs, outs, tm, rows_per_mod):
    m, d = x.shape
    n = w.shape[1]
    r = shift.shape[1]
    cos, sa, sb = tabs
    tab_blocks = cos.shape[0] // tm
    mod_spec = pl.BlockSpec((1, r, d), lambda i: (i * tm // rows_per_mod, 0, 0))
    tab_spec = pl.BlockSpec((tm, LANES), lambda i: (i % tab_blocks, 0))
    return pl.pallas_call(
        functools.partial(_in_proj_kernel, segs=segs),
        grid=(m // tm,),
        in_specs=[pl.BlockSpec((tm, d), lambda i: (i, 0)),
                  pl.BlockSpec((1, d), lambda i: (0, 0)),
                  mod_spec, mod_spec,
                  pl.BlockSpec((d, n), lambda i: (0, 0)),
                  tab_spec, tab_spec, tab_spec,
                  pl.BlockSpec((1, LANES), lambda i: (0, 0))],
        out_specs=[pl.BlockSpec((tm, wd), lambda i: (i, 0)) for wd, _ in outs],
        out_shape=[jax.ShapeDtypeStruct((m, wd), dt) for wd, dt in outs],
        compiler_params=_cparams("parallel"),
        name="in_proj",
    )(x, g, shift, scale, w, cos, sa, sb, b_f)


def _rope_tables(pos):
    inv_freq = ROPE_THETA ** (-jnp.arange(ROPE_HALF, dtype=f32) / ROPE_HALF)
    ang = pos.astype(f32)[:, None] * inv_freq[None, :]
    cos, sin = jnp.cos(ang), jnp.sin(ang)
    t = pos.shape[0]
    ones = jnp.ones((t, HEAD_DIM - ROPE_DIMS), f32)
    zeros = jnp.zeros((t, HEAD_DIM - ROPE_DIMS), f32)
    zh = jnp.zeros((t, ROPE_HALF), f32)
    c = jnp.concatenate([cos, cos, ones], axis=1)
    sa = jnp.concatenate([zh, sin, zeros], axis=1)
    sb = jnp.concatenate([-sin, zh, zeros], axis=1)
    rep = LANES // HEAD_DIM
    return tuple(jnp.tile(a, (1, rep)) for a in (c, sa, sb))


WT_Q = 0
WT_V = N_HEADS * HEAD_DIM
WT_K = WT_V + KV_COLS
WT_X = WT_K + KV_COLS
WT_IK = WT_X + N_IDX_HEADS * IDX_DIM
WT_IW = WT_IK + IDX_DIM
IW_ROWS = 2 * SUBLANES
ROW_BLOCK = 512


def _rope_rows(a, cos_t, sin_t):
    parts = []
    for b0 in range(0, a.shape[0], HEAD_DIM):
        x1 = a[b0:b0 + ROPE_HALF]
        x2 = a[b0 + ROPE_HALF:b0 + ROPE_DIMS]
        parts += [x1 * cos_t - x2 * sin_t, x1 * sin_t + x2 * cos_t, a[b0 + ROPE_DIMS:b0 + HEAD_DIM]]
    return jnp.concatenate(parts, axis=0)


def _in_proj_t_kernel(x_ref, g_ref, shift_ref, scale_ref, w_ref, wt_ref, cos_ref, sa_ref, sb_ref,
                      cost_ref, sint_ref, bf_ref, *outs, is_a):
    if is_a:
        qt_ref, vt_ref, kpad_ref, kt32_ref, vt32_ref, iqt_ref, iwt_ref, ikb_ref, ikt32_ref = outs
    else:
        qt_ref, vt_ref, kpad_ref, kt32_ref, vt32_ref, lft_ref = outs
    h = _norm_mod(x_ref[...], g_ref[...], shift_ref[0], scale_ref[0]).astype(bf16)
    cos_t, sin_t = cost_ref[...], sint_ref[...]

    def rows(r0, n):
        blk = _dot_nt(wt_ref[r0:r0 + n, :], h)
        return [blk[c * LANES:(c + 1) * LANES] for c in range(n // LANES)]

    def rope_cols(a):
        return (a * cos_ref[...] + pltpu.roll(a, ROPE_HALF, 1) * sa_ref[...]
                + pltpu.roll(a, LANES - ROPE_HALF, 1) * sb_ref[...])

    for r0 in range(0, N_HEADS * HEAD_DIM, ROW_BLOCK):
        for c, a in enumerate(rows(WT_Q + r0, ROW_BLOCK)):
            if is_a:
                a = _rope_rows(a, cos_t, sin_t)
            lo = r0 + c * LANES
            qt_ref[lo:lo + LANES, :] = (a * (Q_SCALE * LOG2E)).astype(bf16)
    for c, (v, k) in enumerate(zip(rows(WT_V, KV_COLS), rows(WT_K, KV_COLS))):
        sl = slice(c * LANES, (c + 1) * LANES)
        vt32_ref[0, sl, :] = v
        vt_ref[sl, :] = v.astype(bf16)
        kt32_ref[0, sl, :] = _rope_rows(k, cos_t, sin_t) if is_a else k
    if is_a:
        for c, a in enumerate(rows(WT_X, N_IDX_HEADS * IDX_DIM)):
            iqt_ref[c * LANES:(c + 1) * LANES, :] = _rope_rows(a, cos_t, sin_t).astype(bf16)
        ikt32_ref[0] = _rope_rows(_dot_nt(wt_ref[WT_IK:WT_IK + IDX_DIM, :], h), cos_t, sin_t)
        iwt_ref[...] = _dot_nt(wt_ref[WT_IW:WT_IW + IW_ROWS, :], h)
    else:
        z = _dot_nt(wt_ref[WT_X:WT_X + N_HEADS, :], h) + bf_ref[...]
        lft_ref[0] = jnp.minimum(z, 0.0) - jnp.log(1.0 + jnp.exp(-jnp.abs(z)))

    lane = lax.broadcasted_iota(i32, (1, LANES), 1)
    std = _dot(h, w_ref[...])
    for c in range(KV_COLS // LANES):
        a = std[:, c * LANES:(c + 1) * LANES]
        if is_a:
            a = rope_cols(a)
        kpad_ref[0, 2 * c] = jnp.where(lane < HEAD_DIM, a, 0.0).astype(bf16)
        kpad_ref[0, 2 * c + 1] = jnp.where(lane < HEAD_DIM, pltpu.roll(a, HEAD_DIM, 1),
                                           0.0).astype(bf16)
    if is_a:
        ikb_ref[...] = rope_cols(std[:, KV_COLS:KV_COLS + LANES])[:, :IDX_DIM].astype(bf16)


def _in_proj_t(x, g, shift, scale, w, wt, tabs, tabs_t, b_f, is_a, tm, t):
    m, d = x.shape
    bsz = m // t
    tpb = t // tm
    col = lambda rows_: pl.BlockSpec((rows_, tm), lambda i: (0, i))
    seq = lambda rows_: pl.BlockSpec((1, rows_, tm), lambda i: (i // tpb, 0, i % tpb))
    seq_shape = lambda rows_: jax.ShapeDtypeStruct((bsz, rows_, t), f32)
    const = lambda a: pl.BlockSpec(a.shape, lambda i: (0,) * a.ndim)
    mod_spec = pl.BlockSpec((1, 1, d), lambda i: (i // tpb, 0, 0))
    tab_spec = pl.BlockSpec((tm, LANES), lambda i: (i % tpb, 0))
    tabt_spec = pl.BlockSpec((ROPE_HALF, tm), lambda i: (0, i % tpb))
    nq = N_HEADS * HEAD_DIM
    out_specs = [col(nq), col(KV_COLS),
                 pl.BlockSpec((1, N_KV_HEADS, tm, LANES), lambda i: (i // tpb, 0, i % tpb, 0)),
                 seq(KV_COLS), seq(KV_COLS)]
    out_shape = [jax.ShapeDtypeStruct((nq, m), bf16), jax.ShapeDtypeStruct((KV_COLS, m), bf16),
                 jax.ShapeDtypeStruct((bsz, N_KV_HEADS, t, LANES), bf16),
                 seq_shape(KV_COLS), seq_shape(KV_COLS)]
    if is_a:
        out_specs += [col(N_IDX_HEADS * IDX_DIM), col(IW_ROWS),
                      pl.BlockSpec((tm, IDX_DIM), lambda i: (i, 0)), seq(IDX_DIM)]
        out_shape += [jax.ShapeDtypeStruct((N_IDX_HEADS * IDX_DIM, m), bf16),
                      jax.ShapeDtypeStruct((IW_ROWS, m), f32),
                      jax.ShapeDtypeStruct((m, IDX_DIM), bf16), seq_shape(IDX_DIM)]
    else:
        out_specs += [seq(N_HEADS)]
        out_shape += [seq_shape(N_HEADS)]
    return pl.pallas_call(
        functools.partial(_in_proj_t_kernel, is_a=is_a),
        grid=(m // tm,),
        in_specs=[pl.BlockSpec((tm, d), lambda i: (i, 0)), const(g), mod_spec, mod_spec,
                  const(w), const(wt), tab_spec, tab_spec, tab_spec, tabt_spec, tabt_spec,
                  const(b_f)],
        out_specs=out_specs, out_shape=out_shape,
        compiler_params=_cparams("parallel"),
        name="in_proj_t",
    )(x, g, shift, scale, w, wt, *tabs, *tabs_t, b_f)


def _rope_tables_t(pos):
    inv_freq = ROPE_THETA ** (-jnp.arange(ROPE_HALF, dtype=f32) / ROPE_HALF)
    ang = inv_freq[:, None] * pos.astype(f32)[None, :]
    return jnp.cos(ang), jnp.sin(ang)


AUG_STRIDE = 4
AUG_ONES = GROUP * AUG_STRIDE


ATTN_TK = 256
SUM_ROWS = 2 * SUBLANES
ACC_ROWS = HEAD_DIM + SUM_ROWS


def _t_init(m_sc, acc_sc):
    m_sc[...] = jnp.full(m_sc.shape, NEG, f32)
    acc_sc[...] = jnp.zeros(acc_sc.shape, f32)


def _t_step(s, vt1, h, m_sc, acc_sc):
    m_old = m_sc[h]
    m_new = jnp.maximum(m_old, jnp.max(s, axis=0, keepdims=True))
    p = jnp.exp2((s - m_new).astype(bf16))
    acc_sc[h] = jnp.exp2(m_old - m_new) * acc_sc[h] + _dot(vt1, p)
    m_sc[h] = m_new


STALE_LIMIT = 64.0


def _t_step_stale(s, vt1, h, m_sc, acc_sc, rise_sc):
    m_old = m_sc[h]
    p = jnp.exp2((s - m_old).astype(bf16))
    s_max = jnp.max(s, axis=0, keepdims=True)
    m_new = jnp.maximum(m_old, s_max)
    acc_sc[h] = (acc_sc[h] + _dot(vt1, p)) * jnp.exp2(m_old - m_new)
    m_sc[h] = m_new
    rise_sc[h] = jnp.maximum(rise_sc[h], s_max - m_old)


def _t_tile(off, tk, kt_of, vt_ref, qa_sc, post, step):
    kts = [kt_of(g) for g in range(N_KV_HEADS)]
    ones = jnp.ones((SUM_ROWS, tk), bf16)
    vts = [jnp.concatenate([vt_ref[g * HEAD_DIM:(g + 1) * HEAD_DIM, pl.ds(off, tk)], ones], axis=0)
           for g in range(N_KV_HEADS)]
    ahead = 2
    pending = [_dot(kts[g], qa_sc[g]) for g in range(ahead)]
    for g in range(N_KV_HEADS):
        s = pending.pop(0)
        if g + ahead < N_KV_HEADS:
            pending.append(_dot(kts[g + ahead], qa_sc[g + ahead]))
        step(post(s), vts[g], g)


def _t_sweep(first, rest, m_sc, acc_sc, rise_sc):
    def exact(s, vt1, g):
        _t_step(s, vt1, g, m_sc, acc_sc)

    def stale(s, vt1, g):
        _t_step_stale(s, vt1, g, m_sc, acc_sc, rise_sc)

    rise_sc[...] = jnp.full(rise_sc.shape, NEG, f32)
    _t_init(m_sc, acc_sc)
    first(exact)
    rest(stale)

    @pl.when(jnp.max(rise_sc[...]) > STALE_LIMIT)
    def _():
        _t_init(m_sc, acc_sc)
        first(exact)
        rest(exact)


def _t_fill_q(qa_sc, h, tq, q, aug):
    g, jj = divmod(h, GROUP)
    qa_sc[g, :HEAD_DIM, jj * tq:(jj + 1) * tq] = q
    qa_sc[g, HEAD_DIM:, jj * tq:(jj + 1) * tq] = aug


def _t_finish(tq, acc_sc, ot_sc):
    for h in range(N_HEADS):
        g, jj = divmod(h, GROUP)
        cols = slice(jj * tq, (jj + 1) * tq)
        ot_sc[h * HEAD_DIM:(h + 1) * HEAD_DIM, :] = (
            acc_sc[g, :HEAD_DIM, cols] / acc_sc[g, HEAD_DIM:HEAD_DIM + 1, cols]).astype(bf16)


def _t_out(x_ref, gate_ref, wot_ref, ot_sc, o_ref):
    y_t = _dot(wot_ref[...], ot_sc[...])
    o_ref[...] = x_ref[...] + gate_ref[0] * y_t.T


def _fox_attn_kernel(x_ref, gate_ref, qt_ref, kpad_ref, vt_ref, cpad_ref, ct_ref, wot_ref,
                     o_ref, kaug_sc, m_sc, rise_sc, acc_sc, qa_sc, ot_sc, *, tq, tk, chunk):
    qi = pl.program_id(1)
    t = kaug_sc.shape[1]

    @pl.when(qi == 0)
    def _():
        r = lax.broadcasted_iota(i32, (LANES, LANES), 0)
        l = lax.broadcasted_iota(i32, (LANES, LANES), 1) - HEAD_DIM
        lane = lax.broadcasted_iota(i32, (1, LANES), 1) - HEAD_DIM
        ones = jnp.where((lane >= AUG_ONES) & (lane < AUG_ONES + 3), 1.0, 0.0)
        for g in range(N_KV_HEADS):
            perm = [((l >= 0) & (l < AUG_ONES) & ((l >> 2) == r - g * GROUP)
                     & ((l & 3) == p)).astype(bf16) for p in range(3)]

            def body(i, c, g=g, perm=perm):
                off = pl.multiple_of(i * chunk, chunk)
                pieces = _split3(cpad_ref[0, pl.ds(off, chunk), :] * -LOG2E)
                aug = sum(_dot(pc, pm) for pc, pm in zip(pieces, perm)) + ones
                kaug_sc[g, pl.ds(off, chunk), :] = (
                    kpad_ref[0, g, pl.ds(off, chunk), :].astype(f32) + aug).astype(bf16)
                return c

            lax.fori_loop(0, t // chunk, body, 0)

    ri = lax.broadcasted_iota(i32, (HEAD_DIM, tq), 0)
    wide = (tk, GROUP * tq)
    d_iota = (lax.broadcasted_iota(i32, wide, 0)
              - (lax.broadcasted_iota(i32, wide, 1) & (tq - 1)))
    for h in range(N_HEADS):
        jj = h % GROUP
        cq = _split3(ct_ref[0, h:h + 1, :] * LOG2E)
        aug = jnp.where((ri >= jj * AUG_STRIDE) & (ri < jj * AUG_STRIDE + 3), 1.0, 0.0)
        for p in range(3):
            aug = jnp.where(ri == AUG_ONES + p, cq[p].astype(f32), aug)
        _t_fill_q(qa_sc, h, tq, qt_ref[h * HEAD_DIM:(h + 1) * HEAD_DIM, :], aug.astype(bf16))
    def tile(j, post, step):
        off = pl.multiple_of(j * tk, tk)
        _t_tile(off, tk, lambda g: kaug_sc[g, pl.ds(off, tk), :], vt_ref, qa_sc, post, step)

    sub = tq // tk

    def diagonal(step):
        for dd in range(sub):
            tile(qi * sub + dd, lambda s, dd=dd: jnp.where(d_iota <= -dd * tk, s, NEG), step)

    def earlier(step):
        def body(i, c):
            tile(qi * sub - 1 - i, lambda s: s, step)
            return c
        lax.fori_loop(0, qi * sub, body, 0)

    _t_sweep(diagonal, earlier, m_sc, acc_sc, rise_sc)
    _t_finish(tq, acc_sc, ot_sc)
    _t_out(x_ref, gate_ref, wot_ref, ot_sc, o_ref)


def _attn_specs(bsz, t, d, tq):
    nq = t // tq
    return dict(
        x=pl.BlockSpec((tq, d), lambda b, q: (b * nq + q, 0)),
        gate=pl.BlockSpec((1, 1, d), lambda b, q: (b, 0, 0)),
        qt=pl.BlockSpec((N_HEADS * HEAD_DIM, tq), lambda b, q: (0, b * nq + q)),
        kpad=pl.BlockSpec((1, N_KV_HEADS, t, LANES), lambda b, q: (b, 0, 0, 0)),
        vt=pl.BlockSpec((KV_COLS, t), lambda b, q: (0, b)),
        wot=pl.BlockSpec((d, N_HEADS * HEAD_DIM), lambda b, q: (0, 0)),
        scratch=[pltpu.VMEM((N_KV_HEADS, 1, GROUP * tq), f32),
                 pltpu.VMEM((N_KV_HEADS, 1, GROUP * tq), f32),
                 pltpu.VMEM((N_KV_HEADS, ACC_ROWS, GROUP * tq), f32),
                 pltpu.VMEM((N_KV_HEADS, LANES, GROUP * tq), bf16),
                 pltpu.VMEM((N_HEADS * HEAD_DIM, tq), bf16)])


def _fox_attn(x, gate, qt, kpad, vt, c_pad, c_t, wot, tq):
    bsz, _, t, _ = kpad.shape
    m, d = x.shape
    sp = _attn_specs(bsz, t, d, tq)
    return pl.pallas_call(
        functools.partial(_fox_attn_kernel, tq=tq, tk=ATTN_TK, chunk=512),
        grid=(bsz, t // tq),
        in_specs=[sp["x"], sp["gate"], sp["qt"], sp["kpad"], sp["vt"],
                  pl.BlockSpec((1, t, LANES), lambda b, q: (b, 0, 0)),
                  pl.BlockSpec((1, N_HEADS, tq), lambda b, q: (b, 0, q)),
                  sp["wot"]],
        out_specs=sp["x"],
        out_shape=jax.ShapeDtypeStruct((m, d), f32),
        scratch_shapes=[pltpu.VMEM((N_KV_HEADS, t, LANES), bf16)] + sp["scratch"],
        compiler_params=_cparams("parallel", "arbitrary"),
        name="fox_attn",
    )(x, gate, qt, kpad, vt, c_pad, c_t, wot)


def _cumsum_t_kernel(lft_ref, ct_ref, carry_sc, *, tb):
    @pl.when(pl.program_id(1) == 0)
    def _():
        carry_sc[...] = jnp.zeros(carry_sc.shape, f32)

    tri = (lax.broadcasted_iota(i32, (tb, tb), 0)
           <= lax.broadcasted_iota(i32, (tb, tb), 1)).astype(bf16)
    lft = lft_ref[0]
    csum = sum(_dot(p, tri) for p in _split3(lft))
    ct_ref[0] = csum + carry_sc[...]
    carry_sc[...] = carry_sc[...] + jnp.sum(lft, axis=1, keepdims=True)


def _cumsum_t(lf_t, tb):
    b, nh, t = lf_t.shape
    spec = pl.BlockSpec((1, nh, tb), lambda bi, j: (bi, 0, j))
    return pl.pallas_call(
        functools.partial(_cumsum_t_kernel, tb=tb),
        grid=(b, t // tb),
        in_specs=[spec],
        out_specs=spec,
        out_shape=jax.ShapeDtypeStruct((b, nh, t), f32),
        scratch_shapes=[pltpu.VMEM((nh, 1), f32)],
        compiler_params=_cparams("parallel", "arbitrary"),
        name="cumsum_t",
    )(lf_t)


def _kth_threshold(count_ge, shape, n_sel):
    zero = jnp.zeros(shape, i32)
    ans = jnp.where(count_ge(zero) >= n_sel, zero, jnp.full(shape, INT_MIN, i32))

    def bit_body(i, ans):
        cand = ans | jnp.left_shift(jnp.int32(1), 30 - i)
        return jnp.where(count_ge(cand) >= n_sel, cand, ans)

    return lax.fori_loop(0, 31, bit_body, ans)


BF16_SUBLANES = 2 * SUBLANES


def _kth_two_level(count_coarse_ge, count_ge, shape, n_sel):
    def coarse_value(a):
        pattern = a ^ ((a >> 31) & 0x7FFF)
        return pltpu.bitcast(pattern << 16, f32).astype(bf16)

    zero = jnp.zeros(shape, i32)
    ans = jnp.where(count_coarse_ge(coarse_value(zero)) >= n_sel, zero,
                    jnp.full(shape, -2 ** 15, i32))

    def bit_body(i, ans):
        cand = ans | jnp.left_shift(jnp.int32(1), 14 - i)
        return jnp.where(count_coarse_ge(coarse_value(cand)) >= n_sel, cand, ans)

    ans = lax.fori_loop(0, 15, bit_body, ans)
    pattern = (ans ^ ((ans >> 31) & 0x7FFF)) << 16
    centre = pattern ^ ((pattern >> 31) & 0x7FFFFFFF)
    tiny = (ans >= -2 ** 7) & (ans < 2 ** 7)
    centre = jnp.where(tiny, 0, centre)
    half = jnp.where(tiny, 2 ** 23, 2 ** 15)

    def cond(state):
        i, _, _, _, n_open = state
        return (i <= 24) & (n_open > 0)

    def body(state):
        i, lo, hi, exact, _ = state
        live = (exact == 0) & (lo < hi)
        mid = lo + ((hi - lo + 1) >> 1)
        mid = jnp.where(tiny & (lo == 0) & (i == 1), 1, mid)
        c = count_ge(mid)
        ge = c >= n_sel
        lo = jnp.where(live & ge, mid, lo)
        hi = jnp.where(live & jnp.logical_not(ge), mid - 1, hi)
        exact = jnp.where(live & (c == n_sel), 1, exact)
        n_open = jnp.sum(jnp.where((exact == 0) & (lo < hi), 1, 0))
        return i + 1, lo, hi, exact, n_open

    state = (jnp.int32(0), centre - half, centre + half, zero, jnp.int32(1))
    _, lo, _, exact, _ = lax.while_loop(cond, body, state)
    return lo, exact


def _tie_cut(count_tied_below, need, shape, n_bits):
    def body(i, p):
        cand = p | jnp.left_shift(jnp.int32(1), n_bits - 1 - i)
        return jnp.where(count_tied_below(cand) < need, cand, p)
    return lax.fori_loop(0, n_bits, body, jnp.zeros(shape, i32))


def _fold_lanes(ind, c):
    for cc in range(ind.shape[1] // LANES):
        c = c + ind[:, cc * LANES:(cc + 1) * LANES]
    return c


def _dsa_attn_kernel(x_ref, gate_ref, qt_ref, kpad_ref, vt_ref, iqt_ref, iwt_ref, ik_ref,
                     wot_ref, o_ref, key_sc, coarse_sc, bias_sc, cut_sc, m_sc, rise_sc, acc_sc,
                     qa_sc, ot_sc,
                     *, tq, n_sel, n_bits):
    qi = pl.program_id(1)
    n_k = qi + 1
    row = lax.broadcasted_iota(i32, (tq, tq), 0)
    d_iota = row - lax.broadcasted_iota(i32, (tq, tq), 1)

    def score_body(j, c):
        off = pl.multiple_of(j * tq, tq)
        ikt = ik_ref[0, pl.ds(off, tq), :]
        dots = [_dot(ikt, iqt_ref[h * IDX_DIM:(h + 1) * IDX_DIM, :]) for h in range(N_IDX_HEADS)]
        s = jnp.zeros((tq, tq), f32)
        for h in range(N_IDX_HEADS):
            s = s + jnp.maximum(dots[h], 0.0) * iwt_ref[h:h + 1, :]
        causal = d_iota <= (qi - j) * tq
        key_sc[pl.ds(off, tq), :] = jnp.where(causal, _sort_key(s), KEY_NEG_INF)
        coarse_sc[pl.ds(off, tq), :] = jnp.where(causal, s, -jnp.inf).astype(bf16)
        return c

    lax.fori_loop(0, n_k, score_body, 0)

    def count_coarse_ge(v):
        def body(j, c):
            off = pl.multiple_of(j * tq, tq)
            ind = jnp.where(coarse_sc[pl.ds(off, tq), :] >= v, jnp.ones((), bf16),
                            jnp.zeros((), bf16))
            for r in range(0, tq, BF16_SUBLANES):
                c = c + ind[r:r + BF16_SUBLANES]
            return c
        c = lax.fori_loop(0, n_k, body, jnp.zeros((BF16_SUBLANES, tq), bf16))
        return jnp.sum(c.astype(f32), axis=0, keepdims=True)

    def count(pred):
        def body(j, c):
            off = pl.multiple_of(j * tq, tq)
            ind = jnp.where(pred(key_sc[pl.ds(off, tq), :], j), 1, 0)
            for r in range(0, tq, SUBLANES):
                c = c + ind[r:r + SUBLANES]
            return c
        c = lax.fori_loop(0, n_k, body, jnp.zeros((SUBLANES, tq), i32))
        return jnp.sum(c, axis=0, keepdims=True)

    thr, exact = _kth_two_level(count_coarse_ge, lambda cand: count(lambda kt, j: kt >= cand),
                                (1, tq), n_sel)
    keep_all = n_k * tq <= n_sel
    thr = jnp.where(keep_all, INT_MIN, thr)
    exact = jnp.where(keep_all, 1, exact)
    real = thr > KEY_NEG_INF
    cut_sc[...] = jnp.where(real, INT_MAX, -1)

    @pl.when(jnp.max(jnp.where(real, 1 - exact, 0)) > 0)
    def _():
        need = n_sel - count(lambda kt, j: kt > thr)
        cut = _tie_cut(lambda cand: count(lambda kt, j: (kt == thr) & (row + j * tq < cand)),
                       need, (1, tq), n_bits)
        cut_sc[...] = jnp.where(real, jnp.where(exact == 1, INT_MAX, cut), -1)

    cut = cut_sc[...]
    floor = jnp.maximum(thr, KEY_NEG_INF)

    def bias_body(j, c):
        off = pl.multiple_of(j * tq, tq)
        kt = key_sc[pl.ds(off, tq), :]
        sel = (kt > floor) | ((kt == thr) & (row + j * tq <= cut))
        bias_sc[pl.ds(off, tq), :] = jnp.where(sel, 0.0, NEG)
        return c

    lax.fori_loop(0, n_k, bias_body, 0)

    for h in range(N_HEADS):
        _t_fill_q(qa_sc, h, tq, qt_ref[h * HEAD_DIM:(h + 1) * HEAD_DIM, :],
                  jnp.zeros((LANES - HEAD_DIM, tq), bf16))
    def tile(j, step):
        off = pl.multiple_of(j * tq, tq)

        def masked(s):
            return jnp.concatenate([s[:, jj * tq:(jj + 1) * tq] + bias_sc[pl.ds(off, tq), :]
                                    for jj in range(GROUP)], axis=1)

        _t_tile(off, tq, lambda g: kpad_ref[0, g, pl.ds(off, tq), :], vt_ref, qa_sc, masked, step)

    def oldest(step):
        tile(0, step)

    def later(step):
        def body(j, c):
            tile(j, step)
            return c
        lax.fori_loop(1, n_k, body, 0)

    _t_sweep(oldest, later, m_sc, acc_sc, rise_sc)
    _t_finish(tq, acc_sc, ot_sc)
    _t_out(x_ref, gate_ref, wot_ref, ot_sc, o_ref)


def _dsa_attn(x, gate, qt, kpad, vt, iqt, iwt, ik, wot, tq, n_sel):
    bsz, _, t, _ = kpad.shape
    m, d = x.shape
    nq = t // tq
    sp = _attn_specs(bsz, t, d, tq)
    assert tq >= n_sel and t // BF16_SUBLANES <= 256
    return pl.pallas_call(
        functools.partial(_dsa_attn_kernel, tq=tq, n_sel=n_sel, n_bits=(t - 1).bit_length()),
        grid=(bsz, nq),
        in_specs=[sp["x"], sp["gate"], sp["qt"], sp["kpad"], sp["vt"],
                  pl.BlockSpec((N_IDX_HEADS * IDX_DIM, tq), lambda b, q: (0, b * nq + q)),
                  pl.BlockSpec((2 * SUBLANES, tq), lambda b, q: (0, b * nq + q)),
                  pl.BlockSpec((1, t, IDX_DIM), lambda b, q: (b, 0, 0)),
                  sp["wot"]],
        out_specs=sp["x"],
        out_shape=jax.ShapeDtypeStruct((m, d), f32),
        scratch_shapes=[pltpu.VMEM((t, tq), i32), pltpu.VMEM((t, tq), bf16),
                        pltpu.VMEM((t, tq), f32), pltpu.VMEM((1, tq), i32)] + sp["scratch"],
        compiler_params=_cparams("parallel", "arbitrary"),
        name="dsa_attn",
    )(x, gate, qt, kpad, vt, iqt, iwt, ik, wot)


def _out_proj_kernel(x_ref, o_ref, w_ref, gate_ref, y_ref):
    y_ref[...] = x_ref[...] + gate_ref[0] * _dot(o_ref[...], w_ref[...])


def _out_proj(x, o, w, gate, tm, rows_per_mod):
    m, d = x.shape
    kdim = o.shape[1]
    r = gate.shape[1]
    return pl.pallas_call(
        _out_proj_kernel,
        grid=(m // tm,),
        in_specs=[pl.BlockSpec((tm, d), lambda i: (i, 0)),
                  pl.BlockSpec((tm, kdim), lambda i: (i, 0)),
                  pl.BlockSpec((kdim, d), lambda i: (0, 0)),
                  pl.BlockSpec((1, r, d), lambda i: (i * tm // rows_per_mod, 0, 0))],
        out_specs=pl.BlockSpec((tm, d), lambda i: (i, 0)),
        out_shape=jax.ShapeDtypeStruct((m, d), f32),
        compiler_params=_cparams("parallel"),
        name="out_proj",
    )(x, o, w, gate)


FFN_CHUNK = 256
FFN_HALO = 2 * SUBLANES


def _ffn_kernel(x_ref, xh_ref, g_ref, shift_ref, scale_ref, gate_ref, wa_ref, wb_ref,
                cw_ref, cb_ref, wd_ref, gfin_ref, o_ref, tail_ref, acc_sc, hext_sc,
                *, tm, tiles_per_seq, final):
    i = pl.program_id(0)
    x = x_ref[...]
    keep = jnp.where(i % tiles_per_seq == 0, 0.0, 1.0)
    hext_sc[:FFN_HALO, :] = (_norm_mod(xh_ref[...], g_ref[...], shift_ref[0], scale_ref[0])
                             * keep).astype(bf16)
    hext_sc[FFN_HALO:, :] = _norm_mod(x, g_ref[...], shift_ref[0], scale_ref[0]).astype(bf16)
    for c in range(D_FF // FFN_CHUNK):
        sl = slice(c * FFN_CHUNK, (c + 1) * FFN_CHUNK)
        a_ext = _dot(hext_sc[...], wa_ref[:, sl])
        b = _dot(hext_sc[FFN_HALO:, :], wb_ref[:, sl])
        a = a_ext[FFN_HALO:]
        p1 = pltpu.roll(a_ext, 1, 0)[FFN_HALO:]
        p2 = pltpu.roll(a_ext, 2, 0)[FFN_HALO:]
        conv = cb_ref[:, sl] + p2 * cw_ref[0:1, sl] + p1 * cw_ref[1:2, sl] + a * cw_ref[2:3, sl]
        y = _dot((_silu(conv) * b).astype(bf16), wd_ref[sl, :])
        if c == 0:
            acc_sc[...] = y
        else:
            acc_sc[...] += y
        tail_ref[0, :, sl] = a_ext[FFN_HALO + tm - SUBLANES:]
    y = x + gate_ref[0] * acc_sc[...]
    if final:
        y = y * lax.rsqrt(jnp.mean(y * y, axis=-1, keepdims=True) + EPS) * gfin_ref[...]
    o_ref[...] = y


def _ffn(x, g, shift, scale, gate, wa, wb, cw, cb, wd, g_final, final, tm, rows_per_seq):
    m, d = x.shape
    n_tiles = m // tm
    hb = tm // FFN_HALO
    mod_spec = pl.BlockSpec((1, 1, d), lambda i: (i * tm // rows_per_seq, 0, 0))
    const = lambda shape: pl.BlockSpec(shape, lambda i: (0,) * len(shape))
    return pl.pallas_call(
        functools.partial(_ffn_kernel, tm=tm, tiles_per_seq=rows_per_seq // tm, final=final),
        grid=(n_tiles,),
        in_specs=[pl.BlockSpec((tm, d), lambda i: (i, 0)),
                  pl.BlockSpec((FFN_HALO, d), lambda i: (jnp.maximum(i * hb - 1, 0), 0)),
                  const((1, d)), mod_spec, mod_spec, mod_spec,
                  const((d, D_FF)), const((d, D_FF)), const((SUBLANES, D_FF)),
                  const((1, D_FF)), const((D_FF, d)), const((1, d))],
        out_specs=[pl.BlockSpec((tm, d), lambda i: (i, 0)),
                   pl.BlockSpec((1, SUBLANES, D_FF), lambda i: (i, 0, 0))],
        out_shape=[jax.ShapeDtypeStruct((m, d), f32),
                   jax.ShapeDtypeStruct((n_tiles, SUBLANES, D_FF), f32)],
        scratch_shapes=[pltpu.VMEM((tm, d), f32), pltpu.VMEM((FFN_HALO + tm, d), bf16)],
        compiler_params=_cparams("parallel"),
        name="ffn",
    )(x, x, g, shift, scale, gate, wa, wb, cw, cb, wd, g_final)


def _ffn_sample_kernel(x_ref, g_ref, shift_ref, scale_ref, gate_ref, p2_ref, p1_ref, wa_ref,
                       wb_ref, cw_ref, cb_ref, wd_ref, o_ref, a_ref, acc_sc):
    j = pl.program_id(0)

    @pl.when(j == 0)
    def _():
        acc_sc[...] = jnp.zeros(acc_sc.shape, f32)

    x = x_ref[...]
    h = _norm_mod(x, g_ref[...], shift_ref[0], scale_ref[0]).astype(bf16)
    a = _dot(h, wa_ref[...])
    b = _dot(h, wb_ref[...])
    conv = (cb_ref[...] + p2_ref[...] * cw_ref[0:1, :] + p1_ref[...] * cw_ref[1:2, :]
            + a * cw_ref[2:3, :])
    acc_sc[...] += _dot((_silu(conv) * b).astype(bf16), wd_ref[...])
    a_ref[...] = a

    @pl.when(j == pl.num_programs(0) - 1)
    def _():
        o_ref[...] = x + gate_ref[0] * acc_sc[...]


def _ffn_sample(x, g, shift, scale, gate, p2, p1, wa, wb, cw, cb, wd, tn):
    m, d = x.shape
    const = lambda shape: pl.BlockSpec(shape, lambda j: (0,) * len(shape))
    return pl.pallas_call(
        _ffn_sample_kernel,
        grid=(D_FF // tn,),
        in_specs=[const((m, d)), const((1, d)), const((1, m, d)), const((1, m, d)),
                  const((1, m, d)),
                  pl.BlockSpec((m, tn), lambda j: (0, j)), pl.BlockSpec((m, tn), lambda j: (0, j)),
                  pl.BlockSpec((d, tn), lambda j: (0, j)), pl.BlockSpec((d, tn), lambda j: (0, j)),
                  pl.BlockSpec((SUBLANES, tn), lambda j: (0, j)),
                  pl.BlockSpec((1, tn), lambda j: (0, j)),
                  pl.BlockSpec((tn, d), lambda j: (j, 0))],
        out_specs=[const((m, d)), pl.BlockSpec((m, tn), lambda j: (0, j))],
        out_shape=[jax.ShapeDtypeStruct((m, d), f32), jax.ShapeDtypeStruct((m, D_FF), f32)],
        scratch_shapes=[pltpu.VMEM((m, d), f32)],
        compiler_params=_cparams("arbitrary"),
        name="ffn_sample",
    )(x, g, shift, scale, gate, p2, p1, wa, wb, cw, cb, wd)


def _final_norm_kernel(x_ref, g_ref, o_ref):
    x = x_ref[...]
    o_ref[...] = x * lax.rsqrt(jnp.mean(x * x, axis=-1, keepdims=True) + EPS) * g_ref[...]


def _final_norm(x, g, tm):
    m, d = x.shape
    return pl.pallas_call(
        _final_norm_kernel,
        grid=(m // tm,),
        in_specs=[pl.BlockSpec((tm, d), lambda i: (i, 0)), pl.BlockSpec((1, d), lambda i: (0, 0))],
        out_specs=pl.BlockSpec((tm, d), lambda i: (i, 0)),
        out_shape=jax.ShapeDtypeStruct((m, d), f32),
        compiler_params=_cparams("parallel"),
        name="final_norm",
    )(x, g)


def _seq_page_copies(pt_ref, hbm, layer, seq, buf, slot, sem, n_pages):
    return [pltpu.make_async_copy(hbm.at[layer, pt_ref[seq, p]],
                                  buf.at[slot, :, pl.ds(p * PAGE_SIZE, PAGE_SIZE)], sem.at[slot])
            for p in range(n_pages)]


def _gather_seq(pt_ref, hbm, layer, buf, sem, n_pages):
    b = pl.program_id(0)
    slot = b % 2

    @pl.when(b == 0)
    def _():
        for cp in _seq_page_copies(pt_ref, hbm, layer, 0, buf, 0, sem, n_pages):
            cp.start()

    @pl.when(b + 1 < pl.num_programs(0))
    def _():
        for cp in _seq_page_copies(pt_ref, hbm, layer, b + 1, buf, 1 - slot, sem, n_pages):
            cp.start()

    for cp in _seq_page_copies(pt_ref, hbm, layer, b, buf, slot, sem, n_pages):
        cp.wait()
    return slot


def _sample_score_kernel(pt_ref, iq_ref, iw_ref, kidx_hbm, s_ref, buf, sem, *, layer, n_pages):
    slot = _gather_seq(pt_ref, kidx_hbm, layer, buf, sem, n_pages)
    d = _dot(iq_ref[0], buf[slot].astype(bf16))
    s_ref[0] = jnp.sum(jnp.maximum(d, 0.0) * iw_ref[0], axis=0, keepdims=True)


def _sample_scores(page_table, iq, iw, cache_kidx_t, layer):
    s, n_pages = page_table.shape
    length = n_pages * PAGE_SIZE
    grid_spec = pltpu.PrefetchScalarGridSpec(
        num_scalar_prefetch=1, grid=(s,),
        in_specs=[pl.BlockSpec((1, N_IDX_HEADS, IDX_DIM), lambda b, pt: (b, 0, 0)),
                  pl.BlockSpec((1, N_IDX_HEADS, 1), lambda b, pt: (b, 0, 0)),
                  pl.BlockSpec(memory_space=pl.ANY)],
        out_specs=pl.BlockSpec((1, 1, length), lambda b, pt: (b, 0, 0)),
        scratch_shapes=[pltpu.VMEM((2, IDX_DIM, length), f32), pltpu.SemaphoreType.DMA((2,))])
    out = pl.pallas_call(
        functools.partial(_sample_score_kernel, layer=layer, n_pages=n_pages),
        grid_spec=grid_spec,
        out_shape=jax.ShapeDtypeStruct((s, 1, length), f32),
        compiler_params=_cparams("arbitrary"),
        name="sample_scores",
    )(page_table, iq, iw, cache_kidx_t)
    return out.reshape(s, length)


SEL_TILE = 1024


def _sample_select_kernel(s_ref, iq_ref, ikt_ref, iw_ref, bias_ref, bnew_ref, key_sc,
                          *, n_sel, n_bits):
    rows, length = s_ref.shape
    n_tiles = length // SEL_TILE
    prod = iq_ref[...].astype(f32) * ikt_ref[...].astype(f32)
    grp = (lax.broadcasted_iota(i32, (prod.shape[1], LANES), 0) // IDX_DIM
           == lax.broadcasted_iota(i32, (prod.shape[1], LANES), 1)).astype(bf16)
    d_new = sum(_dot(p, grp) for p in _split3(prod))
    s_new = jnp.sum(jnp.maximum(d_new, 0.0) * iw_ref[...], axis=1, keepdims=True)
    key_new = _sort_key(s_new)
    key_sc[...] = _sort_key(s_ref[...])
    col = lax.broadcasted_iota(i32, (rows, SEL_TILE), 1)

    def count(pred, pred_new):
        def body(j, c):
            off = pl.multiple_of(j * SEL_TILE, SEL_TILE)
            return _fold_lanes(jnp.where(pred(key_sc[:, pl.ds(off, SEL_TILE)], j), 1, 0), c)
        c = lax.fori_loop(0, n_tiles, body, jnp.zeros((rows, LANES), i32))
        return jnp.sum(c, axis=1, keepdims=True) + jnp.where(pred_new, 1, 0)

    thr = _kth_threshold(lambda cand: count(lambda kt, j: kt >= cand, key_new >= cand),
                         (rows, 1), n_sel)
    need = n_sel - count(lambda kt, j: kt > thr, key_new > thr)
    cut = _tie_cut(lambda cand: count(lambda kt, j: (kt == thr) & (col + j * SEL_TILE < cand),
                                      (key_new == thr) & (length < cand)),
                   need, (rows, 1), n_bits)

    def bias_body(j, c):
        off = pl.multiple_of(j * SEL_TILE, SEL_TILE)
        kt = key_sc[:, pl.ds(off, SEL_TILE)]
        sel = (kt > thr) | ((kt == thr) & (col + j * SEL_TILE <= cut))
        bias_ref[:, pl.ds(off, SEL_TILE)] = jnp.where(sel, 0.0, NEG)
        return c

    lax.fori_loop(0, n_tiles, bias_body, 0)
    sel_new = (key_new > thr) | ((key_new == thr) & (length <= cut))
    bnew_ref[...] = jnp.broadcast_to(jnp.where(sel_new, 0.0, NEG), bnew_ref.shape)


def _sample_select(scores, iq, ik_tiled, iw, n_sel):
    rows, length = scores.shape
    return pl.pallas_call(
        functools.partial(_sample_select_kernel, n_sel=n_sel, n_bits=length.bit_length()),
        out_shape=[jax.ShapeDtypeStruct((rows, length), f32),
                   jax.ShapeDtypeStruct((rows, LANES), f32)],
        scratch_shapes=[pltpu.VMEM((rows, length), i32)],
        compiler_params=pltpu.CompilerParams(vmem_limit_bytes=VMEM_LIMIT),
        name="sample_select",
    )(scores, iq, ik_tiled, iw)


SUFFIX_CHUNK = 256


def _suffix_bias_kernel(pt_ref, lfnew_ref, lf_hbm, bias_ref, buf, sem, *, layer, n_pages):
    slot = _gather_seq(pt_ref, lf_hbm, layer, buf, sem, n_pages)
    w = SUFFIX_CHUNK
    tri = (lax.broadcasted_iota(i32, (w, w), 0)
           > lax.broadcasted_iota(i32, (w, w), 1)).astype(bf16)
    n_chunks = n_pages * PAGE_SIZE // w
    chunk = lambda c: buf[slot, :, c * w:(c + 1) * w]
    totals = [jnp.sum(chunk(c), axis=1, keepdims=True) for c in range(n_chunks)]
    carry = lfnew_ref[0]
    for c in reversed(range(n_chunks)):
        bias_ref[0, :, c * w:(c + 1) * w] = sum(_dot(p, tri) for p in _split3(chunk(c))) + carry
        carry = carry + totals[c]


def _suffix_bias(page_table, lf_new, cache_logf_t, layer):
    s, n_pages = page_table.shape
    length = n_pages * PAGE_SIZE
    grid_spec = pltpu.PrefetchScalarGridSpec(
        num_scalar_prefetch=1, grid=(s,),
        in_specs=[pl.BlockSpec((1, N_HEADS, 1), lambda b, pt: (b, 0, 0)),
                  pl.BlockSpec(memory_space=pl.ANY)],
        out_specs=pl.BlockSpec((1, N_HEADS, length), lambda b, pt: (b, 0, 0)),
        scratch_shapes=[pltpu.VMEM((2, N_HEADS, length), f32), pltpu.SemaphoreType.DMA((2,))])
    return pl.pallas_call(
        functools.partial(_suffix_bias_kernel, layer=layer, n_pages=n_pages),
        grid_spec=grid_spec,
        out_shape=jax.ShapeDtypeStruct((s, N_HEADS, length), f32),
        compiler_params=_cparams("arbitrary"),
        name="suffix_bias",
    )(page_table, lf_new, cache_logf_t)


DEC_PAGES = 8
DEC_SLOTS = 4


def _kv_chunk_copies(pt_ref, k_hbm, v_hbm, layer, seq, chunk, kbuf, vbuf, slot, sem):
    cps = []
    for i in range(DEC_PAGES):
        page = pt_ref[seq, chunk * DEC_PAGES + i]
        cps.append(pltpu.make_async_copy(k_hbm.at[layer, page], kbuf.at[slot, i], sem.at[0, slot]))
        cps.append(pltpu.make_async_copy(v_hbm.at[layer, page], vbuf.at[slot, i], sem.at[1, slot]))
    return cps


def _decode_attn_kernel(pt_ref, q_ref, knew_ref, vnew_ref, bnew_ref, bias_ref, k_hbm, v_hbm,
                        o_ref, kbuf, vbuf, sem, *, layer, n_chunks):
    b = pl.program_id(0)
    q = q_ref[0]
    width = DEC_PAGES * PAGE_SIZE
    copies = functools.partial(_kv_chunk_copies, pt_ref, k_hbm, v_hbm, layer)

    def start(seq, chunk, slot):
        for cp in copies(seq, chunk, kbuf, vbuf, slot, sem):
            cp.start()

    ahead = DEC_SLOTS - 1
    base = b * n_chunks

    @pl.when(b == 0)
    def _():
        for c in range(ahead):
            start(0, c, c)

    kn = knew_ref[0].astype(bf16).astype(f32)
    m = jnp.sum(q.astype(f32) * kn, axis=1, keepdims=True) + bnew_ref[0][:, 0:1]
    l = jnp.ones_like(m)
    acc = jnp.broadcast_to(vnew_ref[0].astype(bf16).astype(f32), (N_HEADS, KV_COLS))

    for c in range(n_chunks):
        slot = lax.rem(base + c, DEC_SLOTS)
        nxt = c + ahead
        nslot = lax.rem(base + nxt, DEC_SLOTS)
        if nxt < n_chunks:
            start(b, nxt, nslot)
        else:
            @pl.when(b + 1 < pl.num_programs(0))
            def _():
                start(b + 1, nxt - n_chunks, nslot)
        for cp in copies(b, c, kbuf, vbuf, slot, sem):
            cp.wait()
        s = jnp.concatenate([_dot(q, kbuf[slot, i].astype(bf16)) for i in range(DEC_PAGES)],
                            axis=1) + bias_ref[0, :, c * width:(c + 1) * width]
        m_new = jnp.maximum(m, jnp.max(s, axis=1, keepdims=True))
        p = jnp.exp(s - m_new)
        alpha = jnp.exp(m - m_new)
        l = alpha * l + jnp.sum(p, axis=1, keepdims=True)
        pb = p.astype(bf16)
        pv = sum(_dot_nt(pb[:, i * PAGE_SIZE:(i + 1) * PAGE_SIZE], vbuf[slot, i].astype(bf16))
                 for i in range(DEC_PAGES))
        acc = alpha * acc + pv
        m = m_new
    o_ref[0] = acc / l


def _decode_attn(page_table, qmat, k_new, v_new, bias_new, bias, cache_kt, cache_vt, layer):
    s, n_pages = page_table.shape
    hb = bias.shape[1]
    n_chunks = n_pages // DEC_PAGES
    assert n_chunks * DEC_PAGES == n_pages and n_chunks >= DEC_SLOTS - 1
    row = lambda shape: pl.BlockSpec(shape, lambda b, pt: (b, 0, 0))
    hbm = pl.BlockSpec(memory_space=pl.ANY)
    grid_spec = pltpu.PrefetchScalarGridSpec(
        num_scalar_prefetch=1, grid=(s,),
        in_specs=[row((1, N_HEADS, KV_COLS)), row((1, 1, KV_COLS)), row((1, 1, KV_COLS)),
                  row((1, hb, LANES)), row((1, hb, n_pages * PAGE_SIZE)), hbm, hbm],
        out_specs=row((1, N_HEADS, KV_COLS)),
        scratch_shapes=[pltpu.VMEM((DEC_SLOTS, DEC_PAGES, KV_COLS, PAGE_SIZE), f32),
                        pltpu.VMEM((DEC_SLOTS, DEC_PAGES, KV_COLS, PAGE_SIZE), f32),
                        pltpu.SemaphoreType.DMA((2, DEC_SLOTS))])
    return pl.pallas_call(
        functools.partial(_decode_attn_kernel, layer=layer, n_chunks=n_chunks),
        grid_spec=grid_spec,
        out_shape=jax.ShapeDtypeStruct((s, N_HEADS, KV_COLS), f32),
        compiler_params=_cparams("arbitrary"),
        name="decode_attn",
    )(page_table, qmat, k_new, v_new, bias_new, bias, cache_kt, cache_vt)


def _pad_cols(w, n):
    return jnp.pad(w, ((0, 0), (0, n)))


def _decode_q(q_bf):
    s = q_bf.shape[0]
    onehot = (jnp.arange(N_HEADS)[:, None] // GROUP == jnp.arange(N_KV_HEADS)[None, :])
    q = q_bf.reshape(s, N_HEADS, 1, HEAD_DIM) * onehot[None, :, :, None].astype(q_bf.dtype)
    return q.reshape(s, N_HEADS, KV_COLS)


def _decode_o(o_full):
    s = o_full.shape[0]
    o = o_full.reshape(s, N_KV_HEADS, GROUP, N_KV_HEADS, HEAD_DIM)
    return jnp.stack([o[:, g, :, g, :] for g in range(N_KV_HEADS)], axis=1).reshape(s, -1)


def kernel(x_prompt, x_sample, cache_k_a, cache_v_a, cache_kidx_a, cache_k_b, cache_v_b,
           cache_logf_b, state_conv, page_table, c_prompt, c_sample, w_ada, b_ada, g_mix,
           g_ffn, w_in_a, w_o_a, w_in_b, b_f, w_o_b, w_up, conv_w, conv_b, w_down, g_final):
    bsz, t, d = x_prompt.shape
    s = x_sample.shape[0]
    depth = w_ada.shape[0]
    m = bsz * t
    past_len = page_table.shape[1] * PAGE_SIZE
    tm = 512
    tq = 256

    pad_rows = (-(bsz + s)) % SUBLANES
    c_all = jnp.concatenate([c_prompt, c_sample, jnp.zeros((pad_rows, d), f32)], axis=0)
    mods = _adaln(c_all, w_ada, b_ada)
    mods_p = mods[:, :bsz].reshape(depth, bsz, 6, d)
    mods_s = mods[:, bsz:bsz + s].reshape(depth, s, 6, d)

    tabs_p = _rope_tables(jnp.arange(t))
    tabs_pt = _rope_tables_t(jnp.arange(t))
    tabs_s = _rope_tables(jnp.full((s,), past_len))
    zero_bf = jnp.zeros((1, LANES), f32)
    zero_col = jnp.zeros((N_HEADS, 1), f32)

    def rows_t(a):
        return a.reshape(bsz, N_KV_HEADS, HEAD_DIM, t).transpose(0, 3, 1, 2)

    xp = x_prompt.reshape(m, d)
    xs = x_sample.reshape(s, d)
    kv_shape_s = (s, 1, N_KV_HEADS, HEAD_DIM)
    ka_p, va_p, ia_p, kb_p, vb_p, lb_p, cv_p = [], [], [], [], [], [], []
    ka_s, va_s, ia_s, kb_s, vb_s, lb_s, cv_s = [], [], [], [], [], [], []

    def pages_t(cache):
        ct = jnp.moveaxis(cache, 2, -1)
        return ct.reshape(ct.shape[:2] + (-1, PAGE_SIZE))

    cache_k_a, cache_v_a, cache_kidx_a = pages_t(cache_k_a), pages_t(cache_v_a), pages_t(cache_kidx_a)
    cache_k_b, cache_v_b, cache_logf_b = pages_t(cache_k_b), pages_t(cache_v_b), pages_t(cache_logf_b)

    for i in range(depth):
        j = i // 2
        mp = [mods_p[i, :, c][:, None, :] for c in range(6)]
        ms = [mods_s[i, :, c][None] for c in range(6)]
        g_m = g_mix[i][None]
        if i % 2 == 0:
            w = w_in_a[j]
            n_qkvi = QKV_COLS + N_IDX_HEADS * IDX_DIM + IDX_DIM
            w_in = jnp.concatenate([_pad_cols(w[:, :n_qkvi], LANES - IDX_DIM),
                                    _pad_cols(w[:, n_qkvi:], LANES - N_IDX_HEADS)],
                                   axis=1).astype(bf16)
            w_o = w_o_a[j].astype(bf16)
            nq = N_HEADS * HEAD_DIM
            n_iq = QKV_COLS + N_IDX_HEADS * IDX_DIM
            wk = w[:, nq:nq + KV_COLS]
            w_row = jnp.concatenate([wk, _pad_cols(w[:, n_iq:n_qkvi], LANES - IDX_DIM)],
                                    axis=1).astype(bf16)
            w_t = jnp.concatenate([w[:, :nq], w[:, nq + KV_COLS:QKV_COLS], wk, w[:, QKV_COLS:n_qkvi],
                                   _pad_cols(w[:, n_qkvi:], IW_ROWS - N_IDX_HEADS)],
                                  axis=1).T.astype(bf16)
            qt, vt, kpad, kt32, vt32, iqt, iwt, ikb, ikt32 = _in_proj_t(
                xp, g_m, mp[0], mp[1], w_row, w_t, tabs_p, tabs_pt, zero_col, True, tm, t)
            xp = _dsa_attn(xp, mp[2], qt, kpad, vt, iqt, iwt, ikb.reshape(bsz, t, IDX_DIM),
                           w_o.T, tq, min(TOPK_MAX, t // 4))
            ka_p.append(rows_t(kt32))
            va_p.append(rows_t(vt32))
            ia_p.append(ikt32.transpose(0, 2, 1))
            q, k32, kb, v32, vb, iq, ik32, ikb, iw = _in_proj(
                xs, g_m, ms[0], ms[1], w_in, tabs_s, zero_bf, SEGS_A, OUTS_A, s, s)
            scores = _sample_scores(page_table, iq.reshape(s, N_IDX_HEADS, IDX_DIM),
                                    iw[:, :N_IDX_HEADS].reshape(s, N_IDX_HEADS, 1),
                                    cache_kidx_a, j)
            bias, bias_new = _sample_select(scores, iq, jnp.tile(ikb[:, :IDX_DIM], (1, N_IDX_HEADS)),
                                            iw, min(TOPK_MAX, (past_len + 1) // 4))
            o_full = _decode_attn(page_table, _decode_q(q), k32[:, None], v32[:, None],
                                  bias_new[:, None], bias[:, None], cache_k_a, cache_v_a, j)
            xs = _out_proj(xs, _decode_o(o_full).astype(bf16), w_o, ms[2], s, s)
            ka_s.append(k32.reshape(kv_shape_s))
            va_s.append(v32.reshape(kv_shape_s))
            ia_s.append(ik32[:, :IDX_DIM].reshape(s, 1, IDX_DIM))
        else:
            w_in = _pad_cols(w_in_b[j], LANES - N_HEADS).astype(bf16)
            w_o = w_o_b[j].astype(bf16)
            bf_pad = _pad_cols(b_f[j][None], LANES - N_HEADS)
            nq = N_HEADS * HEAD_DIM
            wk = w_in[:, nq:nq + KV_COLS]
            w_t = jnp.concatenate([w_in[:, :nq], w_in[:, nq + KV_COLS:QKV_COLS], wk,
                                   w_in[:, QKV_COLS:QKV_COLS + N_HEADS]], axis=1).T
            qt, vt, kpad, kt32, vt32, lft = _in_proj_t(
                xp, g_m, mp[0], mp[1], wk, w_t, tabs_p, tabs_pt, b_f[j][:, None], False, tm, t)
            c_t = _cumsum_t(lft, tq)
            c_pad = jnp.pad(c_t.transpose(0, 2, 1), ((0, 0), (0, 0), (0, LANES - N_HEADS)))
            xp = _fox_attn(xp, mp[2], qt, kpad, vt, c_pad, c_t, w_o.T, tq)
            kb_p.append(rows_t(kt32))
            vb_p.append(rows_t(vt32))
            lb_p.append(lft.transpose(0, 2, 1))
            q, k32, kb, v32, vb, lf = _in_proj(
                xs, g_m, ms[0], ms[1], w_in, tabs_s, bf_pad, SEGS_B, OUTS_B, s, s)
            bias = _suffix_bias(page_table, lf[:, :N_HEADS, None], cache_logf_b, j)
            o_full = _decode_attn(page_table, _decode_q(q), k32[:, None], v32[:, None],
                                  jnp.zeros((s, N_HEADS, LANES), f32), bias,
                                  cache_k_b, cache_v_b, j)
            xs = _out_proj(xs, _decode_o(o_full).astype(bf16), w_o, ms[2], s, s)
            kb_s.append(k32.reshape(kv_shape_s))
            vb_s.append(v32.reshape(kv_shape_s))
            lb_s.append(lf[:, :N_HEADS].reshape(s, 1, N_HEADS))

        wa = w_up[i][:, :D_FF].astype(bf16)
        wb = w_up[i][:, D_FF:].astype(bf16)
        wd = w_down[i].astype(bf16)
        cw = jnp.pad(conv_w[i], ((0, SUBLANES - CONV_W), (0, 0)))
        cb = conv_b[i][None]
        g_f = g_ffn[i][None]
        xp, tail = _ffn(xp, g_f, mp[3], mp[4], mp[5], wa, wb, cw, cb, wd, g_final[None],
                        i == depth - 1, tm, t)
        cv_p.append(tail.reshape(bsz, t // tm, SUBLANES, D_FF)[:, -1, SUBLANES - (CONV_W - 1):])
        xs, a_s = _ffn_sample(xs, g_f, ms[3], ms[4], ms[5], state_conv[i][:, 0],
                              state_conv[i][:, 1], wa, wb, cw, cb, wd, D_FF // 2)
        cv_s.append(jnp.stack([state_conv[i][:, 1], a_s], axis=1))

    y_prompt = xp.reshape(bsz, t, d)
    y_sample = _final_norm(xs, g_final[None], s).reshape(s, 1, d)
    return (y_prompt, y_sample,
            jnp.stack(ka_p), jnp.stack(va_p), jnp.stack(ia_p),
            jnp.stack(kb_p), jnp.stack(vb_p), jnp.stack(lb_p), jnp.stack(cv_p),
            jnp.stack(ka_s), jnp.stack(va_s), jnp.stack(ia_s),
            jnp.stack(kb_s), jnp.stack(vb_s), jnp.stack(lb_s), jnp.stack(cv_s))
```
